```python
import jax, jax.numpy as jnp
from jax import lax
import numpy as np

D_MODEL = 2048
BATCH = 8
SEQ = 8192
DEPTH = 1

HEAD_DIM = 128
N_HEADS_DIL = 8
N_HEADS_SB = 8
DIL_PATTERNS = ((128, 1), (512, 4), (2048, 16))
BLOCK = 128
ROT_DIM = HEAD_DIM // 4
ROPE_THETA = 500000.0
D_FF = 4 * D_MODEL
PLE_DIM = 256
EPS = 1e-6
W_DIL = N_HEADS_DIL * HEAD_DIM
W_SB = N_HEADS_SB * HEAD_DIM
D_IN = 3 * W_DIL + 3 * W_SB + 2 * D_MODEL

kernel_name = "hybrid_dilated_stickbreaking_gated_block"


def rmsnorm(x, g):
    xf = x.astype(jnp.float32)
    y = xf * lax.rsqrt(jnp.mean(xf * xf, axis=-1, keepdims=True) + EPS)
    return (y * g.astype(jnp.float32)).astype(x.dtype)


def partial_rope(x, pos):
    half = ROT_DIM // 2
    inv = ROPE_THETA ** (-jnp.arange(0, ROT_DIM, 2, dtype=jnp.float32) / ROT_DIM)
    ang = pos[:, None] * inv[None, :]
    cos = jnp.cos(ang)[None, :, None, :]
    sin = jnp.sin(ang)[None, :, None, :]
    xr = x[..., :ROT_DIM].astype(jnp.float32)
    x1, x2 = xr[..., :half], xr[..., half:]
    rot = jnp.concatenate([x1 * cos - x2 * sin, x2 * cos + x1 * sin], axis=-1)
    return jnp.concatenate([rot.astype(x.dtype), x[..., ROT_DIM:]], axis=-1)


def dilated_window(q, k, v, window, dilation):
    B, S, H, Dh = q.shape
    W = window // dilation
    assert W <= BLOCK
    M = S // dilation
    Mp = -(-M // BLOCK) * BLOCK
    nb = Mp // BLOCK

    def to_blocks(t):
        t = t.astype(jnp.float32).reshape(B, M, dilation, H, Dh).transpose(0, 2, 3, 1, 4)
        t = jnp.pad(t, ((0, 0), (0, 0), (0, 0), (0, Mp - M), (0, 0)))
        return t.reshape(B, dilation, H, nb, BLOCK, Dh)

    qb, kb, vb = to_blocks(q), to_blocks(k), to_blocks(v)

    def with_prev(t):
        prev = jnp.pad(t, ((0, 0), (0, 0), (0, 0), (1, 0), (0, 0), (0, 0)))[:, :, :, :-1]
        return jnp.concatenate([prev, t], axis=-2)

    kw, vw = with_prev(kb), with_prev(vb)
    s = jnp.einsum('bdhnqe,bdhnke->bdhnqk', qb, kw) * (HEAD_DIM ** -0.5)
    n_i = jnp.arange(nb)[:, None, None]
    q_i = jnp.arange(BLOCK)[None, :, None]
    k_i = jnp.arange(2 * BLOCK)[None, None, :]
    dist = BLOCK + q_i - k_i
    valid = (dist >= 0) & (dist <= W) & ((n_i > 0) | (k_i >= BLOCK))
    s = jnp.where(valid, s, -jnp.inf)
    m = jnp.max(s, axis=-1, keepdims=True)
    e = jnp.exp(s - m)
    den = jnp.sum(e, axis=-1, keepdims=True)
    o = jnp.einsum('bdhnqk,bdhnke->bdhnqe', e, vw) / den
    lse = (m + jnp.log(den))[..., 0]

    o = o.reshape(B, dilation, H, Mp, Dh)[:, :, :, :M].transpose(0, 3, 1, 2, 4).reshape(B, S, H, Dh)
    lse = lse.reshape(B, dilation, H, Mp)[:, :, :, :M].transpose(0, 3, 1, 2).reshape(B, S, H)
    return o, lse


def dilated_mixture(q, k, v):
    outs, lses = [], []
    for window, dilation in DIL_PATTERNS:
        o, l = dilated_window(q, k, v, window, dilation)
        outs.append(o)
        lses.append(l)
    o = jnp.stack(outs, axis=0)
    w = jax.nn.softmax(jnp.stack(lses, axis=0), axis=0)
    return jnp.sum(w[..., None] * o, axis=0)


def stick_breaking(q, k, v):
    B, S, H, Dh = q.shape
    nq = S // BLOCK
    qf = q.astype(jnp.float32).transpose(0, 2, 1, 3)
    kf = k.astype(jnp.float32).transpose(0, 2, 1, 3)
    vf = v.astype(jnp.float32).transpose(0, 2, 1, 3)
    qblocks = qf.reshape(B, H, nq, BLOCK, Dh).transpose(2, 0, 1, 3, 4)
    key_pos = jnp.arange(S)

    def one_block(args):
        qb, n = args
        z = jnp.einsum('bhqe,bhke->bhqk', qb, kf) * (HEAD_DIM ** -0.5)
        q_pos = n * BLOCK + jnp.arange(BLOCK)
        causal = key_pos[None, :] < q_pos[:, None]
        log_1mb = jnp.where(causal, jax.nn.log_sigmoid(-z), 0.0)
        after = lax.cumsum(log_1mb, axis=3, reverse=True) - log_1mb
        a = jnp.where(causal, jnp.exp(jax.nn.log_sigmoid(z) + after), 0.0)
        return jnp.einsum('bhqk,bhke->bhqe', a, vf)

    o = lax.map(one_block, (qblocks, jnp.arange(nq)))
    return o.transpose(1, 0, 3, 2, 4).reshape(B, S, H, Dh)


def _fwd_setup_inputs(seed: int = 0) -> dict:
    key = jax.random.key(seed)
    ks = jax.random.split(key, 16)

    def w(k, shape, fan_in):
        return jax.random.normal(k, shape, jnp.float32) * (fan_in ** -0.5)

    def gain(k, shape):
        return 1.0 + 0.05 * jax.random.normal(k, shape, jnp.float32)

    return {
        "x": jax.random.normal(ks[0], (BATCH, SEQ, D_MODEL), jnp.float32),
        "p": jax.random.normal(ks[1], (DEPTH, BATCH, SEQ, PLE_DIM), jnp.float32),
        "g_mix": gain(ks[2], (DEPTH, D_MODEL)),
        "w_in": w(ks[3], (DEPTH, D_MODEL, D_IN), D_MODEL),
        "qn_gain": gain(ks[4], (DEPTH, HEAD_DIM)),
        "kn_gain": gain(ks[5], (DEPTH, HEAD_DIM)),
        "w_branch_a": w(ks[6], (DEPTH, W_DIL, D_MODEL), W_DIL),
        "w_branch_b": w(ks[7], (DEPTH, W_SB, D_MODEL), W_SB),
        "w_out": w(ks[8], (DEPTH, D_MODEL, D_MODEL), D_MODEL),
        "g_mlp": gain(ks[9], (DEPTH, D_MODEL)),
        "w_up": w(ks[10], (DEPTH, D_MODEL, D_FF), D_MODEL),
        "w_down": w(ks[11], (DEPTH, D_FF, D_MODEL), D_FF),
        "g_ple": gain(ks[12], (DEPTH, D_MODEL)),
        "w_ple_gate": w(ks[13], (DEPTH, D_MODEL, D_MODEL), D_MODEL),
        "w_ple_proj": w(ks[14], (DEPTH, PLE_DIM, D_MODEL), PLE_DIM),
    }


def _fwd_reference(x, p, g_mix, w_in, qn_gain, kn_gain, w_branch_a, w_branch_b, w_out,
              g_mlp, w_up, w_down, g_ple, w_ple_gate, w_ple_proj):
    B, S, _ = x.shape
    pos = jnp.arange(S, dtype=jnp.float32)
    splits = np.cumsum([W_DIL, W_DIL, W_DIL, W_SB, W_SB, W_SB, D_MODEL])
    for i in range(DEPTH):
        h = rmsnorm(x, g_mix[i])
        proj = h @ w_in[i]
        qa, ka, va, qb, kb, vb, ga, gb = jnp.split(proj, splits, axis=-1)
        qa = qa.reshape(B, S, N_HEADS_DIL, HEAD_DIM)
        ka = ka.reshape(B, S, N_HEADS_DIL, HEAD_DIM)
        va = va.reshape(B, S, N_HEADS_DIL, HEAD_DIM)
        qa = partial_rope(rmsnorm(qa, qn_gain[i]), pos)
        ka = partial_rope(rmsnorm(ka, kn_gain[i]), pos)
        ya = dilated_mixture(qa, ka, va).astype(x.dtype).reshape(B, S, W_DIL)

        qb = qb.reshape(B, S, N_HEADS_SB, HEAD_DIM)
        kb = kb.reshape(B, S, N_HEADS_SB, HEAD_DIM)
        vb = vb.reshape(B, S, N_HEADS_SB, HEAD_DIM)
        yb = stick_breaking(qb, kb, vb).astype(x.dtype).reshape(B, S, W_SB)

        merged = jax.nn.sigmoid(ga) * (ya @ w_branch_a[i]) + jax.nn.sigmoid(gb) * (yb @ w_branch_b[i])
        x = x + merged @ w_out[i]

        hm = rmsnorm(x, g_mlp[i])
        x = x + jnp.square(jax.nn.relu(hm @ w_up[i])) @ w_down[i]

        hp = rmsnorm(x, g_ple[i])
        x = x + (p[i] @ w_ple_proj[i]) * jax.nn.sigmoid(hp @ w_ple_gate[i])
    return x


import jax as _jax
import jax.numpy as _jnp

TWIN_FORMAT = 'train_step'
FWD_PARAMS = ['x', 'p', 'g_mix', 'w_in', 'qn_gain', 'kn_gain', 'w_branch_a', 'w_branch_b', 'w_out', 'g_mlp', 'w_up', 'w_down', 'g_ple', 'w_ple_gate', 'w_ple_proj']
TWIN_WEIGHTS = ['g_mix', 'w_in', 'qn_gain', 'kn_gain', 'w_branch_a', 'w_branch_b', 'w_out', 'g_mlp', 'w_up', 'w_down', 'g_ple', 'w_ple_gate', 'w_ple_proj']
TWIN_DIFF_INPUT = 'x'
TWIN_INPUTS = ['x', 'p', 'g_mix', 'w_in', 'qn_gain', 'kn_gain', 'w_branch_a', 'w_branch_b', 'w_out', 'g_mlp', 'w_up', 'w_down', 'g_ple', 'w_ple_gate', 'w_ple_proj', 'loss_target', 'm_g_mix', 'm_w_in', 'm_qn_gain', 'm_kn_gain', 'm_w_branch_a', 'm_w_branch_b', 'm_w_out', 'm_g_mlp', 'm_w_up', 'm_w_down', 'm_g_ple', 'm_w_ple_gate', 'm_w_ple_proj', 'v_g_mix', 'v_w_in', 'v_qn_gain', 'v_kn_gain', 'v_w_branch_a', 'v_w_branch_b', 'v_w_out', 'v_g_mlp', 'v_w_up', 'v_w_down', 'v_g_ple', 'v_w_ple_gate', 'v_w_ple_proj']
TWIN_OUTPUTS = ['loss', 'grad_x', 'grad_g_mix', 'grad_w_in', 'grad_qn_gain', 'grad_kn_gain', 'grad_w_branch_a', 'grad_w_branch_b', 'grad_w_out', 'grad_g_mlp', 'grad_w_up', 'grad_w_down', 'grad_g_ple', 'grad_w_ple_gate', 'grad_w_ple_proj', 'delta_g_mix', 'delta_w_in', 'delta_qn_gain', 'delta_kn_gain', 'delta_w_branch_a', 'delta_w_branch_b', 'delta_w_out', 'delta_g_mlp', 'delta_w_up', 'delta_w_down', 'delta_g_ple', 'delta_w_ple_gate', 'delta_w_ple_proj', 'new_m_g_mix', 'new_m_w_in', 'new_m_qn_gain', 'new_m_kn_gain', 'new_m_w_branch_a', 'new_m_w_branch_b', 'new_m_w_out', 'new_m_g_mlp', 'new_m_w_up', 'new_m_w_down', 'new_m_g_ple', 'new_m_w_ple_gate', 'new_m_w_ple_proj', 'new_v_g_mix', 'new_v_w_in', 'new_v_qn_gain', 'new_v_kn_gain', 'new_v_w_branch_a', 'new_v_w_branch_b', 'new_v_w_out', 'new_v_g_mlp', 'new_v_w_up', 'new_v_w_down', 'new_v_g_ple', 'new_v_w_ple_gate', 'new_v_w_ple_proj']
TWIN_LEAF_KINDS = {'loss': 'loss', 'grad_x': 'grad_x', 'grad_g_mix': 'grad_w', 'grad_w_in': 'grad_w', 'grad_qn_gain': 'grad_w', 'grad_kn_gain': 'grad_w', 'grad_w_branch_a': 'grad_w', 'grad_w_branch_b': 'grad_w', 'grad_w_out': 'grad_w', 'grad_g_mlp': 'grad_w', 'grad_w_up': 'grad_w', 'grad_w_down': 'grad_w', 'grad_g_ple': 'grad_w', 'grad_w_ple_gate': 'grad_w', 'grad_w_ple_proj': 'grad_w', 'delta_g_mix': 'delta_w', 'delta_w_in': 'delta_w', 'delta_qn_gain': 'delta_w', 'delta_kn_gain': 'delta_w', 'delta_w_branch_a': 'delta_w', 'delta_w_branch_b': 'delta_w', 'delta_w_out': 'delta_w', 'delta_g_mlp': 'delta_w', 'delta_w_up': 'delta_w', 'delta_w_down': 'delta_w', 'delta_g_ple': 'delta_w', 'delta_w_ple_gate': 'delta_w', 'delta_w_ple_proj': 'delta_w', 'new_m_g_mix': 'new_m', 'new_m_w_in': 'new_m', 'new_m_qn_gain': 'new_m', 'new_m_kn_gain': 'new_m', 'new_m_w_branch_a': 'new_m', 'new_m_w_branch_b': 'new_m', 'new_m_w_out': 'new_m', 'new_m_g_mlp': 'new_m', 'new_m_w_up': 'new_m', 'new_m_w_down': 'new_m', 'new_m_g_ple': 'new_m', 'new_m_w_ple_gate': 'new_m', 'new_m_w_ple_proj': 'new_m', 'new_v_g_mix': 'new_v', 'new_v_w_in': 'new_v', 'new_v_qn_gain': 'new_v', 'new_v_kn_gain': 'new_v', 'new_v_w_branch_a': 'new_v', 'new_v_w_branch_b': 'new_v', 'new_v_w_out': 'new_v', 'new_v_g_mlp': 'new_v', 'new_v_w_up': 'new_v', 'new_v_w_down': 'new_v', 'new_v_g_ple': 'new_v', 'new_v_w_ple_gate': 'new_v', 'new_v_w_ple_proj': 'new_v'}


def _forward(args):
    return _fwd_reference(*[args[k] for k in FWD_PARAMS])


def _output_shape():
    def fwd():
        inp = _fwd_setup_inputs(0)
        return _fwd_reference(*[inp[k] for k in FWD_PARAMS])
    out = _jax.eval_shape(fwd)
    return out.shape, out.dtype

N_MICROBATCH = 1
ADAM_LR = 0.001
ADAM_B1 = 0.9
ADAM_B2 = 0.999
ADAM_EPS = 1e-08
ADAM_WD = 0.01
ADAM_STEP = 10
PER_EXAMPLE_BATCH_AXIS = {'x': 0, 'p': 1, 'loss_target': 0}
SHARED_INPUTS = []
_WEIGHT_DTYPES = {'g_mix': _jnp.float32, 'w_in': _jnp.float32, 'qn_gain': _jnp.float32, 'kn_gain': _jnp.float32, 'w_branch_a': _jnp.float32, 'w_branch_b': _jnp.float32, 'w_out': _jnp.float32, 'g_mlp': _jnp.float32, 'w_up': _jnp.float32, 'w_down': _jnp.float32, 'g_ple': _jnp.float32, 'w_ple_gate': _jnp.float32, 'w_ple_proj': _jnp.float32}
MOMENT_SCALE = {'g_mix': 5.145341e+00, 'w_in': 1.284271e-01, 'qn_gain': 1.040033e+00, 'kn_gain': 1.040486e+00, 'w_branch_a': 1.057123e-01, 'w_branch_b': 2.533106e-01, 'w_out': 2.602537e-01, 'g_mlp': 9.637635e+01, 'w_up': 4.910513e-01, 'w_down': 8.000172e+00, 'g_ple': 1.397589e+00, 'w_ple_gate': 9.964311e-01, 'w_ple_proj': 4.448109e-01}


def _to_microbatches(a, axis):
    t = _jnp.moveaxis(a, axis, 0)
    t = t.reshape((N_MICROBATCH, t.shape[0] // N_MICROBATCH) + t.shape[1:])
    return _jnp.moveaxis(t, 1, axis + 1)


def setup_inputs(seed: int = 0) -> dict:
    inp = _fwd_setup_inputs(seed)
    key = _jax.random.fold_in(_jax.random.key(seed), 7919)
    shape, _ = _output_shape()
    out = dict(inp)
    out["loss_target"] = _jax.random.normal(_jax.random.fold_in(key, 0), shape, _jnp.float32)
    for i, name in enumerate(TWIN_WEIGHTS):
        w = inp[name].astype(_jnp.float32)
        if MOMENT_SCALE is None:
            s = _jnp.sqrt(_jnp.mean(_jnp.square(w)) + 1e-30)
        else:
            s = MOMENT_SCALE[name]
        km, kv = _jax.random.split(_jax.random.fold_in(key, i + 1))
        out[name] = w
        out["m_" + name] = s * _jax.random.normal(km, w.shape, _jnp.float32)
        out["v_" + name] = (s * s) * _jax.random.uniform(kv, w.shape, _jnp.float32, 0.5, 1.5)
    if N_MICROBATCH > 1:
        for name, axis in PER_EXAMPLE_BATCH_AXIS.items():
            out[name] = _to_microbatches(out[name], axis)
    return {'x': out['x'], 'p': out['p'], 'g_mix': out['g_mix'], 'w_in': out['w_in'], 'qn_gain': out['qn_gain'], 'kn_gain': out['kn_gain'], 'w_branch_a': out['w_branch_a'], 'w_branch_b': out['w_branch_b'], 'w_out': out['w_out'], 'g_mlp': out['g_mlp'], 'w_up': out['w_up'], 'w_down': out['w_down'], 'g_ple': out['g_ple'], 'w_ple_gate': out['w_ple_gate'], 'w_ple_proj': out['w_ple_proj'], 'loss_target': out['loss_target'], 'm_g_mix': out['m_g_mix'], 'm_w_in': out['m_w_in'], 'm_qn_gain': out['m_qn_gain'], 'm_kn_gain': out['m_kn_gain'], 'm_w_branch_a': out['m_w_branch_a'], 'm_w_branch_b': out['m_w_branch_b'], 'm_w_out': out['m_w_out'], 'm_g_mlp': out['m_g_mlp'], 'm_w_up': out['m_w_up'], 'm_w_down': out['m_w_down'], 'm_g_ple': out['m_g_ple'], 'm_w_ple_gate': out['m_w_ple_gate'], 'm_w_ple_proj': out['m_w_ple_proj'], 'v_g_mix': out['v_g_mix'], 'v_w_in': out['v_w_in'], 'v_qn_gain': out['v_qn_gain'], 'v_kn_gain': out['v_kn_gain'], 'v_w_branch_a': out['v_w_branch_a'], 'v_w_branch_b': out['v_w_branch_b'], 'v_w_out': out['v_w_out'], 'v_g_mlp': out['v_g_mlp'], 'v_w_up': out['v_w_up'], 'v_w_down': out['v_w_down'], 'v_g_ple': out['v_g_ple'], 'v_w_ple_gate': out['v_w_ple_gate'], 'v_w_ple_proj': out['v_w_ple_proj']}


def _loss(weights, diff, rest, loss_target):
    with _jax.named_scope("forward"):
        args = {**rest, TWIN_DIFF_INPUT: diff, **{k: w.astype(_WEIGHT_DTYPES[k]) for k, w in weights.items()}}
        y = _forward(args)
    with _jax.named_scope("loss_head"):
        err = _jnp.square(y.astype(_jnp.float32) - loss_target)
        return 0.5 * _jnp.sum(_jnp.mean(err, axis=-1)) if err.ndim else 0.5 * err


def _adamw(w, g, m, v):
    m = ADAM_B1 * m + (1.0 - ADAM_B1) * g
    v = ADAM_B2 * v + (1.0 - ADAM_B2) * _jnp.square(g)
    m_hat = m / (1.0 - ADAM_B1 ** ADAM_STEP)
    v_hat = v / (1.0 - ADAM_B2 ** ADAM_STEP)
    delta = -ADAM_LR * (m_hat / (_jnp.sqrt(v_hat) + ADAM_EPS) + ADAM_WD * w)
    return delta, m, v


def reference(x, p, g_mix, w_in, qn_gain, kn_gain, w_branch_a, w_branch_b, w_out, g_mlp, w_up, w_down, g_ple, w_ple_gate, w_ple_proj, loss_target, m_g_mix, m_w_in, m_qn_gain, m_kn_gain, m_w_branch_a, m_w_branch_b, m_w_out, m_g_mlp, m_w_up, m_w_down, m_g_ple, m_w_ple_gate, m_w_ple_proj, v_g_mix, v_w_in, v_qn_gain, v_kn_gain, v_w_branch_a, v_w_branch_b, v_w_out, v_g_mlp, v_w_up, v_w_down, v_g_ple, v_w_ple_gate, v_w_ple_proj):
    given = dict(x=x, p=p, g_mix=g_mix, w_in=w_in, qn_gain=qn_gain, kn_gain=kn_gain, w_branch_a=w_branch_a, w_branch_b=w_branch_b, w_out=w_out, g_mlp=g_mlp, w_up=w_up, w_down=w_down, g_ple=g_ple, w_ple_gate=w_ple_gate, w_ple_proj=w_ple_proj, loss_target=loss_target, m_g_mix=m_g_mix, m_w_in=m_w_in, m_qn_gain=m_qn_gain, m_kn_gain=m_kn_gain, m_w_branch_a=m_w_branch_a, m_w_branch_b=m_w_branch_b, m_w_out=m_w_out, m_g_mlp=m_g_mlp, m_w_up=m_w_up, m_w_down=m_w_down, m_g_ple=m_g_ple, m_w_ple_gate=m_w_ple_gate, m_w_ple_proj=m_w_ple_proj, v_g_mix=v_g_mix, v_w_in=v_w_in, v_qn_gain=v_qn_gain, v_kn_gain=v_kn_gain, v_w_branch_a=v_w_branch_a, v_w_branch_b=v_w_branch_b, v_w_out=v_w_out, v_g_mlp=v_g_mlp, v_w_up=v_w_up, v_w_down=v_w_down, v_g_ple=v_g_ple, v_w_ple_gate=v_w_ple_gate, v_w_ple_proj=v_w_ple_proj)
    weights = {n: given[n] for n in TWIN_WEIGHTS}
    shared = {n: given[n] for n in SHARED_INPUTS}
    per_example = {n: given[n] for n in ['x', 'p']}
    grad_fn = _jax.value_and_grad(_loss, argnums=(0, 1))

    def one_microbatch(ex, loss_target):
        ex = dict(ex)
        diff = ex.pop(TWIN_DIFF_INPUT)
        return grad_fn(weights, diff, {**shared, **ex}, loss_target)

    if N_MICROBATCH == 1:
        loss, (grad_w, grad_x) = one_microbatch(per_example, given["loss_target"])
    else:
        def body(carry, xs):
            loss_sum, grad_sum = carry
            l_k, (gw_k, gx_k) = one_microbatch(xs[0], xs[1])
            with _jax.named_scope("update"):
                return (loss_sum + l_k, _jax.tree.map(_jnp.add, grad_sum, gw_k)), gx_k

        init = (_jnp.zeros((), _jnp.float32), _jax.tree.map(_jnp.zeros_like, weights))
        (loss, grad_w), grad_x = _jax.lax.scan(body, init, (per_example, given["loss_target"]))
    with _jax.named_scope("update"):
        delta_w, new_m, new_v = {}, {}, {}
        for n in TWIN_WEIGHTS:
            delta_w[n], new_m[n], new_v[n] = _adamw(weights[n], grad_w[n], given["m_" + n], given["v_" + n])
    return (loss, grad_x, *[grad_w[n] for n in TWIN_WEIGHTS], *[delta_w[n] for n in TWIN_WEIGHTS],
            *[new_m[n] for n in TWIN_WEIGHTS], *[new_v[n] for n in TWIN_WEIGHTS])
```

```python
import functools
import math

import jax
import jax.numpy as jnp
from jax import lax
from jax.experimental import pallas as pl
from jax.experimental.pallas import tpu as pltpu

F32 = jnp.float32
BF16 = jnp.bfloat16
MESH = pl.DeviceIdType.MESH

HEAD = 128
WINDOW = 128
DILATIONS = (1, 4, 16)
ROT = HEAD // 4
ROPE_THETA = 500000.0
EPS = 1e-6
NEG = -1e30
N_DEV = 8

ADAM_LR = 0.001
ADAM_B1 = 0.9
ADAM_B2 = 0.999
ADAM_EPS = 1e-08
ADAM_WD = 0.01
ADAM_STEP = 10

SUBLANES = 8
VMEM_CAP_MB = 56

NT_DIMS = (((1,), (1,)), ((), ()))
TN_DIMS = (((0,), (0,)), ((), ()))


def _params(semantics, vmem_mb):
    return pltpu.CompilerParams(dimension_semantics=semantics, vmem_limit_bytes=min(vmem_mb, VMEM_CAP_MB) << 20)


def _sigmoid(x):
    return 1.0 / (1.0 + jnp.exp(-x))


def _split_bf16(x):
    hi = x.astype(BF16)
    lo = (x - hi.astype(F32)).astype(BF16)
    return hi, lo


def _ew(fn, ins, outs, grid, name, colsums=(), vmem_mb=40):
    n_in, n_out, n_cs = len(ins), len(outs), len(colsums)
    steps = math.prod(grid)

    def body(*refs):
        in_refs = refs[:n_in]
        out_refs = refs[n_in:n_in + n_out]
        cs_refs = refs[n_in + n_out:n_in + n_out + n_cs]
        acc_refs = refs[n_in + n_out + n_cs:]
        vals = fn(*[r[...] for r in in_refs])
        if not isinstance(vals, tuple):
            vals = (vals,)
        for r, v in zip(out_refs, vals[:n_out]):
            r[...] = v.astype(r.dtype)
        if n_cs:
            step = pl.program_id(0)
            for ax in range(1, len(grid)):
                step = step * grid[ax] + pl.program_id(ax)
            for acc, cs, v in zip(acc_refs, cs_refs, vals[n_out:]):
                part = v.reshape(-1, SUBLANES, v.shape[-1]).sum(axis=0)

                @pl.when(step == 0)
                def _(acc=acc, part=part):
                    acc[...] = part

                @pl.when(step > 0)
                def _(acc=acc, part=part):
                    acc[...] += part

                @pl.when(step == steps - 1)
                def _(acc=acc, cs=cs):
                    cs[...] = acc[...].sum(axis=0, keepdims=True)

    out_shape = [jax.ShapeDtypeStruct(s, d) for s, d, _, _ in outs]
    out_specs = [pl.BlockSpec(b, m) for _, _, b, m in outs]
    for w in colsums:
        out_shape.append(jax.ShapeDtypeStruct((1, w), F32))
        out_specs.append(pl.BlockSpec((1, w), lambda *_: (0, 0)))
    sem = ("arbitrary",) * len(grid) if n_cs else ("parallel",) * len(grid)
    res = pl.pallas_call(
        body, name=name, grid=grid,
        in_specs=[pl.BlockSpec(b, m) for _, b, m in ins],
        out_specs=out_specs, out_shape=out_shape,
        scratch_shapes=[pltpu.VMEM((SUBLANES, w), F32) for w in colsums],
        compiler_params=_params(sem, vmem_mb),
    )(*[a for a, _, _ in ins])
    return res


def _row_spec(a, tm):
    return (a, (tm, a.shape[1]), lambda i: (i, 0))


def _const_spec(a):
    return (a, a.shape, lambda *_: (0,) * a.ndim)


def _mm(mode, a, b, *, tm, tn, tk, out_dtypes, name, epilogue=None, extras=(), n_off=0, n_cnt=None,
        out_nb=1, extra_outs=(), vmem_mb=48):
    if mode == "nn":
        m, kdim = a.shape
        nb, _, n = b.shape
        npb = n // tn
        ncols = nb * n
        n_tiles = (ncols // tn) if n_cnt is None else n_cnt
        a_spec = pl.BlockSpec((tm, tk), lambda i, j, k: (i, k))
        b_spec = pl.BlockSpec((None, tk, tn), lambda i, j, k: ((j + n_off) // npb, k, (j + n_off) % npb))
        dims = (((1,), (0,)), ((), ()))
    elif mode == "nt":
        m, kdim = a.shape
        nb, nout, n = b.shape
        kpb = n // tk
        n_tiles = nout // tn
        a_spec = pl.BlockSpec((tm, tk), lambda i, j, k: (i, k))
        b_spec = pl.BlockSpec((None, tn, tk), lambda i, j, k: (k // kpb, j, k % kpb))
        dims = NT_DIMS
    else:
        kdim, m = a.shape
        ncols = b.shape[1]
        n_tiles = ncols // tn
        a_spec = pl.BlockSpec((tk, tm), lambda i, j, k: (k, i))
        b_spec = pl.BlockSpec((tk, tn), lambda i, j, k: (k, j))
        dims = TN_DIMS
    nk = kdim // tk
    assert kdim % tk == 0 and m % tm == 0
    grid = (m // tm, n_tiles, nk)
    n_ex, n_out = len(extras), len(out_dtypes) + len(extra_outs)

    def body(*refs):
        a_ref, b_ref = refs[0], refs[1]
        ex_refs = refs[2:2 + n_ex]
        out_refs = refs[2 + n_ex:2 + n_ex + n_out]

        def finish(acc):
            vals = (acc,) if epilogue is None else epilogue(acc, *[r[...] for r in ex_refs])
            for r, v in zip(out_refs, vals):
                r[...] = v.astype(r.dtype)

        part = lax.dot_general(a_ref[...], b_ref[...], dims, preferred_element_type=F32)
        if nk == 1:
            finish(part)
        else:
            acc_ref = refs[-1]
            k = pl.program_id(2)

            @pl.when(k == 0)
            def _():
                acc_ref[...] = part

            @pl.when(k > 0)
            def _():
                acc_ref[...] += part

            @pl.when(k == nk - 1)
            def _():
                finish(acc_ref[...])

    if mode == "tn":
        npo = (ncols // out_nb) // tn
        out_shape = [jax.ShapeDtypeStruct((out_nb, m, ncols // out_nb), d) for d in out_dtypes]
        out_specs = [pl.BlockSpec((None, tm, tn), lambda i, j, k: (j // npo, i, j % npo)) for _ in out_dtypes]
    else:
        out_shape = [jax.ShapeDtypeStruct((m, n_tiles * tn), d) for d in out_dtypes]
        out_specs = [pl.BlockSpec((tm, tn), lambda i, j, k: (i, j)) for _ in out_dtypes]
    for s, d, blk, imap in extra_outs:
        out_shape.append(jax.ShapeDtypeStruct(s, d))
        out_specs.append(pl.BlockSpec(blk, lambda i, j, k, imap=imap: imap(i, j)))
    ex_specs = [pl.BlockSpec((tm, tn), lambda i, j, k, off=off: (i, j + off)) for _, off in extras]
    return pl.pallas_call(
        body, name=name, grid=grid,
        in_specs=[a_spec, b_spec] + ex_specs,
        out_specs=out_specs, out_shape=out_shape,
        scratch_shapes=[pltpu.VMEM((tm, tn), F32)] if nk > 1 else [],
        compiler_params=_params(("parallel", "parallel", "arbitrary"), vmem_mb),
    )(a, b, *[e for e, _ in extras])


def _rms_fwd(x, g, name):
    tm = 256

    def fn(xv, gv):
        r = lax.rsqrt(jnp.mean(xv * xv, axis=-1, keepdims=True) + EPS)
        return xv * r * gv

    return _ew(fn, [_row_spec(x, tm), _const_spec(g)], [(x.shape, BF16, (tm, x.shape[1]), lambda i: (i, 0))],
               (x.shape[0] // tm,), name)[0]


def _rms_bwd(dh, x, g, res, name):
    tm = 256
    d = x.shape[1]

    def fn(dhv, xv, gv, rv):
        r = lax.rsqrt(jnp.mean(xv * xv, axis=-1, keepdims=True) + EPS)
        xh = xv * r
        dyg = dhv * gv
        dx = rv + r * (dyg - xh * jnp.mean(dyg * xh, axis=-1, keepdims=True))
        return dx, dx, dhv * xh

    spec = lambda dt: (x.shape, dt, (tm, d), lambda i: (i, 0))
    return _ew(fn, [_row_spec(dh, tm), _row_spec(x, tm), _const_spec(g), _row_spec(res, tm)],
               [spec(F32), spec(BF16)], (x.shape[0] // tm,), name, colsums=(d,))


def _rope_tables(s):
    half = ROT // 2
    pos = jnp.arange(s, dtype=F32)
    inv = ROPE_THETA ** (-jnp.arange(0, ROT, 2, dtype=F32) / ROT)
    ang = pos[:, None] * inv[None, :]
    cos, sin = jnp.cos(ang), jnp.sin(ang)
    pad = jnp.zeros((s, HEAD - ROT), F32)
    c = jnp.concatenate([cos, cos, pad + 1.0], axis=1)
    a = jnp.concatenate([-sin, jnp.zeros_like(sin), pad], axis=1)
    b = jnp.concatenate([jnp.zeros_like(sin), sin, pad], axis=1)
    return c, a, b


def _headnorm_rope(proj, off, gain, tabs, n_heads, name):
    tm = 512
    s = proj.shape[0]

    def fn(xv, gv, c, a, b):
        r = lax.rsqrt(jnp.mean(xv * xv, axis=-1, keepdims=True) + EPS)
        y = xv * r * gv
        return c * y + a * pltpu.roll(y, HEAD - ROT // 2, 1) + b * pltpu.roll(y, ROT // 2, 1)

    tab = lambda t: (t, (tm, HEAD), lambda i, h: (i, 0))
    return _ew(fn, [(proj, (tm, HEAD), lambda i, h: (i, off + h)), (gain, (1, HEAD), lambda i, h: (0, 0))]
               + [tab(t) for t in tabs],
               [((s, n_heads * HEAD), BF16, (tm, HEAD), lambda i, h: (i, h))], (s // tm, n_heads), name)[0]


def _headnorm_rope_bwd(dys, proj, off, gain, tabs, n_heads, name):
    tm = 512
    s = proj.shape[0]
    n_dy = len(dys)

    def fn(*vals):
        dy = vals[0]
        for v in vals[1:n_dy]:
            dy = dy + v
        xv, gv, c, a, b = vals[n_dy:]
        dn = c * dy + pltpu.roll(a * dy, ROT // 2, 1) + pltpu.roll(b * dy, HEAD - ROT // 2, 1)
        r = lax.rsqrt(jnp.mean(xv * xv, axis=-1, keepdims=True) + EPS)
        xh = xv * r
        dyg = dn * gv
        dx = r * (dyg - xh * jnp.mean(dyg * xh, axis=-1, keepdims=True))
        return dx, dn * xh

    tab = lambda t: (t, (tm, HEAD), lambda i, h: (i, 0))
    return _ew(fn, [(d, (tm, HEAD), lambda i, h: (i, h)) for d in dys]
               + [(proj, (tm, HEAD), lambda i, h: (i, off + h)), (gain, (1, HEAD), lambda i, h: (0, 0))]
               + [tab(t) for t in tabs],
               [((s, n_heads * HEAD), BF16, (tm, HEAD), lambda i, h: (i, h))], (s // tm, n_heads), name,
               colsums=(HEAD,))


def _phase_major(a, d):
    s, w = a.shape
    if d == 1:
        return a.reshape(1, s, w)
    return a.reshape(s // d, d, w).transpose(1, 0, 2)


def _token_major(a):
    d, m, w = a.shape
    if d == 1:
        return a.reshape(m, w)
    return a.transpose(1, 0, 2).reshape(m * d, w)


def _dil_tq(m):
    return min(512, m)


def _dilated_fwd(q, k, v, voff, n_heads, name):
    d, m, _ = q.shape
    tq = _dil_tq(m)
    nq = m // tq
    per = tq // WINDOW
    scale = HEAD ** -0.5

    def body(q_ref, kc_ref, kp_ref, vc_ref, vp_ref, o_ref, l_ref):
        n = pl.program_id(2)
        kk = jnp.concatenate([kp_ref[...], kc_ref[...]], axis=0)
        vv = jnp.concatenate([vp_ref[...], vc_ref[...]], axis=0)
        s = lax.dot_general(q_ref[...], kk, NT_DIMS, preferred_element_type=F32) * scale
        row = lax.broadcasted_iota(jnp.int32, s.shape, 0)
        col = lax.broadcasted_iota(jnp.int32, s.shape, 1)
        dist = row + WINDOW - col
        valid = (dist >= 0) & (dist <= WINDOW) & ((n > 0) | (col >= WINDOW))
        s = jnp.where(valid, s, NEG)
        mx = jnp.max(s, axis=-1, keepdims=True)
        e = jnp.exp(s - mx)
        den = jnp.sum(e, axis=-1, keepdims=True)
        o = jnp.dot(e.astype(BF16), vv, preferred_element_type=F32) / den
        o_ref[...] = o
        l_ref[...] = jnp.broadcast_to(mx + jnp.log(den), (tq, HEAD))

    cur = lambda off: pl.BlockSpec((None, tq, HEAD), lambda r, h, n: (r, n, off + h))
    prev = lambda off: pl.BlockSpec((None, WINDOW, HEAD), lambda r, h, n: (r, jnp.maximum(n * per - 1, 0), off + h))
    out = jax.ShapeDtypeStruct((d, m, n_heads * HEAD), F32)
    return pl.pallas_call(
        body, name=name, grid=(d, n_heads, nq),
        in_specs=[cur(0), cur(0), prev(0), cur(voff), prev(voff)],
        out_specs=[cur(0), cur(0)], out_shape=[out, out],
        compiler_params=_params(("parallel",) * 3, 32),
    )(q, k, k, v, v)


def _dilated_bwd(q, k, v, voff, dy, lse, delta, n_heads, name):
    d, m, _ = q.shape
    tq = _dil_tq(m)
    nq = m // tq
    per = tq // WINDOW
    last_blk = m // WINDOW - 1
    scale = HEAD ** -0.5

    def body(qc_ref, qn_ref, kc_ref, kp_ref, vc_ref, vp_ref, dyc_ref, dyn_ref, lc_ref, ln_ref, dc_ref, dn_ref,
             dq_ref, dk_ref, dv_ref):
        n = pl.program_id(2)
        qc, kc, vc = qc_ref[...], kc_ref[...], vc_ref[...]
        dyc = dyc_ref[...].astype(BF16)
        kk = jnp.concatenate([kp_ref[...], kc], axis=0)
        vv = jnp.concatenate([vp_ref[...], vc], axis=0)
        s = lax.dot_general(qc, kk, NT_DIMS, preferred_element_type=F32) * scale
        row = lax.broadcasted_iota(jnp.int32, s.shape, 0)
        col = lax.broadcasted_iota(jnp.int32, s.shape, 1)
        dist = row + WINDOW - col
        valid = (dist >= 0) & (dist <= WINDOW) & ((n > 0) | (col >= WINDOW))
        p = jnp.where(valid, jnp.exp(jnp.minimum(s - lc_ref[:, :1], 0.0)), 0.0)
        dp = lax.dot_general(dyc, vv, NT_DIMS, preferred_element_type=F32)
        ds = p * (dp - dc_ref[:, :1]) * scale
        dq_ref[...] = jnp.dot(ds.astype(BF16), kk, preferred_element_type=F32)
        qq = jnp.concatenate([qc, qn_ref[...]], axis=0)
        dyy = jnp.concatenate([dyc, dyn_ref[...].astype(BF16)], axis=0)
        ll = jnp.concatenate([lc_ref[...], ln_ref[...]], axis=0)[:, :1]
        dd = jnp.concatenate([dc_ref[...], dn_ref[...]], axis=0)[:, :1]
        s2 = lax.dot_general(qq, kc, NT_DIMS, preferred_element_type=F32) * scale
        row2 = lax.broadcasted_iota(jnp.int32, s2.shape, 0)
        col2 = lax.broadcasted_iota(jnp.int32, s2.shape, 1)
        dist2 = row2 - col2
        valid2 = (dist2 >= 0) & (dist2 <= WINDOW) & ((n < nq - 1) | (row2 < tq))
        p2 = jnp.where(valid2, jnp.exp(jnp.minimum(s2 - ll, 0.0)), 0.0)
        dp2 = lax.dot_general(dyy, vc, NT_DIMS, preferred_element_type=F32)
        ds2 = p2 * (dp2 - dd) * scale
        dk_ref[...] = lax.dot_general(ds2.astype(BF16), qq, TN_DIMS, preferred_element_type=F32)
        dv_ref[...] = lax.dot_general(p2.astype(BF16), dyy, TN_DIMS, preferred_element_type=F32)

    cur = lambda off: pl.BlockSpec((None, tq, HEAD), lambda r, h, n: (r, n, off + h))
    prev = lambda off: pl.BlockSpec((None, WINDOW, HEAD), lambda r, h, n: (r, jnp.maximum(n * per - 1, 0), off + h))
    nxt = lambda off: pl.BlockSpec((None, WINDOW, HEAD),
                                   lambda r, h, n: (r, jnp.minimum((n + 1) * per, last_blk), off + h))
    out = jax.ShapeDtypeStruct((d, m, n_heads * HEAD), F32)
    return pl.pallas_call(
        body, name=name, grid=(d, n_heads, nq),
        in_specs=[cur(0), nxt(0), cur(0), prev(0), cur(voff), prev(voff), cur(0), nxt(0), cur(0), nxt(0), cur(0),
                  nxt(0)],
        out_specs=[cur(0)] * 3, out_shape=[out] * 3,
        compiler_params=_params(("parallel",) * 3, 40),
    )(q, q, k, k, v, v, dy, dy, lse, lse, delta, delta)


def _mix_fwd(os_, ls_, name):
    tm = 256
    s, w = os_[0].shape
    n = len(os_)

    def fn(*vals):
        o, l = vals[:n], vals[n:]
        mx = functools.reduce(jnp.maximum, l)
        e = [jnp.exp(x - mx) for x in l]
        den = functools.reduce(jnp.add, e)
        y = functools.reduce(jnp.add, [ei * oi for ei, oi in zip(e, o)]) / den
        return y, mx + jnp.log(den)

    return _ew(fn, [_row_spec(a, tm) for a in list(os_) + list(ls_)],
               [((s, w), BF16, (tm, w), lambda i: (i, 0)), ((s, w), F32, (tm, w), lambda i: (i, 0))],
               (s // tm,), name)


def _head_dot(dy, y, n_heads, name):
    tm = 512
    s = dy.shape[0]

    def fn(a, b):
        return jnp.broadcast_to(jnp.sum(a * b.astype(F32), axis=-1, keepdims=True), a.shape)

    blk = lambda a: (a, (tm, HEAD), lambda i, h: (i, h))
    return _ew(fn, [blk(dy), blk(y)], [((s, n_heads * HEAD), F32, (tm, HEAD), lambda i, h: (i, h))],
               (s // tm, n_heads), name)[0]


SB_T = 256


def _softplus(z):
    return jnp.maximum(z, 0.0) + jnp.log1p(jnp.exp(-jnp.abs(z)))


def _sb_fwd(qkv, qoff, koff, voff, n_heads, name):
    s = qkv.shape[0]
    t = SB_T
    scale = HEAD ** -0.5

    def body(q_ref, k_ref, v_ref, o_ref, tot_ref):
        i = pl.program_id(1)
        q = q_ref[...]
        row = lax.broadcasted_iota(jnp.int32, (t, t), 0)
        col = lax.broadcasted_iota(jnp.int32, (t, t), 1)
        after = jnp.where(row > col, 1.0, 0.0).astype(BF16)
        causal = col < row

        def tile(j, carry, o, diagonal):
            start = pl.multiple_of(j * t, t)
            kj = k_ref[pl.ds(start, t), :]
            vj = v_ref[pl.ds(start, t), :]
            z = lax.dot_general(q, kj, NT_DIMS, preferred_element_type=F32) * scale
            sp = _softplus(z)
            l1 = -sp
            if diagonal:
                l1 = jnp.where(causal, l1, 0.0)
            hi, lo = _split_bf16(l1)
            sfx = jnp.dot(hi, after, preferred_element_type=F32) + jnp.dot(lo, after, preferred_element_type=F32)
            a = jnp.exp(jnp.minimum(z - sp + carry + sfx, 0.0))
            if diagonal:
                a = jnp.where(causal, a, 0.0)
            o = o + jnp.dot(a.astype(BF16), vj, preferred_element_type=F32)
            return carry + jnp.sum(l1, axis=-1, keepdims=True), o

        carry, o = tile(i, jnp.zeros((t, 1), F32), jnp.zeros((t, HEAD), F32), True)
        carry, o = lax.fori_loop(0, i, lambda jj, co: tile(i - 1 - jj, co[0], co[1], False), (carry, o))
        o_ref[...] = o.astype(o_ref.dtype)
        tot_ref[...] = jnp.broadcast_to(carry, (t, HEAD))

    full = lambda off: pl.BlockSpec((s, HEAD), lambda h, i: (0, off + h))
    tile_spec = lambda off: pl.BlockSpec((t, HEAD), lambda h, i: (i, off + h))
    return pl.pallas_call(
        body, name=name, grid=(n_heads, s // t),
        in_specs=[tile_spec(qoff), full(koff), full(voff)],
        out_specs=[tile_spec(0), tile_spec(0)],
        out_shape=[jax.ShapeDtypeStruct((s, n_heads * HEAD), BF16), jax.ShapeDtypeStruct((s, n_heads * HEAD), F32)],
        compiler_params=_params(("parallel", "arbitrary"), 40),
    )(qkv, qkv, qkv)


def _sb_bwd(qkv, qoff, koff, voff, do, tot, n_heads, name):
    s = qkv.shape[0]
    t = SB_T
    scale = HEAD ** -0.5

    def body(q_ref, k_ref, v_ref, do_ref, tot_ref, dq_ref, dk_ref, dv_ref):
        i = pl.program_id(1)

        @pl.when(i == 0)
        def _():
            dk_ref[...] = jnp.zeros_like(dk_ref)
            dv_ref[...] = jnp.zeros_like(dv_ref)

        q = q_ref[...]
        do_b = do_ref[...].astype(BF16)
        total = tot_ref[:, :1]
        row = lax.broadcasted_iota(jnp.int32, (t, t), 0)
        col = lax.broadcasted_iota(jnp.int32, (t, t), 1)
        upto = jnp.where(row <= col, 1.0, 0.0).astype(BF16)
        before = jnp.where(row < col, 1.0, 0.0).astype(BF16)
        causal = col < row

        def tile(j, lsum, psum, dq, diagonal):
            start = pl.multiple_of(j * t, t)
            kj = k_ref[pl.ds(start, t), :]
            vj = v_ref[pl.ds(start, t), :]
            z = lax.dot_general(q, kj, NT_DIMS, preferred_element_type=F32) * scale
            sp = _softplus(z)
            l1 = -sp
            if diagonal:
                l1 = jnp.where(causal, l1, 0.0)
            hi, lo = _split_bf16(l1)
            pre = jnp.dot(hi, upto, preferred_element_type=F32) + jnp.dot(lo, upto, preferred_element_type=F32)
            a = jnp.exp(jnp.minimum(z - sp + (total - lsum - pre), 0.0))
            if diagonal:
                a = jnp.where(causal, a, 0.0)
            da = lax.dot_general(do_b, vj, NT_DIMS, preferred_element_type=F32)
            p = a * da
            hi, lo = _split_bf16(p)
            c = psum + jnp.dot(hi, before, preferred_element_type=F32) + jnp.dot(lo, before,
                                                                                 preferred_element_type=F32)
            beta = jnp.exp(z - sp)
            dz = (p * (1.0 - beta) - c * beta) * scale
            if diagonal:
                dz = jnp.where(causal, dz, 0.0)
            dz_b = dz.astype(BF16)
            dq = dq + jnp.dot(dz_b, kj, preferred_element_type=F32)
            dk_ref[pl.ds(start, t), :] += lax.dot_general(dz_b, q, TN_DIMS, preferred_element_type=F32)
            dv_ref[pl.ds(start, t), :] += lax.dot_general(a.astype(BF16), do_b, TN_DIMS, preferred_element_type=F32)
            return lsum + jnp.sum(l1, axis=-1, keepdims=True), psum + jnp.sum(p, axis=-1, keepdims=True), dq

        zero = jnp.zeros((t, 1), F32)
        lsum, psum, dq = lax.fori_loop(0, i, lambda j, c: tile(j, c[0], c[1], c[2], False),
                                       (zero, zero, jnp.zeros((t, HEAD), F32)))
        _, _, dq = tile(i, lsum, psum, dq, True)
        dq_ref[...] = dq.astype(dq_ref.dtype)

    full = lambda off: pl.BlockSpec((s, HEAD), lambda h, i: (0, off + h))
    tile_spec = lambda off: pl.BlockSpec((t, HEAD), lambda h, i: (i, off + h))
    w = n_heads * HEAD
    return pl.pallas_call(
        body, name=name, grid=(n_heads, s // t),
        in_specs=[tile_spec(qoff), full(koff), full(voff), tile_spec(0), tile_spec(0)],
        out_specs=[tile_spec(0), full(0), full(0)],
        out_shape=[jax.ShapeDtypeStruct((s, w), BF16), jax.ShapeDtypeStruct((s, w), F32),
                   jax.ShapeDtypeStruct((s, w), F32)],
        compiler_params=_params(("parallel", "arbitrary"), 48),
    )(qkv, qkv, qkv, do, tot)


def _coords():
    return lax.axis_index("x"), lax.axis_index("y"), lax.axis_index("c")


def _all_gather(shards):
    n = len(shards)

    def body(*refs):
        ins, outs = refs[:n], refs[n:2 * n]
        send_sems, recv_sems, local_sems = refs[2 * n:]
        x, y, c = _coords()
        me, sibling = (x, y, c), (x, y, 1 - c)
        chips = [(1 - x, y), (x, 1 - y), (1 - x, 1 - y)]

        def copy(w, k, block, to, src=None):
            dst = outs[w].at[4 * block[0] + 2 * block[1] + block[2]]
            return pltpu.make_async_remote_copy(
                src_ref=dst if src is None else src, dst_ref=dst,
                send_sem=send_sems.at[7 * w + k], recv_sem=recv_sems.at[7 * w + k],
                device_id=to, device_id_type=MESH)

        mine = [pltpu.make_async_copy(ins[w], outs[w].at[4 * x + 2 * y + c], local_sems.at[w]) for w in range(n)]
        for cp in mine:
            cp.start()
        sent = []
        for w in range(n):
            sent.append(copy(w, 0, me, sibling, src=ins[w]))
            sent += [copy(w, 1 + j, me, (*chip, c), src=ins[w]) for j, chip in enumerate(chips)]
        for cp in sent:
            cp.start()
        for w in range(n):
            for j, chip in enumerate(chips):
                copy(w, 1 + j, (*chip, c), me).wait_recv()
                passed = copy(w, 4 + j, (*chip, c), sibling)
                passed.start()
                sent.append(passed)
        for w in range(n):
            copy(w, 0, sibling, me).wait_recv()
            for j, chip in enumerate(chips):
                copy(w, 4 + j, (*chip, 1 - c), me).wait_recv()
        for cp in sent:
            cp.wait_send()
        for cp in mine:
            cp.wait()

    any_spec = pl.BlockSpec(memory_space=pl.ANY)
    return pl.pallas_call(
        body, name="all_gather_weights",
        in_specs=[any_spec] * n, out_specs=[any_spec] * n,
        out_shape=[jax.ShapeDtypeStruct((N_DEV,) + a.shape, a.dtype) for a in shards],
        scratch_shapes=[pltpu.SemaphoreType.DMA((7 * n,)), pltpu.SemaphoreType.DMA((7 * n,)),
                        pltpu.SemaphoreType.DMA((n,))],
    )(*shards)


def _to_sibling(bufs):
    n = len(bufs)

    def body(*refs):
        ins, outs = refs[:n], refs[n:2 * n]
        send_sems, recv_sems = refs[2 * n:]
        x, y, c = _coords()
        cps = [pltpu.make_async_remote_copy(src_ref=ins[w], dst_ref=outs[w], send_sem=send_sems.at[w],
                                            recv_sem=recv_sems.at[w], device_id=(x, y, 1 - c), device_id_type=MESH)
               for w in range(n)]
        for cp in cps:
            cp.start()
        for cp in cps:
            cp.wait()

    any_spec = pl.BlockSpec(memory_space=pl.ANY)
    return pl.pallas_call(
        body, name="grads_to_sibling",
        in_specs=[any_spec] * n, out_specs=[any_spec] * n,
        out_shape=[jax.ShapeDtypeStruct(a.shape, a.dtype) for a in bufs],
        scratch_shapes=[pltpu.SemaphoreType.DMA((n,)), pltpu.SemaphoreType.DMA((n,))],
    )(*bufs)


def _to_chips(pairs):
    n = len(pairs)

    def body(*refs):
        ins, outs = refs[:n], refs[n:2 * n]
        send_sems, recv_sems = refs[2 * n:]
        x, y, c = _coords()
        chips = [(1 - x, y), (x, 1 - y), (1 - x, 1 - y)]
        cps = []
        for w in range(n):
            for j, chip in enumerate(chips):
                cps.append(pltpu.make_async_remote_copy(
                    src_ref=ins[w].at[2 * chip[0] + chip[1]], dst_ref=outs[w].at[j],
                    send_sem=send_sems.at[3 * w + j], recv_sem=recv_sems.at[3 * w + j],
                    device_id=(*chip, c), device_id_type=MESH))
        for cp in cps:
            cp.start()
        for cp in cps:
            cp.wait()

    any_spec = pl.BlockSpec(memory_space=pl.ANY)
    return pl.pallas_call(
        body, name="grads_to_chips",
        in_specs=[any_spec] * n, out_specs=[any_spec] * n,
        out_shape=[jax.ShapeDtypeStruct((3,) + a.shape[1:], a.dtype) for a in pairs],
        scratch_shapes=[pltpu.SemaphoreType.DMA((3 * n,)), pltpu.SemaphoreType.DMA((3 * n,))],
    )(*pairs)


def _gather_rows(v):
    rows, width = v.shape

    def body(v_ref, out_ref, send_sems, recv_sems):
        x, y, c = _coords()
        out_ref[pl.ds(pl.multiple_of((4 * x + 2 * y + c) * rows, rows), rows), :] = v_ref[...]
        cps = []
        for mask in range(1, N_DEV):
            peer = (x ^ (mask >> 2), y ^ ((mask >> 1) & 1), c ^ (mask & 1))
            dst = out_ref.at[pl.ds(pl.multiple_of((4 * x + 2 * y + c) * rows, rows), rows), :]
            cps.append(pltpu.make_async_remote_copy(
                src_ref=v_ref, dst_ref=dst, send_sem=send_sems.at[mask - 1], recv_sem=recv_sems.at[mask - 1],
                device_id=peer, device_id_type=MESH))
        for cp in cps:
            cp.start()
        for cp in cps:
            cp.wait()

    vmem = pl.BlockSpec(memory_space=pltpu.VMEM)
    return pl.pallas_call(
        body, name="gather_small",
        in_specs=[vmem], out_specs=vmem,
        out_shape=jax.ShapeDtypeStruct((N_DEV * rows, width), F32),
        scratch_shapes=[pltpu.SemaphoreType.DMA((N_DEV - 1,)), pltpu.SemaphoreType.DMA((N_DEV - 1,))],
    )(v)


def _adamw(w, g, m, v):
    m = ADAM_B1 * m + (1.0 - ADAM_B1) * g
    v = ADAM_B2 * v + (1.0 - ADAM_B2) * jnp.square(g)
    m_hat = m / (1.0 - ADAM_B1 ** ADAM_STEP)
    v_hat = v / (1.0 - ADAM_B2 ** ADAM_STEP)
    delta = -ADAM_LR * (m_hat / (jnp.sqrt(v_hat) + ADAM_EPS) + ADAM_WD * w)
    return delta, m, v


def _tile_rows(rows, cols):
    tm = 1 << int(math.log2(max(2 * SUBLANES, (1 << 18) // cols)))
    while rows % tm:
        tm //= 2
    assert tm >= 2 * SUBLANES, (rows, cols)
    return tm


def _pair_sum(own, got, name):
    _, r, c = own.shape
    tm = _tile_rows(r, c)
    blk = lambda a: (a, (None, tm, c), lambda k, i: (k, i, 0))
    return _ew(lambda a, b: a + b, [blk(own), blk(got)], [(own.shape, BF16, (None, tm, c), lambda k, i: (k, i, 0))],
               (4, r // tm), name)[0]


def _reduce_adam(w, m, v, own, sib, far, name):
    r, c = w.shape
    tm = _tile_rows(r, c)

    def fn(wv, mv, vv, a, b, f0, f1, f2):
        g = (a + b) + f0.astype(F32) + f1.astype(F32) + f2.astype(F32)
        return (g,) + _adamw(wv, g, mv, vv)

    blk = lambda a: (a, (tm, c), lambda i: (i, 0))
    far_blk = lambda j: (far, (None, tm, c), lambda i, j=j: (j, i, 0))
    return _ew(fn, [blk(w), blk(m), blk(v), blk(own), blk(sib), far_blk(0), far_blk(1), far_blk(2)],
               [((r, c), F32, (tm, c), lambda i: (i, 0))] * 4, (r // tm,), name)


def _small_adam(gathered, params, moms, vels, widths):
    n = len(params)
    total = gathered.shape[1]

    def body(*refs):
        g_ref = refs[0]
        p_refs, m_refs, v_refs = refs[1:1 + n], refs[1 + n:1 + 2 * n], refs[1 + 2 * n:1 + 3 * n]
        sum_ref = refs[1 + 3 * n]
        outs = refs[2 + 3 * n:]
        g = g_ref[0:1, :]
        for p in range(1, N_DEV):
            g = g + g_ref[p * SUBLANES:p * SUBLANES + 1, :]
        sum_ref[...] = g
        off = 0
        for i, wd in enumerate(widths):
            d, m2, v2 = _adamw(p_refs[i][...], g[:, off:off + wd], m_refs[i][...], v_refs[i][...])
            outs[3 * i][...] = d
            outs[3 * i + 1][...] = m2
            outs[3 * i + 2][...] = v2
            off += wd

    vmem = pl.BlockSpec(memory_space=pltpu.VMEM)
    out_shape = [jax.ShapeDtypeStruct((1, total), F32)]
    for wd in widths:
        out_shape += [jax.ShapeDtypeStruct((1, wd), F32)] * 3
    return pl.pallas_call(
        body, name="small_adam",
        in_specs=[vmem] * (1 + 3 * n), out_specs=[vmem] * len(out_shape), out_shape=out_shape,
    )(gathered, *params, *moms, *vels)


def _cast_bf16(a, name):
    r, c = a.shape
    tm = _tile_rows(r, c)
    return _ew(lambda v: v, [(a, (tm, c), lambda i: (i, 0))], [((r, c), BF16, (tm, c), lambda i: (i, 0))],
               (r // tm,), name)[0]


def _fold_loss(parts, name):
    r, c = parts.shape

    def fn(v):
        return jnp.broadcast_to(jnp.sum(jnp.sum(v, axis=0, keepdims=True), axis=1, keepdims=True), (SUBLANES, HEAD))

    return _ew(fn, [_const_spec(parts)], [((SUBLANES, HEAD), F32, (SUBLANES, HEAD), lambda i: (0, 0))], (1,),
               name)[0][0:1]


def kernel(x, p, g_mix, w_in, qn_gain, kn_gain, w_branch_a, w_branch_b, w_out, g_mlp, w_up, w_down, g_ple, w_ple_gate, w_ple_proj, loss_target, m_g_mix, m_w_in, m_qn_gain, m_kn_gain, m_w_branch_a, m_w_branch_b, m_w_out, m_g_mlp, m_w_up, m_w_down, m_g_ple, m_w_ple_gate, m_w_ple_proj, v_g_mix, v_w_in, v_qn_gain, v_kn_gain, v_w_branch_a, v_w_branch_b, v_w_out, v_g_mlp, v_w_up, v_w_down, v_g_ple, v_w_ple_gate, v_w_ple_proj):
    x2 = x[0]
    tgt = loss_target[0]
    s, d = x2.shape
    wd_ = w_branch_a.shape[1]
    nh = wd_ // HEAD
    dff = w_up.shape[1]
    qkv_w = 6 * wd_
    tiles = lambda cols: cols // HEAD

    big = [w_in[0], w_branch_a[0], w_branch_b[0], w_out[0], w_up[0], w_down[0], w_ple_gate[0], w_ple_proj[0]]
    names = ["w_in", "w_branch_a", "w_branch_b", "w_out", "w_up", "w_down", "w_ple_gate", "w_ple_proj"]
    row_sharded = [False, False, False, True, False, True, True, False]
    gathered = _all_gather([_cast_bf16(a, "cast_" + nm) for a, nm in zip(big, names)])
    full = [g.reshape((1, N_DEV * g.shape[1], g.shape[2])) if rs else g for g, rs in zip(gathered, row_sharded)]
    win, wba, wbb, wout, wup, wdown, wgate, wproj = full

    tm = 1024 if s % 1024 == 0 else s
    tn_of = lambda n: 512 if n % 512 == 0 else (256 if n % 256 == 0 else n)
    tn_in = 256 if win.shape[2] % 256 == 0 else HEAD

    h = _rms_fwd(x2, g_mix, "norm_mix")
    qk_raw = _mm("nn", h, win, tm=tm, tn=tn_in, tk=d, out_dtypes=[F32], name="proj_qk",
                 n_off=0, n_cnt=2 * wd_ // tn_in)[0]
    rest = _mm("nn", h, win, tm=tm, tn=tn_in, tk=d, out_dtypes=[BF16], name="proj_rest",
               n_off=2 * wd_ // tn_in, n_cnt=(win.shape[0] * win.shape[2] - 2 * wd_) // tn_in)[0]
    o_va, o_qb, o_kb, o_vb, o_ga, o_gb = 0, tiles(wd_), tiles(2 * wd_), tiles(3 * wd_), tiles(4 * wd_), tiles(4 * wd_ + d)
    tabs = _rope_tables(s)
    qa = _headnorm_rope(qk_raw, 0, qn_gain, tabs, nh, "rope_q")
    ka = _headnorm_rope(qk_raw, nh, kn_gain, tabs, nh, "rope_k")
    va = rest[:, :wd_]

    outs, lses = [], []
    for dil in DILATIONS:
        o_g, l_g = _dilated_fwd(_phase_major(qa, dil), _phase_major(ka, dil), _phase_major(va, dil), 0, nh,
                                f"dilated_fwd_{dil}")
        outs.append(_token_major(o_g))
        lses.append(_token_major(l_g))
    ya, lse_all = _mix_fwd(outs, lses, "mix_fwd")
    yb, sb_tot = _sb_fwd(rest, o_qb, o_kb, o_vb, nh, "sb_fwd")

    tn_d = tn_of(wba.shape[2])
    za = _mm("nn", ya, wba, tm=tm, tn=tn_d, tk=wd_, out_dtypes=[BF16], name="branch_a")[0]

    def merge(acc, zav, gav, gbv):
        return _sigmoid(gav.astype(F32)) * zav.astype(F32) + _sigmoid(gbv.astype(F32)) * acc, acc

    merged, zb = _mm("nn", yb, wbb, tm=tm, tn=tn_d, tk=wd_, out_dtypes=[BF16, BF16], name="branch_b_merge",
                     epilogue=merge, extras=[(za, 0), (rest, o_ga * HEAD // tn_d), (rest, o_gb * HEAD // tn_d)])
    x1 = _mm("nn", merged, wout, tm=tm, tn=512, tk=d, out_dtypes=[F32], name="out_proj",
             epilogue=lambda acc, xv: (acc + xv,), extras=[(x2, 0)])[0]

    hm = _rms_fwd(x1, g_mlp, "norm_mlp")
    tn_u = tn_of(wup.shape[2])
    u, act = _mm("nn", hm, wup, tm=tm, tn=tn_u, tk=d, out_dtypes=[BF16, BF16], name="mlp_up",
                 epilogue=lambda acc: (acc, jnp.square(jnp.maximum(acc, 0.0))))
    x3 = _mm("nn", act, wdown, tm=tm, tn=512, tk=min(dff, 2048), out_dtypes=[F32], name="mlp_down",
             epilogue=lambda acc, xv: (acc + xv,), extras=[(x1, 0)])[0]

    hp = _rms_fwd(x3, g_ple, "norm_ple")
    p_b = _cast_bf16(p[0, 0], "cast_p")
    pp = _mm("nn", p_b, wproj, tm=tm, tn=tn_of(wproj.shape[2]), tk=p_b.shape[1], out_dtypes=[BF16],
             name="ple_proj")[0]

    def head(acc, ppv, xv, tv):
        sg = _sigmoid(acc)
        ppf = ppv.astype(F32)
        err = xv + ppf * sg - tv
        dy = err / d
        sq = jnp.square(err)
        return dy, dy * sg, dy * ppf * sg * (1.0 - sg), sq.reshape(-1, SUBLANES, sq.shape[-1]).sum(axis=0)

    n_i = s // tm
    dy, d_pp, d_gt, sq_parts = _mm(
        "nn", hp, wgate, tm=tm, tn=512, tk=d, out_dtypes=[F32, BF16, BF16], name="ple_gate_loss", epilogue=head,
        extras=[(pp, 0), (x3, 0), (tgt, 0)],
        extra_outs=[((n_i * SUBLANES, d), F32, (SUBLANES, 512), lambda i, j: (i, j))])
    loss_vec = _fold_loss(sq_parts, "loss_fold") * 0.5 / d

    g_wproj = _mm("tn", p_b, d_pp, tm=p_b.shape[1], tn=tn_of(wproj.shape[2]), tk=tm, out_dtypes=[F32],
                  name="grad_w_ple_proj", out_nb=N_DEV)[0]
    g_wgate = _mm("tn", hp, d_gt, tm=512, tn=1024, tk=tm, out_dtypes=[F32], name="grad_w_ple_gate")[0]
    d_hp = _mm("nt", d_gt, wgate, tm=tm, tn=512, tk=min(d, 1024), out_dtypes=[F32], name="d_hp")[0]
    dx3, dx3_b, g_gple = _rms_bwd(d_hp, x3, g_ple, dy, "norm_ple_bwd")

    d_u = _mm("nt", dx3_b, wdown, tm=tm, tn=1024, tk=min(d, 1024), out_dtypes=[BF16], name="d_u",
              epilogue=lambda acc, uv: (acc * (2.0 * jnp.maximum(uv.astype(F32), 0.0)),), extras=[(u, 0)])[0]
    g_wdown = _mm("tn", act, dx3_b, tm=1024, tn=1024, tk=tm, out_dtypes=[F32], name="grad_w_down")[0]
    g_wup = _mm("tn", hm, d_u, tm=1024, tn=wup.shape[2], tk=tm, out_dtypes=[F32], name="grad_w_up",
                out_nb=N_DEV)[0]
    d_hm = _mm("nt", d_u, wup, tm=tm, tn=1024, tk=wup.shape[2], out_dtypes=[F32], name="d_hm")[0]
    dx1, dx1_b, g_gmlp = _rms_bwd(d_hm, x1, g_mlp, dx3, "norm_mlp_bwd")

    def unmerge(acc, gav, gbv, zav, zbv):
        sa, sb = _sigmoid(gav.astype(F32)), _sigmoid(gbv.astype(F32))
        return acc * sa, acc * sb, acc * zav.astype(F32) * sa * (1.0 - sa), acc * zbv.astype(F32) * sb * (1.0 - sb)

    d_za, d_zb, d_ga, d_gb = _mm(
        "nt", dx1_b, wout, tm=tm, tn=512, tk=min(d, 1024), out_dtypes=[BF16] * 4, name="d_merged", epilogue=unmerge,
        extras=[(rest, o_ga * HEAD // 512), (rest, o_gb * HEAD // 512), (za, 0), (zb, 0)])
    g_wout = _mm("tn", merged, dx1_b, tm=512, tn=1024, tk=tm, out_dtypes=[F32], name="grad_w_out")[0]
    g_wba = _mm("tn", ya, d_za, tm=wd_, tn=wba.shape[2], tk=tm, out_dtypes=[F32], name="grad_w_branch_a",
                out_nb=N_DEV)[0]
    g_wbb = _mm("tn", yb, d_zb, tm=wd_, tn=wbb.shape[2], tk=tm, out_dtypes=[F32], name="grad_w_branch_b",
                out_nb=N_DEV)[0]
    d_ya = _mm("nt", d_za, wba, tm=tm, tn=wd_, tk=wba.shape[2], out_dtypes=[F32], name="d_ya")[0]
    d_yb = _mm("nt", d_zb, wbb, tm=tm, tn=wd_, tk=wbb.shape[2], out_dtypes=[F32], name="d_yb")[0]

    d_qb, d_kb, d_vb = _sb_bwd(rest, o_qb, o_kb, o_vb, d_yb, sb_tot, nh, "sb_bwd")
    delta = _head_dot(d_ya, ya, nh, "mix_delta")
    dqs, dks, dvs = [], [], []
    for dil in DILATIONS:
        dq_g, dk_g, dv_g = _dilated_bwd(
            _phase_major(qa, dil), _phase_major(ka, dil), _phase_major(va, dil), 0, _phase_major(d_ya, dil),
            _phase_major(lse_all, dil), _phase_major(delta, dil), nh, f"dilated_bwd_{dil}")
        dqs.append(_token_major(dq_g))
        dks.append(_token_major(dk_g))
        dvs.append(_token_major(dv_g))
    d_qa, g_qn = _headnorm_rope_bwd(dqs, qk_raw, 0, qn_gain, tabs, nh, "rope_q_bwd")
    d_ka, g_kn = _headnorm_rope_bwd(dks, qk_raw, nh, kn_gain, tabs, nh, "rope_k_bwd")
    tmr = 256
    d_va = _ew(lambda a, b, c: a + b + c, [_row_spec(a, tmr) for a in dvs],
               [((s, wd_), BF16, (tmr, wd_), lambda i: (i, 0))], (s // tmr,), "sum_dv")[0]
    d_proj = jnp.concatenate([d_qa, d_ka, d_va, d_qb, d_kb.astype(BF16), d_vb.astype(BF16), d_ga, d_gb], axis=1)

    g_win = _mm("tn", h, d_proj, tm=1024, tn=win.shape[2], tk=tm, out_dtypes=[F32], name="grad_w_in",
                out_nb=N_DEV)[0]
    d_h = _mm("nt", d_proj, win, tm=tm, tn=1024, tk=win.shape[2], out_dtypes=[F32], name="d_h")[0]
    dx, _, g_gmix = _rms_bwd(d_h, x2, g_mix, dx1, "norm_mix_bwd")

    grads = [g_win, g_wba, g_wbb, g_wout, g_wup, g_wdown, g_wgate, g_wproj]
    grads = [g.reshape((N_DEV, g.shape[1] // N_DEV, g.shape[2])) if rs else g for g, rs in zip(grads, row_sharded)]
    cx, cy, cc = _coords()
    by_core = [g.reshape((4, 2) + g.shape[1:]) for g in grads]
    keep = [lax.dynamic_index_in_dim(g, cc, axis=1, keepdims=False) for g in by_core]
    give = [lax.dynamic_index_in_dim(g, 1 - cc, axis=1, keepdims=False) for g in by_core]
    got = _to_sibling(give)
    pairs = [_pair_sum(a, b, "pair_" + nm) for a, b, nm in zip(keep, got, names)]
    far = _to_chips(pairs)
    chip = 2 * cx + cy
    moms = [m_w_in, m_w_branch_a, m_w_branch_b, m_w_out, m_w_up, m_w_down, m_w_ple_gate, m_w_ple_proj]
    vels = [v_w_in, v_w_branch_a, v_w_branch_b, v_w_out, v_w_up, v_w_down, v_w_ple_gate, v_w_ple_proj]
    big_out = {}
    for i, nm in enumerate(names):
        own = lax.dynamic_index_in_dim(keep[i], chip, axis=0, keepdims=False)
        sib = lax.dynamic_index_in_dim(got[i], chip, axis=0, keepdims=False)
        big_out[nm] = [a[None] for a in _reduce_adam(big[i], moms[i][0], vels[i][0], own, sib, far[i], "adam_" + nm)]

    small_names = ["g_mix", "qn_gain", "kn_gain", "g_mlp", "g_ple"]
    small_p = [g_mix, qn_gain, kn_gain, g_mlp, g_ple]
    small_m = [m_g_mix, m_qn_gain, m_kn_gain, m_g_mlp, m_g_ple]
    small_v = [v_g_mix, v_qn_gain, v_kn_gain, v_g_mlp, v_g_ple]
    small_g = [g_gmix, g_qn, g_kn, g_gmlp, g_gple]
    widths = [a.shape[1] for a in small_p]
    vec = jnp.concatenate(small_g + [loss_vec], axis=1)
    vec = jnp.pad(vec, ((0, SUBLANES - 1), (0, 0)))
    res = _small_adam(_gather_rows(vec), small_p, small_m, small_v, widths)
    summed = res[0]
    small_out, off = {}, 0
    for i, nm in enumerate(small_names):
        small_out[nm] = [summed[:, off:off + widths[i]]] + list(res[1 + 3 * i:4 + 3 * i])
        off += widths[i]
    loss = summed[0, off]

    order = ["g_mix", "w_in", "qn_gain", "kn_gain", "w_branch_a", "w_branch_b", "w_out", "g_mlp", "w_up", "w_down",
             "g_ple", "w_ple_gate", "w_ple_proj"]
    table = {**big_out, **small_out}
    result = [loss, dx[None]]
    for kind in range(4):
        result += [table[nm][kind] for nm in order]
    return tuple(result)
```

```python
import functools
import math

import jax
import jax.numpy as jnp
from jax import lax
from jax.experimental import pallas as pl
from jax.experimental.pallas import tpu as pltpu

F32 = jnp.float32
BF16 = jnp.bfloat16
MESH = pl.DeviceIdType.MESH

HEAD = 128
WINDOW = 128
DILATIONS = (1, 4, 16)
ROT = HEAD // 4
ROPE_THETA = 500000.0
EPS = 1e-6
NEG = -1e30
N_DEV = 8

ADAM_LR = 0.001
ADAM_B1 = 0.9
ADAM_B2 = 0.999
ADAM_EPS = 1e-08
ADAM_WD = 0.01
ADAM_STEP = 10

SUBLANES = 8
VMEM_CAP_MB = 56

NT_DIMS = (((1,), (1,)), ((), ()))
TN_DIMS = (((0,), (0,)), ((), ()))


def _params(semantics, vmem_mb):
    return pltpu.CompilerParams(dimension_semantics=semantics, vmem_limit_bytes=min(vmem_mb, VMEM_CAP_MB) << 20)


def _sigmoid(x):
    return 1.0 / (1.0 + jnp.exp(-x))


def _split_bf16(x):
    hi = x.astype(BF16)
    lo = (x - hi.astype(F32)).astype(BF16)
    return hi, lo


def _ew(fn, ins, outs, grid, name, colsums=(), vmem_mb=40):
    n_in, n_out, n_cs = len(ins), len(outs), len(colsums)
    steps = math.prod(grid)

    def body(*refs):
        in_refs = refs[:n_in]
        out_refs = refs[n_in:n_in + n_out]
        cs_refs = refs[n_in + n_out:n_in + n_out + n_cs]
        acc_refs = refs[n_in + n_out + n_cs:]
        vals = fn(*[r[...] for r in in_refs])
        if not isinstance(vals, tuple):
            vals = (vals,)
        for r, v in zip(out_refs, vals[:n_out]):
            r[...] = v.astype(r.dtype)
        if n_cs:
            step = pl.program_id(0)
            for ax in range(1, len(grid)):
                step = step * grid[ax] + pl.program_id(ax)
            for acc, cs, v in zip(acc_refs, cs_refs, vals[n_out:]):
                part = v.reshape(-1, SUBLANES, v.shape[-1]).sum(axis=0)

                @pl.when(step == 0)
                def _(acc=acc, part=part):
                    acc[...] = part

                @pl.when(step > 0)
                def _(acc=acc, part=part):
                    acc[...] += part

                @pl.when(step == steps - 1)
                def _(acc=acc, cs=cs):
                    cs[...] = acc[...].sum(axis=0, keepdims=True)

    out_shape = [jax.ShapeDtypeStruct(s, d) for s, d, _, _ in outs]
    out_specs = [pl.BlockSpec(b, m) for _, _, b, m in outs]
    for w in colsums:
        out_shape.append(jax.ShapeDtypeStruct((1, w), F32))
        out_specs.append(pl.BlockSpec((1, w), lambda *_: (0, 0)))
    sem = ("arbitrary",) * len(grid) if n_cs else ("parallel",) * len(grid)
    res = pl.pallas_call(
        body, name=name, grid=grid,
        in_specs=[pl.BlockSpec(b, m) for _, b, m in ins],
        out_specs=out_specs, out_shape=out_shape,
        scratch_shapes=[pltpu.VMEM((SUBLANES, w), F32) for w in colsums],
        compiler_params=_params(sem, vmem_mb),
    )(*[a for a, _, _ in ins])
    return res


def _row_spec(a, tm):
    return (a, (tm, a.shape[1]), lambda i: (i, 0))


def _const_spec(a):
    return (a, a.shape, lambda *_: (0,) * a.ndim)


def _mm(mode, a, b, *, tm, tn, tk, out_dtypes, name, epilogue=None, extras=(), n_off=0, n_cnt=None,
        out_nb=1, extra_outs=(), vmem_mb=48):
    if mode == "nn":
        m, kdim = a.shape
        nb, _, n = b.shape
        npb = n // tn
        ncols = nb * n
        n_tiles = (ncols // tn) if n_cnt is None else n_cnt
        a_spec = pl.BlockSpec((tm, tk), lambda i, j, k: (i, k))
        b_spec = pl.BlockSpec((None, tk, tn), lambda i, j, k: ((j + n_off) // npb, k, (j + n_off) % npb))
        dims = (((1,), (0,)), ((), ()))
    elif mode == "nt":
        m, kdim = a.shape
        nb, nout, n = b.shape
        kpb = n // tk
        n_tiles = nout // tn
        a_spec = pl.BlockSpec((tm, tk), lambda i, j, k: (i, k))
        b_spec = pl.BlockSpec((None, tn, tk), lambda i, j, k: (k // kpb, j, k % kpb))
        dims = NT_DIMS
    else:
        kdim, m = a.shape
        ncols = b.shape[1]
        n_tiles = ncols // tn
        a_spec = pl.BlockSpec((tk, tm), lambda i, j, k: (k, i))
        b_spec = pl.BlockSpec((tk, tn), lambda i, j, k: (k, j))
        dims = TN_DIMS
    nk = kdim // tk
    assert kdim % tk == 0 and m % tm == 0
    grid = (m // tm, n_tiles, nk)
    n_ex, n_out = len(extras), len(out_dtypes) + len(extra_outs)

    def body(*refs):
        a_ref, b_ref = refs[0], refs[1]
        ex_refs = refs[2:2 + n_ex]
        out_refs = refs[2 + n_ex:2 + n_ex + n_out]

        def finish(acc):
            vals = (acc,) if epilogue is None else epilogue(acc, *[r[...] for r in ex_refs])
            for r, v in zip(out_refs, vals):
                r[...] = v.astype(r.dtype)

        part = lax.dot_general(a_ref[...], b_ref[...], dims, preferred_element_type=F32)
        if nk == 1:
            finish(part)
        else:
            acc_ref = refs[-1]
            k = pl.program_id(2)

            @pl.when(k == 0)
            def _():
                acc_ref[...] = part

            @pl.when(k > 0)
            def _():
                acc_ref[...] += part

            @pl.when(k == nk - 1)
            def _():
                finish(acc_ref[...])

    if mode == "tn":
        npo = (ncols // out_nb) // tn
        out_shape = [jax.ShapeDtypeStruct((out_nb, m, ncols // out_nb), d) for d in out_dtypes]
        out_specs = [pl.BlockSpec((None, tm, tn), lambda i, j, k: (j // npo, i, j % npo)) for _ in out_dtypes]
    else:
        out_shape = [jax.ShapeDtypeStruct((m, n_tiles * tn), d) for d in out_dtypes]
        out_specs = [pl.BlockSpec((tm, tn), lambda i, j, k: (i, j)) for _ in out_dtypes]
    for s, d, blk, imap in extra_outs:
        out_shape.append(jax.ShapeDtypeStruct(s, d))
        out_specs.append(pl.BlockSpec(blk, lambda i, j, k, imap=imap: imap(i, j)))
    ex_specs = [pl.BlockSpec((tm, tn), lambda i, j, k, off=off: (i, j + off)) for _, off in extras]
    return pl.pallas_call(
        body, name=name, grid=grid,
        in_specs=[a_spec, b_spec] + ex_specs,
        out_specs=out_specs, out_shape=out_shape,
        scratch_shapes=[pltpu.VMEM((tm, tn), F32)] if nk > 1 else [],
        compiler_params=_params(("parallel", "parallel", "arbitrary"), vmem_mb),
    )(a, b, *[e for e, _ in extras])


def _rms_fwd(x, g, name):
    tm = 256

    def fn(xv, gv):
        r = lax.rsqrt(jnp.mean(xv * xv, axis=-1, keepdims=True) + EPS)
        return xv * r * gv

    return _ew(fn, [_row_spec(x, tm), _const_spec(g)], [(x.shape, BF16, (tm, x.shape[1]), lambda i: (i, 0))],
               (x.shape[0] // tm,), name)[0]


def _rms_bwd(dh, x, g, res, name):
    tm = 256
    d = x.shape[1]

    def fn(dhv, xv, gv, rv):
        r = lax.rsqrt(jnp.mean(xv * xv, axis=-1, keepdims=True) + EPS)
        xh = xv * r
        dyg = dhv * gv
        dx = rv + r * (dyg - xh * jnp.mean(dyg * xh, axis=-1, keepdims=True))
        return dx, dx, dhv * xh

    spec = lambda dt: (x.shape, dt, (tm, d), lambda i: (i, 0))
    return _ew(fn, [_row_spec(dh, tm), _row_spec(x, tm), _const_spec(g), _row_spec(res, tm)],
               [spec(F32), spec(BF16)], (x.shape[0] // tm,), name, colsums=(d,))


def _rope_tables(s):
    half = ROT // 2
    pos = jnp.arange(s, dtype=F32)
    inv = ROPE_THETA ** (-jnp.arange(0, ROT, 2, dtype=F32) / ROT)
    ang = pos[:, None] * inv[None, :]
    cos, sin = jnp.cos(ang), jnp.sin(ang)
    pad = jnp.zeros((s, HEAD - ROT), F32)
    c = jnp.concatenate([cos, cos, pad + 1.0], axis=1)
    a = jnp.concatenate([-sin, jnp.zeros_like(sin), pad], axis=1)
    b = jnp.concatenate([jnp.zeros_like(sin), sin, pad], axis=1)
    return c, a, b


def _headnorm_rope(proj, off, gain, tabs, n_heads, name):
    tm = 512
    s = proj.shape[0]

    def fn(xv, gv, c, a, b):
        r = lax.rsqrt(jnp.mean(xv * xv, axis=-1, keepdims=True) + EPS)
        y = xv * r * gv
        return c * y + a * pltpu.roll(y, HEAD - ROT // 2, 1) + b * pltpu.roll(y, ROT // 2, 1)

    tab = lambda t: (t, (tm, HEAD), lambda i, h: (i, 0))
    return _ew(fn, [(proj, (tm, HEAD), lambda i, h: (i, off + h)), (gain, (1, HEAD), lambda i, h: (0, 0))]
               + [tab(t) for t in tabs],
               [((s, n_heads * HEAD), BF16, (tm, HEAD), lambda i, h: (i, h))], (s // tm, n_heads), name)[0]


def _headnorm_rope_bwd(dys, proj, off, gain, tabs, n_heads, name):
    tm = 512
    s = proj.shape[0]
    n_dy = len(dys)

    def fn(*vals):
        dy = vals[0]
        for v in vals[1:n_dy]:
            dy = dy + v
        xv, gv, c, a, b = vals[n_dy:]
        dn = c * dy + pltpu.roll(a * dy, ROT // 2, 1) + pltpu.roll(b * dy, HEAD - ROT // 2, 1)
        r = lax.rsqrt(jnp.mean(xv * xv, axis=-1, keepdims=True) + EPS)
        xh = xv * r
        dyg = dn * gv
        dx = r * (dyg - xh * jnp.mean(dyg * xh, axis=-1, keepdims=True))
        return dx, dn * xh

    tab = lambda t: (t, (tm, HEAD), lambda i, h: (i, 0))
    return _ew(fn, [(d, (tm, HEAD), lambda i, h: (i, h)) for d in dys]
               + [(proj, (tm, HEAD), lambda i, h: (i, off + h)), (gain, (1, HEAD), lambda i, h: (0, 0))]
               + [tab(t) for t in tabs],
               [((s, n_heads * HEAD), BF16, (tm, HEAD), lambda i, h: (i, h))], (s // tm, n_heads), name,
               colsums=(HEAD,))


def _phase_major(a, d):
    s, w = a.shape
    if d == 1:
        return a.reshape(1, s, w)
    return a.reshape(s // d, d, w).transpose(1, 0, 2)


def _token_major(a):
    d, m, w = a.shape
    if d == 1:
        return a.reshape(m, w)
    return a.transpose(1, 0, 2).reshape(m * d, w)


def _dil_tq(m):
    return min(512, m)


def _dilated_fwd(q, k, v, voff, n_heads, name):
    d, m, _ = q.shape
    tq = _dil_tq(m)
    nq = m // tq
    per = tq // WINDOW
    scale = HEAD ** -0.5

    def body(q_ref, kc_ref, kp_ref, vc_ref, vp_ref, o_ref, l_ref):
        n = pl.program_id(2)
        kk = jnp.concatenate([kp_ref[...], kc_ref[...]], axis=0)
        vv = jnp.concatenate([vp_ref[...], vc_ref[...]], axis=0)
        s = lax.dot_general(q_ref[...], kk, NT_DIMS, preferred_element_type=F32) * scale
        row = lax.broadcasted_iota(jnp.int32, s.shape, 0)
        col = lax.broadcasted_iota(jnp.int32, s.shape, 1)
        dist = row + WINDOW - col
        valid = (dist >= 0) & (dist <= WINDOW) & ((n > 0) | (col >= WINDOW))
        s = jnp.where(valid, s, NEG)
        mx = jnp.max(s, axis=-1, keepdims=True)
        e = jnp.exp(s - mx)
        den = jnp.sum(e, axis=-1, keepdims=True)
        o = jnp.dot(e.astype(BF16), vv, preferred_element_type=F32) / den
        o_ref[...] = o
        l_ref[...] = jnp.broadcast_to(mx + jnp.log(den), (tq, HEAD))

    cur = lambda off: pl.BlockSpec((None, tq, HEAD), lambda r, h, n: (r, n, off + h))
    prev = lambda off: pl.BlockSpec((None, WINDOW, HEAD), lambda r, h, n: (r, jnp.maximum(n * per - 1, 0), off + h))
    out = jax.ShapeDtypeStruct((d, m, n_heads * HEAD), F32)
    return pl.pallas_call(
        body, name=name, grid=(d, n_heads, nq),
        in_specs=[cur(0), cur(0), prev(0), cur(voff), prev(voff)],
        out_specs=[cur(0), cur(0)], out_shape=[out, out],
        compiler_params=_params(("parallel",) * 3, 32),
    )(q, k, k, v, v)


def _dilated_bwd(q, k, v, voff, dy, lse, delta, n_heads, name):
    d, m, _ = q.shape
    tq = _dil_tq(m)
    nq = m // tq
    per = tq // WINDOW
    last_blk = m // WINDOW - 1
    scale = HEAD ** -0.5

    def body(qc_ref, qn_ref, kc_ref, kp_ref, vc_ref, vp_ref, dyc_ref, dyn_ref, lc_ref, ln_ref, dc_ref, dn_ref,
             dq_ref, dk_ref, dv_ref):
        n = pl.program_id(2)
        qc, kc, vc = qc_ref[...], kc_ref[...], vc_ref[...]
        dyc = dyc_ref[...].astype(BF16)
        kk = jnp.concatenate([kp_ref[...], kc], axis=0)
        vv = jnp.concatenate([vp_ref[...], vc], axis=0)
        s = lax.dot_general(qc, kk, NT_DIMS, preferred_element_type=F32) * scale
        row = lax.broadcasted_iota(jnp.int32, s.shape, 0)
        col = lax.broadcasted_iota(jnp.int32, s.shape, 1)
        dist = row + WINDOW - col
        valid = (dist >= 0) & (dist <= WINDOW) & ((n > 0) | (col >= WINDOW))
        p = jnp.where(valid, jnp.exp(jnp.minimum(s - lc_ref[:, :1], 0.0)), 0.0)
        dp = lax.dot_general(dyc, vv, NT_DIMS, preferred_element_type=F32)
        ds = p * (dp - dc_ref[:, :1]) * scale
        dq_ref[...] = jnp.dot(ds.astype(BF16), kk, preferred_element_type=F32)
        qq = jnp.concatenate([qc, qn_ref[...]], axis=0)
        dyy = jnp.concatenate([dyc, dyn_ref[...].astype(BF16)], axis=0)
        ll = jnp.concatenate([lc_ref[...], ln_ref[...]], axis=0)[:, :1]
        dd = jnp.concatenate([dc_ref[...], dn_ref[...]], axis=0)[:, :1]
        s2 = lax.dot_general(qq, kc, NT_DIMS, preferred_element_type=F32) * scale
        row2 = lax.broadcasted_iota(jnp.int32, s2.shape, 0)
        col2 = lax.broadcasted_iota(jnp.int32, s2.shape, 1)
        dist2 = row2 - col2
        valid2 = (dist2 >= 0) & (dist2 <= WINDOW) & ((n < nq - 1) | (row2 < tq))
        p2 = jnp.where(valid2, jnp.exp(jnp.minimum(s2 - ll, 0.0)), 0.0)
        dp2 = lax.dot_general(dyy, vc, NT_DIMS, preferred_element_type=F32)
        ds2 = p2 * (dp2 - dd) * scale
        dk_ref[...] = lax.dot_general(ds2.astype(BF16), qq, TN_DIMS, preferred_element_type=F32)
        dv_ref[...] = lax.dot_general(p2.astype(BF16), dyy, TN_DIMS, preferred_element_type=F32)

    cur = lambda off: pl.BlockSpec((None, tq, HEAD), lambda r, h, n: (r, n, off + h))
    prev = lambda off: pl.BlockSpec((None, WINDOW, HEAD), lambda r, h, n: (r, jnp.maximum(n * per - 1, 0), off + h))
    nxt = lambda off: pl.BlockSpec((None, WINDOW, HEAD),
                                   lambda r, h, n: (r, jnp.minimum((n + 1) * per, last_blk), off + h))
    out = jax.ShapeDtypeStruct((d, m, n_heads * HEAD), F32)
    return pl.pallas_call(
        body, name=name, grid=(d, n_heads, nq),
        in_specs=[cur(0), nxt(0), cur(0), prev(0), cur(voff), prev(voff), cur(0), nxt(0), cur(0), nxt(0), cur(0),
                  nxt(0)],
        out_specs=[cur(0)] * 3, out_shape=[out] * 3,
        compiler_params=_params(("parallel",) * 3, 40),
    )(q, q, k, k, v, v, dy, dy, lse, lse, delta, delta)


def _mix_fwd(os_, ls_, name):
    tm = 256
    s, w = os_[0].shape
    n = len(os_)

    def fn(*vals):
        o, l = vals[:n], vals[n:]
        mx = functools.reduce(jnp.maximum, l)
        e = [jnp.exp(x - mx) for x in l]
        den = functools.reduce(jnp.add, e)
        y = functools.reduce(jnp.add, [ei * oi for ei, oi in zip(e, o)]) / den
        return y, mx + jnp.log(den)

    return _ew(fn, [_row_spec(a, tm) for a in list(os_) + list(ls_)],
               [((s, w), BF16, (tm, w), lambda i: (i, 0)), ((s, w), F32, (tm, w), lambda i: (i, 0))],
               (s // tm,), name)


def _head_dot(dy, y, n_heads, name):
    tm = 512
    s = dy.shape[0]

    def fn(a, b):
        return jnp.broadcast_to(jnp.sum(a * b.astype(F32), axis=-1, keepdims=True), a.shape)

    blk = lambda a: (a, (tm, HEAD), lambda i, h: (i, h))
    return _ew(fn, [blk(dy), blk(y)], [((s, n_heads * HEAD), F32, (tm, HEAD), lambda i, h: (i, h))],
               (s // tm, n_heads), name)[0]


SB_TQ = 1024
SB_TB = 512
SB_TK = 256


def _softplus(z):
    return jnp.where(z > 20.0, z, jnp.log(1.0 + jnp.exp(z)))


def _tri(t, cmp):
    rows = lax.broadcasted_iota(jnp.int32, (2 * t, t), 0)
    cols = lax.broadcasted_iota(jnp.int32, (2 * t, t), 1)
    return jnp.where(cmp(jnp.where(rows >= t, rows - t, rows), cols), 1.0, 0.0).astype(BF16)


def _tri_sum(x, tri):
    return jnp.dot(jnp.concatenate(_split_bf16(x), axis=1), tri, preferred_element_type=F32)


def _sb_fwd(qkv, qoff, koff, voff, n_heads, name):
    s = qkv.shape[0]
    tq, tb, tk = min(SB_TQ, s), SB_TB, SB_TK
    scale = HEAD ** -0.5

    def body(q_ref, k_ref, v_ref, o_ref, tot_ref):
        i = pl.program_id(1)
        q = q_ref[...]
        after = _tri(tk, lambda a, b: a > b)
        row = lax.broadcasted_iota(jnp.int32, (tq, tk), 0)
        col = lax.broadcasted_iota(jnp.int32, (tq, tk), 1)

        def sub(start, carry, o, causal):
            kj = k_ref[pl.ds(start, tk), :]
            vj = v_ref[pl.ds(start, tk), :]
            z = lax.dot_general(q, kj, NT_DIMS, preferred_element_type=F32) * scale
            sp = _softplus(z)
            logsig = z - sp
            if causal is not None:
                sp = jnp.where(causal, sp, 0.0)
            a = jnp.exp(logsig - carry - _tri_sum(sp, after))
            if causal is not None:
                a = jnp.where(causal, a, 0.0)
            o = o + jnp.dot(a.astype(BF16), vj, preferred_element_type=F32)
            return carry + jnp.sum(sp, axis=-1, keepdims=True), o

        def block(base, carry, o, diag_off):
            for half in reversed(range(tb // tk)):
                causal = None if diag_off is None else (col + (diag_off + half * tk) < row)
                carry, o = sub(pl.multiple_of(base + half * tk, tk), carry, o, causal)
            return carry, o

        carry, o = jnp.zeros((tq, 1), F32), jnp.zeros((tq, HEAD), F32)
        for b in reversed(range(tq // tb)):
            carry, o = block(i * tq + b * tb, carry, o, b * tb)
        below = i * (tq // tb)
        carry, o = lax.fori_loop(0, below, lambda jj, co: block((below - 1 - jj) * tb, co[0], co[1], None),
                                 (carry, o))
        o_ref[...] = o.astype(o_ref.dtype)
        tot_ref[...] = jnp.broadcast_to(carry, (tq, HEAD))

    t = tq
    full = lambda off: pl.BlockSpec((s, HEAD), lambda h, i: (0, off + h))
    tile_spec = lambda off: pl.BlockSpec((t, HEAD), lambda h, i: (i, off + h))
    return pl.pallas_call(
        body, name=name, grid=(n_heads, s // t),
        in_specs=[tile_spec(qoff), full(koff), full(voff)],
        out_specs=[tile_spec(0), tile_spec(0)],
        out_shape=[jax.ShapeDtypeStruct((s, n_heads * HEAD), BF16), jax.ShapeDtypeStruct((s, n_heads * HEAD), F32)],
        compiler_params=_params(("parallel", "arbitrary"), 40),
    )(qkv, qkv, qkv)


def _sb_bwd(qkv, qoff, koff, voff, do, tot, n_heads, name):
    s = qkv.shape[0]
    tq, tb, tk = min(SB_TQ, s), SB_TB, SB_TK
    scale = HEAD ** -0.5

    def body(q_ref, k_ref, v_ref, do_ref, tot_ref, dq_ref, dk_ref, dv_ref):
        i = pl.program_id(1)

        @pl.when(i == 0)
        def _():
            dk_ref[...] = jnp.zeros_like(dk_ref)
            dv_ref[...] = jnp.zeros_like(dv_ref)

        q = q_ref[...]
        do_b = do_ref[...].astype(BF16)
        total = tot_ref[:, :1]
        upto = _tri(tk, lambda a, b: a <= b)
        before = _tri(tk, lambda a, b: a < b)
        row = lax.broadcasted_iota(jnp.int32, (tq, tk), 0)
        col = lax.broadcasted_iota(jnp.int32, (tq, tk), 1)

        def sub(start, lsum, psum, dq, causal):
            kj = k_ref[pl.ds(start, tk), :]
            vj = v_ref[pl.ds(start, tk), :]
            z = lax.dot_general(q, kj, NT_DIMS, preferred_element_type=F32) * scale
            sp = _softplus(z)
            logsig = z - sp
            if causal is not None:
                sp = jnp.where(causal, sp, 0.0)
            a = jnp.exp(logsig - (total - lsum - _tri_sum(sp, upto)))
            if causal is not None:
                a = jnp.where(causal, a, 0.0)
            da = lax.dot_general(do_b, vj, NT_DIMS, preferred_element_type=F32)
            p = a * da
            c = psum + _tri_sum(p, before)
            beta = jnp.exp(logsig)
            dz = (p - (p + c) * beta) * scale
            if causal is not None:
                dz = jnp.where(causal, dz, 0.0)
            dz_b = dz.astype(BF16)
            dq = dq + jnp.dot(dz_b, kj, preferred_element_type=F32)
            dk_ref[pl.ds(start, tk), :] += lax.dot_general(dz_b, q, TN_DIMS, preferred_element_type=F32)
            dv_ref[pl.ds(start, tk), :] += lax.dot_general(a.astype(BF16), do_b, TN_DIMS, preferred_element_type=F32)
            return lsum + jnp.sum(sp, axis=-1, keepdims=True), psum + jnp.sum(p, axis=-1, keepdims=True), dq

        def block(base, lsum, psum, dq, diag_off):
            for half in range(tb // tk):
                causal = None if diag_off is None else (col + (diag_off + half * tk) < row)
                lsum, psum, dq = sub(pl.multiple_of(base + half * tk, tk), lsum, psum, dq, causal)
            return lsum, psum, dq

        zero = jnp.zeros((tq, 1), F32)
        state = lax.fori_loop(0, i * (tq // tb), lambda j, c: block(j * tb, c[0], c[1], c[2], None),
                              (zero, zero, jnp.zeros((tq, HEAD), F32)))
        for b in range(tq // tb):
            state = block(i * tq + b * tb, *state, b * tb)
        dq_ref[...] = state[2].astype(dq_ref.dtype)

    t = tq
    full = lambda off: pl.BlockSpec((s, HEAD), lambda h, i: (0, off + h))
    tile_spec = lambda off: pl.BlockSpec((t, HEAD), lambda h, i: (i, off + h))
    w = n_heads * HEAD
    return pl.pallas_call(
        body, name=name, grid=(n_heads, s // t),
        in_specs=[tile_spec(qoff), full(koff), full(voff), tile_spec(0), tile_spec(0)],
        out_specs=[tile_spec(0), full(0), full(0)],
        out_shape=[jax.ShapeDtypeStruct((s, w), BF16), jax.ShapeDtypeStruct((s, w), F32),
                   jax.ShapeDtypeStruct((s, w), F32)],
        compiler_params=_params(("parallel", "arbitrary"), 48),
    )(qkv, qkv, qkv, do, tot)


def _coords():
    return lax.axis_index("x"), lax.axis_index("y"), lax.axis_index("c")


def _all_gather(shards):
    n = len(shards)

    def body(*refs):
        ins, outs = refs[:n], refs[n:2 * n]
        send_sems, recv_sems, local_sems = refs[2 * n:]
        x, y, c = _coords()
        me, sibling = (x, y, c), (x, y, 1 - c)
        chips = [(1 - x, y), (x, 1 - y), (1 - x, 1 - y)]

        def copy(w, k, block, to, src=None):
            dst = outs[w].at[4 * block[0] + 2 * block[1] + block[2]]
            return pltpu.make_async_remote_copy(
                src_ref=dst if src is None else src, dst_ref=dst,
                send_sem=send_sems.at[7 * w + k], recv_sem=recv_sems.at[7 * w + k],
                device_id=to, device_id_type=MESH)

        mine = [pltpu.make_async_copy(ins[w], outs[w].at[4 * x + 2 * y + c], local_sems.at[w]) for w in range(n)]
        for cp in mine:
            cp.start()
        sent = []
        for w in range(n):
            sent.append(copy(w, 0, me, sibling, src=ins[w]))
            sent += [copy(w, 1 + j, me, (*chip, c), src=ins[w]) for j, chip in enumerate(chips)]
        for cp in sent:
            cp.start()
        for w in range(n):
            for j, chip in enumerate(chips):
                copy(w, 1 + j, (*chip, c), me).wait_recv()
                passed = copy(w, 4 + j, (*chip, c), sibling)
                passed.start()
                sent.append(passed)
        for w in range(n):
            copy(w, 0, sibling, me).wait_recv()
            for j, chip in enumerate(chips):
                copy(w, 4 + j, (*chip, 1 - c), me).wait_recv()
        for cp in sent:
            cp.wait_send()
        for cp in mine:
            cp.wait()

    any_spec = pl.BlockSpec(memory_space=pl.ANY)
    return pl.pallas_call(
        body, name="all_gather_weights",
        in_specs=[any_spec] * n, out_specs=[any_spec] * n,
        out_shape=[jax.ShapeDtypeStruct((N_DEV,) + a.shape, a.dtype) for a in shards],
        scratch_shapes=[pltpu.SemaphoreType.DMA((7 * n,)), pltpu.SemaphoreType.DMA((7 * n,)),
                        pltpu.SemaphoreType.DMA((n,))],
    )(*shards)


def _to_sibling(bufs):
    n = len(bufs)

    def body(*refs):
        ins, outs = refs[:n], refs[n:2 * n]
        send_sems, recv_sems = refs[2 * n:]
        x, y, c = _coords()
        cps = [pltpu.make_async_remote_copy(src_ref=ins[w], dst_ref=outs[w], send_sem=send_sems.at[w],
                                            recv_sem=recv_sems.at[w], device_id=(x, y, 1 - c), device_id_type=MESH)
               for w in range(n)]
        for cp in cps:
            cp.start()
        for cp in cps:
            cp.wait()

    any_spec = pl.BlockSpec(memory_space=pl.ANY)
    return pl.pallas_call(
        body, name="grads_to_sibling",
        in_specs=[any_spec] * n, out_specs=[any_spec] * n,
        out_shape=[jax.ShapeDtypeStruct(a.shape, a.dtype) for a in bufs],
        scratch_shapes=[pltpu.SemaphoreType.DMA((n,)), pltpu.SemaphoreType.DMA((n,))],
    )(*bufs)


def _to_chips(pairs):
    n = len(pairs)

    def body(*refs):
        ins, outs = refs[:n], refs[n:2 * n]
        send_sems, recv_sems = refs[2 * n:]
        x, y, c = _coords()
        chips = [(1 - x, y), (x, 1 - y), (1 - x, 1 - y)]
        cps = []
        for w in range(n):
            for j, chip in enumerate(chips):
                cps.append(pltpu.make_async_remote_copy(
                    src_ref=ins[w].at[2 * chip[0] + chip[1]], dst_ref=outs[w].at[j],
                    send_sem=send_sems.at[3 * w + j], recv_sem=recv_sems.at[3 * w + j],
                    device_id=(*chip, c), device_id_type=MESH))
        for cp in cps:
            cp.start()
        for cp in cps:
            cp.wait()

    any_spec = pl.BlockSpec(memory_space=pl.ANY)
    return pl.pallas_call(
        body, name="grads_to_chips",
        in_specs=[any_spec] * n, out_specs=[any_spec] * n,
        out_shape=[jax.ShapeDtypeStruct((3,) + a.shape[1:], a.dtype) for a in pairs],
        scratch_shapes=[pltpu.SemaphoreType.DMA((3 * n,)), pltpu.SemaphoreType.DMA((3 * n,))],
    )(*pairs)


def _gather_rows(v):
    rows, width = v.shape

    def body(v_ref, out_ref, send_sems, recv_sems):
        x, y, c = _coords()
        out_ref[pl.ds(pl.multiple_of((4 * x + 2 * y + c) * rows, rows), rows), :] = v_ref[...]
        cps = []
        for mask in range(1, N_DEV):
            peer = (x ^ (mask >> 2), y ^ ((mask >> 1) & 1), c ^ (mask & 1))
            dst = out_ref.at[pl.ds(pl.multiple_of((4 * x + 2 * y + c) * rows, rows), rows), :]
            cps.append(pltpu.make_async_remote_copy(
                src_ref=v_ref, dst_ref=dst, send_sem=send_sems.at[mask - 1], recv_sem=recv_sems.at[mask - 1],
                device_id=peer, device_id_type=MESH))
        for cp in cps:
            cp.start()
        for cp in cps:
            cp.wait()

    vmem = pl.BlockSpec(memory_space=pltpu.VMEM)
    return pl.pallas_call(
        body, name="gather_small",
        in_specs=[vmem], out_specs=vmem,
        out_shape=jax.ShapeDtypeStruct((N_DEV * rows, width), F32),
        scratch_shapes=[pltpu.SemaphoreType.DMA((N_DEV - 1,)), pltpu.SemaphoreType.DMA((N_DEV - 1,))],
    )(v)


def _adamw(w, g, m, v):
    m = ADAM_B1 * m + (1.0 - ADAM_B1) * g
    v = ADAM_B2 * v + (1.0 - ADAM_B2) * jnp.square(g)
    m_hat = m / (1.0 - ADAM_B1 ** ADAM_STEP)
    v_hat = v / (1.0 - ADAM_B2 ** ADAM_STEP)
    delta = -ADAM_LR * (m_hat / (jnp.sqrt(v_hat) + ADAM_EPS) + ADAM_WD * w)
    return delta, m, v


def _tile_rows(rows, cols):
    tm = 1 << int(math.log2(max(2 * SUBLANES, (1 << 18) // cols)))
    while rows % tm:
        tm //= 2
    assert tm >= 2 * SUBLANES, (rows, cols)
    return tm


def _pair_sum(own, got, name):
    _, r, c = own.shape
    tm = _tile_rows(r, c)
    blk = lambda a: (a, (None, tm, c), lambda k, i: (k, i, 0))
    return _ew(lambda a, b: a + b, [blk(own), blk(got)], [(own.shape, BF16, (None, tm, c), lambda k, i: (k, i, 0))],
               (4, r // tm), name)[0]


def _reduce_adam(w, m, v, own, sib, far, name):
    r, c = w.shape
    tm = _tile_rows(r, c)

    def fn(wv, mv, vv, a, b, f0, f1, f2):
        g = (a + b) + f0.astype(F32) + f1.astype(F32) + f2.astype(F32)
        return (g,) + _adamw(wv, g, mv, vv)

    blk = lambda a: (a, (tm, c), lambda i: (i, 0))
    far_blk = lambda j: (far, (None, tm, c), lambda i, j=j: (j, i, 0))
    return _ew(fn, [blk(w), blk(m), blk(v), blk(own), blk(sib), far_blk(0), far_blk(1), far_blk(2)],
               [((r, c), F32, (tm, c), lambda i: (i, 0))] * 4, (r // tm,), name)


def _small_adam(gathered, params, moms, vels, widths):
    n = len(params)
    total = gathered.shape[1]

    def body(*refs):
        g_ref = refs[0]
        p_refs, m_refs, v_refs = refs[1:1 + n], refs[1 + n:1 + 2 * n], refs[1 + 2 * n:1 + 3 * n]
        sum_ref = refs[1 + 3 * n]
        outs = refs[2 + 3 * n:]
        g = g_ref[0:1, :]
        for p in range(1, N_DEV):
            g = g + g_ref[p * SUBLANES:p * SUBLANES + 1, :]
        sum_ref[...] = g
        off = 0
        for i, wd in enumerate(widths):
            d, m2, v2 = _adamw(p_refs[i][...], g[:, off:off + wd], m_refs[i][...], v_refs[i][...])
            outs[3 * i][...] = d
            outs[3 * i + 1][...] = m2
            outs[3 * i + 2][...] = v2
            off += wd

    vmem = pl.BlockSpec(memory_space=pltpu.VMEM)
    out_shape = [jax.ShapeDtypeStruct((1, total), F32)]
    for wd in widths:
        out_shape += [jax.ShapeDtypeStruct((1, wd), F32)] * 3
    return pl.pallas_call(
        body, name="small_adam",
        in_specs=[vmem] * (1 + 3 * n), out_specs=[vmem] * len(out_shape), out_shape=out_shape,
    )(gathered, *params, *moms, *vels)


def _cast_bf16(a, name):
    r, c = a.shape
    tm = _tile_rows(r, c)
    return _ew(lambda v: v, [(a, (tm, c), lambda i: (i, 0))], [((r, c), BF16, (tm, c), lambda i: (i, 0))],
               (r // tm,), name)[0]


def _fold_loss(parts, name):
    r, c = parts.shape

    def fn(v):
        return jnp.broadcast_to(jnp.sum(jnp.sum(v, axis=0, keepdims=True), axis=1, keepdims=True), (SUBLANES, HEAD))

    return _ew(fn, [_const_spec(parts)], [((SUBLANES, HEAD), F32, (SUBLANES, HEAD), lambda i: (0, 0))], (1,),
               name)[0][0:1]


def kernel(x, p, g_mix, w_in, qn_gain, kn_gain, w_branch_a, w_branch_b, w_out, g_mlp, w_up, w_down, g_ple, w_ple_gate, w_ple_proj, loss_target, m_g_mix, m_w_in, m_qn_gain, m_kn_gain, m_w_branch_a, m_w_branch_b, m_w_out, m_g_mlp, m_w_up, m_w_down, m_g_ple, m_w_ple_gate, m_w_ple_proj, v_g_mix, v_w_in, v_qn_gain, v_kn_gain, v_w_branch_a, v_w_branch_b, v_w_out, v_g_mlp, v_w_up, v_w_down, v_g_ple, v_w_ple_gate, v_w_ple_proj):
    x2 = x[0]
    tgt = loss_target[0]
    s, d = x2.shape
    wd_ = w_branch_a.shape[1]
    nh = wd_ // HEAD
    dff = w_up.shape[1]
    qkv_w = 6 * wd_
    tiles = lambda cols: cols // HEAD

    big = [w_in[0], w_branch_a[0], w_branch_b[0], w_out[0], w_up[0], w_down[0], w_ple_gate[0], w_ple_proj[0]]
    names = ["w_in", "w_branch_a", "w_branch_b", "w_out", "w_up", "w_down", "w_ple_gate", "w_ple_proj"]
    row_sharded = [False, False, False, True, False, True, True, False]
    gathered = _all_gather([_cast_bf16(a, "cast_" + nm) for a, nm in zip(big, names)])
    full = [g.reshape((1, N_DEV * g.shape[1], g.shape[2])) if rs else g for g, rs in zip(gathered, row_sharded)]
    win, wba, wbb, wout, wup, wdown, wgate, wproj = full

    tm = 1024 if s % 1024 == 0 else s
    tn_of = lambda n: 512 if n % 512 == 0 else (256 if n % 256 == 0 else n)
    tn_in = 256 if win.shape[2] % 256 == 0 else HEAD

    h = _rms_fwd(x2, g_mix, "norm_mix")
    qk_raw = _mm("nn", h, win, tm=tm, tn=tn_in, tk=d, out_dtypes=[F32], name="proj_qk",
                 n_off=0, n_cnt=2 * wd_ // tn_in)[0]
    rest = _mm("nn", h, win, tm=tm, tn=tn_in, tk=d, out_dtypes=[BF16], name="proj_rest",
               n_off=2 * wd_ // tn_in, n_cnt=(win.shape[0] * win.shape[2] - 2 * wd_) // tn_in)[0]
    o_va, o_qb, o_kb, o_vb, o_ga, o_gb = 0, tiles(wd_), tiles(2 * wd_), tiles(3 * wd_), tiles(4 * wd_), tiles(4 * wd_ + d)
    tabs = _rope_tables(s)
    qa = _headnorm_rope(qk_raw, 0, qn_gain, tabs, nh, "rope_q")
    ka = _headnorm_rope(qk_raw, nh, kn_gain, tabs, nh, "rope_k")
    va = rest[:, :wd_]

    outs, lses = [], []
    for dil in DILATIONS:
        o_g, l_g = _dilated_fwd(_phase_major(qa, dil), _phase_major(ka, dil), _phase_major(va, dil), 0, nh,
                                f"dilated_fwd_{dil}")
        outs.append(_token_major(o_g))
        lses.append(_token_major(l_g))
    ya, lse_all = _mix_fwd(outs, lses, "mix_fwd")
    yb, sb_tot = _sb_fwd(rest, o_qb, o_kb, o_vb, nh, "sb_fwd")

    tn_d = tn_of(wba.shape[2])
    za = _mm("nn", ya, wba, tm=tm, tn=tn_d, tk=wd_, out_dtypes=[BF16], name="branch_a")[0]

    def merge(acc, zav, gav, gbv):
        return _sigmoid(gav.astype(F32)) * zav.astype(F32) + _sigmoid(gbv.astype(F32)) * acc, acc

    merged, zb = _mm("nn", yb, wbb, tm=tm, tn=tn_d, tk=wd_, out_dtypes=[BF16, BF16], name="branch_b_merge",
                     epilogue=merge, extras=[(za, 0), (rest, o_ga * HEAD // tn_d), (rest, o_gb * HEAD // tn_d)])
    x1 = _mm("nn", merged, wout, tm=tm, tn=512, tk=d, out_dtypes=[F32], name="out_proj",
             epilogue=lambda acc, xv: (acc + xv,), extras=[(x2, 0)])[0]

    hm = _rms_fwd(x1, g_mlp, "norm_mlp")
    tn_u = tn_of(wup.shape[2])
    u, act = _mm("nn", hm, wup, tm=tm, tn=tn_u, tk=d, out_dtypes=[BF16, BF16], name="mlp_up",
                 epilogue=lambda acc: (acc, jnp.square(jnp.maximum(acc, 0.0))))
    x3 = _mm("nn", act, wdown, tm=tm, tn=512, tk=min(dff, 2048), out_dtypes=[F32], name="mlp_down",
             epilogue=lambda acc, xv: (acc + xv,), extras=[(x1, 0)])[0]

    hp = _rms_fwd(x3, g_ple, "norm_ple")
    p_b = _cast_bf16(p[0, 0], "cast_p")
    pp = _mm("nn", p_b, wproj, tm=tm, tn=tn_of(wproj.shape[2]), tk=p_b.shape[1], out_dtypes=[BF16],
             name="ple_proj")[0]

    def head(acc, ppv, xv, tv):
        sg = _sigmoid(acc)
        ppf = ppv.astype(F32)
        err = xv + ppf * sg - tv
        dy = err / d
        sq = jnp.square(err)
        return dy, dy * sg, dy * ppf * sg * (1.0 - sg), sq.reshape(-1, SUBLANES, sq.shape[-1]).sum(axis=0)

    n_i = s // tm
    dy, d_pp, d_gt, sq_parts = _mm(
        "nn", hp, wgate, tm=tm, tn=512, tk=d, out_dtypes=[F32, BF16, BF16], name="ple_gate_loss", epilogue=head,
        extras=[(pp, 0), (x3, 0), (tgt, 0)],
        extra_outs=[((n_i * SUBLANES, d), F32, (SUBLANES, 512), lambda i, j: (i, j))])
    loss_vec = _fold_loss(sq_parts, "loss_fold") * 0.5 / d

    g_wproj = _mm("tn", p_b, d_pp, tm=p_b.shape[1], tn=tn_of(wproj.shape[2]), tk=tm, out_dtypes=[F32],
                  name="grad_w_ple_proj", out_nb=N_DEV)[0]
    g_wgate = _mm("tn", hp, d_gt, tm=512, tn=1024, tk=tm, out_dtypes=[F32], name="grad_w_ple_gate")[0]
    d_hp = _mm("nt", d_gt, wgate, tm=tm, tn=512, tk=min(d, 1024), out_dtypes=[F32], name="d_hp")[0]
    dx3, dx3_b, g_gple = _rms_bwd(d_hp, x3, g_ple, dy, "norm_ple_bwd")

    d_u = _mm("nt", dx3_b, wdown, tm=tm, tn=1024, tk=min(d, 1024), out_dtypes=[BF16], name="d_u",
              epilogue=lambda acc, uv: (acc * (2.0 * jnp.maximum(uv.astype(F32), 0.0)),), extras=[(u, 0)])[0]
    g_wdown = _mm("tn", act, dx3_b, tm=1024, tn=1024, tk=tm, out_dtypes=[F32], name="grad_w_down")[0]
    g_wup = _mm("tn", hm, d_u, tm=1024, tn=wup.shape[2], tk=tm, out_dtypes=[F32], name="grad_w_up",
                out_nb=N_DEV)[0]
    d_hm = _mm("nt", d_u, wup, tm=tm, tn=1024, tk=wup.shape[2], out_dtypes=[F32], name="d_hm")[0]
    dx1, dx1_b, g_gmlp = _rms_bwd(d_hm, x1, g_mlp, dx3, "norm_mlp_bwd")

    def unmerge(acc, gav, gbv, zav, zbv):
        sa, sb = _sigmoid(gav.astype(F32)), _sigmoid(gbv.astype(F32))
        return acc * sa, acc * sb, acc * zav.astype(F32) * sa * (1.0 - sa), acc * zbv.astype(F32) * sb * (1.0 - sb)

    d_za, d_zb, d_ga, d_gb = _mm(
        "nt", dx1_b, wout, tm=tm, tn=512, tk=min(d, 1024), out_dtypes=[BF16] * 4, name="d_merged", epilogue=unmerge,
        extras=[(rest, o_ga * HEAD // 512), (rest, o_gb * HEAD // 512), (za, 0), (zb, 0)])
    g_wout = _mm("tn", merged, dx1_b, tm=512, tn=1024, tk=tm, out_dtypes=[F32], name="grad_w_out")[0]
    g_wba = _mm("tn", ya, d_za, tm=wd_, tn=wba.shape[2], tk=tm, out_dtypes=[F32], name="grad_w_branch_a",
                out_nb=N_DEV)[0]
    g_wbb = _mm("tn", yb, d_zb, tm=wd_, tn=wbb.shape[2], tk=tm, out_dtypes=[F32], name="grad_w_branch_b",
                out_nb=N_DEV)[0]
    d_ya = _mm("nt", d_za, wba, tm=tm, tn=wd_, tk=wba.shape[2], out_dtypes=[F32], name="d_ya")[0]
    d_yb = _mm("nt", d_zb, wbb, tm=tm, tn=wd_, tk=wbb.shape[2], out_dtypes=[F32], name="d_yb")[0]

    d_qb, d_kb, d_vb = _sb_bwd(rest, o_qb, o_kb, o_vb, d_yb, sb_tot, nh, "sb_bwd")
    delta = _head_dot(d_ya, ya, nh, "mix_delta")
    dqs, dks, dvs = [], [], []
    for dil in DILATIONS:
        dq_g, dk_g, dv_g = _dilated_bwd(
            _phase_major(qa, dil), _phase_major(ka, dil), _phase_major(va, dil), 0, _phase_major(d_ya, dil),
            _phase_major(lse_all, dil), _phase_major(delta, dil), nh, f"dilated_bwd_{dil}")
        dqs.append(_token_major(dq_g))
        dks.append(_token_major(dk_g))
        dvs.append(_token_major(dv_g))
    d_qa, g_qn = _headnorm_rope_bwd(dqs, qk_raw, 0, qn_gain, tabs, nh, "rope_q_bwd")
    d_ka, g_kn = _headnorm_rope_bwd(dks, qk_raw, nh, kn_gain, tabs, nh, "rope_k_bwd")
    tmr = 256
    d_va = _ew(lambda a, b, c: a + b + c, [_row_spec(a, tmr) for a in dvs],
               [((s, wd_), BF16, (tmr, wd_), lambda i: (i, 0))], (s // tmr,), "sum_dv")[0]
    d_proj = jnp.concatenate([d_qa, d_ka, d_va, d_qb, d_kb.astype(BF16), d_vb.astype(BF16), d_ga, d_gb], axis=1)

    g_win = _mm("tn", h, d_proj, tm=1024, tn=win.shape[2], tk=tm, out_dtypes=[F32], name="grad_w_in",
                out_nb=N_DEV)[0]
    d_h = _mm("nt", d_proj, win, tm=tm, tn=1024, tk=win.shape[2], out_dtypes=[F32], name="d_h")[0]
    dx, _, g_gmix = _rms_bwd(d_h, x2, g_mix, dx1, "norm_mix_bwd")

    grads = [g_win, g_wba, g_wbb, g_wout, g_wup, g_wdown, g_wgate, g_wproj]
    grads = [g.reshape((N_DEV, g.shape[1] // N_DEV, g.shape[2])) if rs else g for g, rs in zip(grads, row_sharded)]
    cx, cy, cc = _coords()
    by_core = [g.reshape((4, 2) + g.shape[1:]) for g in grads]
    keep = [lax.dynamic_index_in_dim(g, cc, axis=1, keepdims=False) for g in by_core]
    give = [lax.dynamic_index_in_dim(g, 1 - cc, axis=1, keepdims=False) for g in by_core]
    got = _to_sibling(give)
    pairs = [_pair_sum(a, b, "pair_" + nm) for a, b, nm in zip(keep, got, names)]
    far = _to_chips(pairs)
    chip = 2 * cx + cy
    moms = [m_w_in, m_w_branch_a, m_w_branch_b, m_w_out, m_w_up, m_w_down, m_w_ple_gate, m_w_ple_proj]
    vels = [v_w_in, v_w_branch_a, v_w_branch_b, v_w_out, v_w_up, v_w_down, v_w_ple_gate, v_w_ple_proj]
    big_out = {}
    for i, nm in enumerate(names):
        own = lax.dynamic_index_in_dim(keep[i], chip, axis=0, keepdims=False)
        sib = lax.dynamic_index_in_dim(got[i], chip, axis=0, keepdims=False)
        big_out[nm] = [a[None] for a in _reduce_adam(big[i], moms[i][0], vels[i][0], own, sib, far[i], "adam_" + nm)]

    small_names = ["g_mix", "qn_gain", "kn_gain", "g_mlp", "g_ple"]
    small_p = [g_mix, qn_gain, kn_gain, g_mlp, g_ple]
    small_m = [m_g_mix, m_qn_gain, m_kn_gain, m_g_mlp, m_g_ple]
    small_v = [v_g_mix, v_qn_gain, v_kn_gain, v_g_mlp, v_g_ple]
    small_g = [g_gmix, g_qn, g_kn, g_gmlp, g_gple]
    widths = [a.shape[1] for a in small_p]
    vec = jnp.concatenate(small_g + [loss_vec], axis=1)
    vec = jnp.pad(vec, ((0, SUBLANES - 1), (0, 0)))
    res = _small_adam(_gather_rows(vec), small_p, small_m, small_v, widths)
    summed = res[0]
    small_out, off = {}, 0
    for i, nm in enumerate(small_names):
        small_out[nm] = [summed[:, off:off + widths[i]]] + list(res[1 + 3 * i:4 + 3 * i])
        off += widths[i]
    loss = summed[0, off]

    order = ["g_mix", "w_in", "qn_gain", "kn_gain", "w_branch_a", "w_branch_b", "w_out", "g_mlp", "w_up", "w_down",
             "g_ple", "w_ple_gate", "w_ple_proj"]
    table = {**big_out, **small_out}
    result = [loss, dx[None]]
    for kind in range(4):
        result += [table[nm][kind] for nm in order]
    return tuple(result)
```

```python
import functools
import math

import jax
import jax.numpy as jnp
from jax import lax
from jax.experimental import pallas as pl
from jax.experimental.pallas import tpu as pltpu

F32 = jnp.float32
BF16 = jnp.bfloat16
MESH = pl.DeviceIdType.MESH

HEAD = 128
WINDOW = 128
DILATIONS = (1, 4, 16)
ROT = HEAD // 4
ROPE_THETA = 500000.0
EPS = 1e-6
NEG = -1e30
N_DEV = 8

ADAM_LR = 0.001
ADAM_B1 = 0.9
ADAM_B2 = 0.999
ADAM_EPS = 1e-08
ADAM_WD = 0.01
ADAM_STEP = 10

SUBLANES = 8
VMEM_CAP_MB = 56

NT_DIMS = (((1,), (1,)), ((), ()))
TN_DIMS = (((0,), (0,)), ((), ()))


def _params(semantics, vmem_mb):
    return pltpu.CompilerParams(dimension_semantics=semantics, vmem_limit_bytes=min(vmem_mb, VMEM_CAP_MB) << 20)


def _sigmoid(x):
    return 1.0 / (1.0 + jnp.exp(-x))


def _split_bf16(x):
    hi = x.astype(BF16)
    lo = (x - hi.astype(F32)).astype(BF16)
    return hi, lo


class _Host:
    def __init__(self, comm, grid):
        self.comm, self.grid = comm, grid
        any_spec = pl.BlockSpec(memory_space=pl.ANY)
        self.ins = list(comm["ins"]) if comm else []
        self.out_shape = list(comm["out_shape"]) if comm else []
        self.scratch = list(comm["scratch"]) if comm else []
        self.in_specs = [any_spec] * len(self.ins)
        self.out_specs = [any_spec] * len(self.out_shape)

    def split(self, refs, n_in, n_out):
        pos = n_in
        c_in = refs[pos:pos + len(self.ins)]
        pos += len(self.ins)
        outs = refs[pos:pos + n_out]
        pos += n_out
        c_out = refs[pos:pos + len(self.out_shape)]
        pos += len(self.out_shape)
        own = len(refs) - pos - len(self.scratch)
        return (c_in, c_out, refs[pos + own:]), (outs, refs[pos:pos + own])

    def stage(self, which, comm_refs):
        if self.comm is None or which not in self.comm["stages"]:
            return
        grid = self.grid
        at = {"first": [0] * len(grid), "mid": [grid[0] // 2] + [0] * (len(grid) - 1),
              "last": [g - 1 for g in grid]}[which]
        cond = functools.reduce(jnp.logical_and, [pl.program_id(ax) == v for ax, v in enumerate(at)])

        @pl.when(cond)
        def _():
            self.comm["run"](which, *comm_refs)


def _ew(fn, ins, outs, grid, name, colsums=(), vmem_mb=40):
    n_in, n_out, n_cs = len(ins), len(outs), len(colsums)
    steps = math.prod(grid)

    def body(*refs):
        in_refs = refs[:n_in]
        out_refs = refs[n_in:n_in + n_out]
        cs_refs = refs[n_in + n_out:n_in + n_out + n_cs]
        acc_refs = refs[n_in + n_out + n_cs:]
        vals = fn(*[r[...] for r in in_refs])
        if not isinstance(vals, tuple):
            vals = (vals,)
        for r, v in zip(out_refs, vals[:n_out]):
            r[...] = v.astype(r.dtype)
        if n_cs:
            step = pl.program_id(0)
            for ax in range(1, len(grid)):
                step = step * grid[ax] + pl.program_id(ax)
            for acc, cs, v in zip(acc_refs, cs_refs, vals[n_out:]):
                part = v.reshape(-1, SUBLANES, v.shape[-1]).sum(axis=0)

                @pl.when(step == 0)
                def _(acc=acc, part=part):
                    acc[...] = part

                @pl.when(step > 0)
                def _(acc=acc, part=part):
                    acc[...] += part

                @pl.when(step == steps - 1)
                def _(acc=acc, cs=cs):
                    cs[...] = acc[...].sum(axis=0, keepdims=True)

    out_shape = [jax.ShapeDtypeStruct(s, d) for s, d, _, _ in outs]
    out_specs = [pl.BlockSpec(b, m) for _, _, b, m in outs]
    for w in colsums:
        out_shape.append(jax.ShapeDtypeStruct((1, w), F32))
        out_specs.append(pl.BlockSpec((1, w), lambda *_: (0, 0)))
    sem = ("arbitrary",) * len(grid) if n_cs else ("parallel",) * len(grid)
    res = pl.pallas_call(
        body, name=name, grid=grid,
        in_specs=[pl.BlockSpec(b, m) for _, b, m in ins],
        out_specs=out_specs, out_shape=out_shape,
        scratch_shapes=[pltpu.VMEM((SUBLANES, w), F32) for w in colsums],
        compiler_params=_params(sem, vmem_mb),
    )(*[a for a, _, _ in ins])
    return res


def _row_spec(a, tm):
    return (a, (tm, a.shape[1]), lambda i: (i, 0))


def _const_spec(a):
    return (a, a.shape, lambda *_: (0,) * a.ndim)


def _mm(mode, a, b, *, tm, tn, tk, out_dtypes, name, epilogue=None, extras=(), n_off=0, n_cnt=None,
        out_nb=1, extra_outs=(), vmem_mb=48, comm=None):
    if mode == "nn":
        m, kdim = a.shape
        nb, _, n = b.shape
        npb = n // tn
        ncols = nb * n
        n_tiles = (ncols // tn) if n_cnt is None else n_cnt
        a_spec = pl.BlockSpec((tm, tk), lambda i, j, k: (i, k))
        b_spec = pl.BlockSpec((None, tk, tn), lambda i, j, k: ((j + n_off) // npb, k, (j + n_off) % npb))
        dims = (((1,), (0,)), ((), ()))
    elif mode == "nt":
        m, kdim = a.shape
        nb, nout, n = b.shape
        kpb = n // tk
        n_tiles = nout // tn
        a_spec = pl.BlockSpec((tm, tk), lambda i, j, k: (i, k))
        b_spec = pl.BlockSpec((None, tn, tk), lambda i, j, k: (k // kpb, j, k % kpb))
        dims = NT_DIMS
    else:
        kdim, m = a.shape
        ncols = b.shape[1]
        n_tiles = ncols // tn
        a_spec = pl.BlockSpec((tk, tm), lambda i, j, k: (k, i))
        b_spec = pl.BlockSpec((tk, tn), lambda i, j, k: (k, j))
        dims = TN_DIMS
    nk = kdim // tk
    assert kdim % tk == 0 and m % tm == 0
    grid = (m // tm, n_tiles, nk)
    n_ex, n_out = len(extras), len(out_dtypes) + len(extra_outs)
    host = _Host(comm, grid)

    def body(*refs):
        a_ref, b_ref = refs[0], refs[1]
        ex_refs = refs[2:2 + n_ex]
        comm_refs, (out_refs, scratch) = host.split(refs, 2 + n_ex, n_out)
        host.stage("first", comm_refs)

        def finish(acc):
            vals = (acc,) * n_out if epilogue is None else epilogue(acc, *[r[...] for r in ex_refs])
            for r, v in zip(out_refs, vals):
                r[...] = v.astype(r.dtype)

        part = lax.dot_general(a_ref[...], b_ref[...], dims, preferred_element_type=F32)
        if nk == 1:
            finish(part)
        else:
            acc_ref = scratch[0]
            k = pl.program_id(2)

            @pl.when(k == 0)
            def _():
                acc_ref[...] = part

            @pl.when(k > 0)
            def _():
                acc_ref[...] += part

            @pl.when(k == nk - 1)
            def _():
                finish(acc_ref[...])

        host.stage("mid", comm_refs)
        host.stage("last", comm_refs)

    if mode == "tn":
        npo = (ncols // out_nb) // tn
        out_shape = [jax.ShapeDtypeStruct((out_nb, m, ncols // out_nb), d) for d in out_dtypes]
        out_specs = [pl.BlockSpec((None, tm, tn), lambda i, j, k: (j // npo, i, j % npo)) for _ in out_dtypes]
    else:
        out_shape = [jax.ShapeDtypeStruct((m, n_tiles * tn), d) for d in out_dtypes]
        out_specs = [pl.BlockSpec((tm, tn), lambda i, j, k: (i, j)) for _ in out_dtypes]
    for s, d, blk, imap in extra_outs:
        out_shape.append(jax.ShapeDtypeStruct(s, d))
        out_specs.append(pl.BlockSpec(blk, lambda i, j, k, imap=imap: imap(i, j)))
    ex_specs = [pl.BlockSpec((tm, tn), lambda i, j, k, off=off: (i, j + off)) for _, off in extras]
    sem = ("parallel", "parallel", "arbitrary") if comm is None else ("arbitrary",) * 3
    return pl.pallas_call(
        body, name=name, grid=grid,
        in_specs=[a_spec, b_spec] + ex_specs + host.in_specs,
        out_specs=out_specs + host.out_specs, out_shape=out_shape + host.out_shape,
        scratch_shapes=([pltpu.VMEM((tm, tn), F32)] if nk > 1 else []) + host.scratch,
        compiler_params=_params(sem, vmem_mb),
    )(a, b, *[e for e, _ in extras], *host.ins)


def _rms_fwd(x, g, name):
    tm = 256

    def fn(xv, gv):
        r = lax.rsqrt(jnp.mean(xv * xv, axis=-1, keepdims=True) + EPS)
        return xv * r * gv

    return _ew(fn, [_row_spec(x, tm), _const_spec(g)], [(x.shape, BF16, (tm, x.shape[1]), lambda i: (i, 0))],
               (x.shape[0] // tm,), name)[0]


def _rms_bwd(dh, x, g, res, name):
    tm = 256
    d = x.shape[1]

    def fn(dhv, xv, gv, rv):
        r = lax.rsqrt(jnp.mean(xv * xv, axis=-1, keepdims=True) + EPS)
        xh = xv * r
        dyg = dhv * gv
        dx = rv + r * (dyg - xh * jnp.mean(dyg * xh, axis=-1, keepdims=True))
        return dx, dx, dhv * xh

    spec = lambda dt: (x.shape, dt, (tm, d), lambda i: (i, 0))
    return _ew(fn, [_row_spec(dh, tm), _row_spec(x, tm), _const_spec(g), _row_spec(res, tm)],
               [spec(F32), spec(BF16)], (x.shape[0] // tm,), name, colsums=(d,))


def _rope_tables(s):
    half = ROT // 2
    pos = jnp.arange(s, dtype=F32)
    inv = ROPE_THETA ** (-jnp.arange(0, ROT, 2, dtype=F32) / ROT)
    ang = pos[:, None] * inv[None, :]
    cos, sin = jnp.cos(ang), jnp.sin(ang)
    pad = jnp.zeros((s, HEAD - ROT), F32)
    c = jnp.concatenate([cos, cos, pad + 1.0], axis=1)
    a = jnp.concatenate([-sin, jnp.zeros_like(sin), pad], axis=1)
    b = jnp.concatenate([jnp.zeros_like(sin), sin, pad], axis=1)
    return c, a, b


def _headnorm_rope(proj, off, gain, tabs, n_heads, name):
    tm = 512
    s = proj.shape[0]

    def fn(xv, gv, c, a, b):
        r = lax.rsqrt(jnp.mean(xv * xv, axis=-1, keepdims=True) + EPS)
        y = xv * r * gv
        return c * y + a * pltpu.roll(y, HEAD - ROT // 2, 1) + b * pltpu.roll(y, ROT // 2, 1)

    tab = lambda t: (t, (tm, HEAD), lambda i, h: (i, 0))
    return _ew(fn, [(proj, (tm, HEAD), lambda i, h: (i, off + h)), (gain, (1, HEAD), lambda i, h: (0, 0))]
               + [tab(t) for t in tabs],
               [((s, n_heads * HEAD), BF16, (tm, HEAD), lambda i, h: (i, h))], (s // tm, n_heads), name)[0]


def _headnorm_rope_bwd(dys, proj, off, gain, tabs, n_heads, name):
    tm = 512
    s = proj.shape[0]
    n_dy = len(dys)

    def fn(*vals):
        dy = vals[0]
        for v in vals[1:n_dy]:
            dy = dy + v
        xv, gv, c, a, b = vals[n_dy:]
        dn = c * dy + pltpu.roll(a * dy, ROT // 2, 1) + pltpu.roll(b * dy, HEAD - ROT // 2, 1)
        r = lax.rsqrt(jnp.mean(xv * xv, axis=-1, keepdims=True) + EPS)
        xh = xv * r
        dyg = dn * gv
        dx = r * (dyg - xh * jnp.mean(dyg * xh, axis=-1, keepdims=True))
        return dx, dn * xh

    tab = lambda t: (t, (tm, HEAD), lambda i, h: (i, 0))
    return _ew(fn, [(d, (tm, HEAD), lambda i, h: (i, h)) for d in dys]
               + [(proj, (tm, HEAD), lambda i, h: (i, off + h)), (gain, (1, HEAD), lambda i, h: (0, 0))]
               + [tab(t) for t in tabs],
               [((s, n_heads * HEAD), BF16, (tm, HEAD), lambda i, h: (i, h))], (s // tm, n_heads), name,
               colsums=(HEAD,))


def _phase_major(a, d):
    s, w = a.shape
    if d == 1:
        return a.reshape(1, s, w)
    return a.reshape(s // d, d, w).transpose(1, 0, 2)


def _token_major(a):
    d, m, w = a.shape
    if d == 1:
        return a.reshape(m, w)
    return a.transpose(1, 0, 2).reshape(m * d, w)


def _dil_tq(m):
    return min(512, m)


def _dilated_fwd(q, k, v, voff, n_heads, name):
    d, m, _ = q.shape
    tq = _dil_tq(m)
    nq = m // tq
    per = tq // WINDOW
    scale = HEAD ** -0.5

    def body(q_ref, kc_ref, kp_ref, vc_ref, vp_ref, o_ref, l_ref):
        n = pl.program_id(2)
        kk = jnp.concatenate([kp_ref[...], kc_ref[...]], axis=0)
        vv = jnp.concatenate([vp_ref[...], vc_ref[...]], axis=0)
        s = lax.dot_general(q_ref[...], kk, NT_DIMS, preferred_element_type=F32) * scale
        row = lax.broadcasted_iota(jnp.int32, s.shape, 0)
        col = lax.broadcasted_iota(jnp.int32, s.shape, 1)
        dist = row + WINDOW - col
        valid = (dist >= 0) & (dist <= WINDOW) & ((n > 0) | (col >= WINDOW))
        s = jnp.where(valid, s, NEG)
        mx = jnp.max(s, axis=-1, keepdims=True)
        e = jnp.exp(s - mx)
        den = jnp.sum(e, axis=-1, keepdims=True)
        o = jnp.dot(e.astype(BF16), vv, preferred_element_type=F32) / den
        o_ref[...] = o
        l_ref[...] = jnp.broadcast_to(mx + jnp.log(den), (tq, HEAD))

    cur = lambda off: pl.BlockSpec((None, tq, HEAD), lambda r, h, n: (r, n, off + h))
    prev = lambda off: pl.BlockSpec((None, WINDOW, HEAD), lambda r, h, n: (r, jnp.maximum(n * per - 1, 0), off + h))
    out = jax.ShapeDtypeStruct((d, m, n_heads * HEAD), F32)
    return pl.pallas_call(
        body, name=name, grid=(d, n_heads, nq),
        in_specs=[cur(0), cur(0), prev(0), cur(voff), prev(voff)],
        out_specs=[cur(0), cur(0)], out_shape=[out, out],
        compiler_params=_params(("parallel",) * 3, 32),
    )(q, k, k, v, v)


def _dilated_bwd(q, k, v, voff, dy, lse, delta, n_heads, name):
    d, m, _ = q.shape
    tq = _dil_tq(m)
    nq = m // tq
    per = tq // WINDOW
    last_blk = m // WINDOW - 1
    scale = HEAD ** -0.5

    def body(qc_ref, qn_ref, kc_ref, kp_ref, vc_ref, vp_ref, dyc_ref, dyn_ref, lc_ref, ln_ref, dc_ref, dn_ref,
             dq_ref, dk_ref, dv_ref):
        n = pl.program_id(2)
        qc, kc, vc = qc_ref[...], kc_ref[...], vc_ref[...]
        dyc = dyc_ref[...].astype(BF16)
        kk = jnp.concatenate([kp_ref[...], kc], axis=0)
        vv = jnp.concatenate([vp_ref[...], vc], axis=0)
        s = lax.dot_general(qc, kk, NT_DIMS, preferred_element_type=F32) * scale
        row = lax.broadcasted_iota(jnp.int32, s.shape, 0)
        col = lax.broadcasted_iota(jnp.int32, s.shape, 1)
        dist = row + WINDOW - col
        valid = (dist >= 0) & (dist <= WINDOW) & ((n > 0) | (col >= WINDOW))
        p = jnp.where(valid, jnp.exp(jnp.minimum(s - lc_ref[:, :1], 0.0)), 0.0)
        dp = lax.dot_general(dyc, vv, NT_DIMS, preferred_element_type=F32)
        ds = p * (dp - dc_ref[:, :1]) * scale
        dq_ref[...] = jnp.dot(ds.astype(BF16), kk, preferred_element_type=F32)
        qq = jnp.concatenate([qc, qn_ref[...]], axis=0)
        dyy = jnp.concatenate([dyc, dyn_ref[...].astype(BF16)], axis=0)
        ll = jnp.concatenate([lc_ref[...], ln_ref[...]], axis=0)[:, :1]
        dd = jnp.concatenate([dc_ref[...], dn_ref[...]], axis=0)[:, :1]
        s2 = lax.dot_general(qq, kc, NT_DIMS, preferred_element_type=F32) * scale
        row2 = lax.broadcasted_iota(jnp.int32, s2.shape, 0)
        col2 = lax.broadcasted_iota(jnp.int32, s2.shape, 1)
        dist2 = row2 - col2
        valid2 = (dist2 >= 0) & (dist2 <= WINDOW) & ((n < nq - 1) | (row2 < tq))
        p2 = jnp.where(valid2, jnp.exp(jnp.minimum(s2 - ll, 0.0)), 0.0)
        dp2 = lax.dot_general(dyy, vc, NT_DIMS, preferred_element_type=F32)
        ds2 = p2 * (dp2 - dd) * scale
        dk_ref[...] = lax.dot_general(ds2.astype(BF16), qq, TN_DIMS, preferred_element_type=F32)
        dv_ref[...] = lax.dot_general(p2.astype(BF16), dyy, TN_DIMS, preferred_element_type=F32)

    cur = lambda off: pl.BlockSpec((None, tq, HEAD), lambda r, h, n: (r, n, off + h))
    prev = lambda off: pl.BlockSpec((None, WINDOW, HEAD), lambda r, h, n: (r, jnp.maximum(n * per - 1, 0), off + h))
    nxt = lambda off: pl.BlockSpec((None, WINDOW, HEAD),
                                   lambda r, h, n: (r, jnp.minimum((n + 1) * per, last_blk), off + h))
    out = jax.ShapeDtypeStruct((d, m, n_heads * HEAD), F32)
    return pl.pallas_call(
        body, name=name, grid=(d, n_heads, nq),
        in_specs=[cur(0), nxt(0), cur(0), prev(0), cur(voff), prev(voff), cur(0), nxt(0), cur(0), nxt(0), cur(0),
                  nxt(0)],
        out_specs=[cur(0)] * 3, out_shape=[out] * 3,
        compiler_params=_params(("parallel",) * 3, 40),
    )(q, q, k, k, v, v, dy, dy, lse, lse, delta, delta)


def _mix_fwd(os_, ls_, name):
    tm = 256
    s, w = os_[0].shape
    n = len(os_)

    def fn(*vals):
        o, l = vals[:n], vals[n:]
        mx = functools.reduce(jnp.maximum, l)
        e = [jnp.exp(x - mx) for x in l]
        den = functools.reduce(jnp.add, e)
        y = functools.reduce(jnp.add, [ei * oi for ei, oi in zip(e, o)]) / den
        return y, mx + jnp.log(den)

    return _ew(fn, [_row_spec(a, tm) for a in list(os_) + list(ls_)],
               [((s, w), BF16, (tm, w), lambda i: (i, 0)), ((s, w), F32, (tm, w), lambda i: (i, 0))],
               (s // tm,), name)


def _head_dot(dy, y, n_heads, name):
    tm = 512
    s = dy.shape[0]

    def fn(a, b):
        return jnp.broadcast_to(jnp.sum(a * b.astype(F32), axis=-1, keepdims=True), a.shape)

    blk = lambda a: (a, (tm, HEAD), lambda i, h: (i, h))
    return _ew(fn, [blk(dy), blk(y)], [((s, n_heads * HEAD), F32, (tm, HEAD), lambda i, h: (i, h))],
               (s // tm, n_heads), name)[0]


SB_TQ = 1024
SB_TB = 512
SB_TK = 256


def _softplus(z):
    return jnp.where(z > 20.0, z, jnp.log(1.0 + jnp.exp(z)))


def _tri(t, cmp):
    rows = lax.broadcasted_iota(jnp.int32, (2 * t, t), 0)
    cols = lax.broadcasted_iota(jnp.int32, (2 * t, t), 1)
    return jnp.where(cmp(jnp.where(rows >= t, rows - t, rows), cols), 1.0, 0.0).astype(BF16)


def _tri_sum(x, tri):
    return jnp.dot(jnp.concatenate(_split_bf16(x), axis=1), tri, preferred_element_type=F32)


def _sb_fwd(qkv, qoff, koff, voff, n_heads, name, comm=None):
    s = qkv.shape[0]
    tq, tb, tk = min(SB_TQ, s), SB_TB, SB_TK
    scale = HEAD ** -0.5
    host = _Host(comm, (n_heads, s // tq))

    def body(*refs):
        q_ref, k_ref, v_ref = refs[:3]
        comm_refs, ((o_ref, tot_ref), _) = host.split(refs, 3, 2)
        host.stage("first", comm_refs)
        i = pl.program_id(1)
        q = q_ref[...]
        after = _tri(tk, lambda a, b: a > b)
        row = lax.broadcasted_iota(jnp.int32, (tq, tk), 0)
        col = lax.broadcasted_iota(jnp.int32, (tq, tk), 1)

        def sub(start, carry, o, causal):
            kj = k_ref[pl.ds(start, tk), :]
            vj = v_ref[pl.ds(start, tk), :]
            z = lax.dot_general(q, kj, NT_DIMS, preferred_element_type=F32) * scale
            sp = _softplus(z)
            logsig = z - sp
            if causal is not None:
                sp = jnp.where(causal, sp, 0.0)
            a = jnp.exp(logsig - carry - _tri_sum(sp, after))
            if causal is not None:
                a = jnp.where(causal, a, 0.0)
            o = o + jnp.dot(a.astype(BF16), vj, preferred_element_type=F32)
            return carry + jnp.sum(sp, axis=-1, keepdims=True), o

        def block(base, carry, o, diag_off):
            for half in reversed(range(tb // tk)):
                causal = None if diag_off is None else (col + (diag_off + half * tk) < row)
                carry, o = sub(pl.multiple_of(base + half * tk, tk), carry, o, causal)
            return carry, o

        carry, o = jnp.zeros((tq, 1), F32), jnp.zeros((tq, HEAD), F32)
        for b in reversed(range(tq // tb)):
            carry, o = block(i * tq + b * tb, carry, o, b * tb)
        below = i * (tq // tb)
        carry, o = lax.fori_loop(0, below, lambda jj, co: block((below - 1 - jj) * tb, co[0], co[1], None),
                                 (carry, o))
        o_ref[...] = o.astype(o_ref.dtype)
        tot_ref[...] = jnp.broadcast_to(carry, (tq, HEAD))
        host.stage("mid", comm_refs)
        host.stage("last", comm_refs)

    t = tq
    full = lambda off: pl.BlockSpec((s, HEAD), lambda h, i: (0, off + h))
    tile_spec = lambda off: pl.BlockSpec((t, HEAD), lambda h, i: (i, off + h))
    out = lambda dt: jax.ShapeDtypeStruct((s, n_heads * HEAD), dt)
    return pl.pallas_call(
        body, name=name, grid=(n_heads, s // t),
        in_specs=[tile_spec(qoff), full(koff), full(voff)] + host.in_specs,
        out_specs=[tile_spec(0), tile_spec(0)] + host.out_specs,
        out_shape=[out(BF16), out(F32)] + host.out_shape,
        scratch_shapes=host.scratch,
        compiler_params=_params(("parallel" if comm is None else "arbitrary", "arbitrary"), 40),
    )(qkv, qkv, qkv, *host.ins)


def _sb_bwd(qkv, qoff, koff, voff, do, tot, n_heads, name, comm=None):
    s = qkv.shape[0]
    tq, tb, tk = min(SB_TQ, s), SB_TB, SB_TK
    scale = HEAD ** -0.5
    host = _Host(comm, (n_heads, s // tq))

    def body(*refs):
        q_ref, k_ref, v_ref, do_ref, tot_ref = refs[:5]
        comm_refs, ((dq_ref, dk_ref, dv_ref), _) = host.split(refs, 5, 3)
        host.stage("first", comm_refs)
        i = pl.program_id(1)

        @pl.when(i == 0)
        def _():
            dk_ref[...] = jnp.zeros_like(dk_ref)
            dv_ref[...] = jnp.zeros_like(dv_ref)

        q = q_ref[...]
        do_b = do_ref[...].astype(BF16)
        total = tot_ref[:, :1]
        upto = _tri(tk, lambda a, b: a <= b)
        before = _tri(tk, lambda a, b: a < b)
        row = lax.broadcasted_iota(jnp.int32, (tq, tk), 0)
        col = lax.broadcasted_iota(jnp.int32, (tq, tk), 1)

        def sub(start, lsum, psum, dq, causal):
            kj = k_ref[pl.ds(start, tk), :]
            vj = v_ref[pl.ds(start, tk), :]
            z = lax.dot_general(q, kj, NT_DIMS, preferred_element_type=F32) * scale
            sp = _softplus(z)
            logsig = z - sp
            if causal is not None:
                sp = jnp.where(causal, sp, 0.0)
            a = jnp.exp(logsig - (total - lsum - _tri_sum(sp, upto)))
            if causal is not None:
                a = jnp.where(causal, a, 0.0)
            da = lax.dot_general(do_b, vj, NT_DIMS, preferred_element_type=F32)
            p = a * da
            c = psum + _tri_sum(p, before)
            beta = jnp.exp(logsig)
            dz = (p - (p + c) * beta) * scale
            if causal is not None:
                dz = jnp.where(causal, dz, 0.0)
            dz_b = dz.astype(BF16)
            dq = dq + jnp.dot(dz_b, kj, preferred_element_type=F32)
            dk_ref[pl.ds(start, tk), :] += lax.dot_general(dz_b, q, TN_DIMS, preferred_element_type=F32)
            dv_ref[pl.ds(start, tk), :] += lax.dot_general(a.astype(BF16), do_b, TN_DIMS, preferred_element_type=F32)
            return lsum + jnp.sum(sp, axis=-1, keepdims=True), psum + jnp.sum(p, axis=-1, keepdims=True), dq

        def block(base, lsum, psum, dq, diag_off):
            for half in range(tb // tk):
                causal = None if diag_off is None else (col + (diag_off + half * tk) < row)
                lsum, psum, dq = sub(pl.multiple_of(base + half * tk, tk), lsum, psum, dq, causal)
            return lsum, psum, dq

        zero = jnp.zeros((tq, 1), F32)
        state = lax.fori_loop(0, i * (tq // tb), lambda j, c: block(j * tb, c[0], c[1], c[2], None),
                              (zero, zero, jnp.zeros((tq, HEAD), F32)))
        for b in range(tq // tb):
            state = block(i * tq + b * tb, *state, b * tb)
        dq_ref[...] = state[2].astype(dq_ref.dtype)
        host.stage("mid", comm_refs)
        host.stage("last", comm_refs)

    t = tq
    full = lambda off: pl.BlockSpec((s, HEAD), lambda h, i: (0, off + h))
    tile_spec = lambda off: pl.BlockSpec((t, HEAD), lambda h, i: (i, off + h))
    w = n_heads * HEAD
    return pl.pallas_call(
        body, name=name, grid=(n_heads, s // t),
        in_specs=[tile_spec(qoff), full(koff), full(voff), tile_spec(0), tile_spec(0)] + host.in_specs,
        out_specs=[tile_spec(0), full(0), full(0)] + host.out_specs,
        out_shape=[jax.ShapeDtypeStruct((s, w), BF16), jax.ShapeDtypeStruct((s, w), F32),
                   jax.ShapeDtypeStruct((s, w), F32)] + host.out_shape,
        scratch_shapes=host.scratch,
        compiler_params=_params(("parallel" if comm is None else "arbitrary", "arbitrary"), 48),
    )(qkv, qkv, qkv, do, tot, *host.ins)


def _coords():
    return lax.axis_index("x"), lax.axis_index("y"), lax.axis_index("c")


def _gather_plan(shards):
    n = len(shards)

    def run(stage, ins, outs, sems):
        send_sems, recv_sems, local_sems = sems
        x, y, c = _coords()
        me, sibling = (x, y, c), (x, y, 1 - c)
        chips = [(1 - x, y), (x, 1 - y), (1 - x, 1 - y)]

        def copy(w, k, block, to, src=None):
            dst = outs[w].at[4 * block[0] + 2 * block[1] + block[2]]
            return pltpu.make_async_remote_copy(
                src_ref=dst if src is None else src, dst_ref=dst,
                send_sem=send_sems.at[7 * w + k], recv_sem=recv_sems.at[7 * w + k],
                device_id=to, device_id_type=MESH)

        mine = [pltpu.make_async_copy(ins[w], outs[w].at[4 * x + 2 * y + c], local_sems.at[w]) for w in range(n)]
        first = []
        for w in range(n):
            first.append(copy(w, 0, me, sibling, src=ins[w]))
            first += [copy(w, 1 + j, me, (*chip, c), src=ins[w]) for j, chip in enumerate(chips)]
        passed = [copy(w, 4 + j, (*chip, c), sibling) for w in range(n) for j, chip in enumerate(chips)]
        if stage == "first":
            for cp in mine + first:
                cp.start()
        elif stage == "mid":
            for w in range(n):
                for j, chip in enumerate(chips):
                    copy(w, 1 + j, (*chip, c), me).wait_recv()
                    passed[3 * w + j].start()
        else:
            for w in range(n):
                copy(w, 0, sibling, me).wait_recv()
                for j, chip in enumerate(chips):
                    copy(w, 4 + j, (*chip, 1 - c), me).wait_recv()
            for cp in first + passed:
                cp.wait_send()
            for cp in mine:
                cp.wait()

    return dict(
        ins=list(shards), run=run, stages=("first", "mid", "last"),
        out_shape=[jax.ShapeDtypeStruct((N_DEV,) + a.shape, a.dtype) for a in shards],
        scratch=[pltpu.SemaphoreType.DMA((7 * n,)), pltpu.SemaphoreType.DMA((7 * n,)), pltpu.SemaphoreType.DMA((n,))])


def _scatter_plan(grads):
    n = len(grads)

    def run(stage, ins, outs, sems):
        send_sems, recv_sems = sems
        x, y, c = _coords()
        cps = []
        for w in range(n):
            for mask in range(1, N_DEV):
                px, py, pc = x ^ (mask >> 2), y ^ ((mask >> 1) & 1), c ^ (mask & 1)
                cps.append(pltpu.make_async_remote_copy(
                    src_ref=ins[w].at[4 * px + 2 * py + pc], dst_ref=outs[w].at[mask - 1],
                    send_sem=send_sems.at[7 * w + mask - 1], recv_sem=recv_sems.at[7 * w + mask - 1],
                    device_id=(px, py, pc), device_id_type=MESH))
        for cp in cps:
            if stage == "first":
                cp.start()
            else:
                cp.wait()

    return dict(
        ins=list(grads), run=run, stages=("first", "last"),
        out_shape=[jax.ShapeDtypeStruct((N_DEV - 1,) + a.shape[1:], a.dtype) for a in grads],
        scratch=[pltpu.SemaphoreType.DMA((7 * n,)), pltpu.SemaphoreType.DMA((7 * n,))])


def _run_plan(plan, name):
    n_in, n_out = len(plan["ins"]), len(plan["out_shape"])

    def body(*refs):
        for stage in plan["stages"]:
            plan["run"](stage, refs[:n_in], refs[n_in:n_in + n_out], refs[n_in + n_out:])

    any_spec = pl.BlockSpec(memory_space=pl.ANY)
    return pl.pallas_call(
        body, name=name, in_specs=[any_spec] * n_in, out_specs=[any_spec] * n_out,
        out_shape=plan["out_shape"], scratch_shapes=plan["scratch"],
    )(*plan["ins"])


def _gather_rows(v):
    rows, width = v.shape

    def body(v_ref, out_ref, send_sems, recv_sems):
        x, y, c = _coords()
        out_ref[pl.ds(pl.multiple_of((4 * x + 2 * y + c) * rows, rows), rows), :] = v_ref[...]
        cps = []
        for mask in range(1, N_DEV):
            peer = (x ^ (mask >> 2), y ^ ((mask >> 1) & 1), c ^ (mask & 1))
            dst = out_ref.at[pl.ds(pl.multiple_of((4 * x + 2 * y + c) * rows, rows), rows), :]
            cps.append(pltpu.make_async_remote_copy(
                src_ref=v_ref, dst_ref=dst, send_sem=send_sems.at[mask - 1], recv_sem=recv_sems.at[mask - 1],
                device_id=peer, device_id_type=MESH))
        for cp in cps:
            cp.start()
        for cp in cps:
            cp.wait()

    vmem = pl.BlockSpec(memory_space=pltpu.VMEM)
    return pl.pallas_call(
        body, name="gather_small",
        in_specs=[vmem], out_specs=vmem,
        out_shape=jax.ShapeDtypeStruct((N_DEV * rows, width), F32),
        scratch_shapes=[pltpu.SemaphoreType.DMA((N_DEV - 1,)), pltpu.SemaphoreType.DMA((N_DEV - 1,))],
    )(v)


def _adamw(w, g, m, v):
    m = ADAM_B1 * m + (1.0 - ADAM_B1) * g
    v = ADAM_B2 * v + (1.0 - ADAM_B2) * jnp.square(g)
    m_hat = m / (1.0 - ADAM_B1 ** ADAM_STEP)
    v_hat = v / (1.0 - ADAM_B2 ** ADAM_STEP)
    delta = -ADAM_LR * (m_hat / (jnp.sqrt(v_hat) + ADAM_EPS) + ADAM_WD * w)
    return delta, m, v


def _tile_rows(rows, cols):
    tm = 1 << int(math.log2(max(2 * SUBLANES, (1 << 18) // cols)))
    while rows % tm:
        tm //= 2
    assert tm >= 2 * SUBLANES, (rows, cols)
    return tm


def _reduce_adam(w, m, v, own, got, name):
    r, c = w.shape
    tm = max(2 * SUBLANES, _tile_rows(r, c) // 2)

    def fn(wv, mv, vv, a, *peers):
        g = a
        for pv in peers:
            g = g + pv.astype(F32)
        return (g,) + _adamw(wv, g, mv, vv)

    blk = lambda a: (a, (tm, c), lambda i: (i, 0))
    got_blk = lambda j: (got, (None, tm, c), lambda i, j=j: (j, i, 0))
    return _ew(fn, [blk(w), blk(m), blk(v), blk(own)] + [got_blk(j) for j in range(N_DEV - 1)],
               [((r, c), F32, (tm, c), lambda i: (i, 0))] * 4, (r // tm,), name)


def _small_adam(gathered, params, moms, vels, widths):
    n = len(params)
    total = gathered.shape[1]

    def body(*refs):
        g_ref = refs[0]
        p_refs, m_refs, v_refs = refs[1:1 + n], refs[1 + n:1 + 2 * n], refs[1 + 2 * n:1 + 3 * n]
        sum_ref = refs[1 + 3 * n]
        outs = refs[2 + 3 * n:]
        g = g_ref[0:1, :]
        for p in range(1, N_DEV):
            g = g + g_ref[p * SUBLANES:p * SUBLANES + 1, :]
        sum_ref[...] = g
        off = 0
        for i, wd in enumerate(widths):
            d, m2, v2 = _adamw(p_refs[i][...], g[:, off:off + wd], m_refs[i][...], v_refs[i][...])
            outs[3 * i][...] = d
            outs[3 * i + 1][...] = m2
            outs[3 * i + 2][...] = v2
            off += wd

    vmem = pl.BlockSpec(memory_space=pltpu.VMEM)
    out_shape = [jax.ShapeDtypeStruct((1, total), F32)]
    for wd in widths:
        out_shape += [jax.ShapeDtypeStruct((1, wd), F32)] * 3
    return pl.pallas_call(
        body, name="small_adam",
        in_specs=[vmem] * (1 + 3 * n), out_specs=[vmem] * len(out_shape), out_shape=out_shape,
    )(gathered, *params, *moms, *vels)


def _cast_bf16(a, name):
    r, c = a.shape
    tm = _tile_rows(r, c)
    return _ew(lambda v: v, [(a, (tm, c), lambda i: (i, 0))], [((r, c), BF16, (tm, c), lambda i: (i, 0))],
               (r // tm,), name)[0]


def _fold_loss(parts, name):
    r, c = parts.shape

    def fn(v):
        return jnp.broadcast_to(jnp.sum(jnp.sum(v, axis=0, keepdims=True), axis=1, keepdims=True), (SUBLANES, HEAD))

    return _ew(fn, [_const_spec(parts)], [((SUBLANES, HEAD), F32, (SUBLANES, HEAD), lambda i: (0, 0))], (1,),
               name)[0][0:1]


def kernel(x, p, g_mix, w_in, qn_gain, kn_gain, w_branch_a, w_branch_b, w_out, g_mlp, w_up, w_down, g_ple, w_ple_gate, w_ple_proj, loss_target, m_g_mix, m_w_in, m_qn_gain, m_kn_gain, m_w_branch_a, m_w_branch_b, m_w_out, m_g_mlp, m_w_up, m_w_down, m_g_ple, m_w_ple_gate, m_w_ple_proj, v_g_mix, v_w_in, v_qn_gain, v_kn_gain, v_w_branch_a, v_w_branch_b, v_w_out, v_g_mlp, v_w_up, v_w_down, v_g_ple, v_w_ple_gate, v_w_ple_proj):
    x2 = x[0]
    tgt = loss_target[0]
    s, d = x2.shape
    wd_ = w_branch_a.shape[1]
    nh = wd_ // HEAD
    dff = w_up.shape[1]
    qkv_w = 6 * wd_
    tiles = lambda cols: cols // HEAD

    big = [w_in[0], w_branch_a[0], w_branch_b[0], w_out[0], w_up[0], w_down[0], w_ple_gate[0], w_ple_proj[0]]
    names = ["w_in", "w_branch_a", "w_branch_b", "w_out", "w_up", "w_down", "w_ple_gate", "w_ple_proj"]
    row_sharded = [False, False, False, True, False, True, True, False]
    shards = [_cast_bf16(a, "cast_" + nm) for a, nm in zip(big, names)]
    as_weight = lambda g, rs: g.reshape((1, N_DEV * g.shape[1], g.shape[2])) if rs else g
    as_blocks = lambda g, rs: g.reshape((N_DEV, g.shape[1] // N_DEV, g.shape[2])) if rs else g
    win = _run_plan(_gather_plan(shards[:1]), "all_gather_w_in")[0]

    tm = 1024 if s % 1024 == 0 else s
    tn_of = lambda n: 512 if n % 512 == 0 else (256 if n % 256 == 0 else n)
    tn_in = 256 if win.shape[2] % 256 == 0 else HEAD

    h = _rms_fwd(x2, g_mix, "norm_mix")
    qk_raw = _mm("nn", h, win, tm=tm, tn=tn_in, tk=d, out_dtypes=[F32], name="proj_qk",
                 n_off=0, n_cnt=2 * wd_ // tn_in)[0]
    rest = _mm("nn", h, win, tm=tm, tn=tn_in, tk=d, out_dtypes=[BF16], name="proj_rest",
               n_off=2 * wd_ // tn_in, n_cnt=(win.shape[0] * win.shape[2] - 2 * wd_) // tn_in)[0]
    o_va, o_qb, o_kb, o_vb, o_ga, o_gb = 0, tiles(wd_), tiles(2 * wd_), tiles(3 * wd_), tiles(4 * wd_), tiles(4 * wd_ + d)
    tabs = _rope_tables(s)
    qa = _headnorm_rope(qk_raw, 0, qn_gain, tabs, nh, "rope_q")
    ka = _headnorm_rope(qk_raw, nh, kn_gain, tabs, nh, "rope_k")
    va = rest[:, :wd_]

    outs, lses = [], []
    for dil in DILATIONS:
        o_g, l_g = _dilated_fwd(_phase_major(qa, dil), _phase_major(ka, dil), _phase_major(va, dil), 0, nh,
                                f"dilated_fwd_{dil}")
        outs.append(_token_major(o_g))
        lses.append(_token_major(l_g))
    ya, lse_all = _mix_fwd(outs, lses, "mix_fwd")
    yb, sb_tot, *others = _sb_fwd(rest, o_qb, o_kb, o_vb, nh, "sb_fwd", comm=_gather_plan(shards[1:]))
    wba, wbb, wout, wup, wdown, wgate, wproj = [as_weight(g, rs) for g, rs in zip(others, row_sharded[1:])]

    tn_d = tn_of(wba.shape[2])
    za = _mm("nn", ya, wba, tm=tm, tn=tn_d, tk=wd_, out_dtypes=[BF16], name="branch_a")[0]

    def merge(acc, zav, gav, gbv):
        return _sigmoid(gav.astype(F32)) * zav.astype(F32) + _sigmoid(gbv.astype(F32)) * acc, acc

    merged, zb = _mm("nn", yb, wbb, tm=tm, tn=tn_d, tk=wd_, out_dtypes=[BF16, BF16], name="branch_b_merge",
                     epilogue=merge, extras=[(za, 0), (rest, o_ga * HEAD // tn_d), (rest, o_gb * HEAD // tn_d)])
    x1 = _mm("nn", merged, wout, tm=tm, tn=512, tk=d, out_dtypes=[F32], name="out_proj",
             epilogue=lambda acc, xv: (acc + xv,), extras=[(x2, 0)])[0]

    hm = _rms_fwd(x1, g_mlp, "norm_mlp")
    tn_u = tn_of(wup.shape[2])
    u, act = _mm("nn", hm, wup, tm=tm, tn=tn_u, tk=d, out_dtypes=[BF16, BF16], name="mlp_up",
                 epilogue=lambda acc: (acc, jnp.square(jnp.maximum(acc, 0.0))))
    x3 = _mm("nn", act, wdown, tm=tm, tn=512, tk=min(dff, 2048), out_dtypes=[F32], name="mlp_down",
             epilogue=lambda acc, xv: (acc + xv,), extras=[(x1, 0)])[0]

    hp = _rms_fwd(x3, g_ple, "norm_ple")
    p_b = _cast_bf16(p[0, 0], "cast_p")
    pp = _mm("nn", p_b, wproj, tm=tm, tn=tn_of(wproj.shape[2]), tk=p_b.shape[1], out_dtypes=[BF16],
             name="ple_proj")[0]

    def head(acc, ppv, xv, tv):
        sg = _sigmoid(acc)
        ppf = ppv.astype(F32)
        err = xv + ppf * sg - tv
        dy = err / d
        sq = jnp.square(err)
        return dy, dy * sg, dy * ppf * sg * (1.0 - sg), sq.reshape(-1, SUBLANES, sq.shape[-1]).sum(axis=0)

    n_i = s // tm
    dy, d_pp, d_gt, sq_parts = _mm(
        "nn", hp, wgate, tm=tm, tn=512, tk=d, out_dtypes=[F32, BF16, BF16], name="ple_gate_loss", epilogue=head,
        extras=[(pp, 0), (x3, 0), (tgt, 0)],
        extra_outs=[((n_i * SUBLANES, d), F32, (SUBLANES, 512), lambda i, j: (i, j))])
    loss_vec = _fold_loss(sq_parts, "loss_fold") * 0.5 / d

    both = [F32, BF16]
    g_wproj = _mm("tn", p_b, d_pp, tm=p_b.shape[1], tn=tn_of(wproj.shape[2]), tk=tm, out_dtypes=both,
                  name="grad_w_ple_proj", out_nb=N_DEV)
    g_wgate = _mm("tn", hp, d_gt, tm=512, tn=1024, tk=tm, out_dtypes=both, name="grad_w_ple_gate")
    d_hp = _mm("nt", d_gt, wgate, tm=tm, tn=512, tk=min(d, 1024), out_dtypes=[F32], name="d_hp")[0]
    dx3, dx3_b, g_gple = _rms_bwd(d_hp, x3, g_ple, dy, "norm_ple_bwd")

    d_u = _mm("nt", dx3_b, wdown, tm=tm, tn=1024, tk=min(d, 1024), out_dtypes=[BF16], name="d_u",
              epilogue=lambda acc, uv: (acc * (2.0 * jnp.maximum(uv.astype(F32), 0.0)),), extras=[(u, 0)])[0]
    g_wdown = _mm("tn", act, dx3_b, tm=1024, tn=1024, tk=tm, out_dtypes=both, name="grad_w_down")
    g_wup = _mm("tn", hm, d_u, tm=1024, tn=wup.shape[2], tk=tm, out_dtypes=both, name="grad_w_up", out_nb=N_DEV)
    d_hm = _mm("nt", d_u, wup, tm=tm, tn=1024, tk=wup.shape[2], out_dtypes=[F32], name="d_hm")[0]
    dx1, dx1_b, g_gmlp = _rms_bwd(d_hm, x1, g_mlp, dx3, "norm_mlp_bwd")

    def unmerge(acc, gav, gbv, zav, zbv):
        sa, sb = _sigmoid(gav.astype(F32)), _sigmoid(gbv.astype(F32))
        return acc * sa, acc * sb, acc * zav.astype(F32) * sa * (1.0 - sa), acc * zbv.astype(F32) * sb * (1.0 - sb)

    d_za, d_zb, d_ga, d_gb = _mm(
        "nt", dx1_b, wout, tm=tm, tn=512, tk=min(d, 1024), out_dtypes=[BF16] * 4, name="d_merged", epilogue=unmerge,
        extras=[(rest, o_ga * HEAD // 512), (rest, o_gb * HEAD // 512), (za, 0), (zb, 0)])
    g_wout = _mm("tn", merged, dx1_b, tm=512, tn=1024, tk=tm, out_dtypes=both, name="grad_w_out")
    g_wba = _mm("tn", ya, d_za, tm=wd_, tn=wba.shape[2], tk=tm, out_dtypes=both, name="grad_w_branch_a",
                out_nb=N_DEV)
    g_wbb = _mm("tn", yb, d_zb, tm=wd_, tn=wbb.shape[2], tk=tm, out_dtypes=both, name="grad_w_branch_b",
                out_nb=N_DEV)
    d_ya = _mm("nt", d_za, wba, tm=tm, tn=wd_, tk=wba.shape[2], out_dtypes=[F32], name="d_ya")[0]
    d_yb = _mm("nt", d_zb, wbb, tm=tm, tn=wd_, tk=wbb.shape[2], out_dtypes=[F32], name="d_yb")[0]

    grads = [None, g_wba, g_wbb, g_wout, g_wup, g_wdown, g_wgate, g_wproj]
    early = _scatter_plan([as_blocks(g[1], rs) for g, rs in zip(grads[1:], row_sharded[1:])])
    d_qb, d_kb, d_vb, *got = _sb_bwd(rest, o_qb, o_kb, o_vb, d_yb, sb_tot, nh, "sb_bwd", comm=early)
    delta = _head_dot(d_ya, ya, nh, "mix_delta")
    dqs, dks, dvs = [], [], []
    for dil in DILATIONS:
        dq_g, dk_g, dv_g = _dilated_bwd(
            _phase_major(qa, dil), _phase_major(ka, dil), _phase_major(va, dil), 0, _phase_major(d_ya, dil),
            _phase_major(lse_all, dil), _phase_major(delta, dil), nh, f"dilated_bwd_{dil}")
        dqs.append(_token_major(dq_g))
        dks.append(_token_major(dk_g))
        dvs.append(_token_major(dv_g))
    d_qa, g_qn = _headnorm_rope_bwd(dqs, qk_raw, 0, qn_gain, tabs, nh, "rope_q_bwd")
    d_ka, g_kn = _headnorm_rope_bwd(dks, qk_raw, nh, kn_gain, tabs, nh, "rope_k_bwd")
    tmr = 256
    d_va = _ew(lambda a, b, c: a + b + c, [_row_spec(a, tmr) for a in dvs],
               [((s, wd_), BF16, (tmr, wd_), lambda i: (i, 0))], (s // tmr,), "sum_dv")[0]
    d_proj = jnp.concatenate([d_qa, d_ka, d_va, d_qb, d_kb.astype(BF16), d_vb.astype(BF16), d_ga, d_gb], axis=1)

    grads[0] = _mm("tn", h, d_proj, tm=1024, tn=win.shape[2], tk=tm, out_dtypes=both, name="grad_w_in",
                   out_nb=N_DEV)
    d_h, got_in = _mm("nt", d_proj, win, tm=tm, tn=1024, tk=win.shape[2], out_dtypes=[F32], name="d_h",
                      comm=_scatter_plan([grads[0][1]]))
    got = [got_in] + got
    dx, _, g_gmix = _rms_bwd(d_h, x2, g_mix, dx1, "norm_mix_bwd")

    cx, cy, cc = _coords()
    me = 4 * cx + 2 * cy + cc
    moms = [m_w_in, m_w_branch_a, m_w_branch_b, m_w_out, m_w_up, m_w_down, m_w_ple_gate, m_w_ple_proj]
    vels = [v_w_in, v_w_branch_a, v_w_branch_b, v_w_out, v_w_up, v_w_down, v_w_ple_gate, v_w_ple_proj]
    big_out = {}
    for i, nm in enumerate(names):
        own = lax.dynamic_index_in_dim(as_blocks(grads[i][0], row_sharded[i]), me, axis=0, keepdims=False)
        big_out[nm] = [a[None] for a in _reduce_adam(big[i], moms[i][0], vels[i][0], own, got[i], "adam_" + nm)]

    small_names = ["g_mix", "qn_gain", "kn_gain", "g_mlp", "g_ple"]
    small_p = [g_mix, qn_gain, kn_gain, g_mlp, g_ple]
    small_m = [m_g_mix, m_qn_gain, m_kn_gain, m_g_mlp, m_g_ple]
    small_v = [v_g_mix, v_qn_gain, v_kn_gain, v_g_mlp, v_g_ple]
    small_g = [g_gmix, g_qn, g_kn, g_gmlp, g_gple]
    widths = [a.shape[1] for a in small_p]
    vec = jnp.concatenate(small_g + [loss_vec], axis=1)
    vec = jnp.pad(vec, ((0, SUBLANES - 1), (0, 0)))
    res = _small_adam(_gather_rows(vec), small_p, small_m, small_v, widths)
    summed = res[0]
    small_out, off = {}, 0
    for i, nm in enumerate(small_names):
        small_out[nm] = [summed[:, off:off + widths[i]]] + list(res[1 + 3 * i:4 + 3 * i])
        off += widths[i]
    loss = summed[0, off]

    order = ["g_mix", "w_in", "qn_gain", "kn_gain", "w_branch_a", "w_branch_b", "w_out", "g_mlp", "w_up", "w_down",
             "g_ple", "w_ple_gate", "w_ple_proj"]
    table = {**big_out, **small_out}
    result = [loss, dx[None]]
    for kind in range(4):
        result += [table[nm][kind] for nm in order]
    return tuple(result)
```

```python
import functools
import math

import jax
import jax.numpy as jnp
from jax import lax
from jax.experimental import pallas as pl
from jax.experimental.pallas import tpu as pltpu

F32 = jnp.float32
BF16 = jnp.bfloat16
MESH = pl.DeviceIdType.MESH

HEAD = 128
WINDOW = 128
DILATIONS = (1, 4, 16)
ROT = HEAD // 4
ROPE_THETA = 500000.0
EPS = 1e-6
NEG = -1e30
N_DEV = 8

ADAM_LR = 0.001
ADAM_B1 = 0.9
ADAM_B2 = 0.999
ADAM_EPS = 1e-08
ADAM_WD = 0.01
ADAM_STEP = 10

SUBLANES = 8
VMEM_CAP_MB = 56

NT_DIMS = (((1,), (1,)), ((), ()))
TN_DIMS = (((0,), (0,)), ((), ()))


def _params(semantics, vmem_mb):
    return pltpu.CompilerParams(dimension_semantics=semantics, vmem_limit_bytes=min(vmem_mb, VMEM_CAP_MB) << 20)


def _sigmoid(x):
    return 1.0 / (1.0 + jnp.exp(-x))


def _split_bf16(x):
    hi = x.astype(BF16)
    lo = (x - hi.astype(F32)).astype(BF16)
    return hi, lo


class _Host:
    def __init__(self, comm, grid):
        self.comm, self.grid = comm, grid
        any_spec = pl.BlockSpec(memory_space=pl.ANY)
        self.ins = list(comm["ins"]) if comm else []
        self.out_shape = list(comm["out_shape"]) if comm else []
        self.scratch = list(comm["scratch"]) if comm else []
        self.in_specs = [any_spec] * len(self.ins)
        self.out_specs = [any_spec] * len(self.out_shape)

    def split(self, refs, n_in, n_out):
        pos = n_in
        c_in = refs[pos:pos + len(self.ins)]
        pos += len(self.ins)
        outs = refs[pos:pos + n_out]
        pos += n_out
        c_out = refs[pos:pos + len(self.out_shape)]
        pos += len(self.out_shape)
        own = len(refs) - pos - len(self.scratch)
        return (c_in, c_out, refs[pos + own:]), (outs, refs[pos:pos + own])

    def stage(self, which, comm_refs):
        if self.comm is None or which not in self.comm["stages"]:
            return
        grid = self.grid
        at = {"first": [0] * len(grid), "mid": [grid[0] // 2] + [0] * (len(grid) - 1),
              "last": [g - 1 for g in grid]}[which]
        cond = functools.reduce(jnp.logical_and, [pl.program_id(ax) == v for ax, v in enumerate(at)])

        @pl.when(cond)
        def _():
            self.comm["run"](which, *comm_refs)


def _ew(fn, ins, outs, grid, name, colsums=(), vmem_mb=40):
    n_in, n_out, n_cs = len(ins), len(outs), len(colsums)
    steps = math.prod(grid)

    def body(*refs):
        in_refs = refs[:n_in]
        out_refs = refs[n_in:n_in + n_out]
        cs_refs = refs[n_in + n_out:n_in + n_out + n_cs]
        acc_refs = refs[n_in + n_out + n_cs:]
        vals = fn(*[r[...] for r in in_refs])
        if not isinstance(vals, tuple):
            vals = (vals,)
        for r, v in zip(out_refs, vals[:n_out]):
            r[...] = v.astype(r.dtype)
        if n_cs:
            step = pl.program_id(0)
            for ax in range(1, len(grid)):
                step = step * grid[ax] + pl.program_id(ax)
            for acc, cs, v in zip(acc_refs, cs_refs, vals[n_out:]):
                part = v.reshape(-1, SUBLANES, v.shape[-1]).sum(axis=0)

                @pl.when(step == 0)
                def _(acc=acc, part=part):
                    acc[...] = part

                @pl.when(step > 0)
                def _(acc=acc, part=part):
                    acc[...] += part

                @pl.when(step == steps - 1)
                def _(acc=acc, cs=cs):
                    cs[...] = acc[...].sum(axis=0, keepdims=True)

    out_shape = [jax.ShapeDtypeStruct(s, d) for s, d, _, _ in outs]
    out_specs = [pl.BlockSpec(b, m) for _, _, b, m in outs]
    for w in colsums:
        out_shape.append(jax.ShapeDtypeStruct((1, w), F32))
        out_specs.append(pl.BlockSpec((1, w), lambda *_: (0, 0)))
    sem = ("arbitrary",) * len(grid) if n_cs else ("parallel",) * len(grid)
    res = pl.pallas_call(
        body, name=name, grid=grid,
        in_specs=[pl.BlockSpec(b, m) for _, b, m in ins],
        out_specs=out_specs, out_shape=out_shape,
        scratch_shapes=[pltpu.VMEM((SUBLANES, w), F32) for w in colsums],
        compiler_params=_params(sem, vmem_mb),
    )(*[a for a, _, _ in ins])
    return res


def _row_spec(a, tm):
    return (a, (tm, a.shape[1]), lambda i: (i, 0))


def _const_spec(a):
    return (a, a.shape, lambda *_: (0,) * a.ndim)


def _mm(mode, a, b, *, tm, tn, tk, out_dtypes, name, epilogue=None, extras=(), n_off=0, n_cnt=None,
        out_nb=1, extra_outs=(), vmem_mb=52, comm=None):
    if mode == "nn":
        m, kdim = a.shape
        nb, _, n = b.shape
        npb = n // tn
        ncols = nb * n
        n_tiles = (ncols // tn) if n_cnt is None else n_cnt
        a_spec = pl.BlockSpec((tm, tk), lambda i, j, k: (i, k))
        b_spec = pl.BlockSpec((None, tk, tn), lambda i, j, k: ((j + n_off) // npb, k, (j + n_off) % npb))
        dims = (((1,), (0,)), ((), ()))
    elif mode == "nt":
        m, kdim = a.shape
        nb, nout, n = b.shape
        kpb = n // tk
        n_tiles = nout // tn
        a_spec = pl.BlockSpec((tm, tk), lambda i, j, k: (i, k))
        b_spec = pl.BlockSpec((None, tn, tk), lambda i, j, k: (k // kpb, j, k % kpb))
        dims = NT_DIMS
    else:
        kdim, m = a.shape
        ncols = b.shape[1]
        n_tiles = ncols // tn
        a_spec = pl.BlockSpec((tk, tm), lambda i, j, k: (k, i))
        b_spec = pl.BlockSpec((tk, tn), lambda i, j, k: (k, j))
        dims = TN_DIMS
    nk = kdim // tk
    assert kdim % tk == 0 and m % tm == 0
    grid = (m // tm, n_tiles, nk)
    n_ex, n_out = len(extras), len(out_dtypes) + len(extra_outs)
    host = _Host(comm, grid)

    def body(*refs):
        a_ref, b_ref = refs[0], refs[1]
        ex_refs = refs[2:2 + n_ex]
        comm_refs, (out_refs, scratch) = host.split(refs, 2 + n_ex, n_out)
        host.stage("first", comm_refs)

        def finish(acc):
            vals = (acc,) * n_out if epilogue is None else epilogue(acc, *[r[...] for r in ex_refs])
            for r, v in zip(out_refs, vals):
                r[...] = v.astype(r.dtype)

        def part():
            return lax.dot_general(a_ref[...], b_ref[...], dims, preferred_element_type=F32)

        if nk == 1:
            finish(part())
        else:
            acc_ref = scratch[0]
            k = pl.program_id(2)

            @pl.when(k == 0)
            def _():
                acc_ref[...] = part()

            @pl.when((k > 0) & (k < nk - 1))
            def _():
                acc_ref[...] += part()

            @pl.when(k == nk - 1)
            def _():
                finish(acc_ref[...] + part())

        host.stage("mid", comm_refs)
        host.stage("last", comm_refs)

    if mode == "tn":
        npo = (ncols // out_nb) // tn
        out_shape = [jax.ShapeDtypeStruct((out_nb, m, ncols // out_nb), d) for d in out_dtypes]
        out_specs = [pl.BlockSpec((None, tm, tn), lambda i, j, k: (j // npo, i, j % npo)) for _ in out_dtypes]
    else:
        out_shape = [jax.ShapeDtypeStruct((m, n_tiles * tn), d) for d in out_dtypes]
        out_specs = [pl.BlockSpec((tm, tn), lambda i, j, k: (i, j)) for _ in out_dtypes]
    for s, d, blk, imap in extra_outs:
        out_shape.append(jax.ShapeDtypeStruct(s, d))
        out_specs.append(pl.BlockSpec(blk, lambda i, j, k, imap=imap: imap(i, j)))
    ex_specs = [pl.BlockSpec((tm, tn), lambda i, j, k, off=off: (i, j + off)) for _, off in extras]
    sem = ("parallel", "parallel", "arbitrary") if comm is None else ("arbitrary",) * 3
    return pl.pallas_call(
        body, name=name, grid=grid,
        in_specs=[a_spec, b_spec] + ex_specs + host.in_specs,
        out_specs=out_specs + host.out_specs, out_shape=out_shape + host.out_shape,
        scratch_shapes=([pltpu.VMEM((tm, tn), F32)] if nk > 1 else []) + host.scratch,
        compiler_params=_params(sem, vmem_mb),
    )(a, b, *[e for e, _ in extras], *host.ins)


def _rms_fwd(x, g, name):
    tm = 256

    def fn(xv, gv):
        r = lax.rsqrt(jnp.mean(xv * xv, axis=-1, keepdims=True) + EPS)
        return xv * r * gv

    return _ew(fn, [_row_spec(x, tm), _const_spec(g)], [(x.shape, BF16, (tm, x.shape[1]), lambda i: (i, 0))],
               (x.shape[0] // tm,), name)[0]


def _rms_bwd(dh, x, g, res, name):
    tm = 256
    d = x.shape[1]

    def fn(dhv, xv, gv, rv):
        r = lax.rsqrt(jnp.mean(xv * xv, axis=-1, keepdims=True) + EPS)
        xh = xv * r
        dyg = dhv * gv
        dx = rv + r * (dyg - xh * jnp.mean(dyg * xh, axis=-1, keepdims=True))
        return dx, dx, dhv * xh

    spec = lambda dt: (x.shape, dt, (tm, d), lambda i: (i, 0))
    return _ew(fn, [_row_spec(dh, tm), _row_spec(x, tm), _const_spec(g), _row_spec(res, tm)],
               [spec(F32), spec(BF16)], (x.shape[0] // tm,), name, colsums=(d,))


def _rope_tables(s):
    half = ROT // 2
    pos = jnp.arange(s, dtype=F32)
    inv = ROPE_THETA ** (-jnp.arange(0, ROT, 2, dtype=F32) / ROT)
    ang = pos[:, None] * inv[None, :]
    cos, sin = jnp.cos(ang), jnp.sin(ang)
    pad = jnp.zeros((s, HEAD - ROT), F32)
    c = jnp.concatenate([cos, cos, pad + 1.0], axis=1)
    a = jnp.concatenate([-sin, jnp.zeros_like(sin), pad], axis=1)
    b = jnp.concatenate([jnp.zeros_like(sin), sin, pad], axis=1)
    return c, a, b


def _headnorm_rope(proj, off, gain, tabs, n_heads, name):
    tm = 512
    s = proj.shape[0]

    def fn(xv, gv, c, a, b):
        r = lax.rsqrt(jnp.mean(xv * xv, axis=-1, keepdims=True) + EPS)
        y = xv * r * gv
        return c * y + a * pltpu.roll(y, HEAD - ROT // 2, 1) + b * pltpu.roll(y, ROT // 2, 1)

    tab = lambda t: (t, (tm, HEAD), lambda i, h: (i, 0))
    return _ew(fn, [(proj, (tm, HEAD), lambda i, h: (i, off + h)), (gain, (1, HEAD), lambda i, h: (0, 0))]
               + [tab(t) for t in tabs],
               [((s, n_heads * HEAD), BF16, (tm, HEAD), lambda i, h: (i, h))], (s // tm, n_heads), name)[0]


def _headnorm_rope_bwd(dys, proj, off, gain, tabs, n_heads, name):
    tm = 512
    s = proj.shape[0]
    n_dy = len(dys)

    def fn(*vals):
        dy = vals[0]
        for v in vals[1:n_dy]:
            dy = dy + v
        xv, gv, c, a, b = vals[n_dy:]
        dn = c * dy + pltpu.roll(a * dy, ROT // 2, 1) + pltpu.roll(b * dy, HEAD - ROT // 2, 1)
        r = lax.rsqrt(jnp.mean(xv * xv, axis=-1, keepdims=True) + EPS)
        xh = xv * r
        dyg = dn * gv
        dx = r * (dyg - xh * jnp.mean(dyg * xh, axis=-1, keepdims=True))
        return dx, dn * xh

    tab = lambda t: (t, (tm, HEAD), lambda i, h: (i, 0))
    return _ew(fn, [(d, (tm, HEAD), lambda i, h: (i, h)) for d in dys]
               + [(proj, (tm, HEAD), lambda i, h: (i, off + h)), (gain, (1, HEAD), lambda i, h: (0, 0))]
               + [tab(t) for t in tabs],
               [((s, n_heads * HEAD), BF16, (tm, HEAD), lambda i, h: (i, h))], (s // tm, n_heads), name,
               colsums=(HEAD,))


def _phase_major(a, d):
    s, w = a.shape
    if d == 1:
        return a.reshape(1, s, w)
    return a.reshape(s // d, d, w).transpose(1, 0, 2)


def _token_major(a):
    d, m, w = a.shape
    if d == 1:
        return a.reshape(m, w)
    return a.transpose(1, 0, 2).reshape(m * d, w)


def _dil_tq(m):
    return min(1024, m)


def _dilated_fwd(q, k, v, voff, n_heads, name):
    d, m, _ = q.shape
    tq = _dil_tq(m)
    nq = m // tq
    per = tq // WINDOW
    scale = HEAD ** -0.5

    def body(q_ref, kc_ref, kp_ref, vc_ref, vp_ref, o_ref, l_ref):
        n = pl.program_id(2)
        kk = jnp.concatenate([kp_ref[...], kc_ref[...]], axis=0)
        vv = jnp.concatenate([vp_ref[...], vc_ref[...]], axis=0)
        row = lax.broadcasted_iota(jnp.int32, (WINDOW, 2 * WINDOW), 0)
        col = lax.broadcasted_iota(jnp.int32, (WINDOW, 2 * WINDOW), 1)
        band = (col >= row) & (col <= row + WINDOW)
        q = q_ref[...]
        rows = [slice(b * WINDOW, (b + 1) * WINDOW) for b in range(per)]
        keys = [slice(b * WINDOW, (b + 2) * WINDOW) for b in range(per)]
        s = [lax.dot_general(q[rows[b]], kk[keys[b]], NT_DIMS, preferred_element_type=F32) * scale
             for b in range(per)]
        es, outs, lses = [], [], []
        for b in range(per):
            valid = band if b else band & ((n > 0) | (col >= WINDOW))
            sb = jnp.where(valid, s[b], NEG)
            mx = jnp.max(sb, axis=-1, keepdims=True)
            e = jnp.exp(sb - mx)
            den = jnp.sum(e, axis=-1, keepdims=True)
            es.append((e.astype(BF16), den))
            lses.append(jnp.broadcast_to(mx + jnp.log(den), (WINDOW, HEAD)))
        for b in range(per):
            outs.append(jnp.dot(es[b][0], vv[keys[b]], preferred_element_type=F32) / es[b][1])
        o_ref[...] = jnp.concatenate(outs, axis=0)
        l_ref[...] = jnp.concatenate(lses, axis=0)

    cur = lambda off: pl.BlockSpec((None, tq, HEAD), lambda r, h, n: (r, n, off + h))
    prev = lambda off: pl.BlockSpec((None, WINDOW, HEAD), lambda r, h, n: (r, jnp.maximum(n * per - 1, 0), off + h))
    out = jax.ShapeDtypeStruct((d, m, n_heads * HEAD), F32)
    return pl.pallas_call(
        body, name=name, grid=(d, n_heads, nq),
        in_specs=[cur(0), cur(0), prev(0), cur(voff), prev(voff)],
        out_specs=[cur(0), cur(0)], out_shape=[out, out],
        compiler_params=_params(("parallel",) * 3, 32),
    )(q, k, k, v, v)


def _dilated_bwd(q, k, v, voff, dy, lse, delta, n_heads, name):
    d, m, _ = q.shape
    tq = _dil_tq(m)
    nq = m // tq
    per = tq // WINDOW
    last_blk = m // WINDOW - 1
    scale = HEAD ** -0.5

    def body(qc_ref, qn_ref, kc_ref, kp_ref, vc_ref, vp_ref, dyc_ref, dyn_ref, lc_ref, ln_ref, dc_ref, dn_ref,
             dq_ref, dk_ref, dv_ref):
        n = pl.program_id(2)
        kk = jnp.concatenate([kp_ref[...], kc_ref[...]], axis=0)
        vv = jnp.concatenate([vp_ref[...], vc_ref[...]], axis=0)
        qq = jnp.concatenate([qc_ref[...], qn_ref[...]], axis=0)
        dyy = jnp.concatenate([dyc_ref[...], dyn_ref[...]], axis=0).astype(BF16)
        ll = jnp.concatenate([lc_ref[...], ln_ref[...]], axis=0)
        dd = jnp.concatenate([dc_ref[...], dn_ref[...]], axis=0)
        row = lax.broadcasted_iota(jnp.int32, (WINDOW, 2 * WINDOW), 0)
        col = lax.broadcasted_iota(jnp.int32, (WINDOW, 2 * WINDOW), 1)
        band = (col >= row) & (col <= row + WINDOW)
        rows = [slice(b * WINDOW, (b + 1) * WINDOW) for b in range(per)]
        wide = [slice(b * WINDOW, (b + 2) * WINDOW) for b in range(per)]
        nt = lambda a, b: lax.dot_general(a, b, NT_DIMS, preferred_element_type=F32)
        s = [nt(qq[rows[b]], kk[wide[b]]) * scale for b in range(per)]
        dp = [nt(dyy[rows[b]], vv[wide[b]]) for b in range(per)]
        ds = []
        for b in range(per):
            valid = band if b else band & ((n > 0) | (col >= WINDOW))
            p = jnp.where(valid, jnp.exp(jnp.minimum(s[b] - ll[rows[b]][:, :1], 0.0)), 0.0)
            ds.append((p * (dp[b] - dd[rows[b]][:, :1]) * scale).astype(BF16))
        dq = [jnp.dot(ds[b], kk[wide[b]], preferred_element_type=F32) for b in range(per)]
        kc, vc = kc_ref[...], vc_ref[...]
        s2 = [nt(kc[rows[b]], qq[wide[b]]) * scale for b in range(per)]
        dp2 = [nt(vc[rows[b]], dyy[wide[b]]) for b in range(per)]
        ds2, p2 = [], []
        for b in range(per):
            valid = band if b < per - 1 else band & ((n < nq - 1) | (col < WINDOW))
            pb = jnp.where(valid, jnp.exp(jnp.minimum(s2[b] - ll[wide[b]].T, 0.0)), 0.0)
            ds2.append((pb * (dp2[b] - dd[wide[b]].T) * scale).astype(BF16))
            p2.append(pb.astype(BF16))
        dk = [jnp.dot(ds2[b], qq[wide[b]], preferred_element_type=F32) for b in range(per)]
        dv = [jnp.dot(p2[b], dyy[wide[b]], preferred_element_type=F32) for b in range(per)]
        dq_ref[...] = jnp.concatenate(dq, axis=0)
        dk_ref[...] = jnp.concatenate(dk, axis=0)
        dv_ref[...] = jnp.concatenate(dv, axis=0)

    cur = lambda off: pl.BlockSpec((None, tq, HEAD), lambda r, h, n: (r, n, off + h))
    prev = lambda off: pl.BlockSpec((None, WINDOW, HEAD), lambda r, h, n: (r, jnp.maximum(n * per - 1, 0), off + h))
    nxt = lambda off: pl.BlockSpec((None, WINDOW, HEAD),
                                   lambda r, h, n: (r, jnp.minimum((n + 1) * per, last_blk), off + h))
    out = jax.ShapeDtypeStruct((d, m, n_heads * HEAD), F32)
    return pl.pallas_call(
        body, name=name, grid=(d, n_heads, nq),
        in_specs=[cur(0), nxt(0), cur(0), prev(0), cur(voff), prev(voff), cur(0), nxt(0), cur(0), nxt(0), cur(0),
                  nxt(0)],
        out_specs=[cur(0)] * 3, out_shape=[out] * 3,
        compiler_params=_params(("parallel",) * 3, 40),
    )(q, q, k, k, v, v, dy, dy, lse, lse, delta, delta)


def _mix_fwd(os_, ls_, name):
    tm = 256
    s, w = os_[0].shape
    n = len(os_)

    def fn(*vals):
        o, l = vals[:n], vals[n:]
        mx = functools.reduce(jnp.maximum, l)
        e = [jnp.exp(x - mx) for x in l]
        den = functools.reduce(jnp.add, e)
        y = functools.reduce(jnp.add, [ei * oi for ei, oi in zip(e, o)]) / den
        return y, mx + jnp.log(den)

    return _ew(fn, [_row_spec(a, tm) for a in list(os_) + list(ls_)],
               [((s, w), BF16, (tm, w), lambda i: (i, 0)), ((s, w), F32, (tm, w), lambda i: (i, 0))],
               (s // tm,), name)


def _head_dot(dy, y, n_heads, name):
    tm = 512
    s = dy.shape[0]

    def fn(a, b):
        return jnp.broadcast_to(jnp.sum(a * b.astype(F32), axis=-1, keepdims=True), a.shape)

    blk = lambda a: (a, (tm, HEAD), lambda i, h: (i, h))
    return _ew(fn, [blk(dy), blk(y)], [((s, n_heads * HEAD), F32, (tm, HEAD), lambda i, h: (i, h))],
               (s // tm, n_heads), name)[0]


SB_TQ = 1024
SB_TB = 512
SB_TK = 256


def _softplus(z):
    return jnp.where(z > 20.0, z, jnp.log(1.0 + jnp.exp(z)))


def _tri(t, cmp):
    rows = lax.broadcasted_iota(jnp.int32, (2 * t, t), 0)
    cols = lax.broadcasted_iota(jnp.int32, (2 * t, t), 1)
    return jnp.where(cmp(jnp.where(rows >= t, rows - t, rows), cols), 1.0, 0.0).astype(BF16)


def _tri_sum(x, tri):
    return jnp.dot(jnp.concatenate(_split_bf16(x), axis=1), tri, preferred_element_type=F32)


def _sb_fwd(qkv, qoff, koff, voff, n_heads, name, comm=None):
    s = qkv.shape[0]
    tq, tb, tk = min(SB_TQ, s), SB_TB, SB_TK
    scale = HEAD ** -0.5
    host = _Host(comm, (n_heads, s // tq))

    def body(*refs):
        q_ref, k_ref, v_ref = refs[:3]
        comm_refs, ((o_ref, tot_ref), _) = host.split(refs, 3, 2)
        host.stage("first", comm_refs)
        i = pl.program_id(1)
        q = q_ref[...]
        after = _tri(tk, lambda a, b: a > b)
        row = lax.broadcasted_iota(jnp.int32, (tq, tk), 0)
        col = lax.broadcasted_iota(jnp.int32, (tq, tk), 1)

        def block(base, carry, o, diag_off):
            halves = list(reversed(range(tb // tk)))
            starts = [pl.multiple_of(base + h * tk, tk) for h in halves]
            masks = [None if diag_off is None else (col + (diag_off + h * tk) < row) for h in halves]
            z = [lax.dot_general(q, k_ref[pl.ds(st, tk), :], NT_DIMS, preferred_element_type=F32) * scale
                 for st in starts]
            sp = [_softplus(zz) for zz in z]
            logsig = [zz - ss for zz, ss in zip(z, sp)]
            sp = [ss if m is None else jnp.where(m, ss, 0.0) for ss, m in zip(sp, masks)]
            sfx = [_tri_sum(ss, after) for ss in sp]
            probs = []
            for ls, sx, ss, m in zip(logsig, sfx, sp, masks):
                a = jnp.exp(ls - carry - sx)
                probs.append((a if m is None else jnp.where(m, a, 0.0)).astype(BF16))
                carry = carry + jnp.sum(ss, axis=-1, keepdims=True)
            for a, st in zip(probs, starts):
                o = o + jnp.dot(a, v_ref[pl.ds(st, tk), :], preferred_element_type=F32)
            return carry, o

        carry, o = jnp.zeros((tq, 1), F32), jnp.zeros((tq, HEAD), F32)
        for b in reversed(range(tq // tb)):
            carry, o = block(i * tq + b * tb, carry, o, b * tb)
        below = i * (tq // tb)
        carry, o = lax.fori_loop(0, below, lambda jj, co: block((below - 1 - jj) * tb, co[0], co[1], None),
                                 (carry, o))
        o_ref[...] = o.astype(o_ref.dtype)
        tot_ref[...] = jnp.broadcast_to(carry, (tq, HEAD))
        host.stage("mid", comm_refs)
        host.stage("last", comm_refs)

    t = tq
    full = lambda off: pl.BlockSpec((s, HEAD), lambda h, i: (0, off + h))
    tile_spec = lambda off: pl.BlockSpec((t, HEAD), lambda h, i: (i, off + h))
    out = lambda dt: jax.ShapeDtypeStruct((s, n_heads * HEAD), dt)
    return pl.pallas_call(
        body, name=name, grid=(n_heads, s // t),
        in_specs=[tile_spec(qoff), full(koff), full(voff)] + host.in_specs,
        out_specs=[tile_spec(0), tile_spec(0)] + host.out_specs,
        out_shape=[out(BF16), out(F32)] + host.out_shape,
        scratch_shapes=host.scratch,
        compiler_params=_params(("parallel" if comm is None else "arbitrary", "arbitrary"), 40),
    )(qkv, qkv, qkv, *host.ins)


def _sb_bwd(qkv, qoff, koff, voff, do, tot, n_heads, name, comm=None):
    s = qkv.shape[0]
    tq, tb, tk = min(SB_TQ, s), SB_TB, SB_TK
    scale = HEAD ** -0.5
    host = _Host(comm, (n_heads, s // tq))

    def body(*refs):
        q_ref, k_ref, v_ref, do_ref, tot_ref = refs[:5]
        comm_refs, ((dq_ref, dk_ref, dv_ref), _) = host.split(refs, 5, 3)
        host.stage("first", comm_refs)
        i = pl.program_id(1)

        @pl.when(i == 0)
        def _():
            dk_ref[...] = jnp.zeros_like(dk_ref)
            dv_ref[...] = jnp.zeros_like(dv_ref)

        q = q_ref[...]
        do_b = do_ref[...].astype(BF16)
        total = tot_ref[:, :1]
        upto = _tri(tk, lambda a, b: a <= b)
        before = _tri(tk, lambda a, b: a < b)
        row = lax.broadcasted_iota(jnp.int32, (tq, tk), 0)
        col = lax.broadcasted_iota(jnp.int32, (tq, tk), 1)

        def block(base, lsum, psum, dq, diag_off):
            halves = list(range(tb // tk))
            starts = [pl.multiple_of(base + h * tk, tk) for h in halves]
            masks = [None if diag_off is None else (col + (diag_off + h * tk) < row) for h in halves]
            keep = lambda x, m: x if m is None else jnp.where(m, x, 0.0)
            ks = [k_ref[pl.ds(st, tk), :] for st in starts]
            z = [lax.dot_general(q, kj, NT_DIMS, preferred_element_type=F32) * scale for kj in ks]
            da = [lax.dot_general(do_b, v_ref[pl.ds(st, tk), :], NT_DIMS, preferred_element_type=F32)
                  for st in starts]
            sp = [_softplus(zz) for zz in z]
            logsig = [zz - ss for zz, ss in zip(z, sp)]
            sp = [keep(ss, m) for ss, m in zip(sp, masks)]
            pre = [_tri_sum(ss, upto) for ss in sp]
            probs, p = [], []
            for ls, px, ss, m, dd in zip(logsig, pre, sp, masks, da):
                a = keep(jnp.exp(ls - (total - lsum - px)), m)
                probs.append(a.astype(BF16))
                p.append(a * dd)
                lsum = lsum + jnp.sum(ss, axis=-1, keepdims=True)
            cs = [_tri_sum(pp, before) for pp in p]
            dzs = []
            for ls, pp, cc, m in zip(logsig, p, cs, masks):
                dzs.append(keep((pp - (pp + (psum + cc)) * jnp.exp(ls)) * scale, m).astype(BF16))
                psum = psum + jnp.sum(pp, axis=-1, keepdims=True)
            for dz_b, kj in zip(dzs, ks):
                dq = dq + jnp.dot(dz_b, kj, preferred_element_type=F32)
            for dz_b, a, st in zip(dzs, probs, starts):
                dk_ref[pl.ds(st, tk), :] += lax.dot_general(dz_b, q, TN_DIMS, preferred_element_type=F32)
                dv_ref[pl.ds(st, tk), :] += lax.dot_general(a, do_b, TN_DIMS, preferred_element_type=F32)
            return lsum, psum, dq

        zero = jnp.zeros((tq, 1), F32)
        state = lax.fori_loop(0, i * (tq // tb), lambda j, c: block(j * tb, c[0], c[1], c[2], None),
                              (zero, zero, jnp.zeros((tq, HEAD), F32)))
        for b in range(tq // tb):
            state = block(i * tq + b * tb, *state, b * tb)
        dq_ref[...] = state[2].astype(dq_ref.dtype)
        host.stage("mid", comm_refs)
        host.stage("last", comm_refs)

    t = tq
    full = lambda off: pl.BlockSpec((s, HEAD), lambda h, i: (0, off + h))
    tile_spec = lambda off: pl.BlockSpec((t, HEAD), lambda h, i: (i, off + h))
    w = n_heads * HEAD
    return pl.pallas_call(
        body, name=name, grid=(n_heads, s // t),
        in_specs=[tile_spec(qoff), full(koff), full(voff), tile_spec(0), tile_spec(0)] + host.in_specs,
        out_specs=[tile_spec(0), full(0), full(0)] + host.out_specs,
        out_shape=[jax.ShapeDtypeStruct((s, w), BF16), jax.ShapeDtypeStruct((s, w), F32),
                   jax.ShapeDtypeStruct((s, w), F32)] + host.out_shape,
        scratch_shapes=host.scratch,
        compiler_params=_params(("parallel" if comm is None else "arbitrary", "arbitrary"), 48),
    )(qkv, qkv, qkv, do, tot, *host.ins)


def _coords():
    return lax.axis_index("x"), lax.axis_index("y"), lax.axis_index("c")


def _gather_plan(shards):
    n = len(shards)

    def run(stage, ins, outs, sems):
        send_sems, recv_sems, local_sems = sems
        x, y, c = _coords()
        me, sibling = (x, y, c), (x, y, 1 - c)
        chips = [(1 - x, y), (x, 1 - y), (1 - x, 1 - y)]

        def copy(w, k, block, to, src=None):
            dst = outs[w].at[4 * block[0] + 2 * block[1] + block[2]]
            return pltpu.make_async_remote_copy(
                src_ref=dst if src is None else src, dst_ref=dst,
                send_sem=send_sems.at[7 * w + k], recv_sem=recv_sems.at[7 * w + k],
                device_id=to, device_id_type=MESH)

        mine = [pltpu.make_async_copy(ins[w], outs[w].at[4 * x + 2 * y + c], local_sems.at[w]) for w in range(n)]
        first = []
        for w in range(n):
            first.append(copy(w, 0, me, sibling, src=ins[w]))
            first += [copy(w, 1 + j, me, (*chip, c), src=ins[w]) for j, chip in enumerate(chips)]
        passed = [copy(w, 4 + j, (*chip, c), sibling) for w in range(n) for j, chip in enumerate(chips)]
        if stage == "first":
            for cp in mine + first:
                cp.start()
        elif stage == "mid":
            for w in range(n):
                for j, chip in enumerate(chips):
                    copy(w, 1 + j, (*chip, c), me).wait_recv()
                    passed[3 * w + j].start()
        else:
            for w in range(n):
                copy(w, 0, sibling, me).wait_recv()
                for j, chip in enumerate(chips):
                    copy(w, 4 + j, (*chip, 1 - c), me).wait_recv()
            for cp in first + passed:
                cp.wait_send()
            for cp in mine:
                cp.wait()

    return dict(
        ins=list(shards), run=run, stages=("first", "mid", "last"),
        out_shape=[jax.ShapeDtypeStruct((N_DEV,) + a.shape, a.dtype) for a in shards],
        scratch=[pltpu.SemaphoreType.DMA((7 * n,)), pltpu.SemaphoreType.DMA((7 * n,)), pltpu.SemaphoreType.DMA((n,))])


def _scatter_plan(grads):
    n = len(grads)

    def run(stage, ins, outs, sems):
        send_sems, recv_sems = sems
        x, y, c = _coords()
        cps = []
        for w in range(n):
            for mask in range(1, N_DEV):
                px, py, pc = x ^ (mask >> 2), y ^ ((mask >> 1) & 1), c ^ (mask & 1)
                cps.append(pltpu.make_async_remote_copy(
                    src_ref=ins[w].at[4 * px + 2 * py + pc], dst_ref=outs[w].at[mask - 1],
                    send_sem=send_sems.at[7 * w + mask - 1], recv_sem=recv_sems.at[7 * w + mask - 1],
                    device_id=(px, py, pc), device_id_type=MESH))
        for cp in cps:
            if stage == "first":
                cp.start()
            else:
                cp.wait()

    return dict(
        ins=list(grads), run=run, stages=("first", "last"),
        out_shape=[jax.ShapeDtypeStruct((N_DEV - 1,) + a.shape[1:], a.dtype) for a in grads],
        scratch=[pltpu.SemaphoreType.DMA((7 * n,)), pltpu.SemaphoreType.DMA((7 * n,))])


def _run_plan(plan, name):
    n_in, n_out = len(plan["ins"]), len(plan["out_shape"])

    def body(*refs):
        for stage in plan["stages"]:
            plan["run"](stage, refs[:n_in], refs[n_in:n_in + n_out], refs[n_in + n_out:])

    any_spec = pl.BlockSpec(memory_space=pl.ANY)
    return pl.pallas_call(
        body, name=name, in_specs=[any_spec] * n_in, out_specs=[any_spec] * n_out,
        out_shape=plan["out_shape"], scratch_shapes=plan["scratch"],
    )(*plan["ins"])


def _gather_rows(v):
    rows, width = v.shape

    def body(v_ref, out_ref, send_sems, recv_sems):
        x, y, c = _coords()
        out_ref[pl.ds(pl.multiple_of((4 * x + 2 * y + c) * rows, rows), rows), :] = v_ref[...]
        cps = []
        for mask in range(1, N_DEV):
            peer = (x ^ (mask >> 2), y ^ ((mask >> 1) & 1), c ^ (mask & 1))
            dst = out_ref.at[pl.ds(pl.multiple_of((4 * x + 2 * y + c) * rows, rows), rows), :]
            cps.append(pltpu.make_async_remote_copy(
                src_ref=v_ref, dst_ref=dst, send_sem=send_sems.at[mask - 1], recv_sem=recv_sems.at[mask - 1],
                device_id=peer, device_id_type=MESH))
        for cp in cps:
            cp.start()
        for cp in cps:
            cp.wait()

    vmem = pl.BlockSpec(memory_space=pltpu.VMEM)
    return pl.pallas_call(
        body, name="gather_small",
        in_specs=[vmem], out_specs=vmem,
        out_shape=jax.ShapeDtypeStruct((N_DEV * rows, width), F32),
        scratch_shapes=[pltpu.SemaphoreType.DMA((N_DEV - 1,)), pltpu.SemaphoreType.DMA((N_DEV - 1,))],
    )(v)


def _adamw(w, g, m, v):
    m = ADAM_B1 * m + (1.0 - ADAM_B1) * g
    v = ADAM_B2 * v + (1.0 - ADAM_B2) * jnp.square(g)
    m_hat = m / (1.0 - ADAM_B1 ** ADAM_STEP)
    v_hat = v / (1.0 - ADAM_B2 ** ADAM_STEP)
    delta = -ADAM_LR * (m_hat / (jnp.sqrt(v_hat) + ADAM_EPS) + ADAM_WD * w)
    return delta, m, v


def _tile_rows(rows, cols):
    tm = 1 << int(math.log2(max(2 * SUBLANES, (1 << 18) // cols)))
    while rows % tm:
        tm //= 2
    assert tm >= 2 * SUBLANES, (rows, cols)
    return tm


def _reduce_adam(w, m, v, own, got, name):
    r, c = w.shape
    tm = max(2 * SUBLANES, _tile_rows(r, c) // 2)

    def fn(wv, mv, vv, a, *peers):
        g = a
        for pv in peers:
            g = g + pv.astype(F32)
        return (g,) + _adamw(wv, g, mv, vv)

    blk = lambda a: (a, (tm, c), lambda i: (i, 0))
    got_blk = lambda j: (got, (None, tm, c), lambda i, j=j: (j, i, 0))
    return _ew(fn, [blk(w), blk(m), blk(v), blk(own)] + [got_blk(j) for j in range(N_DEV - 1)],
               [((r, c), F32, (tm, c), lambda i: (i, 0))] * 4, (r // tm,), name)


def _small_adam(gathered, params, moms, vels, widths):
    n = len(params)
    total = gathered.shape[1]

    def body(*refs):
        g_ref = refs[0]
        p_refs, m_refs, v_refs = refs[1:1 + n], refs[1 + n:1 + 2 * n], refs[1 + 2 * n:1 + 3 * n]
        sum_ref = refs[1 + 3 * n]
        outs = refs[2 + 3 * n:]
        g = g_ref[0:1, :]
        for p in range(1, N_DEV):
            g = g + g_ref[p * SUBLANES:p * SUBLANES + 1, :]
        sum_ref[...] = g
        off = 0
        for i, wd in enumerate(widths):
            d, m2, v2 = _adamw(p_refs[i][...], g[:, off:off + wd], m_refs[i][...], v_refs[i][...])
            outs[3 * i][...] = d
            outs[3 * i + 1][...] = m2
            outs[3 * i + 2][...] = v2
            off += wd

    vmem = pl.BlockSpec(memory_space=pltpu.VMEM)
    out_shape = [jax.ShapeDtypeStruct((1, total), F32)]
    for wd in widths:
        out_shape += [jax.ShapeDtypeStruct((1, wd), F32)] * 3
    return pl.pallas_call(
        body, name="small_adam",
        in_specs=[vmem] * (1 + 3 * n), out_specs=[vmem] * len(out_shape), out_shape=out_shape,
    )(gathered, *params, *moms, *vels)


def _cast_bf16(a, name):
    r, c = a.shape
    tm = _tile_rows(r, c)
    return _ew(lambda v: v, [(a, (tm, c), lambda i: (i, 0))], [((r, c), BF16, (tm, c), lambda i: (i, 0))],
               (r // tm,), name)[0]


def _fold_loss(parts, name):
    r, c = parts.shape

    def fn(v):
        return jnp.broadcast_to(jnp.sum(jnp.sum(v, axis=0, keepdims=True), axis=1, keepdims=True), (SUBLANES, HEAD))

    return _ew(fn, [_const_spec(parts)], [((SUBLANES, HEAD), F32, (SUBLANES, HEAD), lambda i: (0, 0))], (1,),
               name)[0][0:1]


def kernel(x, p, g_mix, w_in, qn_gain, kn_gain, w_branch_a, w_branch_b, w_out, g_mlp, w_up, w_down, g_ple, w_ple_gate, w_ple_proj, loss_target, m_g_mix, m_w_in, m_qn_gain, m_kn_gain, m_w_branch_a, m_w_branch_b, m_w_out, m_g_mlp, m_w_up, m_w_down, m_g_ple, m_w_ple_gate, m_w_ple_proj, v_g_mix, v_w_in, v_qn_gain, v_kn_gain, v_w_branch_a, v_w_branch_b, v_w_out, v_g_mlp, v_w_up, v_w_down, v_g_ple, v_w_ple_gate, v_w_ple_proj):
    x2 = x[0]
    tgt = loss_target[0]
    s, d = x2.shape
    wd_ = w_branch_a.shape[1]
    nh = wd_ // HEAD
    dff = w_up.shape[1]
    qkv_w = 6 * wd_
    tiles = lambda cols: cols // HEAD

    big = [w_in[0], w_branch_a[0], w_branch_b[0], w_out[0], w_up[0], w_down[0], w_ple_gate[0], w_ple_proj[0]]
    names = ["w_in", "w_branch_a", "w_branch_b", "w_out", "w_up", "w_down", "w_ple_gate", "w_ple_proj"]
    row_sharded = [False, False, False, True, False, True, True, False]
    shards = [_cast_bf16(a, "cast_" + nm) for a, nm in zip(big, names)]
    as_weight = lambda g, rs: g.reshape((1, N_DEV * g.shape[1], g.shape[2])) if rs else g
    as_blocks = lambda g, rs: g.reshape((N_DEV, g.shape[1] // N_DEV, g.shape[2])) if rs else g
    win = _run_plan(_gather_plan(shards[:1]), "all_gather_w_in")[0]

    tm = 1024 if s % 1024 == 0 else s
    tm_in = tk_s = 2048 if s % 2048 == 0 else tm
    tn_of = lambda n: 512 if n % 512 == 0 else (256 if n % 256 == 0 else n)
    tn_in = 256 if win.shape[2] % 256 == 0 else HEAD

    h = _rms_fwd(x2, g_mix, "norm_mix")
    qk_raw = _mm("nn", h, win, tm=tm_in, tn=tn_in, tk=d, out_dtypes=[F32], name="proj_qk",
                 n_off=0, n_cnt=2 * wd_ // tn_in)[0]
    rest = _mm("nn", h, win, tm=tm_in, tn=tn_in, tk=d, out_dtypes=[BF16], name="proj_rest",
               n_off=2 * wd_ // tn_in, n_cnt=(win.shape[0] * win.shape[2] - 2 * wd_) // tn_in)[0]
    o_va, o_qb, o_kb, o_vb, o_ga, o_gb = 0, tiles(wd_), tiles(2 * wd_), tiles(3 * wd_), tiles(4 * wd_), tiles(4 * wd_ + d)
    tabs = _rope_tables(s)
    qa = _headnorm_rope(qk_raw, 0, qn_gain, tabs, nh, "rope_q")
    ka = _headnorm_rope(qk_raw, nh, kn_gain, tabs, nh, "rope_k")
    va = rest[:, :wd_]

    outs, lses = [], []
    for dil in DILATIONS:
        o_g, l_g = _dilated_fwd(_phase_major(qa, dil), _phase_major(ka, dil), _phase_major(va, dil), 0, nh,
                                f"dilated_fwd_{dil}")
        outs.append(_token_major(o_g))
        lses.append(_token_major(l_g))
    ya, lse_all = _mix_fwd(outs, lses, "mix_fwd")
    yb, sb_tot, *others = _sb_fwd(rest, o_qb, o_kb, o_vb, nh, "sb_fwd", comm=_gather_plan(shards[1:]))
    wba, wbb, wout, wup, wdown, wgate, wproj = [as_weight(g, rs) for g, rs in zip(others, row_sharded[1:])]

    tn_d = tn_of(wba.shape[2])
    za = _mm("nn", ya, wba, tm=tm, tn=tn_d, tk=wd_, out_dtypes=[BF16], name="branch_a")[0]

    def merge(acc, zav, gav, gbv):
        return _sigmoid(gav.astype(F32)) * zav.astype(F32) + _sigmoid(gbv.astype(F32)) * acc, acc

    merged, zb = _mm("nn", yb, wbb, tm=tm, tn=tn_d, tk=wd_, out_dtypes=[BF16, BF16], name="branch_b_merge",
                     epilogue=merge, extras=[(za, 0), (rest, o_ga * HEAD // tn_d), (rest, o_gb * HEAD // tn_d)])
    x1 = _mm("nn", merged, wout, tm=tm, tn=512, tk=d, out_dtypes=[F32], name="out_proj",
             epilogue=lambda acc, xv: (acc + xv,), extras=[(x2, 0)])[0]

    hm = _rms_fwd(x1, g_mlp, "norm_mlp")
    tn_u = tn_of(wup.shape[2])
    u, act = _mm("nn", hm, wup, tm=tm, tn=tn_u, tk=d, out_dtypes=[BF16, BF16], name="mlp_up",
                 epilogue=lambda acc: (acc, jnp.square(jnp.maximum(acc, 0.0))))
    x3 = _mm("nn", act, wdown, tm=tm, tn=512, tk=min(dff, 2048), out_dtypes=[F32], name="mlp_down",
             epilogue=lambda acc, xv: (acc + xv,), extras=[(x1, 0)])[0]

    hp = _rms_fwd(x3, g_ple, "norm_ple")
    p_b = _cast_bf16(p[0, 0], "cast_p")
    pp = _mm("nn", p_b, wproj, tm=tm, tn=tn_of(wproj.shape[2]), tk=p_b.shape[1], out_dtypes=[BF16],
             name="ple_proj")[0]

    def head(acc, ppv, xv, tv):
        sg = _sigmoid(acc)
        ppf = ppv.astype(F32)
        err = xv + ppf * sg - tv
        dy = err / d
        sq = jnp.square(err)
        return dy, dy * sg, dy * ppf * sg * (1.0 - sg), sq.reshape(-1, SUBLANES, sq.shape[-1]).sum(axis=0)

    n_i = s // tm
    dy, d_pp, d_gt, sq_parts = _mm(
        "nn", hp, wgate, tm=tm, tn=512, tk=d, out_dtypes=[F32, BF16, BF16], name="ple_gate_loss", epilogue=head,
        extras=[(pp, 0), (x3, 0), (tgt, 0)],
        extra_outs=[((n_i * SUBLANES, d), F32, (SUBLANES, 512), lambda i, j: (i, j))])
    loss_vec = _fold_loss(sq_parts, "loss_fold") * 0.5 / d

    both = [F32, BF16]
    g_wproj = _mm("tn", p_b, d_pp, tm=p_b.shape[1], tn=tn_of(wproj.shape[2]), tk=tk_s, out_dtypes=both,
                  name="grad_w_ple_proj", out_nb=N_DEV)
    g_wgate = _mm("tn", hp, d_gt, tm=512, tn=1024, tk=tk_s, out_dtypes=both, name="grad_w_ple_gate")
    d_hp = _mm("nt", d_gt, wgate, tm=tm, tn=512, tk=d, out_dtypes=[F32], name="d_hp")[0]
    dx3, dx3_b, g_gple = _rms_bwd(d_hp, x3, g_ple, dy, "norm_ple_bwd")

    d_u = _mm("nt", dx3_b, wdown, tm=tm, tn=1024, tk=d, out_dtypes=[BF16], name="d_u",
              epilogue=lambda acc, uv: (acc * (2.0 * jnp.maximum(uv.astype(F32), 0.0)),), extras=[(u, 0)])[0]
    g_wdown = _mm("tn", act, dx3_b, tm=1024, tn=1024, tk=tk_s, out_dtypes=both, name="grad_w_down")
    g_wup = _mm("tn", hm, d_u, tm=1024, tn=wup.shape[2], tk=tk_s, out_dtypes=both, name="grad_w_up", out_nb=N_DEV)
    d_hm = _mm("nt", d_u, wup, tm=tm, tn=1024, tk=wup.shape[2], out_dtypes=[F32], name="d_hm")[0]
    dx1, dx1_b, g_gmlp = _rms_bwd(d_hm, x1, g_mlp, dx3, "norm_mlp_bwd")

    def unmerge(acc, gav, gbv, zav, zbv):
        sa, sb = _sigmoid(gav.astype(F32)), _sigmoid(gbv.astype(F32))
        return acc * sa, acc * sb, acc * zav.astype(F32) * sa * (1.0 - sa), acc * zbv.astype(F32) * sb * (1.0 - sb)

    d_za, d_zb, d_ga, d_gb = _mm(
        "nt", dx1_b, wout, tm=tm, tn=512, tk=d, out_dtypes=[BF16] * 4, name="d_merged", epilogue=unmerge,
        extras=[(rest, o_ga * HEAD // 512), (rest, o_gb * HEAD // 512), (za, 0), (zb, 0)])
    g_wout = _mm("tn", merged, dx1_b, tm=512, tn=1024, tk=tk_s, out_dtypes=both, name="grad_w_out")
    g_wba = _mm("tn", ya, d_za, tm=wd_, tn=wba.shape[2], tk=tk_s, out_dtypes=both, name="grad_w_branch_a",
                out_nb=N_DEV)
    g_wbb = _mm("tn", yb, d_zb, tm=wd_, tn=wbb.shape[2], tk=tk_s, out_dtypes=both, name="grad_w_branch_b",
                out_nb=N_DEV)
    d_ya = _mm("nt", d_za, wba, tm=tm, tn=wd_, tk=wba.shape[2], out_dtypes=[F32], name="d_ya")[0]
    d_yb = _mm("nt", d_zb, wbb, tm=tm, tn=wd_, tk=wbb.shape[2], out_dtypes=[F32], name="d_yb")[0]

    grads = [None, g_wba, g_wbb, g_wout, g_wup, g_wdown, g_wgate, g_wproj]
    early = _scatter_plan([as_blocks(g[1], rs) for g, rs in zip(grads[1:], row_sharded[1:])])
    d_qb, d_kb, d_vb, *got = _sb_bwd(rest, o_qb, o_kb, o_vb, d_yb, sb_tot, nh, "sb_bwd", comm=early)
    delta = _head_dot(d_ya, ya, nh, "mix_delta")
    dqs, dks, dvs = [], [], []
    for dil in DILATIONS:
        dq_g, dk_g, dv_g = _dilated_bwd(
            _phase_major(qa, dil), _phase_major(ka, dil), _phase_major(va, dil), 0, _phase_major(d_ya, dil),
            _phase_major(lse_all, dil), _phase_major(delta, dil), nh, f"dilated_bwd_{dil}")
        dqs.append(_token_major(dq_g))
        dks.append(_token_major(dk_g))
        dvs.append(_token_major(dv_g))
    d_qa, g_qn = _headnorm_rope_bwd(dqs, qk_raw, 0, qn_gain, tabs, nh, "rope_q_bwd")
    d_ka, g_kn = _headnorm_rope_bwd(dks, qk_raw, nh, kn_gain, tabs, nh, "rope_k_bwd")
    tmr = 256
    d_va = _ew(lambda a, b, c: a + b + c, [_row_spec(a, tmr) for a in dvs],
               [((s, wd_), BF16, (tmr, wd_), lambda i: (i, 0))], (s // tmr,), "sum_dv")[0]
    d_proj = jnp.concatenate([d_qa, d_ka, d_va, d_qb, d_kb.astype(BF16), d_vb.astype(BF16), d_ga, d_gb], axis=1)

    grads[0] = _mm("tn", h, d_proj, tm=1024, tn=win.shape[2], tk=tm, out_dtypes=both, name="grad_w_in",
                   out_nb=N_DEV)
    d_h, got_in = _mm("nt", d_proj, win, tm=tm, tn=1024, tk=win.shape[2], out_dtypes=[F32], name="d_h",
                      comm=_scatter_plan([grads[0][1]]))
    got = [got_in] + got
    dx, _, g_gmix = _rms_bwd(d_h, x2, g_mix, dx1, "norm_mix_bwd")

    cx, cy, cc = _coords()
    me = 4 * cx + 2 * cy + cc
    moms = [m_w_in, m_w_branch_a, m_w_branch_b, m_w_out, m_w_up, m_w_down, m_w_ple_gate, m_w_ple_proj]
    vels = [v_w_in, v_w_branch_a, v_w_branch_b, v_w_out, v_w_up, v_w_down, v_w_ple_gate, v_w_ple_proj]
    big_out = {}
    for i, nm in enumerate(names):
        own = lax.dynamic_index_in_dim(as_blocks(grads[i][0], row_sharded[i]), me, axis=0, keepdims=False)
        big_out[nm] = [a[None] for a in _reduce_adam(big[i], moms[i][0], vels[i][0], own, got[i], "adam_" + nm)]

    small_names = ["g_mix", "qn_gain", "kn_gain", "g_mlp", "g_ple"]
    small_p = [g_mix, qn_gain, kn_gain, g_mlp, g_ple]
    small_m = [m_g_mix, m_qn_gain, m_kn_gain, m_g_mlp, m_g_ple]
    small_v = [v_g_mix, v_qn_gain, v_kn_gain, v_g_mlp, v_g_ple]
    small_g = [g_gmix, g_qn, g_kn, g_gmlp, g_gple]
    widths = [a.shape[1] for a in small_p]
    vec = jnp.concatenate(small_g + [loss_vec], axis=1)
    vec = jnp.pad(vec, ((0, SUBLANES - 1), (0, 0)))
    res = _small_adam(_gather_rows(vec), small_p, small_m, small_v, widths)
    summed = res[0]
    small_out, off = {}, 0
    for i, nm in enumerate(small_names):
        small_out[nm] = [summed[:, off:off + widths[i]]] + list(res[1 + 3 * i:4 + 3 * i])
        off += widths[i]
    loss = summed[0, off]

    order = ["g_mix", "w_in", "qn_gain", "kn_gain", "w_branch_a", "w_branch_b", "w_out", "g_mlp", "w_up", "w_down",
             "g_ple", "w_ple_gate", "w_ple_proj"]
    table = {**big_out, **small_out}
    result = [loss, dx[None]]
    for kind in range(4):
        result += [table[nm][kind] for nm in order]
    return tuple(result)
```

```python
import functools
import math

import jax
import jax.numpy as jnp
from jax import lax
from jax.experimental import pallas as pl
from jax.experimental.pallas import tpu as pltpu

F32 = jnp.float32
BF16 = jnp.bfloat16
MESH = pl.DeviceIdType.MESH

HEAD = 128
WINDOW = 128
DILATIONS = (1, 4, 16)
ROT = HEAD // 4
ROPE_THETA = 500000.0
EPS = 1e-6
NEG = -1e30
N_DEV = 8

ADAM_LR = 0.001
ADAM_B1 = 0.9
ADAM_B2 = 0.999
ADAM_EPS = 1e-08
ADAM_WD = 0.01
ADAM_STEP = 10

SUBLANES = 8
VMEM_CAP_MB = 56

NT_DIMS = (((1,), (1,)), ((), ()))
TN_DIMS = (((0,), (0,)), ((), ()))


def _params(semantics, vmem_mb):
    return pltpu.CompilerParams(dimension_semantics=semantics, vmem_limit_bytes=min(vmem_mb, VMEM_CAP_MB) << 20)


def _sigmoid(x):
    return 1.0 / (1.0 + jnp.exp(-x))


def _split_bf16(x):
    hi = x.astype(BF16)
    lo = (x - hi.astype(F32)).astype(BF16)
    return hi, lo


class _Host:
    def __init__(self, comm, grid):
        self.comm, self.grid = comm, grid
        any_spec = pl.BlockSpec(memory_space=pl.ANY)
        self.ins = list(comm["ins"]) if comm else []
        self.out_shape = list(comm["out_shape"]) if comm else []
        self.scratch = list(comm["scratch"]) if comm else []
        self.in_specs = [any_spec] * len(self.ins)
        self.out_specs = [any_spec] * len(self.out_shape)

    def split(self, refs, n_in, n_out):
        pos = n_in
        c_in = refs[pos:pos + len(self.ins)]
        pos += len(self.ins)
        outs = refs[pos:pos + n_out]
        pos += n_out
        c_out = refs[pos:pos + len(self.out_shape)]
        pos += len(self.out_shape)
        own = len(refs) - pos - len(self.scratch)
        return (c_in, c_out, refs[pos + own:]), (outs, refs[pos:pos + own])

    def stage(self, which, comm_refs):
        if self.comm is None or which not in self.comm["stages"]:
            return
        grid = self.grid
        at = {"first": [0] * len(grid), "mid": [grid[0] // 2] + [0] * (len(grid) - 1),
              "last": [g - 1 for g in grid]}[which]
        cond = functools.reduce(jnp.logical_and, [pl.program_id(ax) == v for ax, v in enumerate(at)])

        @pl.when(cond)
        def _():
            self.comm["run"](which, *comm_refs)


def _ew(fn, ins, outs, grid, name, colsums=(), vmem_mb=40):
    n_in, n_out, n_cs = len(ins), len(outs), len(colsums)
    steps = math.prod(grid)

    def body(*refs):
        in_refs = refs[:n_in]
        out_refs = refs[n_in:n_in + n_out]
        cs_refs = refs[n_in + n_out:n_in + n_out + n_cs]
        acc_refs = refs[n_in + n_out + n_cs:]
        vals = fn(*[r[...] for r in in_refs])
        if not isinstance(vals, tuple):
            vals = (vals,)
        for r, v in zip(out_refs, vals[:n_out]):
            r[...] = v.astype(r.dtype)
        if n_cs:
            step = pl.program_id(0)
            for ax in range(1, len(grid)):
                step = step * grid[ax] + pl.program_id(ax)
            for acc, cs, v in zip(acc_refs, cs_refs, vals[n_out:]):
                part = v.reshape(-1, SUBLANES, v.shape[-1]).sum(axis=0)

                @pl.when(step == 0)
                def _(acc=acc, part=part):
                    acc[...] = part

                @pl.when(step > 0)
                def _(acc=acc, part=part):
                    acc[...] += part

                @pl.when(step == steps - 1)
                def _(acc=acc, cs=cs):
                    cs[...] = acc[...].sum(axis=0, keepdims=True)

    out_shape = [jax.ShapeDtypeStruct(s, d) for s, d, _, _ in outs]
    out_specs = [pl.BlockSpec(b, m) for _, _, b, m in outs]
    for w in colsums:
        out_shape.append(jax.ShapeDtypeStruct((1, w), F32))
        out_specs.append(pl.BlockSpec((1, w), lambda *_: (0, 0)))
    sem = ("arbitrary",) * len(grid) if n_cs else ("parallel",) * len(grid)
    res = pl.pallas_call(
        body, name=name, grid=grid,
        in_specs=[pl.BlockSpec(b, m) for _, b, m in ins],
        out_specs=out_specs, out_shape=out_shape,
        scratch_shapes=[pltpu.VMEM((SUBLANES, w), F32) for w in colsums],
        compiler_params=_params(sem, vmem_mb),
    )(*[a for a, _, _ in ins])
    return res


def _row_spec(a, tm):
    return (a, (tm, a.shape[1]), lambda i: (i, 0))


def _const_spec(a):
    return (a, a.shape, lambda *_: (0,) * a.ndim)


def _mm(mode, a, b, *, tm, tn, tk, out_dtypes, name, epilogue=None, extras=(), n_off=0, n_cnt=None,
        out_nb=1, extra_outs=(), vmem_mb=52, comm=None):
    if mode == "nn":
        m, kdim = a.shape
        nb, _, n = b.shape
        npb = n // tn
        ncols = nb * n
        n_tiles = (ncols // tn) if n_cnt is None else n_cnt
        a_spec = pl.BlockSpec((tm, tk), lambda i, j, k: (i, k))
        b_spec = pl.BlockSpec((None, tk, tn), lambda i, j, k: ((j + n_off) // npb, k, (j + n_off) % npb))
        dims = (((1,), (0,)), ((), ()))
    elif mode == "nt":
        m, kdim = a.shape
        nb, nout, n = b.shape
        kpb = n // tk
        n_tiles = nout // tn
        a_spec = pl.BlockSpec((tm, tk), lambda i, j, k: (i, k))
        b_spec = pl.BlockSpec((None, tn, tk), lambda i, j, k: (k // kpb, j, k % kpb))
        dims = NT_DIMS
    else:
        kdim, m = a.shape
        ncols = b.shape[1]
        n_tiles = ncols // tn
        a_spec = pl.BlockSpec((tk, tm), lambda i, j, k: (k, i))
        b_spec = pl.BlockSpec((tk, tn), lambda i, j, k: (k, j))
        dims = TN_DIMS
    nk = kdim // tk
    assert kdim % tk == 0 and m % tm == 0
    grid = (m // tm, n_tiles, nk)
    n_ex, n_out = len(extras), len(out_dtypes) + len(extra_outs)
    host = _Host(comm, grid)

    def body(*refs):
        a_ref, b_ref = refs[0], refs[1]
        ex_refs = refs[2:2 + n_ex]
        comm_refs, (out_refs, scratch) = host.split(refs, 2 + n_ex, n_out)
        host.stage("first", comm_refs)

        def finish(acc):
            vals = (acc,) * n_out if epilogue is None else epilogue(acc, *[r[...] for r in ex_refs])
            for r, v in zip(out_refs, vals):
                r[...] = v.astype(r.dtype)

        def part():
            return lax.dot_general(a_ref[...], b_ref[...], dims, preferred_element_type=F32)

        if nk == 1:
            finish(part())
        else:
            acc_ref = scratch[0]
            k = pl.program_id(2)

            @pl.when(k == 0)
            def _():
                acc_ref[...] = part()

            @pl.when((k > 0) & (k < nk - 1))
            def _():
                acc_ref[...] += part()

            @pl.when(k == nk - 1)
            def _():
                finish(acc_ref[...] + part())

        host.stage("mid", comm_refs)
        host.stage("last", comm_refs)

    if mode == "tn":
        npo = (ncols // out_nb) // tn
        out_shape = [jax.ShapeDtypeStruct((out_nb, m, ncols // out_nb), d) for d in out_dtypes]
        out_specs = [pl.BlockSpec((None, tm, tn), lambda i, j, k: (j // npo, i, j % npo)) for _ in out_dtypes]
    else:
        out_shape = [jax.ShapeDtypeStruct((m, n_tiles * tn), d) for d in out_dtypes]
        out_specs = [pl.BlockSpec((tm, tn), lambda i, j, k: (i, j)) for _ in out_dtypes]
    for s, d, blk, imap in extra_outs:
        out_shape.append(jax.ShapeDtypeStruct(s, d))
        out_specs.append(pl.BlockSpec(blk, lambda i, j, k, imap=imap: imap(i, j)))
    ex_specs = [pl.BlockSpec((tm, tn), lambda i, j, k, off=off: (i, j + off)) for _, off in extras]
    sem = ("parallel", "parallel", "arbitrary") if comm is None else ("arbitrary",) * 3
    return pl.pallas_call(
        body, name=name, grid=grid,
        in_specs=[a_spec, b_spec] + ex_specs + host.in_specs,
        out_specs=out_specs + host.out_specs, out_shape=out_shape + host.out_shape,
        scratch_shapes=([pltpu.VMEM((tm, tn), F32)] if nk > 1 else []) + host.scratch,
        compiler_params=_params(sem, vmem_mb),
    )(a, b, *[e for e, _ in extras], *host.ins)


def _rms_fwd(x, g, name):
    tm = 256

    def fn(xv, gv):
        r = lax.rsqrt(jnp.mean(xv * xv, axis=-1, keepdims=True) + EPS)
        return xv * r * gv

    return _ew(fn, [_row_spec(x, tm), _const_spec(g)], [(x.shape, BF16, (tm, x.shape[1]), lambda i: (i, 0))],
               (x.shape[0] // tm,), name)[0]


def _rms_bwd(dh, x, g, res, name):
    tm = 256
    d = x.shape[1]

    def fn(dhv, xv, gv, rv):
        r = lax.rsqrt(jnp.mean(xv * xv, axis=-1, keepdims=True) + EPS)
        xh = xv * r
        dyg = dhv * gv
        dx = rv + r * (dyg - xh * jnp.mean(dyg * xh, axis=-1, keepdims=True))
        return dx, dx, dhv * xh

    spec = lambda dt: (x.shape, dt, (tm, d), lambda i: (i, 0))
    return _ew(fn, [_row_spec(dh, tm), _row_spec(x, tm), _const_spec(g), _row_spec(res, tm)],
               [spec(F32), spec(BF16)], (x.shape[0] // tm,), name, colsums=(d,))


def _rope_tables(s):
    half = ROT // 2
    pos = jnp.arange(s, dtype=F32)
    inv = ROPE_THETA ** (-jnp.arange(0, ROT, 2, dtype=F32) / ROT)
    ang = pos[:, None] * inv[None, :]
    cos, sin = jnp.cos(ang), jnp.sin(ang)
    pad = jnp.zeros((s, HEAD - ROT), F32)
    c = jnp.concatenate([cos, cos, pad + 1.0], axis=1)
    a = jnp.concatenate([-sin, jnp.zeros_like(sin), pad], axis=1)
    b = jnp.concatenate([jnp.zeros_like(sin), sin, pad], axis=1)
    return c, a, b


def _headnorm_rope(proj, off, gain, tabs, n_heads, name):
    tm = 512
    s = proj.shape[0]

    def fn(xv, gv, c, a, b):
        r = lax.rsqrt(jnp.mean(xv * xv, axis=-1, keepdims=True) + EPS)
        y = xv * r * gv
        return c * y + a * pltpu.roll(y, HEAD - ROT // 2, 1) + b * pltpu.roll(y, ROT // 2, 1)

    tab = lambda t: (t, (tm, HEAD), lambda i, h: (i, 0))
    return _ew(fn, [(proj, (tm, HEAD), lambda i, h: (i, off + h)), (gain, (1, HEAD), lambda i, h: (0, 0))]
               + [tab(t) for t in tabs],
               [((s, n_heads * HEAD), BF16, (tm, HEAD), lambda i, h: (i, h))], (s // tm, n_heads), name)[0]


def _headnorm_rope_bwd(dys, proj, off, gain, tabs, n_heads, name):
    tm = 512
    s = proj.shape[0]
    n_dy = len(dys)

    def fn(*vals):
        dy = vals[0]
        for v in vals[1:n_dy]:
            dy = dy + v
        xv, gv, c, a, b = vals[n_dy:]
        dn = c * dy + pltpu.roll(a * dy, ROT // 2, 1) + pltpu.roll(b * dy, HEAD - ROT // 2, 1)
        r = lax.rsqrt(jnp.mean(xv * xv, axis=-1, keepdims=True) + EPS)
        xh = xv * r
        dyg = dn * gv
        dx = r * (dyg - xh * jnp.mean(dyg * xh, axis=-1, keepdims=True))
        return dx, dn * xh

    tab = lambda t: (t, (tm, HEAD), lambda i, h: (i, 0))
    return _ew(fn, [(d, (tm, HEAD), lambda i, h: (i, h)) for d in dys]
               + [(proj, (tm, HEAD), lambda i, h: (i, off + h)), (gain, (1, HEAD), lambda i, h: (0, 0))]
               + [tab(t) for t in tabs],
               [((s, n_heads * HEAD), BF16, (tm, HEAD), lambda i, h: (i, h))], (s // tm, n_heads), name,
               colsums=(HEAD,))


def _phase_major(a, d):
    s, w = a.shape
    if d == 1:
        return a.reshape(1, s, w)
    return a.reshape(s // d, d, w).transpose(1, 0, 2)


def _token_major(a):
    d, m, w = a.shape
    if d == 1:
        return a.reshape(m, w)
    return a.transpose(1, 0, 2).reshape(m * d, w)


def _dil_tq(m):
    return min(1024, m)


def _dilated_fwd(q, k, v, voff, n_heads, name):
    d, m, _ = q.shape
    tq = _dil_tq(m)
    nq = m // tq
    per = tq // WINDOW
    scale = HEAD ** -0.5

    def body(q_ref, kc_ref, kp_ref, vc_ref, vp_ref, o_ref, l_ref):
        n = pl.program_id(2)
        kk = jnp.concatenate([kp_ref[...], kc_ref[...]], axis=0)
        vv = jnp.concatenate([vp_ref[...], vc_ref[...]], axis=0)
        row = lax.broadcasted_iota(jnp.int32, (WINDOW, 2 * WINDOW), 0)
        col = lax.broadcasted_iota(jnp.int32, (WINDOW, 2 * WINDOW), 1)
        band = (col >= row) & (col <= row + WINDOW)
        q = q_ref[...]
        rows = [slice(b * WINDOW, (b + 1) * WINDOW) for b in range(per)]
        keys = [slice(b * WINDOW, (b + 2) * WINDOW) for b in range(per)]
        s = [lax.dot_general(q[rows[b]], kk[keys[b]], NT_DIMS, preferred_element_type=F32) * scale
             for b in range(per)]
        es, outs, lses = [], [], []
        for b in range(per):
            valid = band if b else band & ((n > 0) | (col >= WINDOW))
            sb = jnp.where(valid, s[b], NEG)
            mx = jnp.max(sb, axis=-1, keepdims=True)
            e = jnp.exp(sb - mx)
            den = jnp.sum(e, axis=-1, keepdims=True)
            es.append((e.astype(BF16), den))
            lses.append(jnp.broadcast_to(mx + jnp.log(den), (WINDOW, HEAD)))
        for b in range(per):
            outs.append(jnp.dot(es[b][0], vv[keys[b]], preferred_element_type=F32) / es[b][1])
        o_ref[...] = jnp.concatenate(outs, axis=0)
        l_ref[...] = jnp.concatenate(lses, axis=0)

    cur = lambda off: pl.BlockSpec((None, tq, HEAD), lambda r, h, n: (r, n, off + h))
    prev = lambda off: pl.BlockSpec((None, WINDOW, HEAD), lambda r, h, n: (r, jnp.maximum(n * per - 1, 0), off + h))
    out = jax.ShapeDtypeStruct((d, m, n_heads * HEAD), F32)
    return pl.pallas_call(
        body, name=name, grid=(d, n_heads, nq),
        in_specs=[cur(0), cur(0), prev(0), cur(voff), prev(voff)],
        out_specs=[cur(0), cur(0)], out_shape=[out, out],
        compiler_params=_params(("parallel",) * 3, 32),
    )(q, k, k, v, v)


def _dilated_bwd(q, k, v, voff, dy, stats, n_heads, name):
    d, m, _ = q.shape
    tq = _dil_tq(m)
    nq = m // tq
    per = tq // WINDOW
    last_blk = m // WINDOW - 1
    scale = HEAD ** -0.5

    def body(qc_ref, qn_ref, kc_ref, kp_ref, vc_ref, vp_ref, dyc_ref, dyn_ref, sc_ref, sn_ref,
             dq_ref, dk_ref, dv_ref):
        n = pl.program_id(2)
        kk = jnp.concatenate([kp_ref[...], kc_ref[...]], axis=0)
        vv = jnp.concatenate([vp_ref[...], vc_ref[...]], axis=0)
        qq = jnp.concatenate([qc_ref[...], qn_ref[...]], axis=0)
        dyy = jnp.concatenate([dyc_ref[...], dyn_ref[...]], axis=0)
        st = jnp.concatenate([sc_ref[...], sn_ref[...]], axis=0)
        half = HEAD // 2
        row = lax.broadcasted_iota(jnp.int32, (WINDOW, 2 * WINDOW), 0)
        col = lax.broadcasted_iota(jnp.int32, (WINDOW, 2 * WINDOW), 1)
        band = (col >= row) & (col <= row + WINDOW)
        rows = [slice(b * WINDOW, (b + 1) * WINDOW) for b in range(per)]
        wide = [slice(b * WINDOW, (b + 2) * WINDOW) for b in range(per)]
        nt = lambda a, b: lax.dot_general(a, b, NT_DIMS, preferred_element_type=F32)
        s = [nt(qq[rows[b]], kk[wide[b]]) * scale for b in range(per)]
        dp = [nt(dyy[rows[b]], vv[wide[b]]) for b in range(per)]
        ds = []
        for b in range(per):
            valid = band if b else band & ((n > 0) | (col >= WINDOW))
            p = jnp.where(valid, jnp.exp(jnp.minimum(s[b] - st[rows[b]][:, :1], 0.0)), 0.0)
            ds.append((p * (dp[b] - st[rows[b]][:, half:half + 1]) * scale).astype(BF16))
        dq = [jnp.dot(ds[b], kk[wide[b]], preferred_element_type=F32) for b in range(per)]
        kc, vc = kc_ref[...], vc_ref[...]
        s2 = [nt(kc[rows[b]], qq[wide[b]]) * scale for b in range(per)]
        dp2 = [nt(vc[rows[b]], dyy[wide[b]]) for b in range(per)]
        ds2, p2 = [], []
        for b in range(per):
            valid = band if b < per - 1 else band & ((n < nq - 1) | (col < WINDOW))
            st_t = st[wide[b]].T
            pb = jnp.where(valid, jnp.exp(jnp.minimum(s2[b] - st_t[:1], 0.0)), 0.0)
            ds2.append((pb * (dp2[b] - st_t[half:half + 1]) * scale).astype(BF16))
            p2.append(pb.astype(BF16))
        dk = [jnp.dot(ds2[b], qq[wide[b]], preferred_element_type=F32) for b in range(per)]
        dv = [jnp.dot(p2[b], dyy[wide[b]], preferred_element_type=F32) for b in range(per)]
        dq_ref[...] = jnp.concatenate(dq, axis=0)
        dk_ref[...] = jnp.concatenate(dk, axis=0)
        dv_ref[...] = jnp.concatenate(dv, axis=0)

    cur = lambda off: pl.BlockSpec((None, tq, HEAD), lambda r, h, n: (r, n, off + h))
    prev = lambda off: pl.BlockSpec((None, WINDOW, HEAD), lambda r, h, n: (r, jnp.maximum(n * per - 1, 0), off + h))
    nxt = lambda off: pl.BlockSpec((None, WINDOW, HEAD),
                                   lambda r, h, n: (r, jnp.minimum((n + 1) * per, last_blk), off + h))
    out = jax.ShapeDtypeStruct((d, m, n_heads * HEAD), F32)
    return pl.pallas_call(
        body, name=name, grid=(d, n_heads, nq),
        in_specs=[cur(0), nxt(0), cur(0), prev(0), cur(voff), prev(voff), cur(0), nxt(0), cur(0), nxt(0)],
        out_specs=[cur(0)] * 3, out_shape=[out] * 3,
        compiler_params=_params(("parallel",) * 3, 40),
    )(q, q, k, k, v, v, dy, dy, stats, stats)


def _mix_fwd(os_, ls_, name):
    tm = 256
    s, w = os_[0].shape
    n = len(os_)

    def fn(*vals):
        o, l = vals[:n], vals[n:]
        mx = functools.reduce(jnp.maximum, l)
        e = [jnp.exp(x - mx) for x in l]
        den = functools.reduce(jnp.add, e)
        y = functools.reduce(jnp.add, [ei * oi for ei, oi in zip(e, o)]) / den
        return y, mx + jnp.log(den)

    return _ew(fn, [_row_spec(a, tm) for a in list(os_) + list(ls_)],
               [((s, w), BF16, (tm, w), lambda i: (i, 0)), ((s, w), F32, (tm, w), lambda i: (i, 0))],
               (s // tm,), name)


def _mix_stats(dy, y, lse, n_heads, name):
    tm = 512
    s = dy.shape[0]

    def fn(a, b, l):
        delta = jnp.sum(a * b.astype(F32), axis=-1, keepdims=True)
        lane = lax.broadcasted_iota(jnp.int32, a.shape, 1)
        return a, jnp.where(lane < HEAD // 2, l, delta)

    blk = lambda a: (a, (tm, HEAD), lambda i, h: (i, h))
    out = lambda dt: ((s, n_heads * HEAD), dt, (tm, HEAD), lambda i, h: (i, h))
    return _ew(fn, [blk(dy), blk(y), blk(lse)], [out(BF16), out(F32)], (s // tm, n_heads), name)


SB_TQ = 1024
SB_TB = 512
SB_TK = 256


def _softplus(z):
    return jnp.where(z > 20.0, z, jnp.log(1.0 + jnp.exp(z)))


def _tri(t, cmp):
    rows = lax.broadcasted_iota(jnp.int32, (2 * t, t), 0)
    cols = lax.broadcasted_iota(jnp.int32, (2 * t, t), 1)
    return jnp.where(cmp(jnp.where(rows >= t, rows - t, rows), cols), 1.0, 0.0).astype(BF16)


def _tri_sum(x, tri):
    return jnp.dot(jnp.concatenate(_split_bf16(x), axis=1), tri, preferred_element_type=F32)


def _causal(rows, cols):
    return lax.broadcasted_iota(jnp.int32, (rows, cols), 1) < lax.broadcasted_iota(jnp.int32, (rows, cols), 0)


def _rowsum(x):
    return jnp.broadcast_to(jnp.sum(x, axis=-1, keepdims=True), (x.shape[0], HEAD))


def _over_keys(c, width):
    return jnp.concatenate([c] * (width // HEAD), axis=1)


def _from(x, r0):
    return x if r0 == 0 else x[r0:]


def _add_from(x, r0, upd):
    return x + upd if r0 == 0 else jnp.concatenate([x[:r0], x[r0:] + upd], axis=0)


def _sb_fwd(qkv, qoff, koff, voff, n_heads, name, comm=None):
    s = qkv.shape[0]
    tq, tb, tk = min(SB_TQ, s), SB_TB, SB_TK
    scale = HEAD ** -0.5
    host = _Host(comm, (n_heads, s // tq))

    def body(*refs):
        q_ref, k_ref, v_ref = refs[:3]
        comm_refs, ((o_ref, tot_ref), _) = host.split(refs, 3, 2)
        host.stage("first", comm_refs)
        i = pl.program_id(1)
        q = q_ref[...]
        after = _tri(tk, lambda a, b: a > b)

        def block(base, carry, o, diag_off):
            halves = list(reversed(range(tb // tk)))
            starts = [pl.multiple_of(base + h * tk, tk) for h in halves]
            r0s = [0 if diag_off is None else diag_off + h * tk for h in halves]
            masks = [None if diag_off is None else _causal(tq - r0, tk) for r0 in r0s]
            z = [lax.dot_general(_from(q, r0),k_ref[pl.ds(st, tk), :], NT_DIMS, preferred_element_type=F32) * scale
                 for st, r0 in zip(starts, r0s)]
            sp = [_softplus(zz) for zz in z]
            logsig = [zz - ss for zz, ss in zip(z, sp)]
            sp = [ss if m is None else jnp.where(m, ss, 0.0) for ss, m in zip(sp, masks)]
            sfx = [_tri_sum(ss, after) for ss in sp]
            probs = []
            for ls, sx, ss, m, r0 in zip(logsig, sfx, sp, masks, r0s):
                a = jnp.exp(ls - _over_keys(_from(carry, r0), tk) - sx)
                probs.append((a if m is None else jnp.where(m, a, 0.0)).astype(BF16))
                carry = _add_from(carry, r0, _rowsum(ss))
            for a, st, r0 in zip(probs, starts, r0s):
                o = _add_from(o, r0, jnp.dot(a, v_ref[pl.ds(st, tk), :], preferred_element_type=F32))
            return carry, o

        carry, o = jnp.zeros((tq, HEAD), F32), jnp.zeros((tq, HEAD), F32)
        for b in reversed(range(tq // tb)):
            carry, o = block(i * tq + b * tb, carry, o, b * tb)
        below = i * (tq // tb)
        carry, o = lax.fori_loop(0, below, lambda jj, co: block((below - 1 - jj) * tb, co[0], co[1], None),
                                 (carry, o))
        o_ref[...] = o.astype(o_ref.dtype)
        tot_ref[...] = carry
        host.stage("mid", comm_refs)
        host.stage("last", comm_refs)

    t = tq
    full = lambda off: pl.BlockSpec((s, HEAD), lambda h, i: (0, off + h))
    tile_spec = lambda off: pl.BlockSpec((t, HEAD), lambda h, i: (i, off + h))
    out = lambda dt: jax.ShapeDtypeStruct((s, n_heads * HEAD), dt)
    return pl.pallas_call(
        body, name=name, grid=(n_heads, s // t),
        in_specs=[tile_spec(qoff), full(koff), full(voff)] + host.in_specs,
        out_specs=[tile_spec(0), tile_spec(0)] + host.out_specs,
        out_shape=[out(BF16), out(F32)] + host.out_shape,
        scratch_shapes=host.scratch,
        compiler_params=_params(("parallel" if comm is None else "arbitrary", "arbitrary"), 40),
    )(qkv, qkv, qkv, *host.ins)


def _sb_bwd(qkv, qoff, koff, voff, do, tot, n_heads, name, comm=None):
    s = qkv.shape[0]
    tq, tb, tk = min(SB_TQ, s), SB_TB, SB_TK
    scale = HEAD ** -0.5
    host = _Host(comm, (n_heads, s // tq))

    def body(*refs):
        q_ref, k_ref, v_ref, do_ref, tot_ref = refs[:5]
        comm_refs, ((dq_ref, dk_ref, dv_ref), _) = host.split(refs, 5, 3)
        host.stage("first", comm_refs)
        i = pl.program_id(1)

        @pl.when(i == 0)
        def _():
            dk_ref[...] = jnp.zeros_like(dk_ref)
            dv_ref[...] = jnp.zeros_like(dv_ref)

        q = q_ref[...]
        do_b = do_ref[...].astype(BF16)
        total = tot_ref[...]
        upto = _tri(tk, lambda a, b: a <= b)
        before = _tri(tk, lambda a, b: a < b)
        def block(base, lsum, psum, dq, diag_off):
            halves = list(range(tb // tk))
            starts = [pl.multiple_of(base + h * tk, tk) for h in halves]
            r0s = [0 if diag_off is None else diag_off + h * tk for h in halves]
            masks = [None if diag_off is None else _causal(tq - r0, tk) for r0 in r0s]
            keep = lambda x, m: x if m is None else jnp.where(m, x, 0.0)
            ks = [k_ref[pl.ds(st, tk), :] for st in starts]
            z = [lax.dot_general(_from(q, r0),kj, NT_DIMS, preferred_element_type=F32) * scale
                 for kj, r0 in zip(ks, r0s)]
            da = [lax.dot_general(_from(do_b, r0), v_ref[pl.ds(st, tk), :], NT_DIMS, preferred_element_type=F32)
                  for st, r0 in zip(starts, r0s)]
            sp = [_softplus(zz) for zz in z]
            logsig = [zz - ss for zz, ss in zip(z, sp)]
            sp = [keep(ss, m) for ss, m in zip(sp, masks)]
            pre = [_tri_sum(ss, upto) for ss in sp]
            probs, p = [], []
            for ls, px, ss, m, dd, r0 in zip(logsig, pre, sp, masks, da, r0s):
                a = keep(jnp.exp(ls - (_over_keys(_from(total, r0) - _from(lsum, r0), tk) - px)), m)
                probs.append(a.astype(BF16))
                p.append(a * dd)
                lsum = _add_from(lsum, r0, _rowsum(ss))
            cs = [_tri_sum(pp, before) for pp in p]
            dzs = []
            for ls, pp, cc, m, r0 in zip(logsig, p, cs, masks, r0s):
                c_all = _over_keys(_from(psum, r0), tk) + cc
                dzs.append(keep((pp - (pp + c_all) * jnp.exp(ls)) * scale, m).astype(BF16))
                psum = _add_from(psum, r0, _rowsum(pp))
            for dz_b, kj, r0 in zip(dzs, ks, r0s):
                dq = _add_from(dq, r0, jnp.dot(dz_b, kj, preferred_element_type=F32))
            for dz_b, a, st, r0 in zip(dzs, probs, starts, r0s):
                dk_ref[pl.ds(st, tk), :] += lax.dot_general(dz_b, _from(q, r0), TN_DIMS, preferred_element_type=F32)
                dv_ref[pl.ds(st, tk), :] += lax.dot_general(a, _from(do_b, r0), TN_DIMS,
                                                            preferred_element_type=F32)
            return lsum, psum, dq

        zero = jnp.zeros((tq, HEAD), F32)
        state = lax.fori_loop(0, i * (tq // tb), lambda j, c: block(j * tb, c[0], c[1], c[2], None),
                              (zero, zero, jnp.zeros((tq, HEAD), F32)))
        for b in range(tq // tb):
            state = block(i * tq + b * tb, *state, b * tb)
        dq_ref[...] = state[2].astype(dq_ref.dtype)
        host.stage("mid", comm_refs)
        host.stage("last", comm_refs)

    t = tq
    full = lambda off: pl.BlockSpec((s, HEAD), lambda h, i: (0, off + h))
    tile_spec = lambda off: pl.BlockSpec((t, HEAD), lambda h, i: (i, off + h))
    w = n_heads * HEAD
    return pl.pallas_call(
        body, name=name, grid=(n_heads, s // t),
        in_specs=[tile_spec(qoff), full(koff), full(voff), tile_spec(0), tile_spec(0)] + host.in_specs,
        out_specs=[tile_spec(0), full(0), full(0)] + host.out_specs,
        out_shape=[jax.ShapeDtypeStruct((s, w), BF16), jax.ShapeDtypeStruct((s, w), F32),
                   jax.ShapeDtypeStruct((s, w), F32)] + host.out_shape,
        scratch_shapes=host.scratch,
        compiler_params=_params(("parallel" if comm is None else "arbitrary", "arbitrary"), 48),
    )(qkv, qkv, qkv, do, tot, *host.ins)


def _coords():
    return lax.axis_index("x"), lax.axis_index("y"), lax.axis_index("c")


def _gather_plan(shards):
    n = len(shards)

    def run(stage, ins, outs, sems):
        send_sems, recv_sems, local_sems = sems
        x, y, c = _coords()
        me, sibling = (x, y, c), (x, y, 1 - c)
        chips = [(1 - x, y), (x, 1 - y), (1 - x, 1 - y)]

        def copy(w, k, block, to, src=None):
            dst = outs[w].at[4 * block[0] + 2 * block[1] + block[2]]
            return pltpu.make_async_remote_copy(
                src_ref=dst if src is None else src, dst_ref=dst,
                send_sem=send_sems.at[7 * w + k], recv_sem=recv_sems.at[7 * w + k],
                device_id=to, device_id_type=MESH)

        def mine():
            return [pltpu.make_async_copy(ins[w], outs[w].at[4 * x + 2 * y + c], local_sems.at[w]) for w in range(n)]

        def first():
            cps = []
            for w in range(n):
                cps.append(copy(w, 0, me, sibling, src=ins[w]))
                cps += [copy(w, 1 + j, me, (*chip, c), src=ins[w]) for j, chip in enumerate(chips)]
            return cps

        def passed():
            return [copy(w, 4 + j, (*chip, c), sibling) for w in range(n) for j, chip in enumerate(chips)]

        if stage == "first":
            for cp in mine() + first():
                cp.start()
        elif stage == "mid":
            onward = passed()
            for w in range(n):
                for j, chip in enumerate(chips):
                    copy(w, 1 + j, (*chip, c), me).wait_recv()
                    onward[3 * w + j].start()
        else:
            for w in range(n):
                copy(w, 0, sibling, me).wait_recv()
                for j, chip in enumerate(chips):
                    copy(w, 4 + j, (*chip, 1 - c), me).wait_recv()
            for cp in first() + passed():
                cp.wait_send()
            for cp in mine():
                cp.wait()

    return dict(
        ins=list(shards), run=run, stages=("first", "mid", "last"),
        out_shape=[jax.ShapeDtypeStruct((N_DEV,) + a.shape, a.dtype) for a in shards],
        scratch=[pltpu.SemaphoreType.DMA((7 * n,)), pltpu.SemaphoreType.DMA((7 * n,)), pltpu.SemaphoreType.DMA((n,))])


def _scatter_plan(grads):
    n = len(grads)

    def run(stage, ins, outs, sems):
        send_sems, recv_sems = sems
        x, y, c = _coords()
        cps = []
        for w in range(n):
            for mask in range(1, N_DEV):
                px, py, pc = x ^ (mask >> 2), y ^ ((mask >> 1) & 1), c ^ (mask & 1)
                cps.append(pltpu.make_async_remote_copy(
                    src_ref=ins[w].at[4 * px + 2 * py + pc], dst_ref=outs[w].at[mask - 1],
                    send_sem=send_sems.at[7 * w + mask - 1], recv_sem=recv_sems.at[7 * w + mask - 1],
                    device_id=(px, py, pc), device_id_type=MESH))
        for cp in cps:
            if stage == "first":
                cp.start()
            else:
                cp.wait()

    return dict(
        ins=list(grads), run=run, stages=("first", "last"),
        out_shape=[jax.ShapeDtypeStruct((N_DEV - 1,) + a.shape[1:], a.dtype) for a in grads],
        scratch=[pltpu.SemaphoreType.DMA((7 * n,)), pltpu.SemaphoreType.DMA((7 * n,))])


def _run_plan(plan, name):
    n_in, n_out = len(plan["ins"]), len(plan["out_shape"])

    def body(*refs):
        for stage in plan["stages"]:
            plan["run"](stage, refs[:n_in], refs[n_in:n_in + n_out], refs[n_in + n_out:])

    any_spec = pl.BlockSpec(memory_space=pl.ANY)
    return pl.pallas_call(
        body, name=name, in_specs=[any_spec] * n_in, out_specs=[any_spec] * n_out,
        out_shape=plan["out_shape"], scratch_shapes=plan["scratch"],
    )(*plan["ins"])


def _gather_rows(v):
    rows, width = v.shape

    def body(v_ref, out_ref, send_sems, recv_sems):
        x, y, c = _coords()
        out_ref[pl.ds(pl.multiple_of((4 * x + 2 * y + c) * rows, rows), rows), :] = v_ref[...]
        cps = []
        for mask in range(1, N_DEV):
            peer = (x ^ (mask >> 2), y ^ ((mask >> 1) & 1), c ^ (mask & 1))
            dst = out_ref.at[pl.ds(pl.multiple_of((4 * x + 2 * y + c) * rows, rows), rows), :]
            cps.append(pltpu.make_async_remote_copy(
                src_ref=v_ref, dst_ref=dst, send_sem=send_sems.at[mask - 1], recv_sem=recv_sems.at[mask - 1],
                device_id=peer, device_id_type=MESH))
        for cp in cps:
            cp.start()
        for cp in cps:
            cp.wait()

    vmem = pl.BlockSpec(memory_space=pltpu.VMEM)
    return pl.pallas_call(
        body, name="gather_small",
        in_specs=[vmem], out_specs=vmem,
        out_shape=jax.ShapeDtypeStruct((N_DEV * rows, width), F32),
        scratch_shapes=[pltpu.SemaphoreType.DMA((N_DEV - 1,)), pltpu.SemaphoreType.DMA((N_DEV - 1,))],
    )(v)


def _adamw(w, g, m, v):
    m = ADAM_B1 * m + (1.0 - ADAM_B1) * g
    v = ADAM_B2 * v + (1.0 - ADAM_B2) * jnp.square(g)
    m_hat = m / (1.0 - ADAM_B1 ** ADAM_STEP)
    v_hat = v / (1.0 - ADAM_B2 ** ADAM_STEP)
    delta = -ADAM_LR * (m_hat / (jnp.sqrt(v_hat) + ADAM_EPS) + ADAM_WD * w)
    return delta, m, v


def _tile_rows(rows, cols):
    tm = 1 << int(math.log2(max(2 * SUBLANES, (1 << 18) // cols)))
    while rows % tm:
        tm //= 2
    assert tm >= 2 * SUBLANES, (rows, cols)
    return tm


def _reduce_adam(w, m, v, own, got, name):
    r, c = w.shape
    tm = max(2 * SUBLANES, _tile_rows(r, c) // 2)

    def fn(wv, mv, vv, a, *peers):
        g = a
        for pv in peers:
            g = g + pv.astype(F32)
        return (g,) + _adamw(wv, g, mv, vv)

    blk = lambda a: (a, (tm, c), lambda i: (i, 0))
    got_blk = lambda j: (got, (None, tm, c), lambda i, j=j: (j, i, 0))
    return _ew(fn, [blk(w), blk(m), blk(v), blk(own)] + [got_blk(j) for j in range(N_DEV - 1)],
               [((r, c), F32, (tm, c), lambda i: (i, 0))] * 4, (r // tm,), name)


def _small_adam(gathered, params, moms, vels, widths):
    n = len(params)
    total = gathered.shape[1]

    def body(*refs):
        g_ref = refs[0]
        p_refs, m_refs, v_refs = refs[1:1 + n], refs[1 + n:1 + 2 * n], refs[1 + 2 * n:1 + 3 * n]
        sum_ref = refs[1 + 3 * n]
        outs = refs[2 + 3 * n:]
        g = g_ref[0:1, :]
        for p in range(1, N_DEV):
            g = g + g_ref[p * SUBLANES:p * SUBLANES + 1, :]
        sum_ref[...] = g
        off = 0
        for i, wd in enumerate(widths):
            d, m2, v2 = _adamw(p_refs[i][...], g[:, off:off + wd], m_refs[i][...], v_refs[i][...])
            outs[3 * i][...] = d
            outs[3 * i + 1][...] = m2
            outs[3 * i + 2][...] = v2
            off += wd

    vmem = pl.BlockSpec(memory_space=pltpu.VMEM)
    out_shape = [jax.ShapeDtypeStruct((1, total), F32)]
    for wd in widths:
        out_shape += [jax.ShapeDtypeStruct((1, wd), F32)] * 3
    return pl.pallas_call(
        body, name="small_adam",
        in_specs=[vmem] * (1 + 3 * n), out_specs=[vmem] * len(out_shape), out_shape=out_shape,
    )(gathered, *params, *moms, *vels)


def _cast_bf16(a, name):
    r, c = a.shape
    tm = _tile_rows(r, c)
    return _ew(lambda v: v, [(a, (tm, c), lambda i: (i, 0))], [((r, c), BF16, (tm, c), lambda i: (i, 0))],
               (r // tm,), name)[0]


def _fold_loss(parts, name):
    r, c = parts.shape

    def fn(v):
        return jnp.broadcast_to(jnp.sum(jnp.sum(v, axis=0, keepdims=True), axis=1, keepdims=True), (SUBLANES, HEAD))

    return _ew(fn, [_const_spec(parts)], [((SUBLANES, HEAD), F32, (SUBLANES, HEAD), lambda i: (0, 0))], (1,),
               name)[0][0:1]


def kernel(x, p, g_mix, w_in, qn_gain, kn_gain, w_branch_a, w_branch_b, w_out, g_mlp, w_up, w_down, g_ple, w_ple_gate, w_ple_proj, loss_target, m_g_mix, m_w_in, m_qn_gain, m_kn_gain, m_w_branch_a, m_w_branch_b, m_w_out, m_g_mlp, m_w_up, m_w_down, m_g_ple, m_w_ple_gate, m_w_ple_proj, v_g_mix, v_w_in, v_qn_gain, v_kn_gain, v_w_branch_a, v_w_branch_b, v_w_out, v_g_mlp, v_w_up, v_w_down, v_g_ple, v_w_ple_gate, v_w_ple_proj):
    x2 = x[0]
    tgt = loss_target[0]
    s, d = x2.shape
    wd_ = w_branch_a.shape[1]
    nh = wd_ // HEAD
    dff = w_up.shape[1]
    qkv_w = 6 * wd_
    tiles = lambda cols: cols // HEAD

    big = [w_in[0], w_branch_a[0], w_branch_b[0], w_out[0], w_up[0], w_down[0], w_ple_gate[0], w_ple_proj[0]]
    names = ["w_in", "w_branch_a", "w_branch_b", "w_out", "w_up", "w_down", "w_ple_gate", "w_ple_proj"]
    row_sharded = [False, False, False, True, False, True, True, False]
    shards = [_cast_bf16(a, "cast_" + nm) for a, nm in zip(big, names)]
    as_weight = lambda g, rs: g.reshape((1, N_DEV * g.shape[1], g.shape[2])) if rs else g
    as_blocks = lambda g, rs: g.reshape((N_DEV, g.shape[1] // N_DEV, g.shape[2])) if rs else g
    win = _run_plan(_gather_plan(shards[:1]), "all_gather_w_in")[0]

    tm = 1024 if s % 1024 == 0 else s
    tm_in = tk_s = 2048 if s % 2048 == 0 else tm
    tn_of = lambda n: 512 if n % 512 == 0 else (256 if n % 256 == 0 else n)
    tn_in = 256 if win.shape[2] % 256 == 0 else HEAD

    h = _rms_fwd(x2, g_mix, "norm_mix")
    qk_raw = _mm("nn", h, win, tm=tm_in, tn=tn_in, tk=d, out_dtypes=[F32], name="proj_qk",
                 n_off=0, n_cnt=2 * wd_ // tn_in)[0]
    rest = _mm("nn", h, win, tm=tm_in, tn=tn_in, tk=d, out_dtypes=[BF16], name="proj_rest",
               n_off=2 * wd_ // tn_in, n_cnt=(win.shape[0] * win.shape[2] - 2 * wd_) // tn_in)[0]
    o_va, o_qb, o_kb, o_vb, o_ga, o_gb = 0, tiles(wd_), tiles(2 * wd_), tiles(3 * wd_), tiles(4 * wd_), tiles(4 * wd_ + d)
    tabs = _rope_tables(s)
    qa = _headnorm_rope(qk_raw, 0, qn_gain, tabs, nh, "rope_q")
    ka = _headnorm_rope(qk_raw, nh, kn_gain, tabs, nh, "rope_k")
    va = rest[:, :wd_]

    outs, lses = [], []
    for dil in DILATIONS:
        o_g, l_g = _dilated_fwd(_phase_major(qa, dil), _phase_major(ka, dil), _phase_major(va, dil), 0, nh,
                                f"dilated_fwd_{dil}")
        outs.append(_token_major(o_g))
        lses.append(_token_major(l_g))
    ya, lse_all = _mix_fwd(outs, lses, "mix_fwd")
    yb, sb_tot, *others = _sb_fwd(rest, o_qb, o_kb, o_vb, nh, "sb_fwd", comm=_gather_plan(shards[1:]))
    wba, wbb, wout, wup, wdown, wgate, wproj = [as_weight(g, rs) for g, rs in zip(others, row_sharded[1:])]

    tn_d = tn_of(wba.shape[2])
    za = _mm("nn", ya, wba, tm=tm, tn=tn_d, tk=wd_, out_dtypes=[BF16], name="branch_a")[0]

    def merge(acc, zav, gav, gbv):
        return _sigmoid(gav.astype(F32)) * zav.astype(F32) + _sigmoid(gbv.astype(F32)) * acc, acc

    merged, zb = _mm("nn", yb, wbb, tm=tm, tn=tn_d, tk=wd_, out_dtypes=[BF16, BF16], name="branch_b_merge",
                     epilogue=merge, extras=[(za, 0), (rest, o_ga * HEAD // tn_d), (rest, o_gb * HEAD // tn_d)])
    x1 = _mm("nn", merged, wout, tm=tm, tn=512, tk=d, out_dtypes=[F32], name="out_proj",
             epilogue=lambda acc, xv: (acc + xv,), extras=[(x2, 0)])[0]

    hm = _rms_fwd(x1, g_mlp, "norm_mlp")
    tn_u = tn_of(wup.shape[2])
    u, act = _mm("nn", hm, wup, tm=tm, tn=tn_u, tk=d, out_dtypes=[BF16, BF16], name="mlp_up",
                 epilogue=lambda acc: (acc, jnp.square(jnp.maximum(acc, 0.0))))
    x3 = _mm("nn", act, wdown, tm=tm, tn=1024, tk=min(dff, 2048), out_dtypes=[F32], name="mlp_down",
             epilogue=lambda acc, xv: (acc + xv,), extras=[(x1, 0)])[0]

    hp = _rms_fwd(x3, g_ple, "norm_ple")
    p_b = _cast_bf16(p[0, 0], "cast_p")
    pp = _mm("nn", p_b, wproj, tm=tm, tn=tn_of(wproj.shape[2]), tk=p_b.shape[1], out_dtypes=[BF16],
             name="ple_proj")[0]

    def head(acc, ppv, xv, tv):
        sg = _sigmoid(acc)
        ppf = ppv.astype(F32)
        err = xv + ppf * sg - tv
        dy = err / d
        sq = jnp.square(err)
        return dy, dy * sg, dy * ppf * sg * (1.0 - sg), sq.reshape(-1, SUBLANES, sq.shape[-1]).sum(axis=0)

    n_i = s // tm
    dy, d_pp, d_gt, sq_parts = _mm(
        "nn", hp, wgate, tm=tm, tn=512, tk=d, out_dtypes=[F32, BF16, BF16], name="ple_gate_loss", epilogue=head,
        extras=[(pp, 0), (x3, 0), (tgt, 0)],
        extra_outs=[((n_i * SUBLANES, d), F32, (SUBLANES, 512), lambda i, j: (i, j))])
    loss_vec = _fold_loss(sq_parts, "loss_fold") * 0.5 / d

    both = [F32, BF16]
    g_wproj = _mm("tn", p_b, d_pp, tm=p_b.shape[1], tn=tn_of(wproj.shape[2]), tk=tk_s, out_dtypes=both,
                  name="grad_w_ple_proj", out_nb=N_DEV)
    g_wgate = _mm("tn", hp, d_gt, tm=512, tn=1024, tk=tk_s, out_dtypes=both, name="grad_w_ple_gate")
    d_hp = _mm("nt", d_gt, wgate, tm=tm, tn=512, tk=d, out_dtypes=[F32], name="d_hp")[0]
    dx3, dx3_b, g_gple = _rms_bwd(d_hp, x3, g_ple, dy, "norm_ple_bwd")

    d_u = _mm("nt", dx3_b, wdown, tm=tm, tn=1024, tk=d, out_dtypes=[BF16], name="d_u",
              epilogue=lambda acc, uv: (acc * (2.0 * jnp.maximum(uv.astype(F32), 0.0)),), extras=[(u, 0)])[0]
    g_wdown = _mm("tn", act, dx3_b, tm=1024, tn=1024, tk=tk_s, out_dtypes=both, name="grad_w_down")
    g_wup = _mm("tn", hm, d_u, tm=1024, tn=wup.shape[2], tk=tk_s, out_dtypes=both, name="grad_w_up", out_nb=N_DEV)
    d_hm = _mm("nt", d_u, wup, tm=tm, tn=1024, tk=wup.shape[2], out_dtypes=[F32], name="d_hm")[0]
    dx1, dx1_b, g_gmlp = _rms_bwd(d_hm, x1, g_mlp, dx3, "norm_mlp_bwd")

    def unmerge(acc, gav, gbv, zav, zbv):
        sa, sb = _sigmoid(gav.astype(F32)), _sigmoid(gbv.astype(F32))
        return acc * sa, acc * sb, acc * zav.astype(F32) * sa * (1.0 - sa), acc * zbv.astype(F32) * sb * (1.0 - sb)

    d_za, d_zb, d_ga, d_gb = _mm(
        "nt", dx1_b, wout, tm=tm, tn=512, tk=d, out_dtypes=[BF16] * 4, name="d_merged", epilogue=unmerge,
        extras=[(rest, o_ga * HEAD // 512), (rest, o_gb * HEAD // 512), (za, 0), (zb, 0)])
    g_wout = _mm("tn", merged, dx1_b, tm=512, tn=1024, tk=tk_s, out_dtypes=both, name="grad_w_out")
    g_wba = _mm("tn", ya, d_za, tm=wd_, tn=wba.shape[2], tk=tk_s, out_dtypes=both, name="grad_w_branch_a",
                out_nb=N_DEV)
    g_wbb = _mm("tn", yb, d_zb, tm=wd_, tn=wbb.shape[2], tk=tk_s, out_dtypes=both, name="grad_w_branch_b",
                out_nb=N_DEV)
    d_ya = _mm("nt", d_za, wba, tm=tm, tn=wd_, tk=wba.shape[2], out_dtypes=[F32], name="d_ya")[0]
    d_yb = _mm("nt", d_zb, wbb, tm=tm, tn=wd_, tk=wbb.shape[2], out_dtypes=[F32], name="d_yb")[0]

    grads = [None, g_wba, g_wbb, g_wout, g_wup, g_wdown, g_wgate, g_wproj]
    early = _scatter_plan([as_blocks(g[1], rs) for g, rs in zip(grads[1:], row_sharded[1:])])
    d_qb, d_kb, d_vb, *got = _sb_bwd(rest, o_qb, o_kb, o_vb, d_yb, sb_tot, nh, "sb_bwd", comm=early)
    d_ya_b, stats = _mix_stats(d_ya, ya, lse_all, nh, "mix_stats")
    dqs, dks, dvs = [], [], []
    for dil in DILATIONS:
        dq_g, dk_g, dv_g = _dilated_bwd(
            _phase_major(qa, dil), _phase_major(ka, dil), _phase_major(va, dil), 0, _phase_major(d_ya_b, dil),
            _phase_major(stats, dil), nh, f"dilated_bwd_{dil}")
        dqs.append(_token_major(dq_g))
        dks.append(_token_major(dk_g))
        dvs.append(_token_major(dv_g))
    d_qa, g_qn = _headnorm_rope_bwd(dqs, qk_raw, 0, qn_gain, tabs, nh, "rope_q_bwd")
    d_ka, g_kn = _headnorm_rope_bwd(dks, qk_raw, nh, kn_gain, tabs, nh, "rope_k_bwd")
    tmr = 256
    d_va = _ew(lambda a, b, c: a + b + c, [_row_spec(a, tmr) for a in dvs],
               [((s, wd_), BF16, (tmr, wd_), lambda i: (i, 0))], (s // tmr,), "sum_dv")[0]
    d_proj = jnp.concatenate([d_qa, d_ka, d_va, d_qb, d_kb.astype(BF16), d_vb.astype(BF16), d_ga, d_gb], axis=1)

    grads[0] = _mm("tn", h, d_proj, tm=1024, tn=win.shape[2], tk=tm, out_dtypes=both, name="grad_w_in",
                   out_nb=N_DEV)
    d_h, got_in = _mm("nt", d_proj, win, tm=tm, tn=1024, tk=win.shape[2], out_dtypes=[F32], name="d_h",
                      comm=_scatter_plan([grads[0][1]]))
    got = [got_in] + got
    dx, _, g_gmix = _rms_bwd(d_h, x2, g_mix, dx1, "norm_mix_bwd")

    cx, cy, cc = _coords()
    me = 4 * cx + 2 * cy + cc
    moms = [m_w_in, m_w_branch_a, m_w_branch_b, m_w_out, m_w_up, m_w_down, m_w_ple_gate, m_w_ple_proj]
    vels = [v_w_in, v_w_branch_a, v_w_branch_b, v_w_out, v_w_up, v_w_down, v_w_ple_gate, v_w_ple_proj]
    big_out = {}
    for i, nm in enumerate(names):
        own = lax.dynamic_index_in_dim(as_blocks(grads[i][0], row_sharded[i]), me, axis=0, keepdims=False)
        big_out[nm] = [a[None] for a in _reduce_adam(big[i], moms[i][0], vels[i][0], own, got[i], "adam_" + nm)]

    small_names = ["g_mix", "qn_gain", "kn_gain", "g_mlp", "g_ple"]
    small_p = [g_mix, qn_gain, kn_gain, g_mlp, g_ple]
    small_m = [m_g_mix, m_qn_gain, m_kn_gain, m_g_mlp, m_g_ple]
    small_v = [v_g_mix, v_qn_gain, v_kn_gain, v_g_mlp, v_g_ple]
    small_g = [g_gmix, g_qn, g_kn, g_gmlp, g_gple]
    widths = [a.shape[1] for a in small_p]
    vec = jnp.concatenate(small_g + [loss_vec], axis=1)
    vec = jnp.pad(vec, ((0, SUBLANES - 1), (0, 0)))
    res = _small_adam(_gather_rows(vec), small_p, small_m, small_v, widths)
    summed = res[0]
    small_out, off = {}, 0
    for i, nm in enumerate(small_names):
        small_out[nm] = [summed[:, off:off + widths[i]]] + list(res[1 + 3 * i:4 + 3 * i])
        off += widths[i]
    loss = summed[0, off]

    order = ["g_mix", "w_in", "qn_gain", "kn_gain", "w_branch_a", "w_branch_b", "w_out", "g_mlp", "w_up", "w_down",
             "g_ple", "w_ple_gate", "w_ple_proj"]
    table = {**big_out, **small_out}
    result = [loss, dx[None]]
    for kind in range(4):
        result += [table[nm][kind] for nm in order]
    return tuple(result)
```

```python
import functools
import math

import jax
import jax.numpy as jnp
from jax import lax
from jax.experimental import pallas as pl
from jax.experimental.pallas import tpu as pltpu

F32 = jnp.float32
BF16 = jnp.bfloat16
MESH = pl.DeviceIdType.MESH

HEAD = 128
WINDOW = 128
DILATIONS = (1, 4, 16)
ROT = HEAD // 4
ROPE_THETA = 500000.0
EPS = 1e-6
NEG = -1e30
N_DEV = 8

ADAM_LR = 0.001
ADAM_B1 = 0.9
ADAM_B2 = 0.999
ADAM_EPS = 1e-08
ADAM_WD = 0.01
ADAM_STEP = 10

SUBLANES = 8
VMEM_CAP_MB = 56

NT_DIMS = (((1,), (1,)), ((), ()))
TN_DIMS = (((0,), (0,)), ((), ()))


def _params(semantics, vmem_mb):
    return pltpu.CompilerParams(dimension_semantics=semantics, vmem_limit_bytes=min(vmem_mb, VMEM_CAP_MB) << 20)


def _sigmoid(x):
    return 1.0 / (1.0 + jnp.exp(-x))


def _split_bf16(x):
    hi = x.astype(BF16)
    lo = (x - hi.astype(F32)).astype(BF16)
    return hi, lo


class _Host:
    def __init__(self, comm, grid):
        self.comm, self.grid = comm, grid
        any_spec = pl.BlockSpec(memory_space=pl.ANY)
        self.ins = list(comm["ins"]) if comm else []
        self.out_shape = list(comm["out_shape"]) if comm else []
        self.scratch = list(comm["scratch"]) if comm else []
        self.in_specs = [any_spec] * len(self.ins)
        self.out_specs = [any_spec] * len(self.out_shape)

    def split(self, refs, n_in, n_out):
        pos = n_in
        c_in = refs[pos:pos + len(self.ins)]
        pos += len(self.ins)
        outs = refs[pos:pos + n_out]
        pos += n_out
        c_out = refs[pos:pos + len(self.out_shape)]
        pos += len(self.out_shape)
        own = len(refs) - pos - len(self.scratch)
        return (c_in, c_out, refs[pos + own:]), (outs, refs[pos:pos + own])

    def stage(self, which, comm_refs):
        if self.comm is None or which not in self.comm["stages"]:
            return
        grid = self.grid
        at = {"first": [0] * len(grid), "mid": [grid[0] // 2] + [0] * (len(grid) - 1),
              "last": [g - 1 for g in grid]}[which]
        cond = functools.reduce(jnp.logical_and, [pl.program_id(ax) == v for ax, v in enumerate(at)])

        @pl.when(cond)
        def _():
            self.comm["run"](which, *comm_refs)


def _ew(fn, ins, outs, grid, name, colsums=(), vmem_mb=40):
    n_in, n_out, n_cs = len(ins), len(outs), len(colsums)
    steps = math.prod(grid)

    def body(*refs):
        in_refs = refs[:n_in]
        out_refs = refs[n_in:n_in + n_out]
        cs_refs = refs[n_in + n_out:n_in + n_out + n_cs]
        acc_refs = refs[n_in + n_out + n_cs:]
        vals = fn(*[r[...] for r in in_refs])
        if not isinstance(vals, tuple):
            vals = (vals,)
        for r, v in zip(out_refs, vals[:n_out]):
            r[...] = v.astype(r.dtype)
        if n_cs:
            step = pl.program_id(0)
            for ax in range(1, len(grid)):
                step = step * grid[ax] + pl.program_id(ax)
            for acc, cs, v in zip(acc_refs, cs_refs, vals[n_out:]):
                part = v.reshape(-1, SUBLANES, v.shape[-1]).sum(axis=0)

                @pl.when(step == 0)
                def _(acc=acc, part=part):
                    acc[...] = part

                @pl.when(step > 0)
                def _(acc=acc, part=part):
                    acc[...] += part

                @pl.when(step == steps - 1)
                def _(acc=acc, cs=cs):
                    cs[...] = acc[...].sum(axis=0, keepdims=True)

    out_shape = [jax.ShapeDtypeStruct(s, d) for s, d, _, _ in outs]
    out_specs = [pl.BlockSpec(b, m) for _, _, b, m in outs]
    for w in colsums:
        out_shape.append(jax.ShapeDtypeStruct((1, w), F32))
        out_specs.append(pl.BlockSpec((1, w), lambda *_: (0, 0)))
    sem = ("arbitrary",) * len(grid) if n_cs else ("parallel",) * len(grid)
    res = pl.pallas_call(
        body, name=name, grid=grid,
        in_specs=[pl.BlockSpec(b, m) for _, b, m in ins],
        out_specs=out_specs, out_shape=out_shape,
        scratch_shapes=[pltpu.VMEM((SUBLANES, w), F32) for w in colsums],
        compiler_params=_params(sem, vmem_mb),
    )(*[a for a, _, _ in ins])
    return res


def _row_spec(a, tm):
    return (a, (tm, a.shape[1]), lambda i: (i, 0))


def _const_spec(a):
    return (a, a.shape, lambda *_: (0,) * a.ndim)


def _mm(mode, a, b, *, tm, tn, tk, out_dtypes, name, epilogue=None, extras=(), n_off=0, n_cnt=None,
        out_nb=1, extra_outs=(), vmem_mb=52, comm=None):
    if mode == "nn":
        m, kdim = a.shape
        nb, _, n = b.shape
        npb = n // tn
        ncols = nb * n
        n_tiles = (ncols // tn) if n_cnt is None else n_cnt
        a_spec = pl.BlockSpec((tm, tk), lambda i, j, k: (i, k))
        b_spec = pl.BlockSpec((None, tk, tn), lambda i, j, k: ((j + n_off) // npb, k, (j + n_off) % npb))
        dims = (((1,), (0,)), ((), ()))
    elif mode == "nt":
        m, kdim = a.shape
        nb, nout, n = b.shape
        kpb = n // tk
        n_tiles = nout // tn
        a_spec = pl.BlockSpec((tm, tk), lambda i, j, k: (i, k))
        b_spec = pl.BlockSpec((None, tn, tk), lambda i, j, k: (k // kpb, j, k % kpb))
        dims = NT_DIMS
    else:
        kdim, m = a.shape
        ncols = b.shape[1]
        n_tiles = ncols // tn
        a_spec = pl.BlockSpec((tk, tm), lambda i, j, k: (k, i))
        b_spec = pl.BlockSpec((tk, tn), lambda i, j, k: (k, j))
        dims = TN_DIMS
    nk = kdim // tk
    assert kdim % tk == 0 and m % tm == 0
    grid = (m // tm, n_tiles, nk)
    n_ex, n_out = len(extras), len(out_dtypes) + len(extra_outs)
    host = _Host(comm, grid)

    def body(*refs):
        a_ref, b_ref = refs[0], refs[1]
        ex_refs = refs[2:2 + n_ex]
        comm_refs, (out_refs, scratch) = host.split(refs, 2 + n_ex, n_out)
        host.stage("first", comm_refs)

        def finish(acc):
            vals = (acc,) * n_out if epilogue is None else epilogue(acc, *[r[...] for r in ex_refs])
            for r, v in zip(out_refs, vals):
                r[...] = v.astype(r.dtype)

        def part():
            return lax.dot_general(a_ref[...], b_ref[...], dims, preferred_element_type=F32)

        if nk == 1:
            finish(part())
        else:
            acc_ref = scratch[0]
            k = pl.program_id(2)

            @pl.when(k == 0)
            def _():
                acc_ref[...] = part()

            @pl.when((k > 0) & (k < nk - 1))
            def _():
                acc_ref[...] += part()

            @pl.when(k == nk - 1)
            def _():
                finish(acc_ref[...] + part())

        host.stage("mid", comm_refs)
        host.stage("last", comm_refs)

    if mode == "tn":
        npo = (ncols // out_nb) // tn
        out_shape = [jax.ShapeDtypeStruct((out_nb, m, ncols // out_nb), d) for d in out_dtypes]
        out_specs = [pl.BlockSpec((None, tm, tn), lambda i, j, k: (j // npo, i, j % npo)) for _ in out_dtypes]
    else:
        out_shape = [jax.ShapeDtypeStruct((m, n_tiles * tn), d) for d in out_dtypes]
        out_specs = [pl.BlockSpec((tm, tn), lambda i, j, k: (i, j)) for _ in out_dtypes]
    for s, d, blk, imap in extra_outs:
        out_shape.append(jax.ShapeDtypeStruct(s, d))
        out_specs.append(pl.BlockSpec(blk, lambda i, j, k, imap=imap: imap(i, j)))
    ex_specs = [pl.BlockSpec((tm, tn), lambda i, j, k, off=off: (i, j + off)) for _, off in extras]
    sem = ("parallel", "parallel", "arbitrary") if comm is None else ("arbitrary",) * 3
    return pl.pallas_call(
        body, name=name, grid=grid,
        in_specs=[a_spec, b_spec] + ex_specs + host.in_specs,
        out_specs=out_specs + host.out_specs, out_shape=out_shape + host.out_shape,
        scratch_shapes=([pltpu.VMEM((tm, tn), F32)] if nk > 1 else []) + host.scratch,
        compiler_params=_params(sem, vmem_mb),
    )(a, b, *[e for e, _ in extras], *host.ins)


def _rms_fwd(x, g, name):
    tm = 256

    def fn(xv, gv):
        r = lax.rsqrt(jnp.mean(xv * xv, axis=-1, keepdims=True) + EPS)
        return xv * r * gv

    return _ew(fn, [_row_spec(x, tm), _const_spec(g)], [(x.shape, BF16, (tm, x.shape[1]), lambda i: (i, 0))],
               (x.shape[0] // tm,), name)[0]


def _rms_bwd(dh, x, g, res, name):
    tm = 256
    d = x.shape[1]

    def fn(dhv, xv, gv, rv):
        r = lax.rsqrt(jnp.mean(xv * xv, axis=-1, keepdims=True) + EPS)
        xh = xv * r
        dyg = dhv * gv
        dx = rv + r * (dyg - xh * jnp.mean(dyg * xh, axis=-1, keepdims=True))
        return dx, dx, dhv * xh

    spec = lambda dt: (x.shape, dt, (tm, d), lambda i: (i, 0))
    return _ew(fn, [_row_spec(dh, tm), _row_spec(x, tm), _const_spec(g), _row_spec(res, tm)],
               [spec(F32), spec(BF16)], (x.shape[0] // tm,), name, colsums=(d,))


def _rope_tables(s):
    half = ROT // 2
    pos = jnp.arange(s, dtype=F32)
    inv = ROPE_THETA ** (-jnp.arange(0, ROT, 2, dtype=F32) / ROT)
    ang = pos[:, None] * inv[None, :]
    cos, sin = jnp.cos(ang), jnp.sin(ang)
    pad = jnp.zeros((s, HEAD - ROT), F32)
    c = jnp.concatenate([cos, cos, pad + 1.0], axis=1)
    a = jnp.concatenate([-sin, jnp.zeros_like(sin), pad], axis=1)
    b = jnp.concatenate([jnp.zeros_like(sin), sin, pad], axis=1)
    return c, a, b


def _heads(x, n_heads):
    return [x[:, h * HEAD:(h + 1) * HEAD] for h in range(n_heads)]


def _headnorm_rope(proj, part, gain, tabs, n_heads, name):
    tm = 512
    s = proj.shape[0]
    w = n_heads * HEAD

    def fn(xs, gv, c, a, b):
        outs = []
        for xv in _heads(xs, n_heads):
            r = lax.rsqrt(jnp.mean(xv * xv, axis=-1, keepdims=True) + EPS)
            y = xv * r * gv
            outs.append(c * y + a * pltpu.roll(y, HEAD - ROT // 2, 1) + b * pltpu.roll(y, ROT // 2, 1))
        return jnp.concatenate(outs, axis=1)

    tab = lambda t: (t, (tm, HEAD), lambda i: (i, 0))
    return _ew(fn, [(proj, (tm, w), lambda i: (i, part)), (gain, (1, HEAD), lambda i: (0, 0))] + [tab(t) for t in tabs],
               [((s, w), BF16, (tm, w), lambda i: (i, 0))], (s // tm,), name)[0]


def _headnorm_rope_bwd(dys, proj, part, gain, tabs, n_heads, name):
    tm = 256
    s = proj.shape[0]
    w = n_heads * HEAD
    n_dy = len(dys)

    def fn(*vals):
        dy_all = vals[0]
        for v in vals[1:n_dy]:
            dy_all = dy_all + v
        xs, gv, c, a, b = vals[n_dy:]
        dxs, dgain = [], None
        for dy, xv in zip(_heads(dy_all, n_heads), _heads(xs, n_heads)):
            dn = c * dy + pltpu.roll(a * dy, ROT // 2, 1) + pltpu.roll(b * dy, HEAD - ROT // 2, 1)
            r = lax.rsqrt(jnp.mean(xv * xv, axis=-1, keepdims=True) + EPS)
            xh = xv * r
            dyg = dn * gv
            dxs.append(r * (dyg - xh * jnp.mean(dyg * xh, axis=-1, keepdims=True)))
            dgain = dn * xh if dgain is None else dgain + dn * xh
        return jnp.concatenate(dxs, axis=1), dgain

    tab = lambda t: (t, (tm, HEAD), lambda i: (i, 0))
    return _ew(fn, [(d, (tm, w), lambda i: (i, 0)) for d in dys]
               + [(proj, (tm, w), lambda i: (i, part)), (gain, (1, HEAD), lambda i: (0, 0))] + [tab(t) for t in tabs],
               [((s, w), BF16, (tm, w), lambda i: (i, 0))], (s // tm,), name, colsums=(HEAD,))


def _phase_major(a, d):
    s, w = a.shape
    if d == 1:
        return a.reshape(1, s, w)
    return a.reshape(s // d, d, w).transpose(1, 0, 2)


def _token_major(a):
    d, m, w = a.shape
    if d == 1:
        return a.reshape(m, w)
    return a.transpose(1, 0, 2).reshape(m * d, w)


def _dil_tq(m):
    return min(1024, m)


def _dilated_fwd(q, k, v, voff, n_heads, name):
    d, m, _ = q.shape
    tq = _dil_tq(m)
    nq = m // tq
    per = tq // WINDOW
    scale = HEAD ** -0.5

    def body(q_ref, kc_ref, kp_ref, vc_ref, vp_ref, o_ref, l_ref):
        n = pl.program_id(2)
        kk = jnp.concatenate([kp_ref[...], kc_ref[...]], axis=0)
        vv = jnp.concatenate([vp_ref[...], vc_ref[...]], axis=0)
        row = lax.broadcasted_iota(jnp.int32, (WINDOW, 2 * WINDOW), 0)
        col = lax.broadcasted_iota(jnp.int32, (WINDOW, 2 * WINDOW), 1)
        band = (col >= row) & (col <= row + WINDOW)
        q = q_ref[...]
        rows = [slice(b * WINDOW, (b + 1) * WINDOW) for b in range(per)]
        keys = [slice(b * WINDOW, (b + 2) * WINDOW) for b in range(per)]
        s = [lax.dot_general(q[rows[b]], kk[keys[b]], NT_DIMS, preferred_element_type=F32) * scale
             for b in range(per)]
        es, outs, lses = [], [], []
        for b in range(per):
            valid = band if b else band & ((n > 0) | (col >= WINDOW))
            sb = jnp.where(valid, s[b], NEG)
            mx = jnp.max(sb, axis=-1, keepdims=True)
            e = jnp.exp(sb - mx)
            den = jnp.sum(e, axis=-1, keepdims=True)
            es.append((e.astype(BF16), den))
            lses.append(jnp.broadcast_to(mx + jnp.log(den), (WINDOW, HEAD)))
        for b in range(per):
            outs.append(jnp.dot(es[b][0], vv[keys[b]], preferred_element_type=F32) / es[b][1])
        o_ref[...] = jnp.concatenate(outs, axis=0)
        l_ref[...] = jnp.concatenate(lses, axis=0)

    cur = lambda off: pl.BlockSpec((None, tq, HEAD), lambda r, h, n: (r, n, off + h))
    prev = lambda off: pl.BlockSpec((None, WINDOW, HEAD), lambda r, h, n: (r, jnp.maximum(n * per - 1, 0), off + h))
    out = jax.ShapeDtypeStruct((d, m, n_heads * HEAD), F32)
    return pl.pallas_call(
        body, name=name, grid=(d, n_heads, nq),
        in_specs=[cur(0), cur(0), prev(0), cur(voff), prev(voff)],
        out_specs=[cur(0), cur(0)], out_shape=[out, out],
        compiler_params=_params(("parallel",) * 3, 32),
    )(q, k, k, v, v)


def _dilated_bwd(q, k, v, voff, dy, stats, n_heads, name):
    d, m, _ = q.shape
    tq = _dil_tq(m)
    nq = m // tq
    per = tq // WINDOW
    last_blk = m // WINDOW - 1
    scale = HEAD ** -0.5

    def body(qc_ref, qn_ref, kc_ref, kp_ref, vc_ref, vp_ref, dyc_ref, dyn_ref, sc_ref, sn_ref,
             dq_ref, dk_ref, dv_ref):
        n = pl.program_id(2)
        kk = jnp.concatenate([kp_ref[...], kc_ref[...]], axis=0)
        vv = jnp.concatenate([vp_ref[...], vc_ref[...]], axis=0)
        qq = jnp.concatenate([qc_ref[...], qn_ref[...]], axis=0)
        dyy = jnp.concatenate([dyc_ref[...], dyn_ref[...]], axis=0)
        st = jnp.concatenate([sc_ref[...], sn_ref[...]], axis=0)
        half = HEAD // 2
        row = lax.broadcasted_iota(jnp.int32, (WINDOW, 2 * WINDOW), 0)
        col = lax.broadcasted_iota(jnp.int32, (WINDOW, 2 * WINDOW), 1)
        band = (col >= row) & (col <= row + WINDOW)
        rows = [slice(b * WINDOW, (b + 1) * WINDOW) for b in range(per)]
        wide = [slice(b * WINDOW, (b + 2) * WINDOW) for b in range(per)]
        nt = lambda a, b: lax.dot_general(a, b, NT_DIMS, preferred_element_type=F32)
        s = [nt(qq[rows[b]], kk[wide[b]]) * scale for b in range(per)]
        dp = [nt(dyy[rows[b]], vv[wide[b]]) for b in range(per)]
        ds = []
        for b in range(per):
            valid = band if b else band & ((n > 0) | (col >= WINDOW))
            p = jnp.where(valid, jnp.exp(jnp.minimum(s[b] - st[rows[b]][:, :1], 0.0)), 0.0)
            ds.append((p * (dp[b] - st[rows[b]][:, half:half + 1]) * scale).astype(BF16))
        dq = [jnp.dot(ds[b], kk[wide[b]], preferred_element_type=F32) for b in range(per)]
        kc, vc = kc_ref[...], vc_ref[...]
        s2 = [nt(kc[rows[b]], qq[wide[b]]) * scale for b in range(per)]
        dp2 = [nt(vc[rows[b]], dyy[wide[b]]) for b in range(per)]
        ds2, p2 = [], []
        for b in range(per):
            valid = band if b < per - 1 else band & ((n < nq - 1) | (col < WINDOW))
            st_t = st[wide[b]].T
            pb = jnp.where(valid, jnp.exp(jnp.minimum(s2[b] - st_t[:1], 0.0)), 0.0)
            ds2.append((pb * (dp2[b] - st_t[half:half + 1]) * scale).astype(BF16))
            p2.append(pb.astype(BF16))
        dk = [jnp.dot(ds2[b], qq[wide[b]], preferred_element_type=F32) for b in range(per)]
        dv = [jnp.dot(p2[b], dyy[wide[b]], preferred_element_type=F32) for b in range(per)]
        dq_ref[...] = jnp.concatenate(dq, axis=0)
        dk_ref[...] = jnp.concatenate(dk, axis=0)
        dv_ref[...] = jnp.concatenate(dv, axis=0)

    cur = lambda off: pl.BlockSpec((None, tq, HEAD), lambda r, h, n: (r, n, off + h))
    prev = lambda off: pl.BlockSpec((None, WINDOW, HEAD), lambda r, h, n: (r, jnp.maximum(n * per - 1, 0), off + h))
    nxt = lambda off: pl.BlockSpec((None, WINDOW, HEAD),
                                   lambda r, h, n: (r, jnp.minimum((n + 1) * per, last_blk), off + h))
    out = jax.ShapeDtypeStruct((d, m, n_heads * HEAD), F32)
    return pl.pallas_call(
        body, name=name, grid=(d, n_heads, nq),
        in_specs=[cur(0), nxt(0), cur(0), prev(0), cur(voff), prev(voff), cur(0), nxt(0), cur(0), nxt(0)],
        out_specs=[cur(0)] * 3, out_shape=[out] * 3,
        compiler_params=_params(("parallel",) * 3, 40),
    )(q, q, k, k, v, v, dy, dy, stats, stats)


def _mix_fwd(os_, ls_, name):
    tm = 256
    s, w = os_[0].shape
    n = len(os_)

    def fn(*vals):
        o, l = vals[:n], vals[n:]
        mx = functools.reduce(jnp.maximum, l)
        e = [jnp.exp(x - mx) for x in l]
        den = functools.reduce(jnp.add, e)
        y = functools.reduce(jnp.add, [ei * oi for ei, oi in zip(e, o)]) / den
        return y, mx + jnp.log(den)

    return _ew(fn, [_row_spec(a, tm) for a in list(os_) + list(ls_)],
               [((s, w), BF16, (tm, w), lambda i: (i, 0)), ((s, w), F32, (tm, w), lambda i: (i, 0))],
               (s // tm,), name)


def _mix_stats(dy, y, lse, n_heads, name):
    tm = 256
    s = dy.shape[0]
    w = n_heads * HEAD

    def fn(dys, ys, ls):
        lane = lax.broadcasted_iota(jnp.int32, (tm, HEAD), 1)
        packed = []
        for a, b, l in zip(_heads(dys, n_heads), _heads(ys, n_heads), _heads(ls, n_heads)):
            delta = jnp.sum(a * b.astype(F32), axis=-1, keepdims=True)
            packed.append(jnp.where(lane < HEAD // 2, l, delta))
        return dys, jnp.concatenate(packed, axis=1)

    blk = lambda a: (a, (tm, w), lambda i: (i, 0))
    out = lambda dt: ((s, w), dt, (tm, w), lambda i: (i, 0))
    return _ew(fn, [blk(dy), blk(y), blk(lse)], [out(BF16), out(F32)], (s // tm,), name)


SB_TQ = 1024
SB_TB = 512
SB_TK = 256


def _softplus(z):
    return jnp.where(z > 20.0, z, jnp.log(1.0 + jnp.exp(z)))


def _tri(t, cmp):
    rows = lax.broadcasted_iota(jnp.int32, (2 * t, t), 0)
    cols = lax.broadcasted_iota(jnp.int32, (2 * t, t), 1)
    return jnp.where(cmp(jnp.where(rows >= t, rows - t, rows), cols), 1.0, 0.0).astype(BF16)


def _tri_sum(x, tri):
    return jnp.dot(jnp.concatenate(_split_bf16(x), axis=1), tri, preferred_element_type=F32)


def _causal(rows, cols):
    return lax.broadcasted_iota(jnp.int32, (rows, cols), 1) < lax.broadcasted_iota(jnp.int32, (rows, cols), 0)


def _rowsum(x):
    return jnp.broadcast_to(jnp.sum(x, axis=-1, keepdims=True), (x.shape[0], HEAD))


def _over_keys(c, width):
    return jnp.concatenate([c] * (width // HEAD), axis=1)


def _from(x, r0):
    return x if r0 == 0 else x[r0:]


def _add_from(x, r0, upd):
    return x + upd if r0 == 0 else jnp.concatenate([x[:r0], x[r0:] + upd], axis=0)


def _sb_fwd(qkv, qoff, koff, voff, n_heads, name, comm=None):
    s = qkv.shape[0]
    tq, tb, tk = min(SB_TQ, s), SB_TB, SB_TK
    scale = HEAD ** -0.5
    host = _Host(comm, (n_heads, s // tq))

    def body(*refs):
        q_ref, k_ref, v_ref = refs[:3]
        comm_refs, ((o_ref, tot_ref), _) = host.split(refs, 3, 2)
        host.stage("first", comm_refs)
        i = pl.program_id(1)
        q = q_ref[...]
        after = _tri(tk, lambda a, b: a > b)

        def block(base, carry, o, diag_off):
            halves = list(reversed(range(tb // tk)))
            starts = [pl.multiple_of(base + h * tk, tk) for h in halves]
            r0s = [0 if diag_off is None else diag_off + h * tk for h in halves]
            masks = [None if diag_off is None else _causal(tq - r0, tk) for r0 in r0s]
            z = [lax.dot_general(_from(q, r0),k_ref[pl.ds(st, tk), :], NT_DIMS, preferred_element_type=F32) * scale
                 for st, r0 in zip(starts, r0s)]
            sp = [_softplus(zz) for zz in z]
            logsig = [zz - ss for zz, ss in zip(z, sp)]
            sp = [ss if m is None else jnp.where(m, ss, 0.0) for ss, m in zip(sp, masks)]
            sfx = [_tri_sum(ss, after) for ss in sp]
            probs = []
            for ls, sx, ss, m, r0 in zip(logsig, sfx, sp, masks, r0s):
                a = jnp.exp(ls - _over_keys(_from(carry, r0), tk) - sx)
                probs.append((a if m is None else jnp.where(m, a, 0.0)).astype(BF16))
                carry = _add_from(carry, r0, _rowsum(ss))
            for a, st, r0 in zip(probs, starts, r0s):
                o = _add_from(o, r0, jnp.dot(a, v_ref[pl.ds(st, tk), :], preferred_element_type=F32))
            return carry, o

        carry, o = jnp.zeros((tq, HEAD), F32), jnp.zeros((tq, HEAD), F32)
        for b in reversed(range(tq // tb)):
            carry, o = block(i * tq + b * tb, carry, o, b * tb)
        below = i * (tq // tb)
        carry, o = lax.fori_loop(0, below, lambda jj, co: block((below - 1 - jj) * tb, co[0], co[1], None),
                                 (carry, o))
        o_ref[...] = o.astype(o_ref.dtype)
        tot_ref[...] = carry
        host.stage("mid", comm_refs)
        host.stage("last", comm_refs)

    t = tq
    full = lambda off: pl.BlockSpec((s, HEAD), lambda h, i: (0, off + h))
    tile_spec = lambda off: pl.BlockSpec((t, HEAD), lambda h, i: (i, off + h))
    out = lambda dt: jax.ShapeDtypeStruct((s, n_heads * HEAD), dt)
    return pl.pallas_call(
        body, name=name, grid=(n_heads, s // t),
        in_specs=[tile_spec(qoff), full(koff), full(voff)] + host.in_specs,
        out_specs=[tile_spec(0), tile_spec(0)] + host.out_specs,
        out_shape=[out(BF16), out(F32)] + host.out_shape,
        scratch_shapes=host.scratch,
        compiler_params=_params(("parallel" if comm is None else "arbitrary", "arbitrary"), 40),
    )(qkv, qkv, qkv, *host.ins)


def _sb_bwd(qkv, qoff, koff, voff, do, tot, n_heads, name, comm=None):
    s = qkv.shape[0]
    tq, tb, tk = min(SB_TQ, s), SB_TB, SB_TK
    scale = HEAD ** -0.5
    host = _Host(comm, (n_heads, s // tq))

    def body(*refs):
        q_ref, k_ref, v_ref, do_ref, tot_ref = refs[:5]
        comm_refs, ((dq_ref, dk_ref, dv_ref), _) = host.split(refs, 5, 3)
        host.stage("first", comm_refs)
        i = pl.program_id(1)

        @pl.when(i == 0)
        def _():
            dk_ref[...] = jnp.zeros_like(dk_ref)
            dv_ref[...] = jnp.zeros_like(dv_ref)

        q = q_ref[...]
        do_b = do_ref[...].astype(BF16)
        total = tot_ref[...]
        upto = _tri(tk, lambda a, b: a <= b)
        before = _tri(tk, lambda a, b: a < b)
        def block(base, lsum, psum, dq, diag_off):
            halves = list(range(tb // tk))
            starts = [pl.multiple_of(base + h * tk, tk) for h in halves]
            r0s = [0 if diag_off is None else diag_off + h * tk for h in halves]
            masks = [None if diag_off is None else _causal(tq - r0, tk) for r0 in r0s]
            keep = lambda x, m: x if m is None else jnp.where(m, x, 0.0)
            ks = [k_ref[pl.ds(st, tk), :] for st in starts]
            z = [lax.dot_general(_from(q, r0),kj, NT_DIMS, preferred_element_type=F32) * scale
                 for kj, r0 in zip(ks, r0s)]
            da = [lax.dot_general(_from(do_b, r0), v_ref[pl.ds(st, tk), :], NT_DIMS, preferred_element_type=F32)
                  for st, r0 in zip(starts, r0s)]
            sp = [_softplus(zz) for zz in z]
            logsig = [zz - ss for zz, ss in zip(z, sp)]
            sp = [keep(ss, m) for ss, m in zip(sp, masks)]
            pre = [_tri_sum(ss, upto) for ss in sp]
            probs, p = [], []
            for ls, px, ss, m, dd, r0 in zip(logsig, pre, sp, masks, da, r0s):
                a = keep(jnp.exp(ls - (_over_keys(_from(total, r0) - _from(lsum, r0), tk) - px)), m)
                probs.append(a.astype(BF16))
                p.append(a * dd)
                lsum = _add_from(lsum, r0, _rowsum(ss))
            cs = [_tri_sum(pp, before) for pp in p]
            dzs = []
            for ls, pp, cc, m, r0 in zip(logsig, p, cs, masks, r0s):
                c_all = _over_keys(_from(psum, r0), tk) + cc
                dzs.append(keep((pp - (pp + c_all) * jnp.exp(ls)) * scale, m).astype(BF16))
                psum = _add_from(psum, r0, _rowsum(pp))
            for dz_b, kj, r0 in zip(dzs, ks, r0s):
                dq = _add_from(dq, r0, jnp.dot(dz_b, kj, preferred_element_type=F32))
            for dz_b, a, st, r0 in zip(dzs, probs, starts, r0s):
                dk_ref[pl.ds(st, tk), :] += lax.dot_general(dz_b, _from(q, r0), TN_DIMS, preferred_element_type=F32)
                dv_ref[pl.ds(st, tk), :] += lax.dot_general(a, _from(do_b, r0), TN_DIMS,
                                                            preferred_element_type=F32)
            return lsum, psum, dq

        zero = jnp.zeros((tq, HEAD), F32)
        state = lax.fori_loop(0, i * (tq // tb), lambda j, c: block(j * tb, c[0], c[1], c[2], None),
                              (zero, zero, jnp.zeros((tq, HEAD), F32)))
        for b in range(tq // tb):
            state = block(i * tq + b * tb, *state, b * tb)
        dq_ref[...] = state[2].astype(dq_ref.dtype)
        host.stage("mid", comm_refs)
        host.stage("last", comm_refs)

    t = tq
    full = lambda off: pl.BlockSpec((s, HEAD), lambda h, i: (0, off + h))
    tile_spec = lambda off: pl.BlockSpec((t, HEAD), lambda h, i: (i, off + h))
    w = n_heads * HEAD
    return pl.pallas_call(
        body, name=name, grid=(n_heads, s // t),
        in_specs=[tile_spec(qoff), full(koff), full(voff), tile_spec(0), tile_spec(0)] + host.in_specs,
        out_specs=[tile_spec(0), full(0), full(0)] + host.out_specs,
        out_shape=[jax.ShapeDtypeStruct((s, w), BF16), jax.ShapeDtypeStruct((s, w), F32),
                   jax.ShapeDtypeStruct((s, w), F32)] + host.out_shape,
        scratch_shapes=host.scratch,
        compiler_params=_params(("parallel" if comm is None else "arbitrary", "arbitrary"), 48),
    )(qkv, qkv, qkv, do, tot, *host.ins)


def _coords():
    return lax.axis_index("x"), lax.axis_index("y"), lax.axis_index("c")


def _gather_plan(shards):
    n = len(shards)

    def run(stage, ins, outs, sems):
        send_sems, recv_sems, local_sems = sems
        x, y, c = _coords()
        me, sibling = (x, y, c), (x, y, 1 - c)
        chips = [(1 - x, y), (x, 1 - y), (1 - x, 1 - y)]

        def copy(w, k, block, to, src=None):
            dst = outs[w].at[4 * block[0] + 2 * block[1] + block[2]]
            return pltpu.make_async_remote_copy(
                src_ref=dst if src is None else src, dst_ref=dst,
                send_sem=send_sems.at[7 * w + k], recv_sem=recv_sems.at[7 * w + k],
                device_id=to, device_id_type=MESH)

        def mine():
            return [pltpu.make_async_copy(ins[w], outs[w].at[4 * x + 2 * y + c], local_sems.at[w]) for w in range(n)]

        def first():
            cps = []
            for w in range(n):
                cps.append(copy(w, 0, me, sibling, src=ins[w]))
                cps += [copy(w, 1 + j, me, (*chip, c), src=ins[w]) for j, chip in enumerate(chips)]
            return cps

        def passed():
            return [copy(w, 4 + j, (*chip, c), sibling) for w in range(n) for j, chip in enumerate(chips)]

        if stage == "first":
            for cp in mine() + first():
                cp.start()
        elif stage == "mid":
            onward = passed()
            for w in range(n):
                for j, chip in enumerate(chips):
                    copy(w, 1 + j, (*chip, c), me).wait_recv()
                    onward[3 * w + j].start()
        else:
            for w in range(n):
                copy(w, 0, sibling, me).wait_recv()
                for j, chip in enumerate(chips):
                    copy(w, 4 + j, (*chip, 1 - c), me).wait_recv()
            for cp in first() + passed():
                cp.wait_send()
            for cp in mine():
                cp.wait()

    return dict(
        ins=list(shards), run=run, stages=("first", "mid", "last"),
        out_shape=[jax.ShapeDtypeStruct((N_DEV,) + a.shape, a.dtype) for a in shards],
        scratch=[pltpu.SemaphoreType.DMA((7 * n,)), pltpu.SemaphoreType.DMA((7 * n,)), pltpu.SemaphoreType.DMA((n,))])


def _scatter_plan(grads):
    n = len(grads)

    def run(stage, ins, outs, sems):
        send_sems, recv_sems = sems
        x, y, c = _coords()
        cps = []
        for w in range(n):
            for mask in range(1, N_DEV):
                px, py, pc = x ^ (mask >> 2), y ^ ((mask >> 1) & 1), c ^ (mask & 1)
                cps.append(pltpu.make_async_remote_copy(
                    src_ref=ins[w].at[4 * px + 2 * py + pc], dst_ref=outs[w].at[mask - 1],
                    send_sem=send_sems.at[7 * w + mask - 1], recv_sem=recv_sems.at[7 * w + mask - 1],
                    device_id=(px, py, pc), device_id_type=MESH))
        for cp in cps:
            if stage == "first":
                cp.start()
            else:
                cp.wait()

    return dict(
        ins=list(grads), run=run, stages=("first", "last"),
        out_shape=[jax.ShapeDtypeStruct((N_DEV - 1,) + a.shape[1:], a.dtype) for a in grads],
        scratch=[pltpu.SemaphoreType.DMA((7 * n,)), pltpu.SemaphoreType.DMA((7 * n,))])


def _run_plan(plan, name):
    n_in, n_out = len(plan["ins"]), len(plan["out_shape"])

    def body(*refs):
        for stage in plan["stages"]:
            plan["run"](stage, refs[:n_in], refs[n_in:n_in + n_out], refs[n_in + n_out:])

    any_spec = pl.BlockSpec(memory_space=pl.ANY)
    return pl.pallas_call(
        body, name=name, in_specs=[any_spec] * n_in, out_specs=[any_spec] * n_out,
        out_shape=plan["out_shape"], scratch_shapes=plan["scratch"],
    )(*plan["ins"])


def _gather_rows(v):
    rows, width = v.shape

    def body(v_ref, out_ref, send_sems, recv_sems):
        x, y, c = _coords()
        out_ref[pl.ds(pl.multiple_of((4 * x + 2 * y + c) * rows, rows), rows), :] = v_ref[...]
        cps = []
        for mask in range(1, N_DEV):
            peer = (x ^ (mask >> 2), y ^ ((mask >> 1) & 1), c ^ (mask & 1))
            dst = out_ref.at[pl.ds(pl.multiple_of((4 * x + 2 * y + c) * rows, rows), rows), :]
            cps.append(pltpu.make_async_remote_copy(
                src_ref=v_ref, dst_ref=dst, send_sem=send_sems.at[mask - 1], recv_sem=recv_sems.at[mask - 1],
                device_id=peer, device_id_type=MESH))
        for cp in cps:
            cp.start()
        for cp in cps:
            cp.wait()

    vmem = pl.BlockSpec(memory_space=pltpu.VMEM)
    return pl.pallas_call(
        body, name="gather_small",
        in_specs=[vmem], out_specs=vmem,
        out_shape=jax.ShapeDtypeStruct((N_DEV * rows, width), F32),
        scratch_shapes=[pltpu.SemaphoreType.DMA((N_DEV - 1,)), pltpu.SemaphoreType.DMA((N_DEV - 1,))],
    )(v)


def _adamw(w, g, m, v):
    m = ADAM_B1 * m + (1.0 - ADAM_B1) * g
    v = ADAM_B2 * v + (1.0 - ADAM_B2) * jnp.square(g)
    m_hat = m / (1.0 - ADAM_B1 ** ADAM_STEP)
    v_hat = v / (1.0 - ADAM_B2 ** ADAM_STEP)
    delta = -ADAM_LR * (m_hat / (jnp.sqrt(v_hat) + ADAM_EPS) + ADAM_WD * w)
    return delta, m, v


def _tile_rows(rows, cols):
    tm = 1 << int(math.log2(max(2 * SUBLANES, (1 << 18) // cols)))
    while rows % tm:
        tm //= 2
    assert tm >= 2 * SUBLANES, (rows, cols)
    return tm


def _reduce_adam(w, m, v, own, got, name):
    r, c = w.shape
    tm = max(2 * SUBLANES, _tile_rows(r, c) // 2)

    def fn(wv, mv, vv, a, *peers):
        g = a
        for pv in peers:
            g = g + pv.astype(F32)
        return (g,) + _adamw(wv, g, mv, vv)

    blk = lambda a: (a, (tm, c), lambda i: (i, 0))
    got_blk = lambda j: (got, (None, tm, c), lambda i, j=j: (j, i, 0))
    return _ew(fn, [blk(w), blk(m), blk(v), blk(own)] + [got_blk(j) for j in range(N_DEV - 1)],
               [((r, c), F32, (tm, c), lambda i: (i, 0))] * 4, (r // tm,), name)


def _small_adam(gathered, params, moms, vels, widths):
    n = len(params)
    total = gathered.shape[1]

    def body(*refs):
        g_ref = refs[0]
        p_refs, m_refs, v_refs = refs[1:1 + n], refs[1 + n:1 + 2 * n], refs[1 + 2 * n:1 + 3 * n]
        sum_ref = refs[1 + 3 * n]
        outs = refs[2 + 3 * n:]
        g = g_ref[0:1, :]
        for p in range(1, N_DEV):
            g = g + g_ref[p * SUBLANES:p * SUBLANES + 1, :]
        sum_ref[...] = g
        off = 0
        for i, wd in enumerate(widths):
            d, m2, v2 = _adamw(p_refs[i][...], g[:, off:off + wd], m_refs[i][...], v_refs[i][...])
            outs[3 * i][...] = d
            outs[3 * i + 1][...] = m2
            outs[3 * i + 2][...] = v2
            off += wd

    vmem = pl.BlockSpec(memory_space=pltpu.VMEM)
    out_shape = [jax.ShapeDtypeStruct((1, total), F32)]
    for wd in widths:
        out_shape += [jax.ShapeDtypeStruct((1, wd), F32)] * 3
    return pl.pallas_call(
        body, name="small_adam",
        in_specs=[vmem] * (1 + 3 * n), out_specs=[vmem] * len(out_shape), out_shape=out_shape,
    )(gathered, *params, *moms, *vels)


def _cast_bf16(a, name):
    r, c = a.shape
    tm = _tile_rows(r, c)
    return _ew(lambda v: v, [(a, (tm, c), lambda i: (i, 0))], [((r, c), BF16, (tm, c), lambda i: (i, 0))],
               (r // tm,), name)[0]


def _fold_loss(parts, name):
    r, c = parts.shape

    def fn(v):
        return jnp.broadcast_to(jnp.sum(jnp.sum(v, axis=0, keepdims=True), axis=1, keepdims=True), (SUBLANES, HEAD))

    return _ew(fn, [_const_spec(parts)], [((SUBLANES, HEAD), F32, (SUBLANES, HEAD), lambda i: (0, 0))], (1,),
               name)[0][0:1]


def kernel(x, p, g_mix, w_in, qn_gain, kn_gain, w_branch_a, w_branch_b, w_out, g_mlp, w_up, w_down, g_ple, w_ple_gate, w_ple_proj, loss_target, m_g_mix, m_w_in, m_qn_gain, m_kn_gain, m_w_branch_a, m_w_branch_b, m_w_out, m_g_mlp, m_w_up, m_w_down, m_g_ple, m_w_ple_gate, m_w_ple_proj, v_g_mix, v_w_in, v_qn_gain, v_kn_gain, v_w_branch_a, v_w_branch_b, v_w_out, v_g_mlp, v_w_up, v_w_down, v_g_ple, v_w_ple_gate, v_w_ple_proj):
    x2 = x[0]
    tgt = loss_target[0]
    s, d = x2.shape
    wd_ = w_branch_a.shape[1]
    nh = wd_ // HEAD
    dff = w_up.shape[1]
    qkv_w = 6 * wd_
    tiles = lambda cols: cols // HEAD

    big = [w_in[0], w_branch_a[0], w_branch_b[0], w_out[0], w_up[0], w_down[0], w_ple_gate[0], w_ple_proj[0]]
    names = ["w_in", "w_branch_a", "w_branch_b", "w_out", "w_up", "w_down", "w_ple_gate", "w_ple_proj"]
    row_sharded = [False, False, False, True, False, True, True, False]
    shards = [_cast_bf16(a, "cast_" + nm) for a, nm in zip(big, names)]
    as_weight = lambda g, rs: g.reshape((1, N_DEV * g.shape[1], g.shape[2])) if rs else g
    as_blocks = lambda g, rs: g.reshape((N_DEV, g.shape[1] // N_DEV, g.shape[2])) if rs else g
    win = _run_plan(_gather_plan(shards[:1]), "all_gather_w_in")[0]

    tm = 1024 if s % 1024 == 0 else s
    tm_in = tk_s = 2048 if s % 2048 == 0 else tm
    tn_of = lambda n: 512 if n % 512 == 0 else (256 if n % 256 == 0 else n)
    tn_in = 256 if win.shape[2] % 256 == 0 else HEAD

    h = _rms_fwd(x2, g_mix, "norm_mix")
    qk_raw = _mm("nn", h, win, tm=tm_in, tn=tn_in, tk=d, out_dtypes=[F32], name="proj_qk",
                 n_off=0, n_cnt=2 * wd_ // tn_in)[0]
    rest = _mm("nn", h, win, tm=tm_in, tn=tn_in, tk=d, out_dtypes=[BF16], name="proj_rest",
               n_off=2 * wd_ // tn_in, n_cnt=(win.shape[0] * win.shape[2] - 2 * wd_) // tn_in)[0]
    o_va, o_qb, o_kb, o_vb, o_ga, o_gb = 0, tiles(wd_), tiles(2 * wd_), tiles(3 * wd_), tiles(4 * wd_), tiles(4 * wd_ + d)
    tabs = _rope_tables(s)
    qa = _headnorm_rope(qk_raw, 0, qn_gain, tabs, nh, "rope_q")
    ka = _headnorm_rope(qk_raw, 1, kn_gain, tabs, nh, "rope_k")
    va = rest[:, :wd_]

    outs, lses = [], []
    for dil in DILATIONS:
        o_g, l_g = _dilated_fwd(_phase_major(qa, dil), _phase_major(ka, dil), _phase_major(va, dil), 0, nh,
                                f"dilated_fwd_{dil}")
        outs.append(_token_major(o_g))
        lses.append(_token_major(l_g))
    ya, lse_all = _mix_fwd(outs, lses, "mix_fwd")
    yb, sb_tot, *others = _sb_fwd(rest, o_qb, o_kb, o_vb, nh, "sb_fwd", comm=_gather_plan(shards[1:]))
    wba, wbb, wout, wup, wdown, wgate, wproj = [as_weight(g, rs) for g, rs in zip(others, row_sharded[1:])]

    tn_d = tn_of(wba.shape[2])
    za = _mm("nn", ya, wba, tm=tm, tn=tn_d, tk=wd_, out_dtypes=[BF16], name="branch_a")[0]

    def merge(acc, zav, gav, gbv):
        return _sigmoid(gav.astype(F32)) * zav.astype(F32) + _sigmoid(gbv.astype(F32)) * acc, acc

    merged, zb = _mm("nn", yb, wbb, tm=tm, tn=tn_d, tk=wd_, out_dtypes=[BF16, BF16], name="branch_b_merge",
                     epilogue=merge, extras=[(za, 0), (rest, o_ga * HEAD // tn_d), (rest, o_gb * HEAD // tn_d)])
    x1 = _mm("nn", merged, wout, tm=tm, tn=512, tk=d, out_dtypes=[F32], name="out_proj",
             epilogue=lambda acc, xv: (acc + xv,), extras=[(x2, 0)])[0]

    hm = _rms_fwd(x1, g_mlp, "norm_mlp")
    tn_u = tn_of(wup.shape[2])
    u, act = _mm("nn", hm, wup, tm=tm, tn=tn_u, tk=d, out_dtypes=[BF16, BF16], name="mlp_up",
                 epilogue=lambda acc: (acc, jnp.square(jnp.maximum(acc, 0.0))))
    x3 = _mm("nn", act, wdown, tm=tm, tn=1024, tk=min(dff, 2048), out_dtypes=[F32], name="mlp_down",
             epilogue=lambda acc, xv: (acc + xv,), extras=[(x1, 0)])[0]

    hp = _rms_fwd(x3, g_ple, "norm_ple")
    p_b = _cast_bf16(p[0, 0], "cast_p")
    pp = _mm("nn", p_b, wproj, tm=tm, tn=tn_of(wproj.shape[2]), tk=p_b.shape[1], out_dtypes=[BF16],
             name="ple_proj")[0]

    def head(acc, ppv, xv, tv):
        sg = _sigmoid(acc)
        ppf = ppv.astype(F32)
        err = xv + ppf * sg - tv
        dy = err / d
        sq = jnp.square(err)
        return dy, dy * sg, dy * ppf * sg * (1.0 - sg), sq.reshape(-1, SUBLANES, sq.shape[-1]).sum(axis=0)

    n_i = s // tm
    dy, d_pp, d_gt, sq_parts = _mm(
        "nn", hp, wgate, tm=tm, tn=512, tk=d, out_dtypes=[F32, BF16, BF16], name="ple_gate_loss", epilogue=head,
        extras=[(pp, 0), (x3, 0), (tgt, 0)],
        extra_outs=[((n_i * SUBLANES, d), F32, (SUBLANES, 512), lambda i, j: (i, j))])
    loss_vec = _fold_loss(sq_parts, "loss_fold") * 0.5 / d

    both = [F32, BF16]
    g_wproj = _mm("tn", p_b, d_pp, tm=p_b.shape[1], tn=tn_of(wproj.shape[2]), tk=tk_s, out_dtypes=both,
                  name="grad_w_ple_proj", out_nb=N_DEV)
    g_wgate = _mm("tn", hp, d_gt, tm=1024, tn=1024, tk=tk_s, out_dtypes=both, name="grad_w_ple_gate")
    d_hp = _mm("nt", d_gt, wgate, tm=tm, tn=512, tk=d, out_dtypes=[F32], name="d_hp")[0]
    dx3, dx3_b, g_gple = _rms_bwd(d_hp, x3, g_ple, dy, "norm_ple_bwd")

    d_u = _mm("nt", dx3_b, wdown, tm=tm, tn=1024, tk=d, out_dtypes=[BF16], name="d_u",
              epilogue=lambda acc, uv: (acc * (2.0 * jnp.maximum(uv.astype(F32), 0.0)),), extras=[(u, 0)])[0]
    g_wdown = _mm("tn", act, dx3_b, tm=1024, tn=1024, tk=tk_s, out_dtypes=both, name="grad_w_down")
    g_wup = _mm("tn", hm, d_u, tm=1024, tn=wup.shape[2], tk=tk_s, out_dtypes=both, name="grad_w_up", out_nb=N_DEV)
    d_hm = _mm("nt", d_u, wup, tm=tm, tn=1024, tk=wup.shape[2], out_dtypes=[F32], name="d_hm")[0]
    dx1, dx1_b, g_gmlp = _rms_bwd(d_hm, x1, g_mlp, dx3, "norm_mlp_bwd")

    def unmerge(acc, gav, gbv, zav, zbv):
        sa, sb = _sigmoid(gav.astype(F32)), _sigmoid(gbv.astype(F32))
        return acc * sa, acc * sb, acc * zav.astype(F32) * sa * (1.0 - sa), acc * zbv.astype(F32) * sb * (1.0 - sb)

    d_za, d_zb, d_ga, d_gb = _mm(
        "nt", dx1_b, wout, tm=tm, tn=512, tk=d, out_dtypes=[BF16] * 4, name="d_merged", epilogue=unmerge,
        extras=[(rest, o_ga * HEAD // 512), (rest, o_gb * HEAD // 512), (za, 0), (zb, 0)])
    g_wout = _mm("tn", merged, dx1_b, tm=1024, tn=1024, tk=tk_s, out_dtypes=both, name="grad_w_out")
    g_wba = _mm("tn", ya, d_za, tm=wd_, tn=wba.shape[2], tk=tk_s, out_dtypes=both, name="grad_w_branch_a",
                out_nb=N_DEV)
    g_wbb = _mm("tn", yb, d_zb, tm=wd_, tn=wbb.shape[2], tk=tk_s, out_dtypes=both, name="grad_w_branch_b",
                out_nb=N_DEV)
    d_ya = _mm("nt", d_za, wba, tm=tm, tn=wd_, tk=wba.shape[2], out_dtypes=[F32], name="d_ya")[0]
    d_yb = _mm("nt", d_zb, wbb, tm=tm, tn=wd_, tk=wbb.shape[2], out_dtypes=[F32], name="d_yb")[0]

    grads = [None, g_wba, g_wbb, g_wout, g_wup, g_wdown, g_wgate, g_wproj]
    early = _scatter_plan([as_blocks(g[1], rs) for g, rs in zip(grads[1:], row_sharded[1:])])
    d_qb, d_kb, d_vb, *got = _sb_bwd(rest, o_qb, o_kb, o_vb, d_yb, sb_tot, nh, "sb_bwd", comm=early)
    d_ya_b, stats = _mix_stats(d_ya, ya, lse_all, nh, "mix_stats")
    dqs, dks, dvs = [], [], []
    for dil in DILATIONS:
        dq_g, dk_g, dv_g = _dilated_bwd(
            _phase_major(qa, dil), _phase_major(ka, dil), _phase_major(va, dil), 0, _phase_major(d_ya_b, dil),
            _phase_major(stats, dil), nh, f"dilated_bwd_{dil}")
        dqs.append(_token_major(dq_g))
        dks.append(_token_major(dk_g))
        dvs.append(_token_major(dv_g))
    d_qa, g_qn = _headnorm_rope_bwd(dqs, qk_raw, 0, qn_gain, tabs, nh, "rope_q_bwd")
    d_ka, g_kn = _headnorm_rope_bwd(dks, qk_raw, 1, kn_gain, tabs, nh, "rope_k_bwd")
    tmr = 256
    d_va = _ew(lambda a, b, c: a + b + c, [_row_spec(a, tmr) for a in dvs],
               [((s, wd_), BF16, (tmr, wd_), lambda i: (i, 0))], (s // tmr,), "sum_dv")[0]
    d_proj = jnp.concatenate([d_qa, d_ka, d_va, d_qb, d_kb.astype(BF16), d_vb.astype(BF16), d_ga, d_gb], axis=1)

    grads[0] = _mm("tn", h, d_proj, tm=1024, tn=win.shape[2], tk=tm, out_dtypes=both, name="grad_w_in",
                   out_nb=N_DEV)
    d_h, got_in = _mm("nt", d_proj, win, tm=tm, tn=1024, tk=win.shape[2], out_dtypes=[F32], name="d_h",
                      comm=_scatter_plan([grads[0][1]]))
    got = [got_in] + got
    dx, _, g_gmix = _rms_bwd(d_h, x2, g_mix, dx1, "norm_mix_bwd")

    cx, cy, cc = _coords()
    me = 4 * cx + 2 * cy + cc
    moms = [m_w_in, m_w_branch_a, m_w_branch_b, m_w_out, m_w_up, m_w_down, m_w_ple_gate, m_w_ple_proj]
    vels = [v_w_in, v_w_branch_a, v_w_branch_b, v_w_out, v_w_up, v_w_down, v_w_ple_gate, v_w_ple_proj]
    big_out = {}
    for i, nm in enumerate(names):
        own = lax.dynamic_index_in_dim(as_blocks(grads[i][0], row_sharded[i]), me, axis=0, keepdims=False)
        big_out[nm] = [a[None] for a in _reduce_adam(big[i], moms[i][0], vels[i][0], own, got[i], "adam_" + nm)]

    small_names = ["g_mix", "qn_gain", "kn_gain", "g_mlp", "g_ple"]
    small_p = [g_mix, qn_gain, kn_gain, g_mlp, g_ple]
    small_m = [m_g_mix, m_qn_gain, m_kn_gain, m_g_mlp, m_g_ple]
    small_v = [v_g_mix, v_qn_gain, v_kn_gain, v_g_mlp, v_g_ple]
    small_g = [g_gmix, g_qn, g_kn, g_gmlp, g_gple]
    widths = [a.shape[1] for a in small_p]
    vec = jnp.concatenate(small_g + [loss_vec], axis=1)
    vec = jnp.pad(vec, ((0, SUBLANES - 1), (0, 0)))
    res = _small_adam(_gather_rows(vec), small_p, small_m, small_v, widths)
    summed = res[0]
    small_out, off = {}, 0
    for i, nm in enumerate(small_names):
        small_out[nm] = [summed[:, off:off + widths[i]]] + list(res[1 + 3 * i:4 + 3 * i])
        off += widths[i]
    loss = summed[0, off]

    order = ["g_mix", "w_in", "qn_gain", "kn_gain", "w_branch_a", "w_branch_b", "w_out", "g_mlp", "w_up", "w_down",
             "g_ple", "w_ple_gate", "w_ple_proj"]
    table = {**big_out, **small_out}
    result = [loss, dx[None]]
    for kind in range(4):
        result += [table[nm][kind] for nm in order]
    return tuple(result)
```

```python
import functools
import math

import jax
import jax.numpy as jnp
from jax import lax
from jax.experimental import pallas as pl
from jax.experimental.pallas import tpu as pltpu

F32 = jnp.float32
BF16 = jnp.bfloat16
MESH = pl.DeviceIdType.MESH

HEAD = 128
WINDOW = 128
DILATIONS = (1, 4, 16)
ROT = HEAD // 4
ROPE_THETA = 500000.0
EPS = 1e-6
NEG = -1e30
N_DEV = 8

ADAM_LR = 0.001
ADAM_B1 = 0.9
ADAM_B2 = 0.999
ADAM_EPS = 1e-08
ADAM_WD = 0.01
ADAM_STEP = 10

SUBLANES = 8
VMEM_CAP_MB = 56

NT_DIMS = (((1,), (1,)), ((), ()))
TN_DIMS = (((0,), (0,)), ((), ()))


def _params(semantics, vmem_mb):
    return pltpu.CompilerParams(dimension_semantics=semantics, vmem_limit_bytes=min(vmem_mb, VMEM_CAP_MB) << 20)


def _sigmoid(x):
    return 0.5 + 0.5 * jnp.tanh(0.5 * x)


def _split_bf16(x):
    hi = x.astype(BF16)
    lo = (x - hi.astype(F32)).astype(BF16)
    return hi, lo


class _Host:
    def __init__(self, comm, grid):
        self.comm, self.grid = comm, grid
        any_spec = pl.BlockSpec(memory_space=pl.ANY)
        self.ins = list(comm["ins"]) if comm else []
        self.out_shape = list(comm["out_shape"]) if comm else []
        self.scratch = list(comm["scratch"]) if comm else []
        self.in_specs = [any_spec] * len(self.ins)
        self.out_specs = [any_spec] * len(self.out_shape)

    def split(self, refs, n_in, n_out):
        pos = n_in
        c_in = refs[pos:pos + len(self.ins)]
        pos += len(self.ins)
        outs = refs[pos:pos + n_out]
        pos += n_out
        c_out = refs[pos:pos + len(self.out_shape)]
        pos += len(self.out_shape)
        own = len(refs) - pos - len(self.scratch)
        return (c_in, c_out, refs[pos + own:]), (outs, refs[pos:pos + own])

    def stage(self, which, comm_refs):
        if self.comm is None or which not in self.comm["stages"]:
            return
        grid = self.grid
        at = {"first": [0] * len(grid), "mid": [grid[0] // 2] + [0] * (len(grid) - 1),
              "last": [g - 1 for g in grid]}[which]
        cond = functools.reduce(jnp.logical_and, [pl.program_id(ax) == v for ax, v in enumerate(at)])

        @pl.when(cond)
        def _():
            self.comm["run"](which, *comm_refs)


def _ew(fn, ins, outs, grid, name, colsums=(), vmem_mb=40):
    n_in, n_out, n_cs = len(ins), len(outs), len(colsums)
    steps = math.prod(grid)

    def body(*refs):
        in_refs = refs[:n_in]
        out_refs = refs[n_in:n_in + n_out]
        cs_refs = refs[n_in + n_out:n_in + n_out + n_cs]
        acc_refs = refs[n_in + n_out + n_cs:]
        vals = fn(*[r[...] for r in in_refs])
        if not isinstance(vals, tuple):
            vals = (vals,)
        for r, v in zip(out_refs, vals[:n_out]):
            r[...] = v.astype(r.dtype)
        if n_cs:
            step = pl.program_id(0)
            for ax in range(1, len(grid)):
                step = step * grid[ax] + pl.program_id(ax)
            for acc, cs, v in zip(acc_refs, cs_refs, vals[n_out:]):
                part = v.reshape(-1, SUBLANES, v.shape[-1]).sum(axis=0)

                @pl.when(step == 0)
                def _(acc=acc, part=part):
                    acc[...] = part

                @pl.when(step > 0)
                def _(acc=acc, part=part):
                    acc[...] += part

                @pl.when(step == steps - 1)
                def _(acc=acc, cs=cs):
                    cs[...] = acc[...].sum(axis=0, keepdims=True)

    out_shape = [jax.ShapeDtypeStruct(s, d) for s, d, _, _ in outs]
    out_specs = [pl.BlockSpec(b, m) for _, _, b, m in outs]
    for w in colsums:
        out_shape.append(jax.ShapeDtypeStruct((1, w), F32))
        out_specs.append(pl.BlockSpec((1, w), lambda *_: (0, 0)))
    sem = ("arbitrary",) * len(grid) if n_cs else ("parallel",) * len(grid)
    res = pl.pallas_call(
        body, name=name, grid=grid,
        in_specs=[pl.BlockSpec(b, m) for _, b, m in ins],
        out_specs=out_specs, out_shape=out_shape,
        scratch_shapes=[pltpu.VMEM((SUBLANES, w), F32) for w in colsums],
        compiler_params=_params(sem, vmem_mb),
    )(*[a for a, _, _ in ins])
    return res


def _row_spec(a, tm):
    return (a, (tm, a.shape[1]), lambda i: (i, 0))


def _const_spec(a):
    return (a, a.shape, lambda *_: (0,) * a.ndim)


def _mm(mode, a, b, *, tm, tn, tk, out_dtypes, name, epilogue=None, extras=(), n_off=0, n_cnt=None,
        out_nb=1, extra_outs=(), vmem_mb=52, comm=None):
    if mode == "nn":
        m, kdim = a.shape
        nb, _, n = b.shape
        npb = n // tn
        ncols = nb * n
        n_tiles = (ncols // tn) if n_cnt is None else n_cnt
        a_spec = pl.BlockSpec((tm, tk), lambda i, j, k: (i, k))
        b_spec = pl.BlockSpec((None, tk, tn), lambda i, j, k: ((j + n_off) // npb, k, (j + n_off) % npb))
        dims = (((1,), (0,)), ((), ()))
    elif mode == "nt":
        m, kdim = a.shape
        nb, nout, n = b.shape
        kpb = n // tk
        n_tiles = nout // tn
        a_spec = pl.BlockSpec((tm, tk), lambda i, j, k: (i, k))
        b_spec = pl.BlockSpec((None, tn, tk), lambda i, j, k: (k // kpb, j, k % kpb))
        dims = NT_DIMS
    else:
        kdim, m = a.shape
        ncols = b.shape[1]
        n_tiles = ncols // tn
        a_spec = pl.BlockSpec((tk, tm), lambda i, j, k: (k, i))
        b_spec = pl.BlockSpec((tk, tn), lambda i, j, k: (k, j))
        dims = TN_DIMS
    nk = kdim // tk
    assert kdim % tk == 0 and m % tm == 0
    grid = (m // tm, n_tiles, nk)
    n_ex, n_out = len(extras), len(out_dtypes) + len(extra_outs)
    host = _Host(comm, grid)

    def body(*refs):
        a_ref, b_ref = refs[0], refs[1]
        ex_refs = refs[2:2 + n_ex]
        comm_refs, (out_refs, scratch) = host.split(refs, 2 + n_ex, n_out)
        host.stage("first", comm_refs)

        def finish(acc):
            vals = (acc,) * n_out if epilogue is None else epilogue(acc, *[r[...] for r in ex_refs])
            for r, v in zip(out_refs, vals):
                r[...] = v.astype(r.dtype)

        def part():
            return lax.dot_general(a_ref[...], b_ref[...], dims, preferred_element_type=F32)

        if nk == 1:
            finish(part())
        else:
            acc_ref = scratch[0]
            k = pl.program_id(2)

            @pl.when(k == 0)
            def _():
                acc_ref[...] = part()

            @pl.when((k > 0) & (k < nk - 1))
            def _():
                acc_ref[...] += part()

            @pl.when(k == nk - 1)
            def _():
                finish(acc_ref[...] + part())

        host.stage("mid", comm_refs)
        host.stage("last", comm_refs)

    if mode == "tn":
        npo = (ncols // out_nb) // tn
        out_shape = [jax.ShapeDtypeStruct((out_nb, m, ncols // out_nb), d) for d in out_dtypes]
        out_specs = [pl.BlockSpec((None, tm, tn), lambda i, j, k: (j // npo, i, j % npo)) for _ in out_dtypes]
    else:
        out_shape = [jax.ShapeDtypeStruct((m, n_tiles * tn), d) for d in out_dtypes]
        out_specs = [pl.BlockSpec((tm, tn), lambda i, j, k: (i, j)) for _ in out_dtypes]
    for s, d, blk, imap in extra_outs:
        out_shape.append(jax.ShapeDtypeStruct(s, d))
        out_specs.append(pl.BlockSpec(blk, lambda i, j, k, imap=imap: imap(i, j)))
    ex_specs = [pl.BlockSpec((tm, tn), lambda i, j, k, off=off: (i, j + off)) for _, off in extras]
    sem = ("parallel", "parallel", "arbitrary") if comm is None else ("arbitrary",) * 3
    return pl.pallas_call(
        body, name=name, grid=grid,
        in_specs=[a_spec, b_spec] + ex_specs + host.in_specs,
        out_specs=out_specs + host.out_specs, out_shape=out_shape + host.out_shape,
        scratch_shapes=([pltpu.VMEM((tm, tn), F32)] if nk > 1 else []) + host.scratch,
        compiler_params=_params(sem, vmem_mb),
    )(a, b, *[e for e, _ in extras], *host.ins)


def _rms_fwd(x, g, name):
    tm = 256

    def fn(xv, gv):
        r = lax.rsqrt(jnp.mean(xv * xv, axis=-1, keepdims=True) + EPS)
        return xv * r * gv

    return _ew(fn, [_row_spec(x, tm), _const_spec(g)], [(x.shape, BF16, (tm, x.shape[1]), lambda i: (i, 0))],
               (x.shape[0] // tm,), name)[0]


def _rms_bwd(dh, x, g, res, name):
    tm = 256
    d = x.shape[1]

    def fn(dhv, xv, gv, rv):
        r = lax.rsqrt(jnp.mean(xv * xv, axis=-1, keepdims=True) + EPS)
        xh = xv * r
        dyg = dhv * gv
        dx = rv + r * (dyg - xh * jnp.mean(dyg * xh, axis=-1, keepdims=True))
        return dx, dx, dhv * xh

    spec = lambda dt: (x.shape, dt, (tm, d), lambda i: (i, 0))
    return _ew(fn, [_row_spec(dh, tm), _row_spec(x, tm), _const_spec(g), _row_spec(res, tm)],
               [spec(F32), spec(BF16)], (x.shape[0] // tm,), name, colsums=(d,))


def _rope_tables(s):
    half = ROT // 2
    pos = jnp.arange(s, dtype=F32)
    inv = ROPE_THETA ** (-jnp.arange(0, ROT, 2, dtype=F32) / ROT)
    ang = pos[:, None] * inv[None, :]
    cos, sin = jnp.cos(ang), jnp.sin(ang)
    pad = jnp.zeros((s, HEAD - ROT), F32)
    c = jnp.concatenate([cos, cos, pad + 1.0], axis=1)
    a = jnp.concatenate([-sin, jnp.zeros_like(sin), pad], axis=1)
    b = jnp.concatenate([jnp.zeros_like(sin), sin, pad], axis=1)
    return c, a, b


def _heads(x, n_heads):
    return [x[:, h * HEAD:(h + 1) * HEAD] for h in range(n_heads)]


def _headnorm_rope(proj, part, gain, tabs, n_heads, name):
    tm = 512
    s = proj.shape[0]
    w = n_heads * HEAD

    def fn(xs, gv, c, a, b):
        outs = []
        for xv in _heads(xs, n_heads):
            r = lax.rsqrt(jnp.mean(xv * xv, axis=-1, keepdims=True) + EPS)
            y = xv * r * gv
            outs.append(c * y + a * pltpu.roll(y, HEAD - ROT // 2, 1) + b * pltpu.roll(y, ROT // 2, 1))
        return jnp.concatenate(outs, axis=1)

    tab = lambda t: (t, (tm, HEAD), lambda i: (i, 0))
    return _ew(fn, [(proj, (tm, w), lambda i: (i, part)), (gain, (1, HEAD), lambda i: (0, 0))] + [tab(t) for t in tabs],
               [((s, w), BF16, (tm, w), lambda i: (i, 0))], (s // tm,), name)[0]


def _headnorm_rope_bwd(dys, proj, part, gain, tabs, n_heads, name):
    tm = 256
    s = proj.shape[0]
    w = n_heads * HEAD
    n_dy = len(dys)

    def fn(*vals):
        dy_all = vals[0]
        for v in vals[1:n_dy]:
            dy_all = dy_all + v
        xs, gv, c, a, b = vals[n_dy:]
        dxs, dgain = [], None
        for dy, xv in zip(_heads(dy_all, n_heads), _heads(xs, n_heads)):
            dn = c * dy + pltpu.roll(a * dy, ROT // 2, 1) + pltpu.roll(b * dy, HEAD - ROT // 2, 1)
            r = lax.rsqrt(jnp.mean(xv * xv, axis=-1, keepdims=True) + EPS)
            xh = xv * r
            dyg = dn * gv
            dxs.append(r * (dyg - xh * jnp.mean(dyg * xh, axis=-1, keepdims=True)))
            dgain = dn * xh if dgain is None else dgain + dn * xh
        return jnp.concatenate(dxs, axis=1), dgain

    tab = lambda t: (t, (tm, HEAD), lambda i: (i, 0))
    return _ew(fn, [(d, (tm, w), lambda i: (i, 0)) for d in dys]
               + [(proj, (tm, w), lambda i: (i, part)), (gain, (1, HEAD), lambda i: (0, 0))] + [tab(t) for t in tabs],
               [((s, w), BF16, (tm, w), lambda i: (i, 0))], (s // tm,), name, colsums=(HEAD,))


def _phase_major(a, d):
    s, w = a.shape
    if d == 1:
        return a.reshape(1, s, w)
    return a.reshape(s // d, d, w).transpose(1, 0, 2)


def _token_major(a):
    d, m, w = a.shape
    if d == 1:
        return a.reshape(m, w)
    return a.transpose(1, 0, 2).reshape(m * d, w)


def _dil_tq(m):
    return min(1024, m)


def _dilated_fwd(q, k, v, voff, n_heads, name):
    d, m, _ = q.shape
    tq = _dil_tq(m)
    nq = m // tq
    per = tq // WINDOW
    scale = HEAD ** -0.5

    def body(q_ref, kc_ref, kp_ref, vc_ref, vp_ref, o_ref, l_ref):
        n = pl.program_id(2)
        kk = jnp.concatenate([kp_ref[...], kc_ref[...]], axis=0)
        vv = jnp.concatenate([vp_ref[...], vc_ref[...]], axis=0)
        row = lax.broadcasted_iota(jnp.int32, (WINDOW, 2 * WINDOW), 0)
        col = lax.broadcasted_iota(jnp.int32, (WINDOW, 2 * WINDOW), 1)
        band = (col >= row) & (col <= row + WINDOW)
        q = q_ref[...]
        rows = [slice(b * WINDOW, (b + 1) * WINDOW) for b in range(per)]
        keys = [slice(b * WINDOW, (b + 2) * WINDOW) for b in range(per)]
        s = [lax.dot_general(q[rows[b]], kk[keys[b]], NT_DIMS, preferred_element_type=F32) * scale
             for b in range(per)]
        es, outs, lses = [], [], []
        for b in range(per):
            valid = band if b else band & ((n > 0) | (col >= WINDOW))
            sb = jnp.where(valid, s[b], NEG)
            mx = jnp.max(sb, axis=-1, keepdims=True)
            e = jnp.exp(sb - mx)
            den = jnp.sum(e, axis=-1, keepdims=True)
            es.append((e.astype(BF16), den))
            lses.append(jnp.broadcast_to(mx + jnp.log(den), (WINDOW, HEAD)))
        for b in range(per):
            outs.append(jnp.dot(es[b][0], vv[keys[b]], preferred_element_type=F32) / es[b][1])
        o_ref[...] = jnp.concatenate(outs, axis=0)
        l_ref[...] = jnp.concatenate(lses, axis=0)

    cur = lambda off: pl.BlockSpec((None, tq, HEAD), lambda r, h, n: (r, n, off + h))
    prev = lambda off: pl.BlockSpec((None, WINDOW, HEAD), lambda r, h, n: (r, jnp.maximum(n * per - 1, 0), off + h))
    out = jax.ShapeDtypeStruct((d, m, n_heads * HEAD), F32)
    return pl.pallas_call(
        body, name=name, grid=(d, n_heads, nq),
        in_specs=[cur(0), cur(0), prev(0), cur(voff), prev(voff)],
        out_specs=[cur(0), cur(0)], out_shape=[out, out],
        compiler_params=_params(("parallel",) * 3, 32),
    )(q, k, k, v, v)


def _dilated_bwd(q, k, v, voff, dy, stats, n_heads, name):
    d, m, _ = q.shape
    tq = _dil_tq(m)
    nq = m // tq
    per = tq // WINDOW
    last_blk = m // WINDOW - 1
    scale = HEAD ** -0.5

    def body(qc_ref, qn_ref, kc_ref, kp_ref, vc_ref, vp_ref, dyc_ref, dyn_ref, sc_ref, sn_ref,
             dq_ref, dk_ref, dv_ref):
        n = pl.program_id(2)
        kk = jnp.concatenate([kp_ref[...], kc_ref[...]], axis=0)
        vv = jnp.concatenate([vp_ref[...], vc_ref[...]], axis=0)
        qq = jnp.concatenate([qc_ref[...], qn_ref[...]], axis=0)
        dyy = jnp.concatenate([dyc_ref[...], dyn_ref[...]], axis=0)
        st = jnp.concatenate([sc_ref[...], sn_ref[...]], axis=0)
        half = HEAD // 2
        row = lax.broadcasted_iota(jnp.int32, (WINDOW, 2 * WINDOW), 0)
        col = lax.broadcasted_iota(jnp.int32, (WINDOW, 2 * WINDOW), 1)
        band = (col >= row) & (col <= row + WINDOW)
        rows = [slice(b * WINDOW, (b + 1) * WINDOW) for b in range(per)]
        wide = [slice(b * WINDOW, (b + 2) * WINDOW) for b in range(per)]
        nt = lambda a, b: lax.dot_general(a, b, NT_DIMS, preferred_element_type=F32)
        s = [nt(qq[rows[b]], kk[wide[b]]) * scale for b in range(per)]
        dp = [nt(dyy[rows[b]], vv[wide[b]]) for b in range(per)]
        ds = []
        for b in range(per):
            valid = band if b else band & ((n > 0) | (col >= WINDOW))
            p = jnp.where(valid, jnp.exp(jnp.minimum(s[b] - st[rows[b]][:, :1], 0.0)), 0.0)
            ds.append((p * (dp[b] - st[rows[b]][:, half:half + 1]) * scale).astype(BF16))
        dq = [jnp.dot(ds[b], kk[wide[b]], preferred_element_type=F32) for b in range(per)]
        kc, vc = kc_ref[...], vc_ref[...]
        s2 = [nt(kc[rows[b]], qq[wide[b]]) * scale for b in range(per)]
        dp2 = [nt(vc[rows[b]], dyy[wide[b]]) for b in range(per)]
        ds2, p2 = [], []
        for b in range(per):
            valid = band if b < per - 1 else band & ((n < nq - 1) | (col < WINDOW))
            st_t = st[wide[b]].T
            pb = jnp.where(valid, jnp.exp(jnp.minimum(s2[b] - st_t[:1], 0.0)), 0.0)
            ds2.append((pb * (dp2[b] - st_t[half:half + 1]) * scale).astype(BF16))
            p2.append(pb.astype(BF16))
        dk = [jnp.dot(ds2[b], qq[wide[b]], preferred_element_type=F32) for b in range(per)]
        dv = [jnp.dot(p2[b], dyy[wide[b]], preferred_element_type=F32) for b in range(per)]
        dq_ref[...] = jnp.concatenate(dq, axis=0)
        dk_ref[...] = jnp.concatenate(dk, axis=0)
        dv_ref[...] = jnp.concatenate(dv, axis=0)

    cur = lambda off: pl.BlockSpec((None, tq, HEAD), lambda r, h, n: (r, n, off + h))
    prev = lambda off: pl.BlockSpec((None, WINDOW, HEAD), lambda r, h, n: (r, jnp.maximum(n * per - 1, 0), off + h))
    nxt = lambda off: pl.BlockSpec((None, WINDOW, HEAD),
                                   lambda r, h, n: (r, jnp.minimum((n + 1) * per, last_blk), off + h))
    out = jax.ShapeDtypeStruct((d, m, n_heads * HEAD), F32)
    return pl.pallas_call(
        body, name=name, grid=(d, n_heads, nq),
        in_specs=[cur(0), nxt(0), cur(0), prev(0), cur(voff), prev(voff), cur(0), nxt(0), cur(0), nxt(0)],
        out_specs=[cur(0)] * 3, out_shape=[out] * 3,
        compiler_params=_params(("parallel",) * 3, 40),
    )(q, q, k, k, v, v, dy, dy, stats, stats)


def _mix_fwd(os_, ls_, name):
    tm = 256
    s, w = os_[0].shape
    n = len(os_)

    def fn(*vals):
        o, l = vals[:n], vals[n:]
        mx = functools.reduce(jnp.maximum, l)
        e = [jnp.exp(x - mx) for x in l]
        den = functools.reduce(jnp.add, e)
        y = functools.reduce(jnp.add, [ei * oi for ei, oi in zip(e, o)]) / den
        return y, mx + jnp.log(den)

    return _ew(fn, [_row_spec(a, tm) for a in list(os_) + list(ls_)],
               [((s, w), BF16, (tm, w), lambda i: (i, 0)), ((s, w), F32, (tm, w), lambda i: (i, 0))],
               (s // tm,), name)


def _mix_stats(dy, y, lse, n_heads, name):
    tm = 256
    s = dy.shape[0]
    w = n_heads * HEAD

    def fn(dys, ys, ls):
        lane = lax.broadcasted_iota(jnp.int32, (tm, HEAD), 1)
        packed = []
        for a, b, l in zip(_heads(dys, n_heads), _heads(ys, n_heads), _heads(ls, n_heads)):
            delta = jnp.sum(a * b.astype(F32), axis=-1, keepdims=True)
            packed.append(jnp.where(lane < HEAD // 2, l, delta))
        return dys, jnp.concatenate(packed, axis=1)

    blk = lambda a: (a, (tm, w), lambda i: (i, 0))
    out = lambda dt: ((s, w), dt, (tm, w), lambda i: (i, 0))
    return _ew(fn, [blk(dy), blk(y), blk(lse)], [out(BF16), out(F32)], (s // tm,), name)


SB_TQ = 1024
SB_TB = 512
SB_TK = 256


def _softplus(z):
    return jnp.where(z > 20.0, z, jnp.log(1.0 + jnp.exp(z)))


def _tri(t, cmp):
    rows = lax.broadcasted_iota(jnp.int32, (2 * t, t), 0)
    cols = lax.broadcasted_iota(jnp.int32, (2 * t, t), 1)
    return jnp.where(cmp(jnp.where(rows >= t, rows - t, rows), cols), 1.0, 0.0).astype(BF16)


def _tri_sum(x, tri):
    return jnp.dot(jnp.concatenate(_split_bf16(x), axis=1), tri, preferred_element_type=F32)


def _causal(rows, cols):
    return lax.broadcasted_iota(jnp.int32, (rows, cols), 1) < lax.broadcasted_iota(jnp.int32, (rows, cols), 0)


def _rowsum(x):
    return jnp.broadcast_to(jnp.sum(x, axis=-1, keepdims=True), (x.shape[0], HEAD))


def _over_keys(c, width):
    return jnp.concatenate([c] * (width // HEAD), axis=1)


def _from(x, r0):
    return x if r0 == 0 else x[r0:]


def _add_from(x, r0, upd):
    return x + upd if r0 == 0 else jnp.concatenate([x[:r0], x[r0:] + upd], axis=0)


def _sb_fwd(qkv, qoff, koff, voff, n_heads, name, comm=None):
    s = qkv.shape[0]
    tq, tb, tk = min(SB_TQ, s), SB_TB, SB_TK
    scale = HEAD ** -0.5
    host = _Host(comm, (n_heads, s // tq))

    def body(*refs):
        q_ref, k_ref, v_ref = refs[:3]
        comm_refs, ((o_ref, tot_ref), _) = host.split(refs, 3, 2)
        host.stage("first", comm_refs)
        i = pl.program_id(1)
        q = q_ref[...]
        after = _tri(tk, lambda a, b: a > b)

        def block(base, carry, o, diag_off):
            halves = list(reversed(range(tb // tk)))
            starts = [pl.multiple_of(base + h * tk, tk) for h in halves]
            r0s = [0 if diag_off is None else diag_off + h * tk for h in halves]
            masks = [None if diag_off is None else _causal(tq - r0, tk) for r0 in r0s]
            z = [lax.dot_general(_from(q, r0),k_ref[pl.ds(st, tk), :], NT_DIMS, preferred_element_type=F32) * scale
                 for st, r0 in zip(starts, r0s)]
            sp = [_softplus(zz) for zz in z]
            logsig = [zz - ss for zz, ss in zip(z, sp)]
            sp = [ss if m is None else jnp.where(m, ss, 0.0) for ss, m in zip(sp, masks)]
            sfx = [_tri_sum(ss, after) for ss in sp]
            probs = []
            for ls, sx, ss, m, r0 in zip(logsig, sfx, sp, masks, r0s):
                a = jnp.exp(ls - _over_keys(_from(carry, r0), tk) - sx)
                probs.append((a if m is None else jnp.where(m, a, 0.0)).astype(BF16))
                carry = _add_from(carry, r0, _rowsum(ss))
            for a, st, r0 in zip(probs, starts, r0s):
                o = _add_from(o, r0, jnp.dot(a, v_ref[pl.ds(st, tk), :], preferred_element_type=F32))
            return carry, o

        carry, o = jnp.zeros((tq, HEAD), F32), jnp.zeros((tq, HEAD), F32)
        for b in reversed(range(tq // tb)):
            carry, o = block(i * tq + b * tb, carry, o, b * tb)
        below = i * (tq // tb)
        carry, o = lax.fori_loop(0, below, lambda jj, co: block((below - 1 - jj) * tb, co[0], co[1], None),
                                 (carry, o))
        o_ref[...] = o.astype(o_ref.dtype)
        tot_ref[...] = carry
        host.stage("mid", comm_refs)
        host.stage("last", comm_refs)

    t = tq
    full = lambda off: pl.BlockSpec((s, HEAD), lambda h, i: (0, off + h))
    tile_spec = lambda off: pl.BlockSpec((t, HEAD), lambda h, i: (i, off + h))
    out = lambda dt: jax.ShapeDtypeStruct((s, n_heads * HEAD), dt)
    return pl.pallas_call(
        body, name=name, grid=(n_heads, s // t),
        in_specs=[tile_spec(qoff), full(koff), full(voff)] + host.in_specs,
        out_specs=[tile_spec(0), tile_spec(0)] + host.out_specs,
        out_shape=[out(BF16), out(F32)] + host.out_shape,
        scratch_shapes=host.scratch,
        compiler_params=_params(("parallel" if comm is None else "arbitrary", "arbitrary"), 40),
    )(qkv, qkv, qkv, *host.ins)


def _sb_bwd(qkv, qoff, koff, voff, do, tot, n_heads, name, comm=None):
    s = qkv.shape[0]
    tq, tb, tk = min(SB_TQ, s), SB_TB, SB_TK
    scale = HEAD ** -0.5
    host = _Host(comm, (n_heads, s // tq))

    def body(*refs):
        q_ref, k_ref, v_ref, do_ref, tot_ref = refs[:5]
        comm_refs, ((dq_ref, dk_ref, dv_ref), _) = host.split(refs, 5, 3)
        host.stage("first", comm_refs)
        i = pl.program_id(1)

        @pl.when(i == 0)
        def _():
            dk_ref[...] = jnp.zeros_like(dk_ref)
            dv_ref[...] = jnp.zeros_like(dv_ref)

        q = q_ref[...]
        do_b = do_ref[...].astype(BF16)
        total = tot_ref[...]
        upto = _tri(tk, lambda a, b: a <= b)
        before = _tri(tk, lambda a, b: a < b)[:tk]
        def block(base, lsum, psum, dq, diag_off):
            halves = list(range(tb // tk))
            starts = [pl.multiple_of(base + h * tk, tk) for h in halves]
            r0s = [0 if diag_off is None else diag_off + h * tk for h in halves]
            masks = [None if diag_off is None else _causal(tq - r0, tk) for r0 in r0s]
            keep = lambda x, m: x if m is None else jnp.where(m, x, 0.0)
            ks = [k_ref[pl.ds(st, tk), :] for st in starts]
            z = [lax.dot_general(_from(q, r0),kj, NT_DIMS, preferred_element_type=F32) * scale
                 for kj, r0 in zip(ks, r0s)]
            da = [lax.dot_general(_from(do_b, r0), v_ref[pl.ds(st, tk), :], NT_DIMS, preferred_element_type=F32)
                  for st, r0 in zip(starts, r0s)]
            sp = [_softplus(zz) for zz in z]
            logsig = [zz - ss for zz, ss in zip(z, sp)]
            sp = [keep(ss, m) for ss, m in zip(sp, masks)]
            pre = [_tri_sum(ss, upto) for ss in sp]
            probs, p = [], []
            for ls, px, ss, m, dd, r0 in zip(logsig, pre, sp, masks, da, r0s):
                a = keep(jnp.exp(ls - (_over_keys(_from(total, r0) - _from(lsum, r0), tk) - px)), m)
                probs.append(a.astype(BF16))
                p.append(a * dd)
                lsum = _add_from(lsum, r0, _rowsum(ss))
            cs = [jnp.dot(pp.astype(BF16), before, preferred_element_type=F32) for pp in p]
            dzs = []
            for ls, pp, cc, m, r0 in zip(logsig, p, cs, masks, r0s):
                c_all = _over_keys(_from(psum, r0), tk) + cc
                dzs.append(keep((pp - (pp + c_all) * jnp.exp(ls)) * scale, m).astype(BF16))
                psum = _add_from(psum, r0, _rowsum(pp))
            for dz_b, kj, r0 in zip(dzs, ks, r0s):
                dq = _add_from(dq, r0, jnp.dot(dz_b, kj, preferred_element_type=F32))
            for dz_b, a, st, r0 in zip(dzs, probs, starts, r0s):
                dk_ref[pl.ds(st, tk), :] += lax.dot_general(dz_b, _from(q, r0), TN_DIMS, preferred_element_type=F32)
                dv_ref[pl.ds(st, tk), :] += lax.dot_general(a, _from(do_b, r0), TN_DIMS,
                                                            preferred_element_type=F32)
            return lsum, psum, dq

        zero = jnp.zeros((tq, HEAD), F32)
        state = lax.fori_loop(0, i * (tq // tb), lambda j, c: block(j * tb, c[0], c[1], c[2], None),
                              (zero, zero, jnp.zeros((tq, HEAD), F32)))
        for b in range(tq // tb):
            state = block(i * tq + b * tb, *state, b * tb)
        dq_ref[...] = state[2].astype(dq_ref.dtype)
        host.stage("mid", comm_refs)
        host.stage("last", comm_refs)

    t = tq
    full = lambda off: pl.BlockSpec((s, HEAD), lambda h, i: (0, off + h))
    tile_spec = lambda off: pl.BlockSpec((t, HEAD), lambda h, i: (i, off + h))
    w = n_heads * HEAD
    return pl.pallas_call(
        body, name=name, grid=(n_heads, s // t),
        in_specs=[tile_spec(qoff), full(koff), full(voff), tile_spec(0), tile_spec(0)] + host.in_specs,
        out_specs=[tile_spec(0), full(0), full(0)] + host.out_specs,
        out_shape=[jax.ShapeDtypeStruct((s, w), BF16), jax.ShapeDtypeStruct((s, w), F32),
                   jax.ShapeDtypeStruct((s, w), F32)] + host.out_shape,
        scratch_shapes=host.scratch,
        compiler_params=_params(("parallel" if comm is None else "arbitrary", "arbitrary"), 48),
    )(qkv, qkv, qkv, do, tot, *host.ins)


def _coords():
    return lax.axis_index("x"), lax.axis_index("y"), lax.axis_index("c")


def _gather_plan(shards):
    n = len(shards)

    def run(stage, ins, outs, sems):
        send_sems, recv_sems, local_sems = sems
        x, y, c = _coords()
        me, sibling = (x, y, c), (x, y, 1 - c)
        chips = [(1 - x, y), (x, 1 - y), (1 - x, 1 - y)]

        def copy(w, k, block, to, src=None):
            dst = outs[w].at[4 * block[0] + 2 * block[1] + block[2]]
            return pltpu.make_async_remote_copy(
                src_ref=dst if src is None else src, dst_ref=dst,
                send_sem=send_sems.at[7 * w + k], recv_sem=recv_sems.at[7 * w + k],
                device_id=to, device_id_type=MESH)

        def mine():
            return [pltpu.make_async_copy(ins[w], outs[w].at[4 * x + 2 * y + c], local_sems.at[w]) for w in range(n)]

        def first():
            cps = []
            for w in range(n):
                cps.append(copy(w, 0, me, sibling, src=ins[w]))
                cps += [copy(w, 1 + j, me, (*chip, c), src=ins[w]) for j, chip in enumerate(chips)]
            return cps

        def passed():
            return [copy(w, 4 + j, (*chip, c), sibling) for w in range(n) for j, chip in enumerate(chips)]

        if stage == "first":
            for cp in mine() + first():
                cp.start()
        elif stage == "mid":
            onward = passed()
            for w in range(n):
                for j, chip in enumerate(chips):
                    copy(w, 1 + j, (*chip, c), me).wait_recv()
                    onward[3 * w + j].start()
        else:
            for w in range(n):
                copy(w, 0, sibling, me).wait_recv()
                for j, chip in enumerate(chips):
                    copy(w, 4 + j, (*chip, 1 - c), me).wait_recv()
            for cp in first() + passed():
                cp.wait_send()
            for cp in mine():
                cp.wait()

    return dict(
        ins=list(shards), run=run, stages=("first", "mid", "last"),
        out_shape=[jax.ShapeDtypeStruct((N_DEV,) + a.shape, a.dtype) for a in shards],
        scratch=[pltpu.SemaphoreType.DMA((7 * n,)), pltpu.SemaphoreType.DMA((7 * n,)), pltpu.SemaphoreType.DMA((n,))])


def _scatter_plan(grads):
    n = len(grads)

    def run(stage, ins, outs, sems):
        send_sems, recv_sems = sems
        x, y, c = _coords()
        cps = []
        for w in range(n):
            for mask in range(1, N_DEV):
                px, py, pc = x ^ (mask >> 2), y ^ ((mask >> 1) & 1), c ^ (mask & 1)
                cps.append(pltpu.make_async_remote_copy(
                    src_ref=ins[w].at[4 * px + 2 * py + pc], dst_ref=outs[w].at[mask - 1],
                    send_sem=send_sems.at[7 * w + mask - 1], recv_sem=recv_sems.at[7 * w + mask - 1],
                    device_id=(px, py, pc), device_id_type=MESH))
        for cp in cps:
            if stage == "first":
                cp.start()
            else:
                cp.wait()

    return dict(
        ins=list(grads), run=run, stages=("first", "last"),
        out_shape=[jax.ShapeDtypeStruct((N_DEV - 1,) + a.shape[1:], a.dtype) for a in grads],
        scratch=[pltpu.SemaphoreType.DMA((7 * n,)), pltpu.SemaphoreType.DMA((7 * n,))])


def _run_plan(plan, name):
    n_in, n_out = len(plan["ins"]), len(plan["out_shape"])

    def body(*refs):
        for stage in plan["stages"]:
            plan["run"](stage, refs[:n_in], refs[n_in:n_in + n_out], refs[n_in + n_out:])

    any_spec = pl.BlockSpec(memory_space=pl.ANY)
    return pl.pallas_call(
        body, name=name, in_specs=[any_spec] * n_in, out_specs=[any_spec] * n_out,
        out_shape=plan["out_shape"], scratch_shapes=plan["scratch"],
    )(*plan["ins"])


def _gather_rows(v):
    rows, width = v.shape

    def body(v_ref, out_ref, send_sems, recv_sems):
        x, y, c = _coords()
        out_ref[pl.ds(pl.multiple_of((4 * x + 2 * y + c) * rows, rows), rows), :] = v_ref[...]
        cps = []
        for mask in range(1, N_DEV):
            peer = (x ^ (mask >> 2), y ^ ((mask >> 1) & 1), c ^ (mask & 1))
            dst = out_ref.at[pl.ds(pl.multiple_of((4 * x + 2 * y + c) * rows, rows), rows), :]
            cps.append(pltpu.make_async_remote_copy(
                src_ref=v_ref, dst_ref=dst, send_sem=send_sems.at[mask - 1], recv_sem=recv_sems.at[mask - 1],
                device_id=peer, device_id_type=MESH))
        for cp in cps:
            cp.start()
        for cp in cps:
            cp.wait()

    vmem = pl.BlockSpec(memory_space=pltpu.VMEM)
    return pl.pallas_call(
        body, name="gather_small",
        in_specs=[vmem], out_specs=vmem,
        out_shape=jax.ShapeDtypeStruct((N_DEV * rows, width), F32),
        scratch_shapes=[pltpu.SemaphoreType.DMA((N_DEV - 1,)), pltpu.SemaphoreType.DMA((N_DEV - 1,))],
    )(v)


def _adamw(w, g, m, v):
    m = ADAM_B1 * m + (1.0 - ADAM_B1) * g
    v = ADAM_B2 * v + (1.0 - ADAM_B2) * jnp.square(g)
    m_hat = m / (1.0 - ADAM_B1 ** ADAM_STEP)
    v_hat = v / (1.0 - ADAM_B2 ** ADAM_STEP)
    delta = -ADAM_LR * (m_hat / (jnp.sqrt(v_hat) + ADAM_EPS) + ADAM_WD * w)
    return delta, m, v


def _tile_rows(rows, cols):
    tm = 1 << int(math.log2(max(2 * SUBLANES, (1 << 18) // cols)))
    while rows % tm:
        tm //= 2
    assert tm >= 2 * SUBLANES, (rows, cols)
    return tm


def _reduce_adam(w, m, v, own, got, name):
    r, c = w.shape
    tm = max(2 * SUBLANES, _tile_rows(r, c) // 2)

    def fn(wv, mv, vv, a, *peers):
        g = a
        for pv in peers:
            g = g + pv.astype(F32)
        return (g,) + _adamw(wv, g, mv, vv)

    blk = lambda a: (a, (tm, c), lambda i: (i, 0))
    got_blk = lambda j: (got, (None, tm, c), lambda i, j=j: (j, i, 0))
    return _ew(fn, [blk(w), blk(m), blk(v), blk(own)] + [got_blk(j) for j in range(N_DEV - 1)],
               [((r, c), F32, (tm, c), lambda i: (i, 0))] * 4, (r // tm,), name)


def _small_adam(gathered, params, moms, vels, widths):
    n = len(params)
    total = gathered.shape[1]

    def body(*refs):
        g_ref = refs[0]
        p_refs, m_refs, v_refs = refs[1:1 + n], refs[1 + n:1 + 2 * n], refs[1 + 2 * n:1 + 3 * n]
        sum_ref = refs[1 + 3 * n]
        outs = refs[2 + 3 * n:]
        g = g_ref[0:1, :]
        for p in range(1, N_DEV):
            g = g + g_ref[p * SUBLANES:p * SUBLANES + 1, :]
        sum_ref[...] = g
        off = 0
        for i, wd in enumerate(widths):
            d, m2, v2 = _adamw(p_refs[i][...], g[:, off:off + wd], m_refs[i][...], v_refs[i][...])
            outs[3 * i][...] = d
            outs[3 * i + 1][...] = m2
            outs[3 * i + 2][...] = v2
            off += wd

    vmem = pl.BlockSpec(memory_space=pltpu.VMEM)
    out_shape = [jax.ShapeDtypeStruct((1, total), F32)]
    for wd in widths:
        out_shape += [jax.ShapeDtypeStruct((1, wd), F32)] * 3
    return pl.pallas_call(
        body, name="small_adam",
        in_specs=[vmem] * (1 + 3 * n), out_specs=[vmem] * len(out_shape), out_shape=out_shape,
    )(gathered, *params, *moms, *vels)


def _cast_bf16(a, name):
    r, c = a.shape
    tm = _tile_rows(r, c)
    return _ew(lambda v: v, [(a, (tm, c), lambda i: (i, 0))], [((r, c), BF16, (tm, c), lambda i: (i, 0))],
               (r // tm,), name)[0]


def _fold_loss(parts, name):
    r, c = parts.shape

    def fn(v):
        return jnp.broadcast_to(jnp.sum(jnp.sum(v, axis=0, keepdims=True), axis=1, keepdims=True), (SUBLANES, HEAD))

    return _ew(fn, [_const_spec(parts)], [((SUBLANES, HEAD), F32, (SUBLANES, HEAD), lambda i: (0, 0))], (1,),
               name)[0][0:1]


def kernel(x, p, g_mix, w_in, qn_gain, kn_gain, w_branch_a, w_branch_b, w_out, g_mlp, w_up, w_down, g_ple, w_ple_gate, w_ple_proj, loss_target, m_g_mix, m_w_in, m_qn_gain, m_kn_gain, m_w_branch_a, m_w_branch_b, m_w_out, m_g_mlp, m_w_up, m_w_down, m_g_ple, m_w_ple_gate, m_w_ple_proj, v_g_mix, v_w_in, v_qn_gain, v_kn_gain, v_w_branch_a, v_w_branch_b, v_w_out, v_g_mlp, v_w_up, v_w_down, v_g_ple, v_w_ple_gate, v_w_ple_proj):
    x2 = x[0]
    tgt = loss_target[0]
    s, d = x2.shape
    wd_ = w_branch_a.shape[1]
    nh = wd_ // HEAD
    dff = w_up.shape[1]
    qkv_w = 6 * wd_
    tiles = lambda cols: cols // HEAD

    big = [w_in[0], w_branch_a[0], w_branch_b[0], w_out[0], w_up[0], w_down[0], w_ple_gate[0], w_ple_proj[0]]
    names = ["w_in", "w_branch_a", "w_branch_b", "w_out", "w_up", "w_down", "w_ple_gate", "w_ple_proj"]
    row_sharded = [False, False, False, True, False, True, True, False]
    shards = [_cast_bf16(a, "cast_" + nm) for a, nm in zip(big, names)]
    as_weight = lambda g, rs: g.reshape((1, N_DEV * g.shape[1], g.shape[2])) if rs else g
    as_blocks = lambda g, rs: g.reshape((N_DEV, g.shape[1] // N_DEV, g.shape[2])) if rs else g
    win = _run_plan(_gather_plan(shards[:1]), "all_gather_w_in")[0]

    tm = 1024 if s % 1024 == 0 else s
    tm_in = tk_s = 2048 if s % 2048 == 0 else tm
    tn_of = lambda n: 512 if n % 512 == 0 else (256 if n % 256 == 0 else n)
    tn_in = 256 if win.shape[2] % 256 == 0 else HEAD

    h = _rms_fwd(x2, g_mix, "norm_mix")
    qk_raw = _mm("nn", h, win, tm=tm_in, tn=tn_in, tk=d, out_dtypes=[F32], name="proj_qk",
                 n_off=0, n_cnt=2 * wd_ // tn_in)[0]
    rest = _mm("nn", h, win, tm=4096 if s % 4096 == 0 else tm_in, tn=tn_in, tk=d, out_dtypes=[BF16], name="proj_rest",
               n_off=2 * wd_ // tn_in, n_cnt=(win.shape[0] * win.shape[2] - 2 * wd_) // tn_in)[0]
    o_va, o_qb, o_kb, o_vb, o_ga, o_gb = 0, tiles(wd_), tiles(2 * wd_), tiles(3 * wd_), tiles(4 * wd_), tiles(4 * wd_ + d)
    tabs = _rope_tables(s)
    qa = _headnorm_rope(qk_raw, 0, qn_gain, tabs, nh, "rope_q")
    ka = _headnorm_rope(qk_raw, 1, kn_gain, tabs, nh, "rope_k")
    va = rest[:, :wd_]

    outs, lses = [], []
    for dil in DILATIONS:
        o_g, l_g = _dilated_fwd(_phase_major(qa, dil), _phase_major(ka, dil), _phase_major(va, dil), 0, nh,
                                f"dilated_fwd_{dil}")
        outs.append(_token_major(o_g))
        lses.append(_token_major(l_g))
    ya, lse_all = _mix_fwd(outs, lses, "mix_fwd")
    yb, sb_tot, *others = _sb_fwd(rest, o_qb, o_kb, o_vb, nh, "sb_fwd", comm=_gather_plan(shards[1:]))
    wba, wbb, wout, wup, wdown, wgate, wproj = [as_weight(g, rs) for g, rs in zip(others, row_sharded[1:])]

    tn_d = tn_of(wba.shape[2])
    za = _mm("nn", ya, wba, tm=tm, tn=tn_d, tk=wd_, out_dtypes=[BF16], name="branch_a")[0]

    def merge(acc, zav, gav, gbv):
        return _sigmoid(gav.astype(F32)) * zav.astype(F32) + _sigmoid(gbv.astype(F32)) * acc, acc

    merged, zb = _mm("nn", yb, wbb, tm=tm, tn=tn_d, tk=wd_, out_dtypes=[BF16, BF16], name="branch_b_merge",
                     epilogue=merge, extras=[(za, 0), (rest, o_ga * HEAD // tn_d), (rest, o_gb * HEAD // tn_d)])
    x1 = _mm("nn", merged, wout, tm=tm, tn=512, tk=d, out_dtypes=[F32], name="out_proj",
             epilogue=lambda acc, xv: (acc + xv,), extras=[(x2, 0)])[0]

    hm = _rms_fwd(x1, g_mlp, "norm_mlp")
    tn_u = min(wup.shape[2], 1024)
    u, act = _mm("nn", hm, wup, tm=tm, tn=tn_u, tk=d, out_dtypes=[BF16, BF16], name="mlp_up",
                 epilogue=lambda acc: (acc, jnp.square(jnp.maximum(acc, 0.0))))
    x3 = _mm("nn", act, wdown, tm=tm, tn=1024, tk=min(dff, 2048), out_dtypes=[F32], name="mlp_down",
             epilogue=lambda acc, xv: (acc + xv,), extras=[(x1, 0)])[0]

    hp = _rms_fwd(x3, g_ple, "norm_ple")
    p_b = _cast_bf16(p[0, 0], "cast_p")
    pp = _mm("nn", p_b, wproj, tm=tm, tn=tn_of(wproj.shape[2]), tk=p_b.shape[1], out_dtypes=[BF16],
             name="ple_proj")[0]

    def head(acc, ppv, xv, tv):
        sg = _sigmoid(acc)
        ppf = ppv.astype(F32)
        err = xv + ppf * sg - tv
        dy = err / d
        sq = jnp.square(err)
        return dy, dy * sg, dy * ppf * sg * (1.0 - sg), sq.reshape(-1, SUBLANES, sq.shape[-1]).sum(axis=0)

    n_i = s // tm
    dy, d_pp, d_gt, sq_parts = _mm(
        "nn", hp, wgate, tm=tm, tn=512, tk=d, out_dtypes=[F32, BF16, BF16], name="ple_gate_loss", epilogue=head,
        extras=[(pp, 0), (x3, 0), (tgt, 0)],
        extra_outs=[((n_i * SUBLANES, d), F32, (SUBLANES, 512), lambda i, j: (i, j))])
    loss_vec = _fold_loss(sq_parts, "loss_fold") * 0.5 / d

    both = [F32, BF16]
    g_wproj = _mm("tn", p_b, d_pp, tm=p_b.shape[1], tn=tn_of(wproj.shape[2]), tk=tk_s, out_dtypes=both,
                  name="grad_w_ple_proj", out_nb=N_DEV)
    g_wgate = _mm("tn", hp, d_gt, tm=1024, tn=1024, tk=tk_s, out_dtypes=both, name="grad_w_ple_gate")
    d_hp = _mm("nt", d_gt, wgate, tm=tm, tn=512, tk=d, out_dtypes=[F32], name="d_hp")[0]
    dx3, dx3_b, g_gple = _rms_bwd(d_hp, x3, g_ple, dy, "norm_ple_bwd")

    d_u = _mm("nt", dx3_b, wdown, tm=tm, tn=1024, tk=d, out_dtypes=[BF16], name="d_u",
              epilogue=lambda acc, uv: (acc * (2.0 * jnp.maximum(uv.astype(F32), 0.0)),), extras=[(u, 0)])[0]
    g_wdown = _mm("tn", act, dx3_b, tm=1024, tn=1024, tk=tk_s, out_dtypes=both, name="grad_w_down")
    g_wup = _mm("tn", hm, d_u, tm=1024, tn=wup.shape[2], tk=tk_s, out_dtypes=both, name="grad_w_up", out_nb=N_DEV)
    d_hm = _mm("nt", d_u, wup, tm=tm, tn=min(d, 2048), tk=wup.shape[2], out_dtypes=[F32], name="d_hm")[0]
    dx1, dx1_b, g_gmlp = _rms_bwd(d_hm, x1, g_mlp, dx3, "norm_mlp_bwd")

    def unmerge(acc, gav, gbv, zav, zbv):
        sa, sb = _sigmoid(gav.astype(F32)), _sigmoid(gbv.astype(F32))
        return acc * sa, acc * sb, acc * zav.astype(F32) * sa * (1.0 - sa), acc * zbv.astype(F32) * sb * (1.0 - sb)

    d_za, d_zb, d_ga, d_gb = _mm(
        "nt", dx1_b, wout, tm=tm, tn=512, tk=d, out_dtypes=[BF16] * 4, name="d_merged", epilogue=unmerge,
        extras=[(rest, o_ga * HEAD // 512), (rest, o_gb * HEAD // 512), (za, 0), (zb, 0)])
    g_wout = _mm("tn", merged, dx1_b, tm=1024, tn=1024, tk=tk_s, out_dtypes=both, name="grad_w_out")
    g_wba = _mm("tn", ya, d_za, tm=wd_, tn=wba.shape[2], tk=tk_s, out_dtypes=both, name="grad_w_branch_a",
                out_nb=N_DEV)
    g_wbb = _mm("tn", yb, d_zb, tm=wd_, tn=wbb.shape[2], tk=tk_s, out_dtypes=both, name="grad_w_branch_b",
                out_nb=N_DEV)
    d_ya = _mm("nt", d_za, wba, tm=tm, tn=wd_, tk=wba.shape[2], out_dtypes=[F32], name="d_ya")[0]
    d_yb = _mm("nt", d_zb, wbb, tm=tm, tn=wd_, tk=wbb.shape[2], out_dtypes=[F32], name="d_yb")[0]

    grads = [None, g_wba, g_wbb, g_wout, g_wup, g_wdown, g_wgate, g_wproj]
    early = _scatter_plan([as_blocks(g[1], rs) for g, rs in zip(grads[1:], row_sharded[1:])])
    d_qb, d_kb, d_vb, *got = _sb_bwd(rest, o_qb, o_kb, o_vb, d_yb, sb_tot, nh, "sb_bwd", comm=early)
    d_ya_b, stats = _mix_stats(d_ya, ya, lse_all, nh, "mix_stats")
    dqs, dks, dvs = [], [], []
    for dil in DILATIONS:
        dq_g, dk_g, dv_g = _dilated_bwd(
            _phase_major(qa, dil), _phase_major(ka, dil), _phase_major(va, dil), 0, _phase_major(d_ya_b, dil),
            _phase_major(stats, dil), nh, f"dilated_bwd_{dil}")
        dqs.append(_token_major(dq_g))
        dks.append(_token_major(dk_g))
        dvs.append(_token_major(dv_g))
    d_qa, g_qn = _headnorm_rope_bwd(dqs, qk_raw, 0, qn_gain, tabs, nh, "rope_q_bwd")
    d_ka, g_kn = _headnorm_rope_bwd(dks, qk_raw, 1, kn_gain, tabs, nh, "rope_k_bwd")
    tmr = 256
    d_va = _ew(lambda a, b, c: a + b + c, [_row_spec(a, tmr) for a in dvs],
               [((s, wd_), BF16, (tmr, wd_), lambda i: (i, 0))], (s // tmr,), "sum_dv")[0]
    d_proj = jnp.concatenate([d_qa, d_ka, d_va, d_qb, d_kb.astype(BF16), d_vb.astype(BF16), d_ga, d_gb], axis=1)

    grads[0] = _mm("tn", h, d_proj, tm=1024, tn=win.shape[2], tk=tk_s, out_dtypes=both, name="grad_w_in",
                   out_nb=N_DEV)
    d_h, got_in = _mm("nt", d_proj, win, tm=tm, tn=min(d, 2048), tk=win.shape[2], out_dtypes=[F32], name="d_h",
                      comm=_scatter_plan([grads[0][1]]))
    got = [got_in] + got
    dx, _, g_gmix = _rms_bwd(d_h, x2, g_mix, dx1, "norm_mix_bwd")

    cx, cy, cc = _coords()
    me = 4 * cx + 2 * cy + cc
    moms = [m_w_in, m_w_branch_a, m_w_branch_b, m_w_out, m_w_up, m_w_down, m_w_ple_gate, m_w_ple_proj]
    vels = [v_w_in, v_w_branch_a, v_w_branch_b, v_w_out, v_w_up, v_w_down, v_w_ple_gate, v_w_ple_proj]
    big_out = {}
    for i, nm in enumerate(names):
        own = lax.dynamic_index_in_dim(as_blocks(grads[i][0], row_sharded[i]), me, axis=0, keepdims=False)
        big_out[nm] = [a[None] for a in _reduce_adam(big[i], moms[i][0], vels[i][0], own, got[i], "adam_" + nm)]

    small_names = ["g_mix", "qn_gain", "kn_gain", "g_mlp", "g_ple"]
    small_p = [g_mix, qn_gain, kn_gain, g_mlp, g_ple]
    small_m = [m_g_mix, m_qn_gain, m_kn_gain, m_g_mlp, m_g_ple]
    small_v = [v_g_mix, v_qn_gain, v_kn_gain, v_g_mlp, v_g_ple]
    small_g = [g_gmix, g_qn, g_kn, g_gmlp, g_gple]
    widths = [a.shape[1] for a in small_p]
    vec = jnp.concatenate(small_g + [loss_vec], axis=1)
    vec = jnp.pad(vec, ((0, SUBLANES - 1), (0, 0)))
    res = _small_adam(_gather_rows(vec), small_p, small_m, small_v, widths)
    summed = res[0]
    small_out, off = {}, 0
    for i, nm in enumerate(small_names):
        small_out[nm] = [summed[:, off:off + widths[i]]] + list(res[1 + 3 * i:4 + 3 * i])
        off += widths[i]
    loss = summed[0, off]

    order = ["g_mix", "w_in", "qn_gain", "kn_gain", "w_branch_a", "w_branch_b", "w_out", "g_mlp", "w_up", "w_down",
             "g_ple", "w_ple_gate", "w_ple_proj"]
    table = {**big_out, **small_out}
    result = [loss, dx[None]]
    for kind in range(4):
        result += [table[nm][kind] for nm in order]
    return tuple(result)
```

```python
import functools
import math

import jax
import jax.numpy as jnp
from jax import lax
from jax.experimental import pallas as pl
from jax.experimental.pallas import tpu as pltpu

F32 = jnp.float32
BF16 = jnp.bfloat16
MESH = pl.DeviceIdType.MESH

HEAD = 128
WINDOW = 128
DILATIONS = (1, 4, 16)
ROT = HEAD // 4
ROPE_THETA = 500000.0
EPS = 1e-6
NEG = -1e30
N_DEV = 8

ADAM_LR = 0.001
ADAM_B1 = 0.9
ADAM_B2 = 0.999
ADAM_EPS = 1e-08
ADAM_WD = 0.01
ADAM_STEP = 10

SUBLANES = 8
VMEM_CAP_MB = 56

NT_DIMS = (((1,), (1,)), ((), ()))
TN_DIMS = (((0,), (0,)), ((), ()))


def _params(semantics, vmem_mb):
    return pltpu.CompilerParams(dimension_semantics=semantics, vmem_limit_bytes=min(vmem_mb, VMEM_CAP_MB) << 20)


def _sigmoid(x):
    return 0.5 + 0.5 * jnp.tanh(0.5 * x)


def _split_bf16(x):
    hi = x.astype(BF16)
    lo = (x - hi.astype(F32)).astype(BF16)
    return hi, lo


class _Host:
    def __init__(self, comm, grid):
        self.comm, self.grid = comm, grid
        any_spec = pl.BlockSpec(memory_space=pl.ANY)
        self.ins = list(comm["ins"]) if comm else []
        self.out_shape = list(comm["out_shape"]) if comm else []
        self.scratch = list(comm["scratch"]) if comm else []
        self.in_specs = [any_spec] * len(self.ins)
        self.out_specs = [any_spec] * len(self.out_shape)

    def split(self, refs, n_in, n_out):
        pos = n_in
        c_in = refs[pos:pos + len(self.ins)]
        pos += len(self.ins)
        outs = refs[pos:pos + n_out]
        pos += n_out
        c_out = refs[pos:pos + len(self.out_shape)]
        pos += len(self.out_shape)
        own = len(refs) - pos - len(self.scratch)
        return (c_in, c_out, refs[pos + own:]), (outs, refs[pos:pos + own])

    def stage(self, which, comm_refs):
        if self.comm is None or which not in self.comm["stages"]:
            return
        grid = self.grid
        at = {"first": [0] * len(grid), "mid": [grid[0] // 2] + [0] * (len(grid) - 1),
              "last": [g - 1 for g in grid]}[which]
        cond = functools.reduce(jnp.logical_and, [pl.program_id(ax) == v for ax, v in enumerate(at)])

        @pl.when(cond)
        def _():
            self.comm["run"](which, *comm_refs)


def _ew(fn, ins, outs, grid, name, colsums=(), vmem_mb=40):
    n_in, n_out, n_cs = len(ins), len(outs), len(colsums)
    steps = math.prod(grid)

    def body(*refs):
        in_refs = refs[:n_in]
        out_refs = refs[n_in:n_in + n_out]
        cs_refs = refs[n_in + n_out:n_in + n_out + n_cs]
        acc_refs = refs[n_in + n_out + n_cs:]
        vals = fn(*[r[...] for r in in_refs])
        if not isinstance(vals, tuple):
            vals = (vals,)
        for r, v in zip(out_refs, vals[:n_out]):
            r[...] = v.astype(r.dtype)
        if n_cs:
            step = pl.program_id(0)
            for ax in range(1, len(grid)):
                step = step * grid[ax] + pl.program_id(ax)
            for acc, cs, v in zip(acc_refs, cs_refs, vals[n_out:]):
                part = v.reshape(-1, SUBLANES, v.shape[-1]).sum(axis=0)

                @pl.when(step == 0)
                def _(acc=acc, part=part):
                    acc[...] = part

                @pl.when(step > 0)
                def _(acc=acc, part=part):
                    acc[...] += part

                @pl.when(step == steps - 1)
                def _(acc=acc, cs=cs):
                    cs[...] = acc[...].sum(axis=0, keepdims=True)

    out_shape = [jax.ShapeDtypeStruct(s, d) for s, d, _, _ in outs]
    out_specs = [pl.BlockSpec(b, m) for _, _, b, m in outs]
    for w in colsums:
        out_shape.append(jax.ShapeDtypeStruct((1, w), F32))
        out_specs.append(pl.BlockSpec((1, w), lambda *_: (0, 0)))
    sem = ("arbitrary",) * len(grid) if n_cs else ("parallel",) * len(grid)
    res = pl.pallas_call(
        body, name=name, grid=grid,
        in_specs=[pl.BlockSpec(b, m) for _, b, m in ins],
        out_specs=out_specs, out_shape=out_shape,
        scratch_shapes=[pltpu.VMEM((SUBLANES, w), F32) for w in colsums],
        compiler_params=_params(sem, vmem_mb),
    )(*[a for a, _, _ in ins])
    return res


def _row_spec(a, tm):
    return (a, (tm, a.shape[1]), lambda i: (i, 0))


def _const_spec(a):
    return (a, a.shape, lambda *_: (0,) * a.ndim)


def _mm(mode, a, b, *, tm, tn, tk, out_dtypes, name, epilogue=None, extras=(), n_off=0, n_cnt=None,
        out_nb=1, extra_outs=(), vmem_mb=52, comm=None):
    if mode == "nn":
        m, kdim = a.shape
        nb, _, n = b.shape
        npb = n // tn
        ncols = nb * n
        n_tiles = (ncols // tn) if n_cnt is None else n_cnt
        a_spec = pl.BlockSpec((tm, tk), lambda i, j, k: (i, k))
        b_spec = pl.BlockSpec((None, tk, tn), lambda i, j, k: ((j + n_off) // npb, k, (j + n_off) % npb))
        dims = (((1,), (0,)), ((), ()))
    elif mode == "nt":
        m, kdim = a.shape
        nb, nout, n = b.shape
        kpb = n // tk
        n_tiles = nout // tn
        a_spec = pl.BlockSpec((tm, tk), lambda i, j, k: (i, k))
        b_spec = pl.BlockSpec((None, tn, tk), lambda i, j, k: (k // kpb, j, k % kpb))
        dims = NT_DIMS
    else:
        kdim, m = a.shape
        ncols = b.shape[1]
        n_tiles = ncols // tn
        a_spec = pl.BlockSpec((tk, tm), lambda i, j, k: (k, i))
        b_spec = pl.BlockSpec((tk, tn), lambda i, j, k: (k, j))
        dims = TN_DIMS
    nk = kdim // tk
    assert kdim % tk == 0 and m % tm == 0
    grid = (m // tm, n_tiles, nk)
    n_ex, n_out = len(extras), len(out_dtypes) + len(extra_outs)
    host = _Host(comm, grid)

    def body(*refs):
        a_ref, b_ref = refs[0], refs[1]
        ex_refs = refs[2:2 + n_ex]
        comm_refs, (out_refs, scratch) = host.split(refs, 2 + n_ex, n_out)
        host.stage("first", comm_refs)

        def finish(acc):
            vals = (acc,) * n_out if epilogue is None else epilogue(acc, *[r[...] for r in ex_refs])
            for r, v in zip(out_refs, vals):
                r[...] = v.astype(r.dtype)

        def part():
            return lax.dot_general(a_ref[...], b_ref[...], dims, preferred_element_type=F32)

        if nk == 1:
            finish(part())
        else:
            acc_ref = scratch[0]
            k = pl.program_id(2)

            @pl.when(k == 0)
            def _():
                acc_ref[...] = part()

            @pl.when((k > 0) & (k < nk - 1))
            def _():
                acc_ref[...] += part()

            @pl.when(k == nk - 1)
            def _():
                finish(acc_ref[...] + part())

        host.stage("mid", comm_refs)
        host.stage("last", comm_refs)

    if mode == "tn":
        npo = (ncols // out_nb) // tn
        out_shape = [jax.ShapeDtypeStruct((out_nb, m, ncols // out_nb), d) for d in out_dtypes]
        out_specs = [pl.BlockSpec((None, tm, tn), lambda i, j, k: (j // npo, i, j % npo)) for _ in out_dtypes]
    else:
        out_shape = [jax.ShapeDtypeStruct((m, n_tiles * tn), d) for d in out_dtypes]
        out_specs = [pl.BlockSpec((tm, tn), lambda i, j, k: (i, j)) for _ in out_dtypes]
    for s, d, blk, imap in extra_outs:
        out_shape.append(jax.ShapeDtypeStruct(s, d))
        out_specs.append(pl.BlockSpec(blk, lambda i, j, k, imap=imap: imap(i, j)))
    ex_specs = [pl.BlockSpec((tm, tn), lambda i, j, k, off=off: (i, j + off)) for _, off in extras]
    sem = ("parallel", "parallel", "arbitrary") if comm is None else ("arbitrary",) * 3
    return pl.pallas_call(
        body, name=name, grid=grid,
        in_specs=[a_spec, b_spec] + ex_specs + host.in_specs,
        out_specs=out_specs + host.out_specs, out_shape=out_shape + host.out_shape,
        scratch_shapes=([pltpu.VMEM((tm, tn), F32)] if nk > 1 else []) + host.scratch,
        compiler_params=_params(sem, vmem_mb),
    )(a, b, *[e for e, _ in extras], *host.ins)


def _rms_fwd(x, g, name):
    tm = 256

    def fn(xv, gv):
        r = lax.rsqrt(jnp.mean(xv * xv, axis=-1, keepdims=True) + EPS)
        return xv * r * gv

    return _ew(fn, [_row_spec(x, tm), _const_spec(g)], [(x.shape, BF16, (tm, x.shape[1]), lambda i: (i, 0))],
               (x.shape[0] // tm,), name)[0]


def _rms_bwd(dh, x, g, res, name):
    tm = 256
    d = x.shape[1]

    def fn(dhv, xv, gv, rv):
        r = lax.rsqrt(jnp.mean(xv * xv, axis=-1, keepdims=True) + EPS)
        xh = xv * r
        dyg = dhv * gv
        dx = rv + r * (dyg - xh * jnp.mean(dyg * xh, axis=-1, keepdims=True))
        return dx, dx, dhv * xh

    spec = lambda dt: (x.shape, dt, (tm, d), lambda i: (i, 0))
    return _ew(fn, [_row_spec(dh, tm), _row_spec(x, tm), _const_spec(g), _row_spec(res, tm)],
               [spec(F32), spec(BF16)], (x.shape[0] // tm,), name, colsums=(d,))


def _rope_tables(s):
    half = ROT // 2
    pos = jnp.arange(s, dtype=F32)
    inv = ROPE_THETA ** (-jnp.arange(0, ROT, 2, dtype=F32) / ROT)
    ang = pos[:, None] * inv[None, :]
    cos, sin = jnp.cos(ang), jnp.sin(ang)
    pad = jnp.zeros((s, HEAD - ROT), F32)
    c = jnp.concatenate([cos, cos, pad + 1.0], axis=1)
    a = jnp.concatenate([-sin, jnp.zeros_like(sin), pad], axis=1)
    b = jnp.concatenate([jnp.zeros_like(sin), sin, pad], axis=1)
    return c, a, b


def _heads(x, n_heads):
    return [x[:, h * HEAD:(h + 1) * HEAD] for h in range(n_heads)]


def _headnorm_rope(proj, part, gain, tabs, n_heads, name):
    tm = 512
    s = proj.shape[0]
    w = n_heads * HEAD

    def fn(xs, gv, c, a, b):
        outs = []
        for xv in _heads(xs, n_heads):
            r = lax.rsqrt(jnp.mean(xv * xv, axis=-1, keepdims=True) + EPS)
            y = xv * r * gv
            outs.append(c * y + a * pltpu.roll(y, HEAD - ROT // 2, 1) + b * pltpu.roll(y, ROT // 2, 1))
        return jnp.concatenate(outs, axis=1)

    tab = lambda t: (t, (tm, HEAD), lambda i: (i, 0))
    return _ew(fn, [(proj, (tm, w), lambda i: (i, part)), (gain, (1, HEAD), lambda i: (0, 0))] + [tab(t) for t in tabs],
               [((s, w), BF16, (tm, w), lambda i: (i, 0))], (s // tm,), name)[0]


def _headnorm_rope_bwd(dys, proj, part, gain, tabs, n_heads, name):
    tm = 256
    s = proj.shape[0]
    w = n_heads * HEAD
    n_dy = len(dys)

    def fn(*vals):
        dy_all = vals[0]
        for v in vals[1:n_dy]:
            dy_all = dy_all + v
        xs, gv, c, a, b = vals[n_dy:]
        dxs, dgain = [], None
        for dy, xv in zip(_heads(dy_all, n_heads), _heads(xs, n_heads)):
            dn = c * dy + pltpu.roll(a * dy, ROT // 2, 1) + pltpu.roll(b * dy, HEAD - ROT // 2, 1)
            r = lax.rsqrt(jnp.mean(xv * xv, axis=-1, keepdims=True) + EPS)
            xh = xv * r
            dyg = dn * gv
            dxs.append(r * (dyg - xh * jnp.mean(dyg * xh, axis=-1, keepdims=True)))
            dgain = dn * xh if dgain is None else dgain + dn * xh
        return jnp.concatenate(dxs, axis=1), dgain

    tab = lambda t: (t, (tm, HEAD), lambda i: (i, 0))
    return _ew(fn, [(d, (tm, w), lambda i: (i, 0)) for d in dys]
               + [(proj, (tm, w), lambda i: (i, part)), (gain, (1, HEAD), lambda i: (0, 0))] + [tab(t) for t in tabs],
               [((s, w), BF16, (tm, w), lambda i: (i, 0))], (s // tm,), name, colsums=(HEAD,))


def _phase_major(a, d):
    s, w = a.shape
    if d == 1:
        return a.reshape(1, s, w)
    return a.reshape(s // d, d, w).transpose(1, 0, 2)


def _token_major(a):
    d, m, w = a.shape
    if d == 1:
        return a.reshape(m, w)
    return a.transpose(1, 0, 2).reshape(m * d, w)


def _dil_tq(m):
    return min(1024, m)


def _dil_heads(tq, n_heads):
    return max(1, min(n_heads, 1024 // tq))


def _tile_up(parts, hb, per):
    cols = [jnp.concatenate(parts[j * per:(j + 1) * per], axis=0) for j in range(hb)]
    return cols[0] if hb == 1 else jnp.concatenate(cols, axis=1)


def _dilated_fwd(q, k, v, voff, n_heads, name):
    d, m, _ = q.shape
    tq = _dil_tq(m)
    nq = m // tq
    per = tq // WINDOW
    hb = _dil_heads(tq, n_heads)
    scale = HEAD ** -0.5
    blocks = [(j, b) for j in range(hb) for b in range(per)]
    lanes = [slice(j * HEAD, (j + 1) * HEAD) for j in range(hb)]

    def body(q_ref, kc_ref, kp_ref, vc_ref, vp_ref, o_ref, l_ref):
        n = pl.program_id(2)
        kk = jnp.concatenate([kp_ref[...], kc_ref[...]], axis=0)
        vv = jnp.concatenate([vp_ref[...], vc_ref[...]], axis=0)
        row = lax.broadcasted_iota(jnp.int32, (WINDOW, 2 * WINDOW), 0)
        col = lax.broadcasted_iota(jnp.int32, (WINDOW, 2 * WINDOW), 1)
        band = (col >= row) & (col <= row + WINDOW)
        q = q_ref[...]
        rows = [slice(b * WINDOW, (b + 1) * WINDOW) for b in range(per)]
        keys = [slice(b * WINDOW, (b + 2) * WINDOW) for b in range(per)]
        s = [lax.dot_general(q[rows[b], lanes[j]], kk[keys[b], lanes[j]], NT_DIMS,
                             preferred_element_type=F32) * scale for j, b in blocks]
        es, outs, lses = [], [], []
        for (j, b), sb in zip(blocks, s):
            valid = band if b else band & ((n > 0) | (col >= WINDOW))
            sb = jnp.where(valid, sb, NEG)
            mx = jnp.max(sb, axis=-1, keepdims=True)
            e = jnp.exp(sb - mx)
            den = jnp.sum(e, axis=-1, keepdims=True)
            es.append((e.astype(BF16), den))
            lses.append(jnp.broadcast_to(mx + jnp.log(den), (WINDOW, HEAD)))
        for (j, b), (e, den) in zip(blocks, es):
            outs.append(jnp.dot(e, vv[keys[b], lanes[j]], preferred_element_type=F32) / den)
        o_ref[...] = _tile_up(outs, hb, per)
        l_ref[...] = _tile_up(lses, hb, per)

    assert voff % hb == 0
    cur = lambda off: pl.BlockSpec((None, tq, hb * HEAD), lambda r, h, n: (r, n, off // hb + h))
    prev = lambda off: pl.BlockSpec((None, WINDOW, hb * HEAD),
                                    lambda r, h, n: (r, jnp.maximum(n * per - 1, 0), off // hb + h))
    out = jax.ShapeDtypeStruct((d, m, n_heads * HEAD), F32)
    return pl.pallas_call(
        body, name=name, grid=(d, n_heads // hb, nq),
        in_specs=[cur(0), cur(0), prev(0), cur(voff), prev(voff)],
        out_specs=[cur(0), cur(0)], out_shape=[out, out],
        compiler_params=_params(("parallel",) * 3, 32),
    )(q, k, k, v, v)


def _dilated_bwd(q, k, v, voff, dy, stats, n_heads, name):
    d, m, _ = q.shape
    tq = _dil_tq(m)
    nq = m // tq
    per = tq // WINDOW
    last_blk = m // WINDOW - 1
    hb = _dil_heads(tq, n_heads)
    scale = HEAD ** -0.5
    blocks = [(j, b) for j in range(hb) for b in range(per)]
    lanes = [slice(j * HEAD, (j + 1) * HEAD) for j in range(hb)]

    def body(qc_ref, qn_ref, kc_ref, kp_ref, vc_ref, vp_ref, dyc_ref, dyn_ref, sc_ref, sn_ref,
             dq_ref, dk_ref, dv_ref):
        n = pl.program_id(2)
        kk = jnp.concatenate([kp_ref[...], kc_ref[...]], axis=0)
        vv = jnp.concatenate([vp_ref[...], vc_ref[...]], axis=0)
        qq = jnp.concatenate([qc_ref[...], qn_ref[...]], axis=0)
        dyy = jnp.concatenate([dyc_ref[...], dyn_ref[...]], axis=0)
        st = jnp.concatenate([sc_ref[...], sn_ref[...]], axis=0)
        half = HEAD // 2
        row = lax.broadcasted_iota(jnp.int32, (WINDOW, 2 * WINDOW), 0)
        col = lax.broadcasted_iota(jnp.int32, (WINDOW, 2 * WINDOW), 1)
        band = (col >= row) & (col <= row + WINDOW)
        rows = [slice(b * WINDOW, (b + 1) * WINDOW) for b in range(per)]
        wide = [slice(b * WINDOW, (b + 2) * WINDOW) for b in range(per)]
        nt = lambda a, b: lax.dot_general(a, b, NT_DIMS, preferred_element_type=F32)
        s = [nt(qq[rows[b], lanes[j]], kk[wide[b], lanes[j]]) * scale for j, b in blocks]
        dp = [nt(dyy[rows[b], lanes[j]], vv[wide[b], lanes[j]]) for j, b in blocks]
        ds = []
        for (j, b), sb, dpb in zip(blocks, s, dp):
            valid = band if b else band & ((n > 0) | (col >= WINDOW))
            stb = st[rows[b], lanes[j]]
            p = jnp.where(valid, jnp.exp(jnp.minimum(sb - stb[:, :1], 0.0)), 0.0)
            ds.append((p * (dpb - stb[:, half:half + 1]) * scale).astype(BF16))
        dq = [jnp.dot(dsb, kk[wide[b], lanes[j]], preferred_element_type=F32) for (j, b), dsb in zip(blocks, ds)]
        kc, vc = kc_ref[...], vc_ref[...]
        s2 = [nt(kc[rows[b], lanes[j]], qq[wide[b], lanes[j]]) * scale for j, b in blocks]
        dp2 = [nt(vc[rows[b], lanes[j]], dyy[wide[b], lanes[j]]) for j, b in blocks]
        ds2, p2 = [], []
        for (j, b), sb, dpb in zip(blocks, s2, dp2):
            valid = band if b < per - 1 else band & ((n < nq - 1) | (col < WINDOW))
            st_t = st[wide[b], lanes[j]].T
            pb = jnp.where(valid, jnp.exp(jnp.minimum(sb - st_t[:1], 0.0)), 0.0)
            ds2.append((pb * (dpb - st_t[half:half + 1]) * scale).astype(BF16))
            p2.append(pb.astype(BF16))
        dk = [jnp.dot(x, qq[wide[b], lanes[j]], preferred_element_type=F32) for (j, b), x in zip(blocks, ds2)]
        dv = [jnp.dot(x, dyy[wide[b], lanes[j]], preferred_element_type=F32) for (j, b), x in zip(blocks, p2)]
        dq_ref[...] = _tile_up(dq, hb, per)
        dk_ref[...] = _tile_up(dk, hb, per)
        dv_ref[...] = _tile_up(dv, hb, per)

    assert voff % hb == 0
    cur = lambda off: pl.BlockSpec((None, tq, hb * HEAD), lambda r, h, n: (r, n, off // hb + h))
    prev = lambda off: pl.BlockSpec((None, WINDOW, hb * HEAD),
                                    lambda r, h, n: (r, jnp.maximum(n * per - 1, 0), off // hb + h))
    nxt = lambda off: pl.BlockSpec((None, WINDOW, hb * HEAD),
                                   lambda r, h, n: (r, jnp.minimum((n + 1) * per, last_blk), off // hb + h))
    out = jax.ShapeDtypeStruct((d, m, n_heads * HEAD), F32)
    return pl.pallas_call(
        body, name=name, grid=(d, n_heads // hb, nq),
        in_specs=[cur(0), nxt(0), cur(0), prev(0), cur(voff), prev(voff), cur(0), nxt(0), cur(0), nxt(0)],
        out_specs=[cur(0)] * 3, out_shape=[out] * 3,
        compiler_params=_params(("parallel",) * 3, 40),
    )(q, q, k, k, v, v, dy, dy, stats, stats)


def _mix_fwd(os_, ls_, name):
    tm = 256
    s, w = os_[0].shape
    n = len(os_)

    def fn(*vals):
        o, l = vals[:n], vals[n:]
        mx = functools.reduce(jnp.maximum, l)
        e = [jnp.exp(x - mx) for x in l]
        den = functools.reduce(jnp.add, e)
        y = functools.reduce(jnp.add, [ei * oi for ei, oi in zip(e, o)]) / den
        return y, mx + jnp.log(den)

    return _ew(fn, [_row_spec(a, tm) for a in list(os_) + list(ls_)],
               [((s, w), BF16, (tm, w), lambda i: (i, 0)), ((s, w), F32, (tm, w), lambda i: (i, 0))],
               (s // tm,), name)


def _mix_stats(dy, y, lse, n_heads, name):
    tm = 256
    s = dy.shape[0]
    w = n_heads * HEAD

    def fn(dys, ys, ls):
        lane = lax.broadcasted_iota(jnp.int32, (tm, HEAD), 1)
        packed = []
        for a, b, l in zip(_heads(dys, n_heads), _heads(ys, n_heads), _heads(ls, n_heads)):
            delta = jnp.sum(a * b.astype(F32), axis=-1, keepdims=True)
            packed.append(jnp.where(lane < HEAD // 2, l, delta))
        return dys, jnp.concatenate(packed, axis=1)

    blk = lambda a: (a, (tm, w), lambda i: (i, 0))
    out = lambda dt: ((s, w), dt, (tm, w), lambda i: (i, 0))
    return _ew(fn, [blk(dy), blk(y), blk(lse)], [out(BF16), out(F32)], (s // tm,), name)


SB_TQ = 1024
SB_TB = 512
SB_TK = 256


def _softplus(z):
    return jnp.where(z > 20.0, z, jnp.log(1.0 + jnp.exp(z)))


def _tri(t, cmp):
    rows = lax.broadcasted_iota(jnp.int32, (2 * t, t), 0)
    cols = lax.broadcasted_iota(jnp.int32, (2 * t, t), 1)
    return jnp.where(cmp(jnp.where(rows >= t, rows - t, rows), cols), 1.0, 0.0).astype(BF16)


def _tri_sum(x, tri):
    return jnp.dot(jnp.concatenate(_split_bf16(x), axis=1), tri, preferred_element_type=F32)


def _causal(rows, cols):
    return lax.broadcasted_iota(jnp.int32, (rows, cols), 1) < lax.broadcasted_iota(jnp.int32, (rows, cols), 0)


def _rowsum(x):
    return jnp.broadcast_to(jnp.sum(x, axis=-1, keepdims=True), (x.shape[0], HEAD))


def _over_keys(c, width):
    return jnp.concatenate([c] * (width // HEAD), axis=1)


def _from(x, r0):
    return x if r0 == 0 else x[r0:]


def _add_from(x, r0, upd):
    return x + upd if r0 == 0 else jnp.concatenate([x[:r0], x[r0:] + upd], axis=0)


def _sb_fwd(qkv, qoff, koff, voff, n_heads, name, comm=None):
    s = qkv.shape[0]
    tq, tb, tk = min(SB_TQ, s), SB_TB, SB_TK
    scale = HEAD ** -0.5
    host = _Host(comm, (n_heads, s // tq))

    def body(*refs):
        q_ref, k_ref, v_ref = refs[:3]
        comm_refs, ((o_ref, tot_ref), _) = host.split(refs, 3, 2)
        host.stage("first", comm_refs)
        i = pl.program_id(1)
        q = q_ref[...]
        after = _tri(tk, lambda a, b: a > b)

        def block(base, carry, o, diag_off):
            halves = list(reversed(range(tb // tk)))
            starts = [pl.multiple_of(base + h * tk, tk) for h in halves]
            r0s = [0 if diag_off is None else diag_off + h * tk for h in halves]
            masks = [None if diag_off is None else _causal(tq - r0, tk) for r0 in r0s]
            z = [lax.dot_general(_from(q, r0),k_ref[pl.ds(st, tk), :], NT_DIMS, preferred_element_type=F32) * scale
                 for st, r0 in zip(starts, r0s)]
            sp = [_softplus(zz) for zz in z]
            logsig = [zz - ss for zz, ss in zip(z, sp)]
            sp = [ss if m is None else jnp.where(m, ss, 0.0) for ss, m in zip(sp, masks)]
            sfx = [_tri_sum(ss, after) for ss in sp]
            probs = []
            for ls, sx, ss, m, r0 in zip(logsig, sfx, sp, masks, r0s):
                a = jnp.exp(ls - _over_keys(_from(carry, r0), tk) - sx)
                probs.append((a if m is None else jnp.where(m, a, 0.0)).astype(BF16))
                carry = _add_from(carry, r0, _rowsum(ss))
            for a, st, r0 in zip(probs, starts, r0s):
                o = _add_from(o, r0, jnp.dot(a, v_ref[pl.ds(st, tk), :], preferred_element_type=F32))
            return carry, o

        carry, o = jnp.zeros((tq, HEAD), F32), jnp.zeros((tq, HEAD), F32)
        for b in reversed(range(tq // tb)):
            carry, o = block(i * tq + b * tb, carry, o, b * tb)
        below = i * (tq // tb)
        carry, o = lax.fori_loop(0, below, lambda jj, co: block((below - 1 - jj) * tb, co[0], co[1], None),
                                 (carry, o))
        o_ref[...] = o.astype(o_ref.dtype)
        tot_ref[...] = carry
        host.stage("mid", comm_refs)
        host.stage("last", comm_refs)

    t = tq
    full = lambda off: pl.BlockSpec((s, HEAD), lambda h, i: (0, off + h))
    tile_spec = lambda off: pl.BlockSpec((t, HEAD), lambda h, i: (i, off + h))
    out = lambda dt: jax.ShapeDtypeStruct((s, n_heads * HEAD), dt)
    return pl.pallas_call(
        body, name=name, grid=(n_heads, s // t),
        in_specs=[tile_spec(qoff), full(koff), full(voff)] + host.in_specs,
        out_specs=[tile_spec(0), tile_spec(0)] + host.out_specs,
        out_shape=[out(BF16), out(F32)] + host.out_shape,
        scratch_shapes=host.scratch,
        compiler_params=_params(("parallel" if comm is None else "arbitrary", "arbitrary"), 40),
    )(qkv, qkv, qkv, *host.ins)


def _sb_bwd(qkv, qoff, koff, voff, do, tot, n_heads, name, comm=None):
    s = qkv.shape[0]
    tq, tb, tk = min(SB_TQ, s), SB_TB, SB_TK
    scale = HEAD ** -0.5
    host = _Host(comm, (n_heads, s // tq))

    def body(*refs):
        q_ref, k_ref, v_ref, do_ref, tot_ref = refs[:5]
        comm_refs, ((dq_ref, dk_ref, dv_ref), _) = host.split(refs, 5, 3)
        host.stage("first", comm_refs)
        i = pl.program_id(1)

        @pl.when(i == 0)
        def _():
            dk_ref[...] = jnp.zeros_like(dk_ref)
            dv_ref[...] = jnp.zeros_like(dv_ref)

        q = q_ref[...]
        do_b = do_ref[...].astype(BF16)
        total = tot_ref[...]
        upto = _tri(tk, lambda a, b: a <= b)
        before = _tri(tk, lambda a, b: a < b)[:tk]
        def block(base, lsum, psum, dq, diag_off):
            halves = list(range(tb // tk))
            starts = [pl.multiple_of(base + h * tk, tk) for h in halves]
            r0s = [0 if diag_off is None else diag_off + h * tk for h in halves]
            masks = [None if diag_off is None else _causal(tq - r0, tk) for r0 in r0s]
            keep = lambda x, m: x if m is None else jnp.where(m, x, 0.0)
            ks = [k_ref[pl.ds(st, tk), :] for st in starts]
            z = [lax.dot_general(_from(q, r0),kj, NT_DIMS, preferred_element_type=F32) * scale
                 for kj, r0 in zip(ks, r0s)]
            da = [lax.dot_general(_from(do_b, r0), v_ref[pl.ds(st, tk), :], NT_DIMS, preferred_element_type=F32)
                  for st, r0 in zip(starts, r0s)]
            sp = [_softplus(zz) for zz in z]
            logsig = [zz - ss for zz, ss in zip(z, sp)]
            sp = [keep(ss, m) for ss, m in zip(sp, masks)]
            pre = [_tri_sum(ss, upto) for ss in sp]
            probs, p = [], []
            for ls, px, ss, m, dd, r0 in zip(logsig, pre, sp, masks, da, r0s):
                a = keep(jnp.exp(ls - (_over_keys(_from(total, r0) - _from(lsum, r0), tk) - px)), m)
                probs.append(a.astype(BF16))
                p.append(a * dd)
                lsum = _add_from(lsum, r0, _rowsum(ss))
            cs = [jnp.dot(pp.astype(BF16), before, preferred_element_type=F32) for pp in p]
            dzs = []
            for ls, pp, cc, m, r0 in zip(logsig, p, cs, masks, r0s):
                c_all = _over_keys(_from(psum, r0), tk) + cc
                dzs.append(keep((pp - (pp + c_all) * jnp.exp(ls)) * scale, m).astype(BF16))
                psum = _add_from(psum, r0, _rowsum(pp))
            for dz_b, kj, r0 in zip(dzs, ks, r0s):
                dq = _add_from(dq, r0, jnp.dot(dz_b, kj, preferred_element_type=F32))
            for dz_b, a, st, r0 in zip(dzs, probs, starts, r0s):
                dk_ref[pl.ds(st, tk), :] += lax.dot_general(dz_b, _from(q, r0), TN_DIMS, preferred_element_type=F32)
                dv_ref[pl.ds(st, tk), :] += lax.dot_general(a, _from(do_b, r0), TN_DIMS,
                                                            preferred_element_type=F32)
            return lsum, psum, dq

        zero = jnp.zeros((tq, HEAD), F32)
        state = lax.fori_loop(0, i * (tq // tb), lambda j, c: block(j * tb, c[0], c[1], c[2], None),
                              (zero, zero, jnp.zeros((tq, HEAD), F32)))
        for b in range(tq // tb):
            state = block(i * tq + b * tb, *state, b * tb)
        dq_ref[...] = state[2].astype(dq_ref.dtype)
        host.stage("mid", comm_refs)
        host.stage("last", comm_refs)

    t = tq
    full = lambda off: pl.BlockSpec((s, HEAD), lambda h, i: (0, off + h))
    tile_spec = lambda off: pl.BlockSpec((t, HEAD), lambda h, i: (i, off + h))
    w = n_heads * HEAD
    return pl.pallas_call(
        body, name=name, grid=(n_heads, s // t),
        in_specs=[tile_spec(qoff), full(koff), full(voff), tile_spec(0), tile_spec(0)] + host.in_specs,
        out_specs=[tile_spec(0), full(0), full(0)] + host.out_specs,
        out_shape=[jax.ShapeDtypeStruct((s, w), BF16), jax.ShapeDtypeStruct((s, w), F32),
                   jax.ShapeDtypeStruct((s, w), F32)] + host.out_shape,
        scratch_shapes=host.scratch,
        compiler_params=_params(("parallel" if comm is None else "arbitrary", "arbitrary"), 48),
    )(qkv, qkv, qkv, do, tot, *host.ins)


def _coords():
    return lax.axis_index("x"), lax.axis_index("y"), lax.axis_index("c")


def _gather_plan(shards):
    n = len(shards)

    def run(stage, ins, outs, sems):
        send_sems, recv_sems, local_sems = sems
        x, y, c = _coords()
        me, sibling = (x, y, c), (x, y, 1 - c)
        chips = [(1 - x, y), (x, 1 - y), (1 - x, 1 - y)]

        def copy(w, k, block, to, src=None):
            dst = outs[w].at[4 * block[0] + 2 * block[1] + block[2]]
            return pltpu.make_async_remote_copy(
                src_ref=dst if src is None else src, dst_ref=dst,
                send_sem=send_sems.at[7 * w + k], recv_sem=recv_sems.at[7 * w + k],
                device_id=to, device_id_type=MESH)

        def mine():
            return [pltpu.make_async_copy(ins[w], outs[w].at[4 * x + 2 * y + c], local_sems.at[w]) for w in range(n)]

        def first():
            cps = []
            for w in range(n):
                cps.append(copy(w, 0, me, sibling, src=ins[w]))
                cps += [copy(w, 1 + j, me, (*chip, c), src=ins[w]) for j, chip in enumerate(chips)]
            return cps

        def passed():
            return [copy(w, 4 + j, (*chip, c), sibling) for w in range(n) for j, chip in enumerate(chips)]

        if stage == "first":
            for cp in mine() + first():
                cp.start()
        elif stage == "mid":
            onward = passed()
            for w in range(n):
                for j, chip in enumerate(chips):
                    copy(w, 1 + j, (*chip, c), me).wait_recv()
                    onward[3 * w + j].start()
        else:
            for w in range(n):
                copy(w, 0, sibling, me).wait_recv()
                for j, chip in enumerate(chips):
                    copy(w, 4 + j, (*chip, 1 - c), me).wait_recv()
            for cp in first() + passed():
                cp.wait_send()
            for cp in mine():
                cp.wait()

    return dict(
        ins=list(shards), run=run, stages=("first", "mid", "last"),
        out_shape=[jax.ShapeDtypeStruct((N_DEV,) + a.shape, a.dtype) for a in shards],
        scratch=[pltpu.SemaphoreType.DMA((7 * n,)), pltpu.SemaphoreType.DMA((7 * n,)), pltpu.SemaphoreType.DMA((n,))])


def _scatter_plan(grads):
    n = len(grads)

    def run(stage, ins, outs, sems):
        send_sems, recv_sems = sems
        x, y, c = _coords()
        cps = []
        for w in range(n):
            for mask in range(1, N_DEV):
                px, py, pc = x ^ (mask >> 2), y ^ ((mask >> 1) & 1), c ^ (mask & 1)
                cps.append(pltpu.make_async_remote_copy(
                    src_ref=ins[w].at[4 * px + 2 * py + pc], dst_ref=outs[w].at[mask - 1],
                    send_sem=send_sems.at[7 * w + mask - 1], recv_sem=recv_sems.at[7 * w + mask - 1],
                    device_id=(px, py, pc), device_id_type=MESH))
        for cp in cps:
            if stage == "first":
                cp.start()
            else:
                cp.wait()

    return dict(
        ins=list(grads), run=run, stages=("first", "last"),
        out_shape=[jax.ShapeDtypeStruct((N_DEV - 1,) + a.shape[1:], a.dtype) for a in grads],
        scratch=[pltpu.SemaphoreType.DMA((7 * n,)), pltpu.SemaphoreType.DMA((7 * n,))])


def _run_plan(plan, name):
    n_in, n_out = len(plan["ins"]), len(plan["out_shape"])

    def body(*refs):
        for stage in plan["stages"]:
            plan["run"](stage, refs[:n_in], refs[n_in:n_in + n_out], refs[n_in + n_out:])

    any_spec = pl.BlockSpec(memory_space=pl.ANY)
    return pl.pallas_call(
        body, name=name, in_specs=[any_spec] * n_in, out_specs=[any_spec] * n_out,
        out_shape=plan["out_shape"], scratch_shapes=plan["scratch"],
    )(*plan["ins"])


def _gather_rows(v):
    rows, width = v.shape

    def body(v_ref, out_ref, send_sems, recv_sems):
        x, y, c = _coords()
        out_ref[pl.ds(pl.multiple_of((4 * x + 2 * y + c) * rows, rows), rows), :] = v_ref[...]
        cps = []
        for mask in range(1, N_DEV):
            peer = (x ^ (mask >> 2), y ^ ((mask >> 1) & 1), c ^ (mask & 1))
            dst = out_ref.at[pl.ds(pl.multiple_of((4 * x + 2 * y + c) * rows, rows), rows), :]
            cps.append(pltpu.make_async_remote_copy(
                src_ref=v_ref, dst_ref=dst, send_sem=send_sems.at[mask - 1], recv_sem=recv_sems.at[mask - 1],
                device_id=peer, device_id_type=MESH))
        for cp in cps:
            cp.start()
        for cp in cps:
            cp.wait()

    vmem = pl.BlockSpec(memory_space=pltpu.VMEM)
    return pl.pallas_call(
        body, name="gather_small",
        in_specs=[vmem], out_specs=vmem,
        out_shape=jax.ShapeDtypeStruct((N_DEV * rows, width), F32),
        scratch_shapes=[pltpu.SemaphoreType.DMA((N_DEV - 1,)), pltpu.SemaphoreType.DMA((N_DEV - 1,))],
    )(v)


def _adamw(w, g, m, v):
    m = ADAM_B1 * m + (1.0 - ADAM_B1) * g
    v = ADAM_B2 * v + (1.0 - ADAM_B2) * jnp.square(g)
    m_hat = m / (1.0 - ADAM_B1 ** ADAM_STEP)
    v_hat = v / (1.0 - ADAM_B2 ** ADAM_STEP)
    delta = -ADAM_LR * (m_hat / (jnp.sqrt(v_hat) + ADAM_EPS) + ADAM_WD * w)
    return delta, m, v


def _tile_rows(rows, cols):
    tm = 1 << int(math.log2(max(2 * SUBLANES, (1 << 18) // cols)))
    while rows % tm:
        tm //= 2
    assert tm >= 2 * SUBLANES, (rows, cols)
    return tm


def _reduce_adam(w, m, v, own, got, name):
    r, c = w.shape
    tm = max(2 * SUBLANES, _tile_rows(r, c) // 2)

    def fn(wv, mv, vv, a, *peers):
        g = a
        for pv in peers:
            g = g + pv.astype(F32)
        return (g,) + _adamw(wv, g, mv, vv)

    blk = lambda a: (a, (tm, c), lambda i: (i, 0))
    got_blk = lambda j: (got, (None, tm, c), lambda i, j=j: (j, i, 0))
    return _ew(fn, [blk(w), blk(m), blk(v), blk(own)] + [got_blk(j) for j in range(N_DEV - 1)],
               [((r, c), F32, (tm, c), lambda i: (i, 0))] * 4, (r // tm,), name)


def _small_adam(gathered, params, moms, vels, widths):
    n = len(params)
    total = gathered.shape[1]

    def body(*refs):
        g_ref = refs[0]
        p_refs, m_refs, v_refs = refs[1:1 + n], refs[1 + n:1 + 2 * n], refs[1 + 2 * n:1 + 3 * n]
        sum_ref = refs[1 + 3 * n]
        outs = refs[2 + 3 * n:]
        g = g_ref[0:1, :]
        for p in range(1, N_DEV):
            g = g + g_ref[p * SUBLANES:p * SUBLANES + 1, :]
        sum_ref[...] = g
        off = 0
        for i, wd in enumerate(widths):
            d, m2, v2 = _adamw(p_refs[i][...], g[:, off:off + wd], m_refs[i][...], v_refs[i][...])
            outs[3 * i][...] = d
            outs[3 * i + 1][...] = m2
            outs[3 * i + 2][...] = v2
            off += wd

    vmem = pl.BlockSpec(memory_space=pltpu.VMEM)
    out_shape = [jax.ShapeDtypeStruct((1, total), F32)]
    for wd in widths:
        out_shape += [jax.ShapeDtypeStruct((1, wd), F32)] * 3
    return pl.pallas_call(
        body, name="small_adam",
        in_specs=[vmem] * (1 + 3 * n), out_specs=[vmem] * len(out_shape), out_shape=out_shape,
    )(gathered, *params, *moms, *vels)


def _cast_bf16(a, name):
    r, c = a.shape
    tm = _tile_rows(r, c)
    return _ew(lambda v: v, [(a, (tm, c), lambda i: (i, 0))], [((r, c), BF16, (tm, c), lambda i: (i, 0))],
               (r // tm,), name)[0]


def _fold_loss(parts, name):
    r, c = parts.shape

    def fn(v):
        return jnp.broadcast_to(jnp.sum(jnp.sum(v, axis=0, keepdims=True), axis=1, keepdims=True), (SUBLANES, HEAD))

    return _ew(fn, [_const_spec(parts)], [((SUBLANES, HEAD), F32, (SUBLANES, HEAD), lambda i: (0, 0))], (1,),
               name)[0][0:1]


def kernel(x, p, g_mix, w_in, qn_gain, kn_gain, w_branch_a, w_branch_b, w_out, g_mlp, w_up, w_down, g_ple, w_ple_gate, w_ple_proj, loss_target, m_g_mix, m_w_in, m_qn_gain, m_kn_gain, m_w_branch_a, m_w_branch_b, m_w_out, m_g_mlp, m_w_up, m_w_down, m_g_ple, m_w_ple_gate, m_w_ple_proj, v_g_mix, v_w_in, v_qn_gain, v_kn_gain, v_w_branch_a, v_w_branch_b, v_w_out, v_g_mlp, v_w_up, v_w_down, v_g_ple, v_w_ple_gate, v_w_ple_proj):
    x2 = x[0]
    tgt = loss_target[0]
    s, d = x2.shape
    wd_ = w_branch_a.shape[1]
    nh = wd_ // HEAD
    dff = w_up.shape[1]
    qkv_w = 6 * wd_
    tiles = lambda cols: cols // HEAD

    big = [w_in[0], w_branch_a[0], w_branch_b[0], w_out[0], w_up[0], w_down[0], w_ple_gate[0], w_ple_proj[0]]
    names = ["w_in", "w_branch_a", "w_branch_b", "w_out", "w_up", "w_down", "w_ple_gate", "w_ple_proj"]
    row_sharded = [False, False, False, True, False, True, True, False]
    shards = [_cast_bf16(a, "cast_" + nm) for a, nm in zip(big, names)]
    as_weight = lambda g, rs: g.reshape((1, N_DEV * g.shape[1], g.shape[2])) if rs else g
    as_blocks = lambda g, rs: g.reshape((N_DEV, g.shape[1] // N_DEV, g.shape[2])) if rs else g
    win = _run_plan(_gather_plan(shards[:1]), "all_gather_w_in")[0]

    tm = 1024 if s % 1024 == 0 else s
    tm_in = tk_s = 2048 if s % 2048 == 0 else tm
    tn_of = lambda n: 512 if n % 512 == 0 else (256 if n % 256 == 0 else n)
    tn_in = 256 if win.shape[2] % 256 == 0 else HEAD

    h = _rms_fwd(x2, g_mix, "norm_mix")
    qk_raw = _mm("nn", h, win, tm=tm_in, tn=tn_in, tk=d, out_dtypes=[F32], name="proj_qk",
                 n_off=0, n_cnt=2 * wd_ // tn_in)[0]
    rest = _mm("nn", h, win, tm=4096 if s % 4096 == 0 else tm_in, tn=tn_in, tk=d, out_dtypes=[BF16], name="proj_rest",
               n_off=2 * wd_ // tn_in, n_cnt=(win.shape[0] * win.shape[2] - 2 * wd_) // tn_in)[0]
    o_va, o_qb, o_kb, o_vb, o_ga, o_gb = 0, tiles(wd_), tiles(2 * wd_), tiles(3 * wd_), tiles(4 * wd_), tiles(4 * wd_ + d)
    tabs = _rope_tables(s)
    qa = _headnorm_rope(qk_raw, 0, qn_gain, tabs, nh, "rope_q")
    ka = _headnorm_rope(qk_raw, 1, kn_gain, tabs, nh, "rope_k")
    va = rest[:, :wd_]

    outs, lses = [], []
    for dil in DILATIONS:
        o_g, l_g = _dilated_fwd(_phase_major(qa, dil), _phase_major(ka, dil), _phase_major(va, dil), 0, nh,
                                f"dilated_fwd_{dil}")
        outs.append(_token_major(o_g))
        lses.append(_token_major(l_g))
    ya, lse_all = _mix_fwd(outs, lses, "mix_fwd")
    yb, sb_tot, *others = _sb_fwd(rest, o_qb, o_kb, o_vb, nh, "sb_fwd", comm=_gather_plan(shards[1:]))
    wba, wbb, wout, wup, wdown, wgate, wproj = [as_weight(g, rs) for g, rs in zip(others, row_sharded[1:])]

    tn_d = tn_of(wba.shape[2])
    za = _mm("nn", ya, wba, tm=tm, tn=tn_d, tk=wd_, out_dtypes=[BF16], name="branch_a")[0]

    def merge(acc, zav, gav, gbv):
        return _sigmoid(gav.astype(F32)) * zav.astype(F32) + _sigmoid(gbv.astype(F32)) * acc, acc

    merged, zb = _mm("nn", yb, wbb, tm=tm, tn=tn_d, tk=wd_, out_dtypes=[BF16, BF16], name="branch_b_merge",
                     epilogue=merge, extras=[(za, 0), (rest, o_ga * HEAD // tn_d), (rest, o_gb * HEAD // tn_d)])
    x1 = _mm("nn", merged, wout, tm=tm, tn=512, tk=d, out_dtypes=[F32], name="out_proj",
             epilogue=lambda acc, xv: (acc + xv,), extras=[(x2, 0)])[0]

    hm = _rms_fwd(x1, g_mlp, "norm_mlp")
    tn_u = min(wup.shape[2], 1024)
    u, act = _mm("nn", hm, wup, tm=tm, tn=tn_u, tk=d, out_dtypes=[BF16, BF16], name="mlp_up",
                 epilogue=lambda acc: (acc, jnp.square(jnp.maximum(acc, 0.0))))
    x3 = _mm("nn", act, wdown, tm=tm, tn=1024, tk=min(dff, 2048), out_dtypes=[F32], name="mlp_down",
             epilogue=lambda acc, xv: (acc + xv,), extras=[(x1, 0)])[0]

    hp = _rms_fwd(x3, g_ple, "norm_ple")
    p_b = _cast_bf16(p[0, 0], "cast_p")
    pp = _mm("nn", p_b, wproj, tm=tm, tn=tn_of(wproj.shape[2]), tk=p_b.shape[1], out_dtypes=[BF16],
             name="ple_proj")[0]

    def head(acc, ppv, xv, tv):
        sg = _sigmoid(acc)
        ppf = ppv.astype(F32)
        err = xv + ppf * sg - tv
        dy = err / d
        sq = jnp.square(err)
        return dy, dy * sg, dy * ppf * sg * (1.0 - sg), sq.reshape(-1, SUBLANES, sq.shape[-1]).sum(axis=0)

    n_i = s // tm
    dy, d_pp, d_gt, sq_parts = _mm(
        "nn", hp, wgate, tm=tm, tn=512, tk=d, out_dtypes=[F32, BF16, BF16], name="ple_gate_loss", epilogue=head,
        extras=[(pp, 0), (x3, 0), (tgt, 0)],
        extra_outs=[((n_i * SUBLANES, d), F32, (SUBLANES, 512), lambda i, j: (i, j))])
    loss_vec = _fold_loss(sq_parts, "loss_fold") * 0.5 / d

    both = [F32, BF16]
    g_wproj = _mm("tn", p_b, d_pp, tm=p_b.shape[1], tn=tn_of(wproj.shape[2]), tk=tk_s, out_dtypes=both,
                  name="grad_w_ple_proj", out_nb=N_DEV)
    g_wgate = _mm("tn", hp, d_gt, tm=1024, tn=1024, tk=tk_s, out_dtypes=both, name="grad_w_ple_gate")
    d_hp = _mm("nt", d_gt, wgate, tm=tm, tn=512, tk=d, out_dtypes=[F32], name="d_hp")[0]
    dx3, dx3_b, g_gple = _rms_bwd(d_hp, x3, g_ple, dy, "norm_ple_bwd")

    d_u = _mm("nt", dx3_b, wdown, tm=tm, tn=1024, tk=d, out_dtypes=[BF16], name="d_u",
              epilogue=lambda acc, uv: (acc * (2.0 * jnp.maximum(uv.astype(F32), 0.0)),), extras=[(u, 0)])[0]
    g_wdown = _mm("tn", act, dx3_b, tm=1024, tn=1024, tk=tk_s, out_dtypes=both, name="grad_w_down")
    g_wup = _mm("tn", hm, d_u, tm=1024, tn=wup.shape[2], tk=tk_s, out_dtypes=both, name="grad_w_up", out_nb=N_DEV)
    d_hm = _mm("nt", d_u, wup, tm=tm, tn=min(d, 2048), tk=wup.shape[2], out_dtypes=[F32], name="d_hm")[0]
    dx1, dx1_b, g_gmlp = _rms_bwd(d_hm, x1, g_mlp, dx3, "norm_mlp_bwd")

    def unmerge(acc, gav, gbv, zav, zbv):
        sa, sb = _sigmoid(gav.astype(F32)), _sigmoid(gbv.astype(F32))
        return acc * sa, acc * sb, acc * zav.astype(F32) * sa * (1.0 - sa), acc * zbv.astype(F32) * sb * (1.0 - sb)

    d_za, d_zb, d_ga, d_gb = _mm(
        "nt", dx1_b, wout, tm=tm, tn=512, tk=d, out_dtypes=[BF16] * 4, name="d_merged", epilogue=unmerge,
        extras=[(rest, o_ga * HEAD // 512), (rest, o_gb * HEAD // 512), (za, 0), (zb, 0)])
    g_wout = _mm("tn", merged, dx1_b, tm=1024, tn=1024, tk=tk_s, out_dtypes=both, name="grad_w_out")
    g_wba = _mm("tn", ya, d_za, tm=wd_, tn=wba.shape[2], tk=tk_s, out_dtypes=both, name="grad_w_branch_a",
                out_nb=N_DEV)
    g_wbb = _mm("tn", yb, d_zb, tm=wd_, tn=wbb.shape[2], tk=tk_s, out_dtypes=both, name="grad_w_branch_b",
                out_nb=N_DEV)
    d_ya = _mm("nt", d_za, wba, tm=tm, tn=wd_, tk=wba.shape[2], out_dtypes=[F32], name="d_ya")[0]
    d_yb = _mm("nt", d_zb, wbb, tm=tm, tn=wd_, tk=wbb.shape[2], out_dtypes=[F32], name="d_yb")[0]

    grads = [None, g_wba, g_wbb, g_wout, g_wup, g_wdown, g_wgate, g_wproj]
    early = _scatter_plan([as_blocks(g[1], rs) for g, rs in zip(grads[1:], row_sharded[1:])])
    d_qb, d_kb, d_vb, *got = _sb_bwd(rest, o_qb, o_kb, o_vb, d_yb, sb_tot, nh, "sb_bwd", comm=early)
    d_ya_b, stats = _mix_stats(d_ya, ya, lse_all, nh, "mix_stats")
    dqs, dks, dvs = [], [], []
    for dil in DILATIONS:
        dq_g, dk_g, dv_g = _dilated_bwd(
            _phase_major(qa, dil), _phase_major(ka, dil), _phase_major(va, dil), 0, _phase_major(d_ya_b, dil),
            _phase_major(stats, dil), nh, f"dilated_bwd_{dil}")
        dqs.append(_token_major(dq_g))
        dks.append(_token_major(dk_g))
        dvs.append(_token_major(dv_g))
    d_qa, g_qn = _headnorm_rope_bwd(dqs, qk_raw, 0, qn_gain, tabs, nh, "rope_q_bwd")
    d_ka, g_kn = _headnorm_rope_bwd(dks, qk_raw, 1, kn_gain, tabs, nh, "rope_k_bwd")
    tmr = 256
    d_va = _ew(lambda a, b, c: a + b + c, [_row_spec(a, tmr) for a in dvs],
               [((s, wd_), BF16, (tmr, wd_), lambda i: (i, 0))], (s // tmr,), "sum_dv")[0]
    d_proj = jnp.concatenate([d_qa, d_ka, d_va, d_qb, d_kb.astype(BF16), d_vb.astype(BF16), d_ga, d_gb], axis=1)

    grads[0] = _mm("tn", h, d_proj, tm=1024, tn=win.shape[2], tk=tk_s, out_dtypes=both, name="grad_w_in",
                   out_nb=N_DEV)
    d_h, got_in = _mm("nt", d_proj, win, tm=tm, tn=min(d, 2048), tk=win.shape[2], out_dtypes=[F32], name="d_h",
                      comm=_scatter_plan([grads[0][1]]))
    got = [got_in] + got
    dx, _, g_gmix = _rms_bwd(d_h, x2, g_mix, dx1, "norm_mix_bwd")

    cx, cy, cc = _coords()
    me = 4 * cx + 2 * cy + cc
    moms = [m_w_in, m_w_branch_a, m_w_branch_b, m_w_out, m_w_up, m_w_down, m_w_ple_gate, m_w_ple_proj]
    vels = [v_w_in, v_w_branch_a, v_w_branch_b, v_w_out, v_w_up, v_w_down, v_w_ple_gate, v_w_ple_proj]
    big_out = {}
    for i, nm in enumerate(names):
        own = lax.dynamic_index_in_dim(as_blocks(grads[i][0], row_sharded[i]), me, axis=0, keepdims=False)
        big_out[nm] = [a[None] for a in _reduce_adam(big[i], moms[i][0], vels[i][0], own, got[i], "adam_" + nm)]

    small_names = ["g_mix", "qn_gain", "kn_gain", "g_mlp", "g_ple"]
    small_p = [g_mix, qn_gain, kn_gain, g_mlp, g_ple]
    small_m = [m_g_mix, m_qn_gain, m_kn_gain, m_g_mlp, m_g_ple]
    small_v = [v_g_mix, v_qn_gain, v_kn_gain, v_g_mlp, v_g_ple]
    small_g = [g_gmix, g_qn, g_kn, g_gmlp, g_gple]
    widths = [a.shape[1] for a in small_p]
    vec = jnp.concatenate(small_g + [loss_vec], axis=1)
    vec = jnp.pad(vec, ((0, SUBLANES - 1), (0, 0)))
    res = _small_adam(_gather_rows(vec), small_p, small_m, small_v, widths)
    summed = res[0]
    small_out, off = {}, 0
    for i, nm in enumerate(small_names):
        small_out[nm] = [summed[:, off:off + widths[i]]] + list(res[1 + 3 * i:4 + 3 * i])
        off += widths[i]
    loss = summed[0, off]

    order = ["g_mix", "w_in", "qn_gain", "kn_gain", "w_branch_a", "w_branch_b", "w_out", "g_mlp", "w_up", "w_down",
             "g_ple", "w_ple_gate", "w_ple_proj"]
    table = {**big_out, **small_out}
    result = [loss, dx[None]]
    for kind in range(4):
        result += [table[nm][kind] for nm in order]
    return tuple(result)
```

```python
import functools
import math

import jax
import jax.numpy as jnp
from jax import lax
from jax.experimental import pallas as pl
from jax.experimental.pallas import tpu as pltpu

F32 = jnp.float32
BF16 = jnp.bfloat16
MESH = pl.DeviceIdType.MESH

HEAD = 128
WINDOW = 128
DILATIONS = (1, 4, 16)
ROT = HEAD // 4
ROPE_THETA = 500000.0
EPS = 1e-6
NEG = -1e30
N_DEV = 8

ADAM_LR = 0.001
ADAM_B1 = 0.9
ADAM_B2 = 0.999
ADAM_EPS = 1e-08
ADAM_WD = 0.01
ADAM_STEP = 10

SUBLANES = 8
VMEM_CAP_MB = 56

NT_DIMS = (((1,), (1,)), ((), ()))
TN_DIMS = (((0,), (0,)), ((), ()))


def _params(semantics, vmem_mb):
    return pltpu.CompilerParams(dimension_semantics=semantics, vmem_limit_bytes=min(vmem_mb, VMEM_CAP_MB) << 20)


def _sigmoid(x):
    return 0.5 + 0.5 * jnp.tanh(0.5 * x)


def _split_bf16(x):
    hi = x.astype(BF16)
    lo = (x - hi.astype(F32)).astype(BF16)
    return hi, lo


class _Host:
    def __init__(self, comm, grid):
        self.comm, self.grid = comm, grid
        any_spec = pl.BlockSpec(memory_space=pl.ANY)
        self.ins = list(comm["ins"]) if comm else []
        self.out_shape = list(comm["out_shape"]) if comm else []
        self.scratch = list(comm["scratch"]) if comm else []
        self.in_specs = [any_spec] * len(self.ins)
        self.out_specs = [any_spec] * len(self.out_shape)

    def split(self, refs, n_in, n_out):
        pos = n_in
        c_in = refs[pos:pos + len(self.ins)]
        pos += len(self.ins)
        outs = refs[pos:pos + n_out]
        pos += n_out
        c_out = refs[pos:pos + len(self.out_shape)]
        pos += len(self.out_shape)
        own = len(refs) - pos - len(self.scratch)
        return (c_in, c_out, refs[pos + own:]), (outs, refs[pos:pos + own])

    def stage(self, which, comm_refs):
        if self.comm is None or which not in self.comm["stages"]:
            return
        grid = self.grid
        at = {"first": [0] * len(grid), "mid": [grid[0] // 2] + [0] * (len(grid) - 1),
              "last": [g - 1 for g in grid]}[which]
        cond = functools.reduce(jnp.logical_and, [pl.program_id(ax) == v for ax, v in enumerate(at)])

        @pl.when(cond)
        def _():
            self.comm["run"](which, *comm_refs)


def _ew(fn, ins, outs, grid, name, colsums=(), vmem_mb=40):
    n_in, n_out, n_cs = len(ins), len(outs), len(colsums)
    steps = math.prod(grid)

    def body(*refs):
        in_refs = refs[:n_in]
        out_refs = refs[n_in:n_in + n_out]
        cs_refs = refs[n_in + n_out:n_in + n_out + n_cs]
        acc_refs = refs[n_in + n_out + n_cs:]
        vals = fn(*[r[...] for r in in_refs])
        if not isinstance(vals, tuple):
            vals = (vals,)
        for r, v in zip(out_refs, vals[:n_out]):
            r[...] = v.astype(r.dtype)
        if n_cs:
            step = pl.program_id(0)
            for ax in range(1, len(grid)):
                step = step * grid[ax] + pl.program_id(ax)
            for acc, cs, v in zip(acc_refs, cs_refs, vals[n_out:]):
                part = v.reshape(-1, SUBLANES, v.shape[-1]).sum(axis=0)

                @pl.when(step == 0)
                def _(acc=acc, part=part):
                    acc[...] = part

                @pl.when(step > 0)
                def _(acc=acc, part=part):
                    acc[...] += part

                @pl.when(step == steps - 1)
                def _(acc=acc, cs=cs):
                    cs[...] = acc[...].sum(axis=0, keepdims=True)

    out_shape = [jax.ShapeDtypeStruct(s, d) for s, d, _, _ in outs]
    out_specs = [pl.BlockSpec(b, m) for _, _, b, m in outs]
    for w in colsums:
        out_shape.append(jax.ShapeDtypeStruct((1, w), F32))
        out_specs.append(pl.BlockSpec((1, w), lambda *_: (0, 0)))
    sem = ("arbitrary",) * len(grid) if n_cs else ("parallel",) * len(grid)
    res = pl.pallas_call(
        body, name=name, grid=grid,
        in_specs=[pl.BlockSpec(b, m) for _, b, m in ins],
        out_specs=out_specs, out_shape=out_shape,
        scratch_shapes=[pltpu.VMEM((SUBLANES, w), F32) for w in colsums],
        compiler_params=_params(sem, vmem_mb),
    )(*[a for a, _, _ in ins])
    return res


def _row_spec(a, tm):
    return (a, (tm, a.shape[1]), lambda i: (i, 0))


def _const_spec(a):
    return (a, a.shape, lambda *_: (0,) * a.ndim)


def _mm(mode, a, b, *, tm, tn, tk, out_dtypes, name, epilogue=None, extras=(), n_off=0, n_cnt=None,
        out_nb=1, extra_outs=(), vmem_mb=52, comm=None):
    if mode == "nn":
        m, kdim = a.shape
        nb, _, n = b.shape
        npb = n // tn
        ncols = nb * n
        n_tiles = (ncols // tn) if n_cnt is None else n_cnt
        a_spec = pl.BlockSpec((tm, tk), lambda i, j, k: (i, k))
        b_spec = pl.BlockSpec((None, tk, tn), lambda i, j, k: ((j + n_off) // npb, k, (j + n_off) % npb))
        dims = (((1,), (0,)), ((), ()))
    elif mode == "nt":
        m, kdim = a.shape
        nb, nout, n = b.shape
        kpb = n // tk
        n_tiles = nout // tn
        a_spec = pl.BlockSpec((tm, tk), lambda i, j, k: (i, k))
        b_spec = pl.BlockSpec((None, tn, tk), lambda i, j, k: (k // kpb, j, k % kpb))
        dims = NT_DIMS
    else:
        kdim, m = a.shape
        n_tiles = (b.shape[1] // tn) if n_cnt is None else n_cnt
        ncols = n_tiles * tn
        a_spec = pl.BlockSpec((tk, tm), lambda i, j, k: (k, i))
        b_spec = pl.BlockSpec((tk, tn), lambda i, j, k: (k, j + n_off))
        dims = TN_DIMS
    nk = kdim // tk
    assert kdim % tk == 0 and m % tm == 0
    grid = (m // tm, n_tiles, nk)
    n_ex, n_out = len(extras), len(out_dtypes) + len(extra_outs)
    host = _Host(comm, grid)

    def body(*refs):
        a_ref, b_ref = refs[0], refs[1]
        ex_refs = refs[2:2 + n_ex]
        comm_refs, (out_refs, scratch) = host.split(refs, 2 + n_ex, n_out)
        host.stage("first", comm_refs)

        def finish(acc):
            vals = (acc,) * n_out if epilogue is None else epilogue(acc, *[r[...] for r in ex_refs])
            for r, v in zip(out_refs, vals):
                r[...] = v.astype(r.dtype)

        def part():
            return lax.dot_general(a_ref[...], b_ref[...], dims, preferred_element_type=F32)

        if nk == 1:
            finish(part())
        else:
            acc_ref = scratch[0]
            k = pl.program_id(2)

            @pl.when(k == 0)
            def _():
                acc_ref[...] = part()

            @pl.when((k > 0) & (k < nk - 1))
            def _():
                acc_ref[...] += part()

            @pl.when(k == nk - 1)
            def _():
                finish(acc_ref[...] + part())

        host.stage("mid", comm_refs)
        host.stage("last", comm_refs)

    if mode == "tn":
        npo = (ncols // out_nb) // tn
        out_shape = [jax.ShapeDtypeStruct((out_nb, m, ncols // out_nb), d) for d in out_dtypes]
        out_specs = [pl.BlockSpec((None, tm, tn), lambda i, j, k: (j // npo, i, j % npo)) for _ in out_dtypes]
    else:
        out_shape = [jax.ShapeDtypeStruct((m, n_tiles * tn), d) for d in out_dtypes]
        out_specs = [pl.BlockSpec((tm, tn), lambda i, j, k: (i, j)) for _ in out_dtypes]
    for s, d, blk, imap in extra_outs:
        out_shape.append(jax.ShapeDtypeStruct(s, d))
        out_specs.append(pl.BlockSpec(blk, lambda i, j, k, imap=imap: imap(i, j)))
    ex_specs = [pl.BlockSpec((tm, tn), lambda i, j, k, off=off: (i, j + off)) for _, off in extras]
    sem = ("parallel", "parallel", "arbitrary") if comm is None else ("arbitrary",) * 3
    return pl.pallas_call(
        body, name=name, grid=grid,
        in_specs=[a_spec, b_spec] + ex_specs + host.in_specs,
        out_specs=out_specs + host.out_specs, out_shape=out_shape + host.out_shape,
        scratch_shapes=([pltpu.VMEM((tm, tn), F32)] if nk > 1 else []) + host.scratch,
        compiler_params=_params(sem, vmem_mb),
    )(a, b, *[e for e, _ in extras], *host.ins)


def _rms_fwd(x, g, name):
    tm = 256

    def fn(xv, gv):
        r = lax.rsqrt(jnp.mean(xv * xv, axis=-1, keepdims=True) + EPS)
        return xv * r * gv

    return _ew(fn, [_row_spec(x, tm), _const_spec(g)], [(x.shape, BF16, (tm, x.shape[1]), lambda i: (i, 0))],
               (x.shape[0] // tm,), name)[0]


def _rms_bwd(dh, x, g, res, name):
    tm = 256
    d = x.shape[1]

    def fn(dhv, xv, gv, rv):
        r = lax.rsqrt(jnp.mean(xv * xv, axis=-1, keepdims=True) + EPS)
        xh = xv * r
        dyg = dhv * gv
        dx = rv + r * (dyg - xh * jnp.mean(dyg * xh, axis=-1, keepdims=True))
        return dx, dx, dhv * xh

    spec = lambda dt: (x.shape, dt, (tm, d), lambda i: (i, 0))
    return _ew(fn, [_row_spec(dh, tm), _row_spec(x, tm), _const_spec(g), _row_spec(res, tm)],
               [spec(F32), spec(BF16)], (x.shape[0] // tm,), name, colsums=(d,))


def _rope_tables(s):
    half = ROT // 2
    pos = jnp.arange(s, dtype=F32)
    inv = ROPE_THETA ** (-jnp.arange(0, ROT, 2, dtype=F32) / ROT)
    ang = pos[:, None] * inv[None, :]
    cos, sin = jnp.cos(ang), jnp.sin(ang)
    pad = jnp.zeros((s, HEAD - ROT), F32)
    c = jnp.concatenate([cos, cos, pad + 1.0], axis=1)
    a = jnp.concatenate([-sin, jnp.zeros_like(sin), pad], axis=1)
    b = jnp.concatenate([jnp.zeros_like(sin), sin, pad], axis=1)
    return c, a, b


def _heads(x, n_heads):
    return [x[:, h * HEAD:(h + 1) * HEAD] for h in range(n_heads)]


def _headnorm_rope(proj, part, gain, tabs, n_heads, name):
    tm = 512
    s = proj.shape[0]
    w = n_heads * HEAD

    def fn(xs, gv, c, a, b):
        outs = []
        for xv in _heads(xs, n_heads):
            r = lax.rsqrt(jnp.mean(xv * xv, axis=-1, keepdims=True) + EPS)
            y = xv * r * gv
            outs.append(c * y + a * pltpu.roll(y, HEAD - ROT // 2, 1) + b * pltpu.roll(y, ROT // 2, 1))
        return jnp.concatenate(outs, axis=1)

    tab = lambda t: (t, (tm, HEAD), lambda i: (i, 0))
    return _ew(fn, [(proj, (tm, w), lambda i: (i, part)), (gain, (1, HEAD), lambda i: (0, 0))] + [tab(t) for t in tabs],
               [((s, w), BF16, (tm, w), lambda i: (i, 0))], (s // tm,), name)[0]


def _headnorm_rope_bwd(dys, proj, part, gain, tabs, n_heads, name):
    tm = 256
    s = proj.shape[0]
    w = n_heads * HEAD
    n_dy = len(dys)

    def fn(*vals):
        dy_all = vals[0]
        for v in vals[1:n_dy]:
            dy_all = dy_all + v
        xs, gv, c, a, b = vals[n_dy:]
        dxs, dgain = [], None
        for dy, xv in zip(_heads(dy_all, n_heads), _heads(xs, n_heads)):
            dn = c * dy + pltpu.roll(a * dy, ROT // 2, 1) + pltpu.roll(b * dy, HEAD - ROT // 2, 1)
            r = lax.rsqrt(jnp.mean(xv * xv, axis=-1, keepdims=True) + EPS)
            xh = xv * r
            dyg = dn * gv
            dxs.append(r * (dyg - xh * jnp.mean(dyg * xh, axis=-1, keepdims=True)))
            dgain = dn * xh if dgain is None else dgain + dn * xh
        return jnp.concatenate(dxs, axis=1), dgain

    tab = lambda t: (t, (tm, HEAD), lambda i: (i, 0))
    return _ew(fn, [(d, (tm, w), lambda i: (i, 0)) for d in dys]
               + [(proj, (tm, w), lambda i: (i, part)), (gain, (1, HEAD), lambda i: (0, 0))] + [tab(t) for t in tabs],
               [((s, w), BF16, (tm, w), lambda i: (i, 0))], (s // tm,), name, colsums=(HEAD,))


def _phase_major(a, d):
    s, w = a.shape
    if d == 1:
        return a.reshape(1, s, w)
    return a.reshape(s // d, d, w).transpose(1, 0, 2)


def _token_major(a):
    d, m, w = a.shape
    if d == 1:
        return a.reshape(m, w)
    return a.transpose(1, 0, 2).reshape(m * d, w)


def _dil_tq(m):
    return min(1024, m)


def _dil_heads(tq, n_heads):
    return max(1, min(n_heads, 1024 // tq))


def _tile_up(parts, hb, per):
    cols = [jnp.concatenate(parts[j * per:(j + 1) * per], axis=0) for j in range(hb)]
    return cols[0] if hb == 1 else jnp.concatenate(cols, axis=1)


def _dilated_fwd(q, k, v, voff, n_heads, name):
    d, m, _ = q.shape
    tq = _dil_tq(m)
    nq = m // tq
    per = tq // WINDOW
    hb = _dil_heads(tq, n_heads)
    scale = HEAD ** -0.5
    blocks = [(j, b) for j in range(hb) for b in range(per)]
    lanes = [slice(j * HEAD, (j + 1) * HEAD) for j in range(hb)]

    def body(q_ref, kc_ref, kp_ref, vc_ref, vp_ref, o_ref, l_ref):
        n = pl.program_id(2)
        kk = jnp.concatenate([kp_ref[...], kc_ref[...]], axis=0)
        vv = jnp.concatenate([vp_ref[...], vc_ref[...]], axis=0)
        row = lax.broadcasted_iota(jnp.int32, (WINDOW, 2 * WINDOW), 0)
        col = lax.broadcasted_iota(jnp.int32, (WINDOW, 2 * WINDOW), 1)
        band = (col >= row) & (col <= row + WINDOW)
        q = q_ref[...]
        rows = [slice(b * WINDOW, (b + 1) * WINDOW) for b in range(per)]
        keys = [slice(b * WINDOW, (b + 2) * WINDOW) for b in range(per)]
        s = [lax.dot_general(q[rows[b], lanes[j]], kk[keys[b], lanes[j]], NT_DIMS,
                             preferred_element_type=F32) * scale for j, b in blocks]
        es, outs, lses = [], [], []
        for (j, b), sb in zip(blocks, s):
            valid = band if b else band & ((n > 0) | (col >= WINDOW))
            sb = jnp.where(valid, sb, NEG)
            mx = jnp.max(sb, axis=-1, keepdims=True)
            e = jnp.exp(sb - mx)
            den = jnp.sum(e, axis=-1, keepdims=True)
            es.append((e.astype(BF16), den))
            lses.append(jnp.broadcast_to(mx + jnp.log(den), (WINDOW, HEAD)))
        for (j, b), (e, den) in zip(blocks, es):
            outs.append(jnp.dot(e, vv[keys[b], lanes[j]], preferred_element_type=F32) / den)
        o_ref[...] = _tile_up(outs, hb, per)
        l_ref[...] = _tile_up(lses, hb, per)

    assert voff % hb == 0
    cur = lambda off: pl.BlockSpec((None, tq, hb * HEAD), lambda r, h, n: (r, n, off // hb + h))
    prev = lambda off: pl.BlockSpec((None, WINDOW, hb * HEAD),
                                    lambda r, h, n: (r, jnp.maximum(n * per - 1, 0), off // hb + h))
    out = jax.ShapeDtypeStruct((d, m, n_heads * HEAD), F32)
    return pl.pallas_call(
        body, name=name, grid=(d, n_heads // hb, nq),
        in_specs=[cur(0), cur(0), prev(0), cur(voff), prev(voff)],
        out_specs=[cur(0), cur(0)], out_shape=[out, out],
        compiler_params=_params(("parallel",) * 3, 32),
    )(q, k, k, v, v)


def _dilated_bwd(q, k, v, voff, dy, stats, n_heads, name):
    d, m, _ = q.shape
    tq = _dil_tq(m)
    nq = m // tq
    per = tq // WINDOW
    last_blk = m // WINDOW - 1
    hb = _dil_heads(tq, n_heads)
    scale = HEAD ** -0.5
    blocks = [(j, b) for j in range(hb) for b in range(per)]
    lanes = [slice(j * HEAD, (j + 1) * HEAD) for j in range(hb)]

    def body(qc_ref, qn_ref, kc_ref, kp_ref, vc_ref, vp_ref, dyc_ref, dyn_ref, sc_ref, sn_ref,
             dq_ref, dk_ref, dv_ref):
        n = pl.program_id(2)
        kk = jnp.concatenate([kp_ref[...], kc_ref[...]], axis=0)
        vv = jnp.concatenate([vp_ref[...], vc_ref[...]], axis=0)
        qq = jnp.concatenate([qc_ref[...], qn_ref[...]], axis=0)
        dyy = jnp.concatenate([dyc_ref[...], dyn_ref[...]], axis=0)
        st = jnp.concatenate([sc_ref[...], sn_ref[...]], axis=0)
        half = HEAD // 2
        row = lax.broadcasted_iota(jnp.int32, (WINDOW, 2 * WINDOW), 0)
        col = lax.broadcasted_iota(jnp.int32, (WINDOW, 2 * WINDOW), 1)
        band = (col >= row) & (col <= row + WINDOW)
        rows = [slice(b * WINDOW, (b + 1) * WINDOW) for b in range(per)]
        wide = [slice(b * WINDOW, (b + 2) * WINDOW) for b in range(per)]
        nt = lambda a, b: lax.dot_general(a, b, NT_DIMS, preferred_element_type=F32)
        s = [nt(qq[rows[b], lanes[j]], kk[wide[b], lanes[j]]) * scale for j, b in blocks]
        dp = [nt(dyy[rows[b], lanes[j]], vv[wide[b], lanes[j]]) for j, b in blocks]
        ds = []
        for (j, b), sb, dpb in zip(blocks, s, dp):
            valid = band if b else band & ((n > 0) | (col >= WINDOW))
            stb = st[rows[b], lanes[j]]
            p = jnp.where(valid, jnp.exp(jnp.minimum(sb - stb[:, :1], 0.0)), 0.0)
            ds.append((p * (dpb - stb[:, half:half + 1]) * scale).astype(BF16))
        dq = [jnp.dot(dsb, kk[wide[b], lanes[j]], preferred_element_type=F32) for (j, b), dsb in zip(blocks, ds)]
        kc, vc = kc_ref[...], vc_ref[...]
        s2 = [nt(kc[rows[b], lanes[j]], qq[wide[b], lanes[j]]) * scale for j, b in blocks]
        dp2 = [nt(vc[rows[b], lanes[j]], dyy[wide[b], lanes[j]]) for j, b in blocks]
        ds2, p2 = [], []
        for (j, b), sb, dpb in zip(blocks, s2, dp2):
            valid = band if b < per - 1 else band & ((n < nq - 1) | (col < WINDOW))
            st_t = st[wide[b], lanes[j]].T
            pb = jnp.where(valid, jnp.exp(jnp.minimum(sb - st_t[:1], 0.0)), 0.0)
            ds2.append((pb * (dpb - st_t[half:half + 1]) * scale).astype(BF16))
            p2.append(pb.astype(BF16))
        dk = [jnp.dot(x, qq[wide[b], lanes[j]], preferred_element_type=F32) for (j, b), x in zip(blocks, ds2)]
        dv = [jnp.dot(x, dyy[wide[b], lanes[j]], preferred_element_type=F32) for (j, b), x in zip(blocks, p2)]
        dq_ref[...] = _tile_up(dq, hb, per)
        dk_ref[...] = _tile_up(dk, hb, per)
        dv_ref[...] = _tile_up(dv, hb, per)

    assert voff % hb == 0
    cur = lambda off: pl.BlockSpec((None, tq, hb * HEAD), lambda r, h, n: (r, n, off // hb + h))
    prev = lambda off: pl.BlockSpec((None, WINDOW, hb * HEAD),
                                    lambda r, h, n: (r, jnp.maximum(n * per - 1, 0), off // hb + h))
    nxt = lambda off: pl.BlockSpec((None, WINDOW, hb * HEAD),
                                   lambda r, h, n: (r, jnp.minimum((n + 1) * per, last_blk), off // hb + h))
    out = jax.ShapeDtypeStruct((d, m, n_heads * HEAD), F32)
    return pl.pallas_call(
        body, name=name, grid=(d, n_heads // hb, nq),
        in_specs=[cur(0), nxt(0), cur(0), prev(0), cur(voff), prev(voff), cur(0), nxt(0), cur(0), nxt(0)],
        out_specs=[cur(0)] * 3, out_shape=[out] * 3,
        compiler_params=_params(("parallel",) * 3, 40),
    )(q, q, k, k, v, v, dy, dy, stats, stats)


def _mix_fwd(os_, ls_, name):
    tm = 256
    s, w = os_[0].shape
    n = len(os_)

    def fn(*vals):
        o, l = vals[:n], vals[n:]
        mx = functools.reduce(jnp.maximum, l)
        e = [jnp.exp(x - mx) for x in l]
        den = functools.reduce(jnp.add, e)
        y = functools.reduce(jnp.add, [ei * oi for ei, oi in zip(e, o)]) / den
        return y, mx + jnp.log(den)

    return _ew(fn, [_row_spec(a, tm) for a in list(os_) + list(ls_)],
               [((s, w), BF16, (tm, w), lambda i: (i, 0)), ((s, w), F32, (tm, w), lambda i: (i, 0))],
               (s // tm,), name)


def _mix_stats(dy, y, lse, n_heads, name):
    tm = 256
    s = dy.shape[0]
    w = n_heads * HEAD

    def fn(dys, ys, ls):
        lane = lax.broadcasted_iota(jnp.int32, (tm, HEAD), 1)
        packed = []
        for a, b, l in zip(_heads(dys, n_heads), _heads(ys, n_heads), _heads(ls, n_heads)):
            delta = jnp.sum(a * b.astype(F32), axis=-1, keepdims=True)
            packed.append(jnp.where(lane < HEAD // 2, l, delta))
        return dys, jnp.concatenate(packed, axis=1)

    blk = lambda a: (a, (tm, w), lambda i: (i, 0))
    out = lambda dt: ((s, w), dt, (tm, w), lambda i: (i, 0))
    return _ew(fn, [blk(dy), blk(y), blk(lse)], [out(BF16), out(F32)], (s // tm,), name)


SB_TQ = 1024
SB_TB = 512
SB_TK = 256


def _softplus(z):
    return jnp.where(z > 20.0, z, jnp.log(1.0 + jnp.exp(z)))


def _tri(t, cmp):
    rows = lax.broadcasted_iota(jnp.int32, (2 * t, t), 0)
    cols = lax.broadcasted_iota(jnp.int32, (2 * t, t), 1)
    return jnp.where(cmp(jnp.where(rows >= t, rows - t, rows), cols), 1.0, 0.0).astype(BF16)


def _tri_sum(x, tri):
    return jnp.dot(jnp.concatenate(_split_bf16(x), axis=1), tri, preferred_element_type=F32)


def _causal(rows, cols):
    return lax.broadcasted_iota(jnp.int32, (rows, cols), 1) < lax.broadcasted_iota(jnp.int32, (rows, cols), 0)


def _rowsum(x):
    return jnp.broadcast_to(jnp.sum(x, axis=-1, keepdims=True), (x.shape[0], HEAD))


def _over_keys(c, width):
    return jnp.concatenate([c] * (width // HEAD), axis=1)


def _from(x, r0):
    return x if r0 == 0 else x[r0:]


def _add_from(x, r0, upd):
    return x + upd if r0 == 0 else jnp.concatenate([x[:r0], x[r0:] + upd], axis=0)


def _sb_fwd(qkv, qoff, koff, voff, n_heads, name, comm=None):
    s = qkv.shape[0]
    tq, tb, tk = min(SB_TQ, s), SB_TB, SB_TK
    scale = HEAD ** -0.5
    host = _Host(comm, (n_heads, s // tq))

    def body(*refs):
        q_ref, k_ref, v_ref = refs[:3]
        comm_refs, ((o_ref, tot_ref), _) = host.split(refs, 3, 2)
        host.stage("first", comm_refs)
        i = pl.program_id(1)
        q = q_ref[...]
        after = _tri(tk, lambda a, b: a > b)

        def block(base, carry, o, diag_off):
            halves = list(reversed(range(tb // tk)))
            starts = [pl.multiple_of(base + h * tk, tk) for h in halves]
            r0s = [0 if diag_off is None else diag_off + h * tk for h in halves]
            masks = [None if diag_off is None else _causal(tq - r0, tk) for r0 in r0s]
            z = [lax.dot_general(_from(q, r0),k_ref[pl.ds(st, tk), :], NT_DIMS, preferred_element_type=F32) * scale
                 for st, r0 in zip(starts, r0s)]
            sp = [_softplus(zz) for zz in z]
            logsig = [zz - ss for zz, ss in zip(z, sp)]
            sp = [ss if m is None else jnp.where(m, ss, 0.0) for ss, m in zip(sp, masks)]
            sfx = [_tri_sum(ss, after) for ss in sp]
            probs = []
            for ls, sx, ss, m, r0 in zip(logsig, sfx, sp, masks, r0s):
                a = jnp.exp(ls - _over_keys(_from(carry, r0), tk) - sx)
                probs.append((a if m is None else jnp.where(m, a, 0.0)).astype(BF16))
                carry = _add_from(carry, r0, _rowsum(ss))
            for a, st, r0 in zip(probs, starts, r0s):
                o = _add_from(o, r0, jnp.dot(a, v_ref[pl.ds(st, tk), :], preferred_element_type=F32))
            return carry, o

        carry, o = jnp.zeros((tq, HEAD), F32), jnp.zeros((tq, HEAD), F32)
        for b in reversed(range(tq // tb)):
            carry, o = block(i * tq + b * tb, carry, o, b * tb)
        below = i * (tq // tb)
        carry, o = lax.fori_loop(0, below, lambda jj, co: block((below - 1 - jj) * tb, co[0], co[1], None),
                                 (carry, o))
        o_ref[...] = o.astype(o_ref.dtype)
        tot_ref[...] = carry
        host.stage("mid", comm_refs)
        host.stage("last", comm_refs)

    t = tq
    full = lambda off: pl.BlockSpec((s, HEAD), lambda h, i: (0, off + h))
    tile_spec = lambda off: pl.BlockSpec((t, HEAD), lambda h, i: (i, off + h))
    out = lambda dt: jax.ShapeDtypeStruct((s, n_heads * HEAD), dt)
    return pl.pallas_call(
        body, name=name, grid=(n_heads, s // t),
        in_specs=[tile_spec(qoff), full(koff), full(voff)] + host.in_specs,
        out_specs=[tile_spec(0), tile_spec(0)] + host.out_specs,
        out_shape=[out(BF16), out(F32)] + host.out_shape,
        scratch_shapes=host.scratch,
        compiler_params=_params(("parallel" if comm is None else "arbitrary", "arbitrary"), 40),
    )(qkv, qkv, qkv, *host.ins)


def _sb_bwd(qkv, qoff, koff, voff, do, tot, n_heads, name, comm=None):
    s = qkv.shape[0]
    tq, tb, tk = min(SB_TQ, s), SB_TB, SB_TK
    scale = HEAD ** -0.5
    host = _Host(comm, (n_heads, s // tq))

    def body(*refs):
        q_ref, k_ref, v_ref, do_ref, tot_ref = refs[:5]
        comm_refs, ((dq_ref, dk_ref, dv_ref), _) = host.split(refs, 5, 3)
        host.stage("first", comm_refs)
        i = pl.program_id(1)

        @pl.when(i == 0)
        def _():
            dk_ref[...] = jnp.zeros_like(dk_ref)
            dv_ref[...] = jnp.zeros_like(dv_ref)

        q = q_ref[...]
        do_b = do_ref[...].astype(BF16)
        total = tot_ref[...]
        upto = _tri(tk, lambda a, b: a <= b)
        before = _tri(tk, lambda a, b: a < b)[:tk]
        def block(base, lsum, psum, dq, diag_off):
            halves = list(range(tb // tk))
            starts = [pl.multiple_of(base + h * tk, tk) for h in halves]
            r0s = [0 if diag_off is None else diag_off + h * tk for h in halves]
            masks = [None if diag_off is None else _causal(tq - r0, tk) for r0 in r0s]
            keep = lambda x, m: x if m is None else jnp.where(m, x, 0.0)
            ks = [k_ref[pl.ds(st, tk), :] for st in starts]
            z = [lax.dot_general(_from(q, r0),kj, NT_DIMS, preferred_element_type=F32) * scale
                 for kj, r0 in zip(ks, r0s)]
            da = [lax.dot_general(_from(do_b, r0), v_ref[pl.ds(st, tk), :], NT_DIMS, preferred_element_type=F32)
                  for st, r0 in zip(starts, r0s)]
            sp = [_softplus(zz) for zz in z]
            logsig = [zz - ss for zz, ss in zip(z, sp)]
            sp = [keep(ss, m) for ss, m in zip(sp, masks)]
            pre = [_tri_sum(ss, upto) for ss in sp]
            probs, p = [], []
            for ls, px, ss, m, dd, r0 in zip(logsig, pre, sp, masks, da, r0s):
                a = keep(jnp.exp(ls - (_over_keys(_from(total, r0) - _from(lsum, r0), tk) - px)), m)
                probs.append(a.astype(BF16))
                p.append(a * dd)
                lsum = _add_from(lsum, r0, _rowsum(ss))
            cs = [jnp.dot(pp.astype(BF16), before, preferred_element_type=F32) for pp in p]
            dzs = []
            for ls, pp, cc, m, r0 in zip(logsig, p, cs, masks, r0s):
                c_all = _over_keys(_from(psum, r0), tk) + cc
                dzs.append(keep((pp - (pp + c_all) * jnp.exp(ls)) * scale, m).astype(BF16))
                psum = _add_from(psum, r0, _rowsum(pp))
            for dz_b, kj, r0 in zip(dzs, ks, r0s):
                dq = _add_from(dq, r0, jnp.dot(dz_b, kj, preferred_element_type=F32))
            for dz_b, a, st, r0 in zip(dzs, probs, starts, r0s):
                dk_ref[pl.ds(st, tk), :] += lax.dot_general(dz_b, _from(q, r0), TN_DIMS, preferred_element_type=F32)
                dv_ref[pl.ds(st, tk), :] += lax.dot_general(a, _from(do_b, r0), TN_DIMS,
                                                            preferred_element_type=F32)
            return lsum, psum, dq

        zero = jnp.zeros((tq, HEAD), F32)
        state = lax.fori_loop(0, i * (tq // tb), lambda j, c: block(j * tb, c[0], c[1], c[2], None),
                              (zero, zero, jnp.zeros((tq, HEAD), F32)))
        for b in range(tq // tb):
            state = block(i * tq + b * tb, *state, b * tb)
        dq_ref[...] = state[2].astype(dq_ref.dtype)
        host.stage("mid", comm_refs)
        host.stage("last", comm_refs)

    t = tq
    full = lambda off: pl.BlockSpec((s, HEAD), lambda h, i: (0, off + h))
    tile_spec = lambda off: pl.BlockSpec((t, HEAD), lambda h, i: (i, off + h))
    w = n_heads * HEAD
    return pl.pallas_call(
        body, name=name, grid=(n_heads, s // t),
        in_specs=[tile_spec(qoff), full(koff), full(voff), tile_spec(0), tile_spec(0)] + host.in_specs,
        out_specs=[tile_spec(0), full(0), full(0)] + host.out_specs,
        out_shape=[jax.ShapeDtypeStruct((s, w), BF16), jax.ShapeDtypeStruct((s, w), F32),
                   jax.ShapeDtypeStruct((s, w), F32)] + host.out_shape,
        scratch_shapes=host.scratch,
        compiler_params=_params(("parallel" if comm is None else "arbitrary", "arbitrary"), 48),
    )(qkv, qkv, qkv, do, tot, *host.ins)


def _coords():
    return lax.axis_index("x"), lax.axis_index("y"), lax.axis_index("c")


def _gather_plan(shards):
    n = len(shards)

    def run(stage, ins, outs, sems):
        send_sems, recv_sems, local_sems = sems
        x, y, c = _coords()
        me, sibling = (x, y, c), (x, y, 1 - c)
        chips = [(1 - x, y), (x, 1 - y), (1 - x, 1 - y)]

        def copy(w, k, block, to, src=None):
            dst = outs[w].at[4 * block[0] + 2 * block[1] + block[2]]
            return pltpu.make_async_remote_copy(
                src_ref=dst if src is None else src, dst_ref=dst,
                send_sem=send_sems.at[7 * w + k], recv_sem=recv_sems.at[7 * w + k],
                device_id=to, device_id_type=MESH)

        def mine():
            return [pltpu.make_async_copy(ins[w], outs[w].at[4 * x + 2 * y + c], local_sems.at[w]) for w in range(n)]

        def first():
            cps = []
            for w in range(n):
                cps.append(copy(w, 0, me, sibling, src=ins[w]))
                cps += [copy(w, 1 + j, me, (*chip, c), src=ins[w]) for j, chip in enumerate(chips)]
            return cps

        def passed():
            return [copy(w, 4 + j, (*chip, c), sibling) for w in range(n) for j, chip in enumerate(chips)]

        if stage == "first":
            for cp in mine() + first():
                cp.start()
        elif stage == "mid":
            onward = passed()
            for w in range(n):
                for j, chip in enumerate(chips):
                    copy(w, 1 + j, (*chip, c), me).wait_recv()
                    onward[3 * w + j].start()
        else:
            for w in range(n):
                copy(w, 0, sibling, me).wait_recv()
                for j, chip in enumerate(chips):
                    copy(w, 4 + j, (*chip, 1 - c), me).wait_recv()
            for cp in first() + passed():
                cp.wait_send()
            for cp in mine():
                cp.wait()

    return dict(
        ins=list(shards), run=run, stages=("first", "mid", "last"),
        out_shape=[jax.ShapeDtypeStruct((N_DEV,) + a.shape, a.dtype) for a in shards],
        scratch=[pltpu.SemaphoreType.DMA((7 * n,)), pltpu.SemaphoreType.DMA((7 * n,)), pltpu.SemaphoreType.DMA((n,))])


def _scatter_plan(grads, only_x=None):
    n = len(grads)

    def run(stage, ins, outs, sems):
        send_sems, recv_sems = sems
        x, y, c = _coords()
        for w in range(n):
            for mask in range(1, N_DEV):
                px, py, pc = x ^ (mask >> 2), y ^ ((mask >> 1) & 1), c ^ (mask & 1)
                k = 7 * w + mask - 1
                block = 4 * px + 2 * py + pc if only_x is None else 2 * py + pc
                cp = pltpu.make_async_remote_copy(
                    src_ref=ins[w].at[block], dst_ref=outs[w].at[mask - 1],
                    send_sem=send_sems.at[k], recv_sem=recv_sems.at[k],
                    device_id=(px, py, pc), device_id_type=MESH)
                if only_x is None:
                    cp.start() if stage == "first" else cp.wait()
                elif stage == "first":
                    pl.when(px == only_x)(cp.start)
                else:
                    pl.when(px == only_x)(cp.wait_send)
                    pl.when(x == only_x)(cp.wait_recv)

    return dict(
        ins=list(grads), run=run, stages=("first", "last"),
        out_shape=[jax.ShapeDtypeStruct((N_DEV - 1,) + a.shape[1:], a.dtype) for a in grads],
        scratch=[pltpu.SemaphoreType.DMA((7 * n,)), pltpu.SemaphoreType.DMA((7 * n,))])


def _run_plan(plan, name):
    n_in, n_out = len(plan["ins"]), len(plan["out_shape"])

    def body(*refs):
        for stage in plan["stages"]:
            plan["run"](stage, refs[:n_in], refs[n_in:n_in + n_out], refs[n_in + n_out:])

    any_spec = pl.BlockSpec(memory_space=pl.ANY)
    return pl.pallas_call(
        body, name=name, in_specs=[any_spec] * n_in, out_specs=[any_spec] * n_out,
        out_shape=plan["out_shape"], scratch_shapes=plan["scratch"],
    )(*plan["ins"])


def _gather_rows(v):
    rows, width = v.shape

    def body(v_ref, out_ref, send_sems, recv_sems):
        x, y, c = _coords()
        out_ref[pl.ds(pl.multiple_of((4 * x + 2 * y + c) * rows, rows), rows), :] = v_ref[...]
        cps = []
        for mask in range(1, N_DEV):
            peer = (x ^ (mask >> 2), y ^ ((mask >> 1) & 1), c ^ (mask & 1))
            dst = out_ref.at[pl.ds(pl.multiple_of((4 * x + 2 * y + c) * rows, rows), rows), :]
            cps.append(pltpu.make_async_remote_copy(
                src_ref=v_ref, dst_ref=dst, send_sem=send_sems.at[mask - 1], recv_sem=recv_sems.at[mask - 1],
                device_id=peer, device_id_type=MESH))
        for cp in cps:
            cp.start()
        for cp in cps:
            cp.wait()

    vmem = pl.BlockSpec(memory_space=pltpu.VMEM)
    return pl.pallas_call(
        body, name="gather_small",
        in_specs=[vmem], out_specs=vmem,
        out_shape=jax.ShapeDtypeStruct((N_DEV * rows, width), F32),
        scratch_shapes=[pltpu.SemaphoreType.DMA((N_DEV - 1,)), pltpu.SemaphoreType.DMA((N_DEV - 1,))],
    )(v)


def _adamw(w, g, m, v):
    m = ADAM_B1 * m + (1.0 - ADAM_B1) * g
    v = ADAM_B2 * v + (1.0 - ADAM_B2) * jnp.square(g)
    m_hat = m / (1.0 - ADAM_B1 ** ADAM_STEP)
    v_hat = v / (1.0 - ADAM_B2 ** ADAM_STEP)
    delta = -ADAM_LR * (m_hat / (jnp.sqrt(v_hat) + ADAM_EPS) + ADAM_WD * w)
    return delta, m, v


def _tile_rows(rows, cols):
    tm = 1 << int(math.log2(max(2 * SUBLANES, (1 << 18) // cols)))
    while rows % tm:
        tm //= 2
    assert tm >= 2 * SUBLANES, (rows, cols)
    return tm


def _reduce_adam(w, m, v, own, got, name):
    r, c = w.shape
    tm = max(2 * SUBLANES, _tile_rows(r, c) // 2)

    def fn(wv, mv, vv, a, *peers):
        g = a
        for pv in peers:
            g = g + pv.astype(F32)
        return (g,) + _adamw(wv, g, mv, vv)

    blk = lambda a: (a, (tm, c), lambda i: (i, 0))
    got_blk = lambda j: (got, (None, tm, c), lambda i, j=j: (j, i, 0))
    return _ew(fn, [blk(w), blk(m), blk(v), blk(own)] + [got_blk(j) for j in range(N_DEV - 1)],
               [((r, c), F32, (tm, c), lambda i: (i, 0))] * 4, (r // tm,), name)


def _small_adam(gathered, params, moms, vels, widths):
    n = len(params)
    total = gathered.shape[1]

    def body(*refs):
        g_ref = refs[0]
        p_refs, m_refs, v_refs = refs[1:1 + n], refs[1 + n:1 + 2 * n], refs[1 + 2 * n:1 + 3 * n]
        sum_ref = refs[1 + 3 * n]
        outs = refs[2 + 3 * n:]
        g = g_ref[0:1, :]
        for p in range(1, N_DEV):
            g = g + g_ref[p * SUBLANES:p * SUBLANES + 1, :]
        sum_ref[...] = g
        off = 0
        for i, wd in enumerate(widths):
            d, m2, v2 = _adamw(p_refs[i][...], g[:, off:off + wd], m_refs[i][...], v_refs[i][...])
            outs[3 * i][...] = d
            outs[3 * i + 1][...] = m2
            outs[3 * i + 2][...] = v2
            off += wd

    vmem = pl.BlockSpec(memory_space=pltpu.VMEM)
    out_shape = [jax.ShapeDtypeStruct((1, total), F32)]
    for wd in widths:
        out_shape += [jax.ShapeDtypeStruct((1, wd), F32)] * 3
    return pl.pallas_call(
        body, name="small_adam",
        in_specs=[vmem] * (1 + 3 * n), out_specs=[vmem] * len(out_shape), out_shape=out_shape,
    )(gathered, *params, *moms, *vels)


def _cast_bf16(a, name):
    r, c = a.shape
    tm = _tile_rows(r, c)
    return _ew(lambda v: v, [(a, (tm, c), lambda i: (i, 0))], [((r, c), BF16, (tm, c), lambda i: (i, 0))],
               (r // tm,), name)[0]


def _fold_loss(parts, name):
    r, c = parts.shape

    def fn(v):
        return jnp.broadcast_to(jnp.sum(jnp.sum(v, axis=0, keepdims=True), axis=1, keepdims=True), (SUBLANES, HEAD))

    return _ew(fn, [_const_spec(parts)], [((SUBLANES, HEAD), F32, (SUBLANES, HEAD), lambda i: (0, 0))], (1,),
               name)[0][0:1]


def kernel(x, p, g_mix, w_in, qn_gain, kn_gain, w_branch_a, w_branch_b, w_out, g_mlp, w_up, w_down, g_ple, w_ple_gate, w_ple_proj, loss_target, m_g_mix, m_w_in, m_qn_gain, m_kn_gain, m_w_branch_a, m_w_branch_b, m_w_out, m_g_mlp, m_w_up, m_w_down, m_g_ple, m_w_ple_gate, m_w_ple_proj, v_g_mix, v_w_in, v_qn_gain, v_kn_gain, v_w_branch_a, v_w_branch_b, v_w_out, v_g_mlp, v_w_up, v_w_down, v_g_ple, v_w_ple_gate, v_w_ple_proj):
    x2 = x[0]
    tgt = loss_target[0]
    s, d = x2.shape
    wd_ = w_branch_a.shape[1]
    nh = wd_ // HEAD
    dff = w_up.shape[1]
    qkv_w = 6 * wd_
    tiles = lambda cols: cols // HEAD

    big = [w_in[0], w_branch_a[0], w_branch_b[0], w_out[0], w_up[0], w_down[0], w_ple_gate[0], w_ple_proj[0]]
    names = ["w_in", "w_branch_a", "w_branch_b", "w_out", "w_up", "w_down", "w_ple_gate", "w_ple_proj"]
    row_sharded = [False, False, False, True, False, True, True, False]
    shards = [_cast_bf16(a, "cast_" + nm) for a, nm in zip(big, names)]
    as_weight = lambda g, rs: g.reshape((1, N_DEV * g.shape[1], g.shape[2])) if rs else g
    as_blocks = lambda g, rs: g.reshape((N_DEV, g.shape[1] // N_DEV, g.shape[2])) if rs else g
    win = _run_plan(_gather_plan(shards[:1]), "all_gather_w_in")[0]

    tm = 1024 if s % 1024 == 0 else s
    tm_in = tk_s = 2048 if s % 2048 == 0 else tm
    tn_of = lambda n: 512 if n % 512 == 0 else (256 if n % 256 == 0 else n)
    tn_in = 256 if win.shape[2] % 256 == 0 else HEAD

    h = _rms_fwd(x2, g_mix, "norm_mix")
    qk_raw = _mm("nn", h, win, tm=tm_in, tn=tn_in, tk=d, out_dtypes=[F32], name="proj_qk",
                 n_off=0, n_cnt=2 * wd_ // tn_in)[0]
    rest = _mm("nn", h, win, tm=4096 if s % 4096 == 0 else tm_in, tn=tn_in, tk=d, out_dtypes=[BF16], name="proj_rest",
               n_off=2 * wd_ // tn_in, n_cnt=(win.shape[0] * win.shape[2] - 2 * wd_) // tn_in)[0]
    o_va, o_qb, o_kb, o_vb, o_ga, o_gb = 0, tiles(wd_), tiles(2 * wd_), tiles(3 * wd_), tiles(4 * wd_), tiles(4 * wd_ + d)
    tabs = _rope_tables(s)
    qa = _headnorm_rope(qk_raw, 0, qn_gain, tabs, nh, "rope_q")
    ka = _headnorm_rope(qk_raw, 1, kn_gain, tabs, nh, "rope_k")
    va = rest[:, :wd_]

    outs, lses = [], []
    for dil in DILATIONS:
        o_g, l_g = _dilated_fwd(_phase_major(qa, dil), _phase_major(ka, dil), _phase_major(va, dil), 0, nh,
                                f"dilated_fwd_{dil}")
        outs.append(_token_major(o_g))
        lses.append(_token_major(l_g))
    ya, lse_all = _mix_fwd(outs, lses, "mix_fwd")
    yb, sb_tot, *others = _sb_fwd(rest, o_qb, o_kb, o_vb, nh, "sb_fwd", comm=_gather_plan(shards[1:]))
    wba, wbb, wout, wup, wdown, wgate, wproj = [as_weight(g, rs) for g, rs in zip(others, row_sharded[1:])]

    tn_d = tn_of(wba.shape[2])
    za = _mm("nn", ya, wba, tm=tm, tn=tn_d, tk=wd_, out_dtypes=[BF16], name="branch_a")[0]

    def merge(acc, zav, gav, gbv):
        return _sigmoid(gav.astype(F32)) * zav.astype(F32) + _sigmoid(gbv.astype(F32)) * acc, acc

    merged, zb = _mm("nn", yb, wbb, tm=tm, tn=tn_d, tk=wd_, out_dtypes=[BF16, BF16], name="branch_b_merge",
                     epilogue=merge, extras=[(za, 0), (rest, o_ga * HEAD // tn_d), (rest, o_gb * HEAD // tn_d)])
    x1 = _mm("nn", merged, wout, tm=tm, tn=512, tk=d, out_dtypes=[F32], name="out_proj",
             epilogue=lambda acc, xv: (acc + xv,), extras=[(x2, 0)])[0]

    hm = _rms_fwd(x1, g_mlp, "norm_mlp")
    tn_u = min(wup.shape[2], 1024)
    u, act = _mm("nn", hm, wup, tm=tm, tn=tn_u, tk=d, out_dtypes=[BF16, BF16], name="mlp_up",
                 epilogue=lambda acc: (acc, jnp.square(jnp.maximum(acc, 0.0))))
    x3 = _mm("nn", act, wdown, tm=tm, tn=1024, tk=min(dff, 2048), out_dtypes=[F32], name="mlp_down",
             epilogue=lambda acc, xv: (acc + xv,), extras=[(x1, 0)])[0]

    hp = _rms_fwd(x3, g_ple, "norm_ple")
    p_b = _cast_bf16(p[0, 0], "cast_p")
    pp = _mm("nn", p_b, wproj, tm=tm, tn=tn_of(wproj.shape[2]), tk=p_b.shape[1], out_dtypes=[BF16],
             name="ple_proj")[0]

    def head(acc, ppv, xv, tv):
        sg = _sigmoid(acc)
        ppf = ppv.astype(F32)
        err = xv + ppf * sg - tv
        dy = err / d
        sq = jnp.square(err)
        return dy, dy * sg, dy * ppf * sg * (1.0 - sg), sq.reshape(-1, SUBLANES, sq.shape[-1]).sum(axis=0)

    n_i = s // tm
    dy, d_pp, d_gt, sq_parts = _mm(
        "nn", hp, wgate, tm=tm, tn=512, tk=d, out_dtypes=[F32, BF16, BF16], name="ple_gate_loss", epilogue=head,
        extras=[(pp, 0), (x3, 0), (tgt, 0)],
        extra_outs=[((n_i * SUBLANES, d), F32, (SUBLANES, 512), lambda i, j: (i, j))])
    loss_vec = _fold_loss(sq_parts, "loss_fold") * 0.5 / d

    both = [F32, BF16]
    g_wproj = _mm("tn", p_b, d_pp, tm=p_b.shape[1], tn=tn_of(wproj.shape[2]), tk=tk_s, out_dtypes=both,
                  name="grad_w_ple_proj", out_nb=N_DEV)
    g_wgate = _mm("tn", hp, d_gt, tm=1024, tn=1024, tk=tk_s, out_dtypes=both, name="grad_w_ple_gate")
    d_hp = _mm("nt", d_gt, wgate, tm=tm, tn=512, tk=d, out_dtypes=[F32], name="d_hp")[0]
    dx3, dx3_b, g_gple = _rms_bwd(d_hp, x3, g_ple, dy, "norm_ple_bwd")

    d_u = _mm("nt", dx3_b, wdown, tm=tm, tn=1024, tk=d, out_dtypes=[BF16], name="d_u",
              epilogue=lambda acc, uv: (acc * (2.0 * jnp.maximum(uv.astype(F32), 0.0)),), extras=[(u, 0)])[0]
    g_wdown = _mm("tn", act, dx3_b, tm=1024, tn=1024, tk=tk_s, out_dtypes=both, name="grad_w_down")
    g_wup = _mm("tn", hm, d_u, tm=1024, tn=wup.shape[2], tk=tk_s, out_dtypes=both, name="grad_w_up", out_nb=N_DEV)
    d_hm = _mm("nt", d_u, wup, tm=tm, tn=min(d, 2048), tk=wup.shape[2], out_dtypes=[F32], name="d_hm")[0]
    dx1, dx1_b, g_gmlp = _rms_bwd(d_hm, x1, g_mlp, dx3, "norm_mlp_bwd")

    def unmerge(acc, gav, gbv, zav, zbv):
        sa, sb = _sigmoid(gav.astype(F32)), _sigmoid(gbv.astype(F32))
        return acc * sa, acc * sb, acc * zav.astype(F32) * sa * (1.0 - sa), acc * zbv.astype(F32) * sb * (1.0 - sb)

    d_za, d_zb, d_ga, d_gb = _mm(
        "nt", dx1_b, wout, tm=tm, tn=512, tk=d, out_dtypes=[BF16] * 4, name="d_merged", epilogue=unmerge,
        extras=[(rest, o_ga * HEAD // 512), (rest, o_gb * HEAD // 512), (za, 0), (zb, 0)])
    g_wout = _mm("tn", merged, dx1_b, tm=1024, tn=1024, tk=tk_s, out_dtypes=both, name="grad_w_out")
    g_wba = _mm("tn", ya, d_za, tm=wd_, tn=wba.shape[2], tk=tk_s, out_dtypes=both, name="grad_w_branch_a",
                out_nb=N_DEV)
    g_wbb = _mm("tn", yb, d_zb, tm=wd_, tn=wbb.shape[2], tk=tk_s, out_dtypes=both, name="grad_w_branch_b",
                out_nb=N_DEV)
    d_ya = _mm("nt", d_za, wba, tm=tm, tn=wd_, tk=wba.shape[2], out_dtypes=[F32], name="d_ya")[0]
    d_yb = _mm("nt", d_zb, wbb, tm=tm, tn=wd_, tk=wbb.shape[2], out_dtypes=[F32], name="d_yb")[0]

    grads = [None, g_wba, g_wbb, g_wout, g_wup, g_wdown, g_wgate, g_wproj]
    early = _scatter_plan([as_blocks(g[1], rs) for g, rs in zip(grads[1:], row_sharded[1:])])
    d_qb, d_kb, d_vb, *got = _sb_bwd(rest, o_qb, o_kb, o_vb, d_yb, sb_tot, nh, "sb_bwd", comm=early)
    d_ya_b, stats = _mix_stats(d_ya, ya, lse_all, nh, "mix_stats")
    dqs, dks, dvs = [], [], []
    for dil in DILATIONS:
        dq_g, dk_g, dv_g = _dilated_bwd(
            _phase_major(qa, dil), _phase_major(ka, dil), _phase_major(va, dil), 0, _phase_major(d_ya_b, dil),
            _phase_major(stats, dil), nh, f"dilated_bwd_{dil}")
        dqs.append(_token_major(dq_g))
        dks.append(_token_major(dk_g))
        dvs.append(_token_major(dv_g))
    d_qa, g_qn = _headnorm_rope_bwd(dqs, qk_raw, 0, qn_gain, tabs, nh, "rope_q_bwd")
    d_ka, g_kn = _headnorm_rope_bwd(dks, qk_raw, 1, kn_gain, tabs, nh, "rope_k_bwd")
    tmr = 256
    d_va = _ew(lambda a, b, c: a + b + c, [_row_spec(a, tmr) for a in dvs],
               [((s, wd_), BF16, (tmr, wd_), lambda i: (i, 0))], (s // tmr,), "sum_dv")[0]
    d_proj = jnp.concatenate([d_qa, d_ka, d_va, d_qb, d_kb.astype(BF16), d_vb.astype(BF16), d_ga, d_gb], axis=1)

    half = N_DEV // 2
    gw = lambda nm, off, comm: _mm("tn", h, d_proj, tm=1024, tn=win.shape[2], tk=tk_s, out_dtypes=both, name=nm,
                                   out_nb=half, n_off=off, n_cnt=half, comm=comm)
    g_lo = gw("grad_w_in_lo", 0, None)
    g_hi_f, g_hi_b, got_lo = gw("grad_w_in_hi", half, _scatter_plan([g_lo[1]], only_x=0))
    d_h, got_hi = _mm("nt", d_proj, win, tm=tm, tn=min(d, 2048), tk=win.shape[2], out_dtypes=[F32], name="d_h",
                      comm=_scatter_plan([g_hi_b], only_x=1))
    dx, _, g_gmix = _rms_bwd(d_h, x2, g_mix, dx1, "norm_mix_bwd")

    cx, cy, cc = _coords()
    me = 4 * cx + 2 * cy + cc
    moms = [m_w_in, m_w_branch_a, m_w_branch_b, m_w_out, m_w_up, m_w_down, m_w_ple_gate, m_w_ple_proj]
    vels = [v_w_in, v_w_branch_a, v_w_branch_b, v_w_out, v_w_up, v_w_down, v_w_ple_gate, v_w_ple_proj]
    mine = lambda g: lax.dynamic_index_in_dim(g, 2 * cy + cc, axis=0, keepdims=False)
    owns = [jnp.where(cx == 0, mine(g_lo[0]), mine(g_hi_f))]
    owns += [lax.dynamic_index_in_dim(as_blocks(g[0], rs), me, axis=0, keepdims=False)
             for g, rs in zip(grads[1:], row_sharded[1:])]
    got = [jnp.where(cx == 0, got_lo, got_hi)] + got
    big_out = {}
    for i, nm in enumerate(names):
        big_out[nm] = [a[None] for a in _reduce_adam(big[i], moms[i][0], vels[i][0], owns[i], got[i], "adam_" + nm)]

    small_names = ["g_mix", "qn_gain", "kn_gain", "g_mlp", "g_ple"]
    small_p = [g_mix, qn_gain, kn_gain, g_mlp, g_ple]
    small_m = [m_g_mix, m_qn_gain, m_kn_gain, m_g_mlp, m_g_ple]
    small_v = [v_g_mix, v_qn_gain, v_kn_gain, v_g_mlp, v_g_ple]
    small_g = [g_gmix, g_qn, g_kn, g_gmlp, g_gple]
    widths = [a.shape[1] for a in small_p]
    vec = jnp.concatenate(small_g + [loss_vec], axis=1)
    vec = jnp.pad(vec, ((0, SUBLANES - 1), (0, 0)))
    res = _small_adam(_gather_rows(vec), small_p, small_m, small_v, widths)
    summed = res[0]
    small_out, off = {}, 0
    for i, nm in enumerate(small_names):
        small_out[nm] = [summed[:, off:off + widths[i]]] + list(res[1 + 3 * i:4 + 3 * i])
        off += widths[i]
    loss = summed[0, off]

    order = ["g_mix", "w_in", "qn_gain", "kn_gain", "w_branch_a", "w_branch_b", "w_out", "g_mlp", "w_up", "w_down",
             "g_ple", "w_ple_gate", "w_ple_proj"]
    table = {**big_out, **small_out}
    result = [loss, dx[None]]
    for kind in range(4):
        result += [table[nm][kind] for nm in order]
    return tuple(result)
```

```python
import functools
import math

import jax
import jax.numpy as jnp
from jax import lax
from jax.experimental import pallas as pl
from jax.experimental.pallas import tpu as pltpu

F32 = jnp.float32
BF16 = jnp.bfloat16
MESH = pl.DeviceIdType.MESH

HEAD = 128
WINDOW = 128
DILATIONS = (1, 4, 16)
ROT = HEAD // 4
ROPE_THETA = 500000.0
EPS = 1e-6
NEG = -1e30
N_DEV = 8

ADAM_LR = 0.001
ADAM_B1 = 0.9
ADAM_B2 = 0.999
ADAM_EPS = 1e-08
ADAM_WD = 0.01
ADAM_STEP = 10

SUBLANES = 8
VMEM_CAP_MB = 56

NT_DIMS = (((1,), (1,)), ((), ()))
TN_DIMS = (((0,), (0,)), ((), ()))


def _params(semantics, vmem_mb):
    return pltpu.CompilerParams(dimension_semantics=semantics, vmem_limit_bytes=min(vmem_mb, VMEM_CAP_MB) << 20)


def _sigmoid(x):
    return 0.5 + 0.5 * jnp.tanh(0.5 * x)


def _split_bf16(x):
    hi = x.astype(BF16)
    lo = (x - hi.astype(F32)).astype(BF16)
    return hi, lo


class _Host:
    def __init__(self, comm, grid):
        self.comm, self.grid = comm, grid
        any_spec = pl.BlockSpec(memory_space=pl.ANY)
        self.ins = list(comm["ins"]) if comm else []
        self.out_shape = list(comm["out_shape"]) if comm else []
        self.scratch = list(comm["scratch"]) if comm else []
        self.in_specs = [any_spec] * len(self.ins)
        self.out_specs = [any_spec] * len(self.out_shape)

    def split(self, refs, n_in, n_out):
        pos = n_in
        c_in = refs[pos:pos + len(self.ins)]
        pos += len(self.ins)
        outs = refs[pos:pos + n_out]
        pos += n_out
        c_out = refs[pos:pos + len(self.out_shape)]
        pos += len(self.out_shape)
        own = len(refs) - pos - len(self.scratch)
        return (c_in, c_out, refs[pos + own:]), (outs, refs[pos:pos + own])

    def stage(self, which, comm_refs):
        if self.comm is None or which not in self.comm["stages"]:
            return
        grid = self.grid
        at = {"first": [0] * len(grid), "mid": [grid[0] // 2] + [0] * (len(grid) - 1),
              "last": [g - 1 for g in grid]}[which]
        cond = functools.reduce(jnp.logical_and, [pl.program_id(ax) == v for ax, v in enumerate(at)])

        @pl.when(cond)
        def _():
            self.comm["run"](which, *comm_refs)


def _ew(fn, ins, outs, grid, name, colsums=(), vmem_mb=40):
    n_in, n_out, n_cs = len(ins), len(outs), len(colsums)
    steps = math.prod(grid)

    def body(*refs):
        in_refs = refs[:n_in]
        out_refs = refs[n_in:n_in + n_out]
        cs_refs = refs[n_in + n_out:n_in + n_out + n_cs]
        acc_refs = refs[n_in + n_out + n_cs:]
        vals = fn(*[r[...] for r in in_refs])
        if not isinstance(vals, tuple):
            vals = (vals,)
        for r, v in zip(out_refs, vals[:n_out]):
            r[...] = v.astype(r.dtype)
        if n_cs:
            step = pl.program_id(0)
            for ax in range(1, len(grid)):
                step = step * grid[ax] + pl.program_id(ax)
            for acc, cs, v in zip(acc_refs, cs_refs, vals[n_out:]):
                part = v.reshape(-1, SUBLANES, v.shape[-1]).sum(axis=0)

                @pl.when(step == 0)
                def _(acc=acc, part=part):
                    acc[...] = part

                @pl.when(step > 0)
                def _(acc=acc, part=part):
                    acc[...] += part

                @pl.when(step == steps - 1)
                def _(acc=acc, cs=cs):
                    cs[...] = acc[...].sum(axis=0, keepdims=True)

    out_shape = [jax.ShapeDtypeStruct(s, d) for s, d, _, _ in outs]
    out_specs = [pl.BlockSpec(b, m) for _, _, b, m in outs]
    for w in colsums:
        out_shape.append(jax.ShapeDtypeStruct((1, w), F32))
        out_specs.append(pl.BlockSpec((1, w), lambda *_: (0, 0)))
    sem = ("arbitrary",) * len(grid) if n_cs else ("parallel",) * len(grid)
    res = pl.pallas_call(
        body, name=name, grid=grid,
        in_specs=[pl.BlockSpec(b, m) for _, b, m in ins],
        out_specs=out_specs, out_shape=out_shape,
        scratch_shapes=[pltpu.VMEM((SUBLANES, w), F32) for w in colsums],
        compiler_params=_params(sem, vmem_mb),
    )(*[a for a, _, _ in ins])
    return res


def _row_spec(a, tm):
    return (a, (tm, a.shape[1]), lambda i: (i, 0))


def _const_spec(a):
    return (a, a.shape, lambda *_: (0,) * a.ndim)


def _mm(mode, a, b, *, tm, tn, tk, out_dtypes, name, epilogue=None, extras=(), n_off=0, n_cnt=None, n_step=1,
        out_nb=1, extra_outs=(), vmem_mb=52, comm=None):
    if mode == "nn":
        m, kdim = a.shape
        nb, _, n = b.shape
        npb = n // tn
        ncols = nb * n
        n_tiles = (ncols // tn) if n_cnt is None else n_cnt
        a_spec = pl.BlockSpec((tm, tk), lambda i, j, k: (i, k))
        b_spec = pl.BlockSpec((None, tk, tn), lambda i, j, k: ((j + n_off) // npb, k, (j + n_off) % npb))
        dims = (((1,), (0,)), ((), ()))
    elif mode == "nt":
        m, kdim = a.shape
        nb, nout, n = b.shape
        kpb = n // tk
        n_tiles = nout // tn
        a_spec = pl.BlockSpec((tm, tk), lambda i, j, k: (i, k))
        b_spec = pl.BlockSpec((None, tn, tk), lambda i, j, k: (k // kpb, j, k % kpb))
        dims = NT_DIMS
    else:
        kdim, m = a.shape
        n_tiles = (b.shape[1] // tn) if n_cnt is None else n_cnt
        ncols = n_tiles * tn
        a_spec = pl.BlockSpec((tk, tm), lambda i, j, k: (k, i))
        b_spec = pl.BlockSpec((tk, tn), lambda i, j, k: (k, j * n_step + n_off))
        dims = TN_DIMS
    nk = kdim // tk
    assert kdim % tk == 0 and m % tm == 0
    grid = (m // tm, n_tiles, nk)
    n_ex, n_out = len(extras), len(out_dtypes) + len(extra_outs)
    host = _Host(comm, grid)

    def body(*refs):
        a_ref, b_ref = refs[0], refs[1]
        ex_refs = refs[2:2 + n_ex]
        comm_refs, (out_refs, scratch) = host.split(refs, 2 + n_ex, n_out)
        host.stage("first", comm_refs)

        def finish(acc):
            vals = (acc,) * n_out if epilogue is None else epilogue(acc, *[r[...] for r in ex_refs])
            for r, v in zip(out_refs, vals):
                r[...] = v.astype(r.dtype)

        def part():
            return lax.dot_general(a_ref[...], b_ref[...], dims, preferred_element_type=F32)

        if nk == 1:
            finish(part())
        else:
            acc_ref = scratch[0]
            k = pl.program_id(2)

            @pl.when(k == 0)
            def _():
                acc_ref[...] = part()

            @pl.when((k > 0) & (k < nk - 1))
            def _():
                acc_ref[...] += part()

            @pl.when(k == nk - 1)
            def _():
                finish(acc_ref[...] + part())

        host.stage("mid", comm_refs)
        host.stage("last", comm_refs)

    if mode == "tn":
        npo = (ncols // out_nb) // tn
        out_shape = [jax.ShapeDtypeStruct((out_nb, m, ncols // out_nb), d) for d in out_dtypes]
        out_specs = [pl.BlockSpec((None, tm, tn), lambda i, j, k: (j // npo, i, j % npo)) for _ in out_dtypes]
    else:
        out_shape = [jax.ShapeDtypeStruct((m, n_tiles * tn), d) for d in out_dtypes]
        out_specs = [pl.BlockSpec((tm, tn), lambda i, j, k: (i, j)) for _ in out_dtypes]
    for s, d, blk, imap in extra_outs:
        out_shape.append(jax.ShapeDtypeStruct(s, d))
        out_specs.append(pl.BlockSpec(blk, lambda i, j, k, imap=imap: imap(i, j)))
    ex_specs = [pl.BlockSpec((tm, tn), lambda i, j, k, off=off: (i, j + off)) for _, off in extras]
    sem = ("parallel", "parallel", "arbitrary") if comm is None else ("arbitrary",) * 3
    return pl.pallas_call(
        body, name=name, grid=grid,
        in_specs=[a_spec, b_spec] + ex_specs + host.in_specs,
        out_specs=out_specs + host.out_specs, out_shape=out_shape + host.out_shape,
        scratch_shapes=([pltpu.VMEM((tm, tn), F32)] if nk > 1 else []) + host.scratch,
        compiler_params=_params(sem, vmem_mb),
    )(a, b, *[e for e, _ in extras], *host.ins)


def _rms_fwd(x, g, name):
    tm = 256

    def fn(xv, gv):
        r = lax.rsqrt(jnp.mean(xv * xv, axis=-1, keepdims=True) + EPS)
        return xv * r * gv

    return _ew(fn, [_row_spec(x, tm), _const_spec(g)], [(x.shape, BF16, (tm, x.shape[1]), lambda i: (i, 0))],
               (x.shape[0] // tm,), name)[0]


def _rms_bwd(dh, x, g, res, name):
    tm = 256
    d = x.shape[1]

    def fn(dhv, xv, gv, rv):
        r = lax.rsqrt(jnp.mean(xv * xv, axis=-1, keepdims=True) + EPS)
        xh = xv * r
        dyg = dhv * gv
        dx = rv + r * (dyg - xh * jnp.mean(dyg * xh, axis=-1, keepdims=True))
        return dx, dx, dhv * xh

    spec = lambda dt: (x.shape, dt, (tm, d), lambda i: (i, 0))
    return _ew(fn, [_row_spec(dh, tm), _row_spec(x, tm), _const_spec(g), _row_spec(res, tm)],
               [spec(F32), spec(BF16)], (x.shape[0] // tm,), name, colsums=(d,))


def _rope_tables(s):
    half = ROT // 2
    pos = jnp.arange(s, dtype=F32)
    inv = ROPE_THETA ** (-jnp.arange(0, ROT, 2, dtype=F32) / ROT)
    ang = pos[:, None] * inv[None, :]
    cos, sin = jnp.cos(ang), jnp.sin(ang)
    pad = jnp.zeros((s, HEAD - ROT), F32)
    c = jnp.concatenate([cos, cos, pad + 1.0], axis=1)
    a = jnp.concatenate([-sin, jnp.zeros_like(sin), pad], axis=1)
    b = jnp.concatenate([jnp.zeros_like(sin), sin, pad], axis=1)
    return c, a, b


def _heads(x, n_heads):
    return [x[:, h * HEAD:(h + 1) * HEAD] for h in range(n_heads)]


def _headnorm_rope(proj, part, gain, tabs, n_heads, name):
    tm = 512
    s = proj.shape[0]
    w = n_heads * HEAD

    def fn(xs, gv, c, a, b):
        outs = []
        for xv in _heads(xs, n_heads):
            r = lax.rsqrt(jnp.mean(xv * xv, axis=-1, keepdims=True) + EPS)
            y = xv * r * gv
            outs.append(c * y + a * pltpu.roll(y, HEAD - ROT // 2, 1) + b * pltpu.roll(y, ROT // 2, 1))
        return jnp.concatenate(outs, axis=1)

    tab = lambda t: (t, (tm, HEAD), lambda i: (i, 0))
    return _ew(fn, [(proj, (tm, w), lambda i: (i, part)), (gain, (1, HEAD), lambda i: (0, 0))] + [tab(t) for t in tabs],
               [((s, w), BF16, (tm, w), lambda i: (i, 0))], (s // tm,), name)[0]


def _headnorm_rope_bwd(dys, proj, part, gain, tabs, n_heads, name):
    tm = 256
    s = proj.shape[0]
    w = n_heads * HEAD
    n_dy = len(dys)

    def fn(*vals):
        dy_all = vals[0]
        for v in vals[1:n_dy]:
            dy_all = dy_all + v
        xs, gv, c, a, b = vals[n_dy:]
        dxs, dgain = [], None
        for dy, xv in zip(_heads(dy_all, n_heads), _heads(xs, n_heads)):
            dn = c * dy + pltpu.roll(a * dy, ROT // 2, 1) + pltpu.roll(b * dy, HEAD - ROT // 2, 1)
            r = lax.rsqrt(jnp.mean(xv * xv, axis=-1, keepdims=True) + EPS)
            xh = xv * r
            dyg = dn * gv
            dxs.append(r * (dyg - xh * jnp.mean(dyg * xh, axis=-1, keepdims=True)))
            dgain = dn * xh if dgain is None else dgain + dn * xh
        return jnp.concatenate(dxs, axis=1), dgain

    tab = lambda t: (t, (tm, HEAD), lambda i: (i, 0))
    return _ew(fn, [(d, (tm, w), lambda i: (i, 0)) for d in dys]
               + [(proj, (tm, w), lambda i: (i, part)), (gain, (1, HEAD), lambda i: (0, 0))] + [tab(t) for t in tabs],
               [((s, w), BF16, (tm, w), lambda i: (i, 0))], (s // tm,), name, colsums=(HEAD,))


def _phase_major(a, d):
    s, w = a.shape
    if d == 1:
        return a.reshape(1, s, w)
    return a.reshape(s // d, d, w).transpose(1, 0, 2)


def _token_major(a):
    d, m, w = a.shape
    if d == 1:
        return a.reshape(m, w)
    return a.transpose(1, 0, 2).reshape(m * d, w)


def _dil_tq(m):
    return min(1024, m)


def _dil_heads(tq, n_heads):
    return max(1, min(n_heads, 1024 // tq))


def _tile_up(parts, hb, per):
    cols = [jnp.concatenate(parts[j * per:(j + 1) * per], axis=0) for j in range(hb)]
    return cols[0] if hb == 1 else jnp.concatenate(cols, axis=1)


def _dilated_fwd(q, k, v, voff, n_heads, name):
    d, m, _ = q.shape
    tq = _dil_tq(m)
    nq = m // tq
    per = tq // WINDOW
    hb = _dil_heads(tq, n_heads)
    scale = HEAD ** -0.5
    blocks = [(j, b) for j in range(hb) for b in range(per)]
    lanes = [slice(j * HEAD, (j + 1) * HEAD) for j in range(hb)]

    def body(q_ref, kc_ref, kp_ref, vc_ref, vp_ref, o_ref, l_ref):
        n = pl.program_id(2)
        kk = jnp.concatenate([kp_ref[...], kc_ref[...]], axis=0)
        vv = jnp.concatenate([vp_ref[...], vc_ref[...]], axis=0)
        row = lax.broadcasted_iota(jnp.int32, (WINDOW, 2 * WINDOW), 0)
        col = lax.broadcasted_iota(jnp.int32, (WINDOW, 2 * WINDOW), 1)
        band = (col >= row) & (col <= row + WINDOW)
        q = q_ref[...]
        rows = [slice(b * WINDOW, (b + 1) * WINDOW) for b in range(per)]
        keys = [slice(b * WINDOW, (b + 2) * WINDOW) for b in range(per)]
        s = [lax.dot_general(q[rows[b], lanes[j]], kk[keys[b], lanes[j]], NT_DIMS,
                             preferred_element_type=F32) * scale for j, b in blocks]
        es, outs, lses = [], [], []
        for (j, b), sb in zip(blocks, s):
            valid = band if b else band & ((n > 0) | (col >= WINDOW))
            sb = jnp.where(valid, sb, NEG)
            mx = jnp.max(sb, axis=-1, keepdims=True)
            e = jnp.exp(sb - mx)
            den = jnp.sum(e, axis=-1, keepdims=True)
            es.append((e.astype(BF16), den))
            lses.append(jnp.broadcast_to(mx + jnp.log(den), (WINDOW, HEAD)))
        for (j, b), (e, den) in zip(blocks, es):
            outs.append(jnp.dot(e, vv[keys[b], lanes[j]], preferred_element_type=F32) / den)
        o_ref[...] = _tile_up(outs, hb, per)
        l_ref[...] = _tile_up(lses, hb, per)

    assert voff % hb == 0
    cur = lambda off: pl.BlockSpec((None, tq, hb * HEAD), lambda r, h, n: (r, n, off // hb + h))
    prev = lambda off: pl.BlockSpec((None, WINDOW, hb * HEAD),
                                    lambda r, h, n: (r, jnp.maximum(n * per - 1, 0), off // hb + h))
    out = jax.ShapeDtypeStruct((d, m, n_heads * HEAD), F32)
    return pl.pallas_call(
        body, name=name, grid=(d, n_heads // hb, nq),
        in_specs=[cur(0), cur(0), prev(0), cur(voff), prev(voff)],
        out_specs=[cur(0), cur(0)], out_shape=[out, out],
        compiler_params=_params(("parallel",) * 3, 32),
    )(q, k, k, v, v)


def _dilated_bwd(q, k, v, voff, dy, stats, n_heads, name):
    d, m, _ = q.shape
    tq = _dil_tq(m)
    nq = m // tq
    per = tq // WINDOW
    last_blk = m // WINDOW - 1
    hb = _dil_heads(tq, n_heads)
    scale = HEAD ** -0.5
    blocks = [(j, b) for j in range(hb) for b in range(per)]
    lanes = [slice(j * HEAD, (j + 1) * HEAD) for j in range(hb)]

    def body(qc_ref, qn_ref, kc_ref, kp_ref, vc_ref, vp_ref, dyc_ref, dyn_ref, sc_ref, sn_ref,
             dq_ref, dk_ref, dv_ref):
        n = pl.program_id(2)
        kk = jnp.concatenate([kp_ref[...], kc_ref[...]], axis=0)
        vv = jnp.concatenate([vp_ref[...], vc_ref[...]], axis=0)
        qq = jnp.concatenate([qc_ref[...], qn_ref[...]], axis=0)
        dyy = jnp.concatenate([dyc_ref[...], dyn_ref[...]], axis=0)
        st = jnp.concatenate([sc_ref[...], sn_ref[...]], axis=0)
        half = HEAD // 2
        row = lax.broadcasted_iota(jnp.int32, (WINDOW, 2 * WINDOW), 0)
        col = lax.broadcasted_iota(jnp.int32, (WINDOW, 2 * WINDOW), 1)
        band = (col >= row) & (col <= row + WINDOW)
        rows = [slice(b * WINDOW, (b + 1) * WINDOW) for b in range(per)]
        wide = [slice(b * WINDOW, (b + 2) * WINDOW) for b in range(per)]
        nt = lambda a, b: lax.dot_general(a, b, NT_DIMS, preferred_element_type=F32)
        s = [nt(qq[rows[b], lanes[j]], kk[wide[b], lanes[j]]) * scale for j, b in blocks]
        dp = [nt(dyy[rows[b], lanes[j]], vv[wide[b], lanes[j]]) for j, b in blocks]
        ds = []
        for (j, b), sb, dpb in zip(blocks, s, dp):
            valid = band if b else band & ((n > 0) | (col >= WINDOW))
            stb = st[rows[b], lanes[j]]
            p = jnp.where(valid, jnp.exp(jnp.minimum(sb - stb[:, :1], 0.0)), 0.0)
            ds.append((p * (dpb - stb[:, half:half + 1]) * scale).astype(BF16))
        dq = [jnp.dot(dsb, kk[wide[b], lanes[j]], preferred_element_type=F32) for (j, b), dsb in zip(blocks, ds)]
        kc, vc = kc_ref[...], vc_ref[...]
        s2 = [nt(kc[rows[b], lanes[j]], qq[wide[b], lanes[j]]) * scale for j, b in blocks]
        dp2 = [nt(vc[rows[b], lanes[j]], dyy[wide[b], lanes[j]]) for j, b in blocks]
        ds2, p2 = [], []
        for (j, b), sb, dpb in zip(blocks, s2, dp2):
            valid = band if b < per - 1 else band & ((n < nq - 1) | (col < WINDOW))
            st_t = st[wide[b], lanes[j]].T
            pb = jnp.where(valid, jnp.exp(jnp.minimum(sb - st_t[:1], 0.0)), 0.0)
            ds2.append((pb * (dpb - st_t[half:half + 1]) * scale).astype(BF16))
            p2.append(pb.astype(BF16))
        dk = [jnp.dot(x, qq[wide[b], lanes[j]], preferred_element_type=F32) for (j, b), x in zip(blocks, ds2)]
        dv = [jnp.dot(x, dyy[wide[b], lanes[j]], preferred_element_type=F32) for (j, b), x in zip(blocks, p2)]
        dq_ref[...] = _tile_up(dq, hb, per)
        dk_ref[...] = _tile_up(dk, hb, per)
        dv_ref[...] = _tile_up(dv, hb, per)

    assert voff % hb == 0
    cur = lambda off: pl.BlockSpec((None, tq, hb * HEAD), lambda r, h, n: (r, n, off // hb + h))
    prev = lambda off: pl.BlockSpec((None, WINDOW, hb * HEAD),
                                    lambda r, h, n: (r, jnp.maximum(n * per - 1, 0), off // hb + h))
    nxt = lambda off: pl.BlockSpec((None, WINDOW, hb * HEAD),
                                   lambda r, h, n: (r, jnp.minimum((n + 1) * per, last_blk), off // hb + h))
    out = jax.ShapeDtypeStruct((d, m, n_heads * HEAD), F32)
    return pl.pallas_call(
        body, name=name, grid=(d, n_heads // hb, nq),
        in_specs=[cur(0), nxt(0), cur(0), prev(0), cur(voff), prev(voff), cur(0), nxt(0), cur(0), nxt(0)],
        out_specs=[cur(0)] * 3, out_shape=[out] * 3,
        compiler_params=_params(("parallel",) * 3, 40),
    )(q, q, k, k, v, v, dy, dy, stats, stats)


def _mix_fwd(os_, ls_, name):
    tm = 256
    s, w = os_[0].shape
    n = len(os_)

    def fn(*vals):
        o, l = vals[:n], vals[n:]
        mx = functools.reduce(jnp.maximum, l)
        e = [jnp.exp(x - mx) for x in l]
        den = functools.reduce(jnp.add, e)
        y = functools.reduce(jnp.add, [ei * oi for ei, oi in zip(e, o)]) / den
        return y, mx + jnp.log(den)

    return _ew(fn, [_row_spec(a, tm) for a in list(os_) + list(ls_)],
               [((s, w), BF16, (tm, w), lambda i: (i, 0)), ((s, w), F32, (tm, w), lambda i: (i, 0))],
               (s // tm,), name)


def _mix_stats(dy, y, lse, n_heads, name):
    tm = 256
    s = dy.shape[0]
    w = n_heads * HEAD

    def fn(dys, ys, ls):
        lane = lax.broadcasted_iota(jnp.int32, (tm, HEAD), 1)
        packed = []
        for a, b, l in zip(_heads(dys, n_heads), _heads(ys, n_heads), _heads(ls, n_heads)):
            delta = jnp.sum(a * b.astype(F32), axis=-1, keepdims=True)
            packed.append(jnp.where(lane < HEAD // 2, l, delta))
        return dys, jnp.concatenate(packed, axis=1)

    blk = lambda a: (a, (tm, w), lambda i: (i, 0))
    out = lambda dt: ((s, w), dt, (tm, w), lambda i: (i, 0))
    return _ew(fn, [blk(dy), blk(y), blk(lse)], [out(BF16), out(F32)], (s // tm,), name)


SB_TQ = 1024
SB_TB = 512
SB_TK = 256


def _softplus(z):
    return jnp.where(z > 20.0, z, jnp.log(1.0 + jnp.exp(z)))


def _tri(t, cmp):
    rows = lax.broadcasted_iota(jnp.int32, (2 * t, t), 0)
    cols = lax.broadcasted_iota(jnp.int32, (2 * t, t), 1)
    return jnp.where(cmp(jnp.where(rows >= t, rows - t, rows), cols), 1.0, 0.0).astype(BF16)


def _tri_sum(x, tri):
    return jnp.dot(jnp.concatenate(_split_bf16(x), axis=1), tri, preferred_element_type=F32)


def _causal(rows, cols):
    return lax.broadcasted_iota(jnp.int32, (rows, cols), 1) < lax.broadcasted_iota(jnp.int32, (rows, cols), 0)


def _rowsum(x):
    return jnp.broadcast_to(jnp.sum(x, axis=-1, keepdims=True), (x.shape[0], HEAD))


def _over_keys(c, width):
    return jnp.concatenate([c] * (width // HEAD), axis=1)


def _from(x, r0):
    return x if r0 == 0 else x[r0:]


def _add_from(x, r0, upd):
    return x + upd if r0 == 0 else jnp.concatenate([x[:r0], x[r0:] + upd], axis=0)


def _sb_fwd(qkv, qoff, koff, voff, n_heads, name, comm=None):
    s = qkv.shape[0]
    tq, tb, tk = min(SB_TQ, s), SB_TB, SB_TK
    scale = HEAD ** -0.5
    host = _Host(comm, (n_heads, s // tq))

    def body(*refs):
        q_ref, k_ref, v_ref = refs[:3]
        comm_refs, ((o_ref, tot_ref), _) = host.split(refs, 3, 2)
        host.stage("first", comm_refs)
        i = pl.program_id(1)
        q = q_ref[...]
        after = _tri(tk, lambda a, b: a > b)

        def block(base, carry, o, diag_off):
            halves = list(reversed(range(tb // tk)))
            starts = [pl.multiple_of(base + h * tk, tk) for h in halves]
            r0s = [0 if diag_off is None else diag_off + h * tk for h in halves]
            masks = [None if diag_off is None else _causal(tq - r0, tk) for r0 in r0s]
            z = [lax.dot_general(_from(q, r0),k_ref[pl.ds(st, tk), :], NT_DIMS, preferred_element_type=F32) * scale
                 for st, r0 in zip(starts, r0s)]
            sp = [_softplus(zz) for zz in z]
            logsig = [zz - ss for zz, ss in zip(z, sp)]
            sp = [ss if m is None else jnp.where(m, ss, 0.0) for ss, m in zip(sp, masks)]
            sfx = [_tri_sum(ss, after) for ss in sp]
            probs = []
            for ls, sx, ss, m, r0 in zip(logsig, sfx, sp, masks, r0s):
                a = jnp.exp(ls - _over_keys(_from(carry, r0), tk) - sx)
                probs.append((a if m is None else jnp.where(m, a, 0.0)).astype(BF16))
                carry = _add_from(carry, r0, _rowsum(ss))
            for a, st, r0 in zip(probs, starts, r0s):
                o = _add_from(o, r0, jnp.dot(a, v_ref[pl.ds(st, tk), :], preferred_element_type=F32))
            return carry, o

        carry, o = jnp.zeros((tq, HEAD), F32), jnp.zeros((tq, HEAD), F32)
        for b in reversed(range(tq // tb)):
            carry, o = block(i * tq + b * tb, carry, o, b * tb)
        below = i * (tq // tb)
        carry, o = lax.fori_loop(0, below, lambda jj, co: block((below - 1 - jj) * tb, co[0], co[1], None),
                                 (carry, o))
        o_ref[...] = o.astype(o_ref.dtype)
        tot_ref[...] = carry
        host.stage("mid", comm_refs)
        host.stage("last", comm_refs)

    t = tq
    full = lambda off: pl.BlockSpec((s, HEAD), lambda h, i: (0, off + h))
    tile_spec = lambda off: pl.BlockSpec((t, HEAD), lambda h, i: (i, off + h))
    out = lambda dt: jax.ShapeDtypeStruct((s, n_heads * HEAD), dt)
    return pl.pallas_call(
        body, name=name, grid=(n_heads, s // t),
        in_specs=[tile_spec(qoff), full(koff), full(voff)] + host.in_specs,
        out_specs=[tile_spec(0), tile_spec(0)] + host.out_specs,
        out_shape=[out(BF16), out(F32)] + host.out_shape,
        scratch_shapes=host.scratch,
        compiler_params=_params(("parallel" if comm is None else "arbitrary", "arbitrary"), 40),
    )(qkv, qkv, qkv, *host.ins)


def _sb_bwd(qkv, qoff, koff, voff, do, tot, n_heads, name, comm=None):
    s = qkv.shape[0]
    tq, tb, tk = min(SB_TQ, s), SB_TB, SB_TK
    scale = HEAD ** -0.5
    host = _Host(comm, (n_heads, s // tq))

    def body(*refs):
        q_ref, k_ref, v_ref, do_ref, tot_ref = refs[:5]
        comm_refs, ((dq_ref, dk_ref, dv_ref), _) = host.split(refs, 5, 3)
        host.stage("first", comm_refs)
        i = pl.program_id(1)

        @pl.when(i == 0)
        def _():
            dk_ref[...] = jnp.zeros_like(dk_ref)
            dv_ref[...] = jnp.zeros_like(dv_ref)

        q = q_ref[...]
        do_b = do_ref[...].astype(BF16)
        total = tot_ref[...]
        upto = _tri(tk, lambda a, b: a <= b)
        before = _tri(tk, lambda a, b: a < b)[:tk]
        def block(base, lsum, psum, dq, diag_off):
            halves = list(range(tb // tk))
            starts = [pl.multiple_of(base + h * tk, tk) for h in halves]
            r0s = [0 if diag_off is None else diag_off + h * tk for h in halves]
            masks = [None if diag_off is None else _causal(tq - r0, tk) for r0 in r0s]
            keep = lambda x, m: x if m is None else jnp.where(m, x, 0.0)
            ks = [k_ref[pl.ds(st, tk), :] for st in starts]
            z = [lax.dot_general(_from(q, r0),kj, NT_DIMS, preferred_element_type=F32) * scale
                 for kj, r0 in zip(ks, r0s)]
            da = [lax.dot_general(_from(do_b, r0), v_ref[pl.ds(st, tk), :], NT_DIMS, preferred_element_type=F32)
                  for st, r0 in zip(starts, r0s)]
            sp = [_softplus(zz) for zz in z]
            logsig = [zz - ss for zz, ss in zip(z, sp)]
            sp = [keep(ss, m) for ss, m in zip(sp, masks)]
            pre = [_tri_sum(ss, upto) for ss in sp]
            probs, p = [], []
            for ls, px, ss, m, dd, r0 in zip(logsig, pre, sp, masks, da, r0s):
                a = keep(jnp.exp(ls - (_over_keys(_from(total, r0) - _from(lsum, r0), tk) - px)), m)
                probs.append(a.astype(BF16))
                p.append(a * dd)
                lsum = _add_from(lsum, r0, _rowsum(ss))
            cs = [jnp.dot(pp.astype(BF16), before, preferred_element_type=F32) for pp in p]
            dzs = []
            for ls, pp, cc, m, r0 in zip(logsig, p, cs, masks, r0s):
                c_all = _over_keys(_from(psum, r0), tk) + cc
                dzs.append(keep((pp - (pp + c_all) * jnp.exp(ls)) * scale, m).astype(BF16))
                psum = _add_from(psum, r0, _rowsum(pp))
            for dz_b, kj, r0 in zip(dzs, ks, r0s):
                dq = _add_from(dq, r0, jnp.dot(dz_b, kj, preferred_element_type=F32))
            for dz_b, a, st, r0 in zip(dzs, probs, starts, r0s):
                dk_ref[pl.ds(st, tk), :] += lax.dot_general(dz_b, _from(q, r0), TN_DIMS, preferred_element_type=F32)
                dv_ref[pl.ds(st, tk), :] += lax.dot_general(a, _from(do_b, r0), TN_DIMS,
                                                            preferred_element_type=F32)
            return lsum, psum, dq

        zero = jnp.zeros((tq, HEAD), F32)
        state = lax.fori_loop(0, i * (tq // tb), lambda j, c: block(j * tb, c[0], c[1], c[2], None),
                              (zero, zero, jnp.zeros((tq, HEAD), F32)))
        for b in range(tq // tb):
            state = block(i * tq + b * tb, *state, b * tb)
        dq_ref[...] = state[2].astype(dq_ref.dtype)
        host.stage("mid", comm_refs)
        host.stage("last", comm_refs)

    t = tq
    full = lambda off: pl.BlockSpec((s, HEAD), lambda h, i: (0, off + h))
    tile_spec = lambda off: pl.BlockSpec((t, HEAD), lambda h, i: (i, off + h))
    w = n_heads * HEAD
    return pl.pallas_call(
        body, name=name, grid=(n_heads, s // t),
        in_specs=[tile_spec(qoff), full(koff), full(voff), tile_spec(0), tile_spec(0)] + host.in_specs,
        out_specs=[tile_spec(0), full(0), full(0)] + host.out_specs,
        out_shape=[jax.ShapeDtypeStruct((s, w), BF16), jax.ShapeDtypeStruct((s, w), F32),
                   jax.ShapeDtypeStruct((s, w), F32)] + host.out_shape,
        scratch_shapes=host.scratch,
        compiler_params=_params(("parallel" if comm is None else "arbitrary", "arbitrary"), 48),
    )(qkv, qkv, qkv, do, tot, *host.ins)


def _coords():
    return lax.axis_index("x"), lax.axis_index("y"), lax.axis_index("c")


def _gather_plan(shards):
    n = len(shards)

    def run(stage, ins, outs, sems):
        send_sems, recv_sems, local_sems = sems
        x, y, c = _coords()
        me, sibling = (x, y, c), (x, y, 1 - c)
        chips = [(1 - x, y), (x, 1 - y), (1 - x, 1 - y)]

        def copy(w, k, block, to, src=None):
            dst = outs[w].at[4 * block[0] + 2 * block[1] + block[2]]
            return pltpu.make_async_remote_copy(
                src_ref=dst if src is None else src, dst_ref=dst,
                send_sem=send_sems.at[7 * w + k], recv_sem=recv_sems.at[7 * w + k],
                device_id=to, device_id_type=MESH)

        def mine():
            return [pltpu.make_async_copy(ins[w], outs[w].at[4 * x + 2 * y + c], local_sems.at[w]) for w in range(n)]

        def first():
            cps = []
            for w in range(n):
                cps.append(copy(w, 0, me, sibling, src=ins[w]))
                cps += [copy(w, 1 + j, me, (*chip, c), src=ins[w]) for j, chip in enumerate(chips)]
            return cps

        def passed():
            return [copy(w, 4 + j, (*chip, c), sibling) for w in range(n) for j, chip in enumerate(chips)]

        if stage == "first":
            for cp in mine() + first():
                cp.start()
        elif stage == "mid":
            onward = passed()
            for w in range(n):
                for j, chip in enumerate(chips):
                    copy(w, 1 + j, (*chip, c), me).wait_recv()
                    onward[3 * w + j].start()
        else:
            for w in range(n):
                copy(w, 0, sibling, me).wait_recv()
                for j, chip in enumerate(chips):
                    copy(w, 4 + j, (*chip, 1 - c), me).wait_recv()
            for cp in first() + passed():
                cp.wait_send()
            for cp in mine():
                cp.wait()

    return dict(
        ins=list(shards), run=run, stages=("first", "mid", "last"),
        out_shape=[jax.ShapeDtypeStruct((N_DEV,) + a.shape, a.dtype) for a in shards],
        scratch=[pltpu.SemaphoreType.DMA((7 * n,)), pltpu.SemaphoreType.DMA((7 * n,)), pltpu.SemaphoreType.DMA((n,))])


def _scatter_plan(grads, only_c=None):
    n = len(grads)

    def run(stage, ins, outs, sems):
        send_sems, recv_sems = sems
        x, y, c = _coords()
        for w in range(n):
            for mask in range(1, N_DEV):
                px, py, pc = x ^ (mask >> 2), y ^ ((mask >> 1) & 1), c ^ (mask & 1)
                k = 7 * w + mask - 1
                block = 4 * px + 2 * py + pc if only_c is None else 2 * px + py
                cp = pltpu.make_async_remote_copy(
                    src_ref=ins[w].at[block], dst_ref=outs[w].at[mask - 1],
                    send_sem=send_sems.at[k], recv_sem=recv_sems.at[k],
                    device_id=(px, py, pc), device_id_type=MESH)
                if only_c is None:
                    cp.start() if stage == "first" else cp.wait()
                elif stage == "first":
                    pl.when(pc == only_c)(cp.start)
                else:
                    pl.when(pc == only_c)(cp.wait_send)
                    pl.when(c == only_c)(cp.wait_recv)

    return dict(
        ins=list(grads), run=run, stages=("first", "last"),
        out_shape=[jax.ShapeDtypeStruct((N_DEV - 1,) + a.shape[1:], a.dtype) for a in grads],
        scratch=[pltpu.SemaphoreType.DMA((7 * n,)), pltpu.SemaphoreType.DMA((7 * n,))])


def _run_plan(plan, name):
    n_in, n_out = len(plan["ins"]), len(plan["out_shape"])

    def body(*refs):
        for stage in plan["stages"]:
            plan["run"](stage, refs[:n_in], refs[n_in:n_in + n_out], refs[n_in + n_out:])

    any_spec = pl.BlockSpec(memory_space=pl.ANY)
    return pl.pallas_call(
        body, name=name, in_specs=[any_spec] * n_in, out_specs=[any_spec] * n_out,
        out_shape=plan["out_shape"], scratch_shapes=plan["scratch"],
    )(*plan["ins"])


def _gather_rows(v):
    rows, width = v.shape

    def body(v_ref, out_ref, send_sems, recv_sems):
        x, y, c = _coords()
        out_ref[pl.ds(pl.multiple_of((4 * x + 2 * y + c) * rows, rows), rows), :] = v_ref[...]
        cps = []
        for mask in range(1, N_DEV):
            peer = (x ^ (mask >> 2), y ^ ((mask >> 1) & 1), c ^ (mask & 1))
            dst = out_ref.at[pl.ds(pl.multiple_of((4 * x + 2 * y + c) * rows, rows), rows), :]
            cps.append(pltpu.make_async_remote_copy(
                src_ref=v_ref, dst_ref=dst, send_sem=send_sems.at[mask - 1], recv_sem=recv_sems.at[mask - 1],
                device_id=peer, device_id_type=MESH))
        for cp in cps:
            cp.start()
        for cp in cps:
            cp.wait()

    vmem = pl.BlockSpec(memory_space=pltpu.VMEM)
    return pl.pallas_call(
        body, name="gather_small",
        in_specs=[vmem], out_specs=vmem,
        out_shape=jax.ShapeDtypeStruct((N_DEV * rows, width), F32),
        scratch_shapes=[pltpu.SemaphoreType.DMA((N_DEV - 1,)), pltpu.SemaphoreType.DMA((N_DEV - 1,))],
    )(v)


def _adamw(w, g, m, v):
    m = ADAM_B1 * m + (1.0 - ADAM_B1) * g
    v = ADAM_B2 * v + (1.0 - ADAM_B2) * jnp.square(g)
    m_hat = m / (1.0 - ADAM_B1 ** ADAM_STEP)
    v_hat = v / (1.0 - ADAM_B2 ** ADAM_STEP)
    delta = -ADAM_LR * (m_hat / (jnp.sqrt(v_hat) + ADAM_EPS) + ADAM_WD * w)
    return delta, m, v


def _tile_rows(rows, cols):
    tm = 1 << int(math.log2(max(2 * SUBLANES, (1 << 18) // cols)))
    while rows % tm:
        tm //= 2
    assert tm >= 2 * SUBLANES, (rows, cols)
    return tm


def _reduce_adam(w, m, v, own, got, name):
    r, c = w.shape
    tm = max(2 * SUBLANES, _tile_rows(r, c) // 2)

    def fn(wv, mv, vv, a, *peers):
        g = a
        for pv in peers:
            g = g + pv.astype(F32)
        return (g,) + _adamw(wv, g, mv, vv)

    blk = lambda a: (a, (tm, c), lambda i: (i, 0))
    got_blk = lambda j: (got, (None, tm, c), lambda i, j=j: (j, i, 0))
    return _ew(fn, [blk(w), blk(m), blk(v), blk(own)] + [got_blk(j) for j in range(N_DEV - 1)],
               [((r, c), F32, (tm, c), lambda i: (i, 0))] * 4, (r // tm,), name)


def _small_adam(gathered, params, moms, vels, widths):
    n = len(params)
    total = gathered.shape[1]

    def body(*refs):
        g_ref = refs[0]
        p_refs, m_refs, v_refs = refs[1:1 + n], refs[1 + n:1 + 2 * n], refs[1 + 2 * n:1 + 3 * n]
        sum_ref = refs[1 + 3 * n]
        outs = refs[2 + 3 * n:]
        g = g_ref[0:1, :]
        for p in range(1, N_DEV):
            g = g + g_ref[p * SUBLANES:p * SUBLANES + 1, :]
        sum_ref[...] = g
        off = 0
        for i, wd in enumerate(widths):
            d, m2, v2 = _adamw(p_refs[i][...], g[:, off:off + wd], m_refs[i][...], v_refs[i][...])
            outs[3 * i][...] = d
            outs[3 * i + 1][...] = m2
            outs[3 * i + 2][...] = v2
            off += wd

    vmem = pl.BlockSpec(memory_space=pltpu.VMEM)
    out_shape = [jax.ShapeDtypeStruct((1, total), F32)]
    for wd in widths:
        out_shape += [jax.ShapeDtypeStruct((1, wd), F32)] * 3
    return pl.pallas_call(
        body, name="small_adam",
        in_specs=[vmem] * (1 + 3 * n), out_specs=[vmem] * len(out_shape), out_shape=out_shape,
    )(gathered, *params, *moms, *vels)


def _cast_bf16(a, name):
    r, c = a.shape
    tm = _tile_rows(r, c)
    return _ew(lambda v: v, [(a, (tm, c), lambda i: (i, 0))], [((r, c), BF16, (tm, c), lambda i: (i, 0))],
               (r // tm,), name)[0]


def _fold_loss(parts, name):
    r, c = parts.shape

    def fn(v):
        return jnp.broadcast_to(jnp.sum(jnp.sum(v, axis=0, keepdims=True), axis=1, keepdims=True), (SUBLANES, HEAD))

    return _ew(fn, [_const_spec(parts)], [((SUBLANES, HEAD), F32, (SUBLANES, HEAD), lambda i: (0, 0))], (1,),
               name)[0][0:1]


def kernel(x, p, g_mix, w_in, qn_gain, kn_gain, w_branch_a, w_branch_b, w_out, g_mlp, w_up, w_down, g_ple, w_ple_gate, w_ple_proj, loss_target, m_g_mix, m_w_in, m_qn_gain, m_kn_gain, m_w_branch_a, m_w_branch_b, m_w_out, m_g_mlp, m_w_up, m_w_down, m_g_ple, m_w_ple_gate, m_w_ple_proj, v_g_mix, v_w_in, v_qn_gain, v_kn_gain, v_w_branch_a, v_w_branch_b, v_w_out, v_g_mlp, v_w_up, v_w_down, v_g_ple, v_w_ple_gate, v_w_ple_proj):
    x2 = x[0]
    tgt = loss_target[0]
    s, d = x2.shape
    wd_ = w_branch_a.shape[1]
    nh = wd_ // HEAD
    dff = w_up.shape[1]
    qkv_w = 6 * wd_
    tiles = lambda cols: cols // HEAD

    big = [w_in[0], w_branch_a[0], w_branch_b[0], w_out[0], w_up[0], w_down[0], w_ple_gate[0], w_ple_proj[0]]
    names = ["w_in", "w_branch_a", "w_branch_b", "w_out", "w_up", "w_down", "w_ple_gate", "w_ple_proj"]
    row_sharded = [False, False, False, True, False, True, True, False]
    shards = [_cast_bf16(a, "cast_" + nm) for a, nm in zip(big, names)]
    as_weight = lambda g, rs: g.reshape((1, N_DEV * g.shape[1], g.shape[2])) if rs else g
    as_blocks = lambda g, rs: g.reshape((N_DEV, g.shape[1] // N_DEV, g.shape[2])) if rs else g
    win = _run_plan(_gather_plan(shards[:1]), "all_gather_w_in")[0]

    tm = 1024 if s % 1024 == 0 else s
    tm_in = tk_s = 2048 if s % 2048 == 0 else tm
    tn_of = lambda n: 512 if n % 512 == 0 else (256 if n % 256 == 0 else n)
    tn_in = 256 if win.shape[2] % 256 == 0 else HEAD

    h = _rms_fwd(x2, g_mix, "norm_mix")
    qk_raw = _mm("nn", h, win, tm=tm_in, tn=tn_in, tk=d, out_dtypes=[F32], name="proj_qk",
                 n_off=0, n_cnt=2 * wd_ // tn_in)[0]
    rest = _mm("nn", h, win, tm=4096 if s % 4096 == 0 else tm_in, tn=tn_in, tk=d, out_dtypes=[BF16], name="proj_rest",
               n_off=2 * wd_ // tn_in, n_cnt=(win.shape[0] * win.shape[2] - 2 * wd_) // tn_in)[0]
    o_va, o_qb, o_kb, o_vb, o_ga, o_gb = 0, tiles(wd_), tiles(2 * wd_), tiles(3 * wd_), tiles(4 * wd_), tiles(4 * wd_ + d)
    tabs = _rope_tables(s)
    qa = _headnorm_rope(qk_raw, 0, qn_gain, tabs, nh, "rope_q")
    ka = _headnorm_rope(qk_raw, 1, kn_gain, tabs, nh, "rope_k")
    va = rest[:, :wd_]

    outs, lses = [], []
    for dil in DILATIONS:
        o_g, l_g = _dilated_fwd(_phase_major(qa, dil), _phase_major(ka, dil), _phase_major(va, dil), 0, nh,
                                f"dilated_fwd_{dil}")
        outs.append(_token_major(o_g))
        lses.append(_token_major(l_g))
    ya, lse_all = _mix_fwd(outs, lses, "mix_fwd")
    yb, sb_tot, *others = _sb_fwd(rest, o_qb, o_kb, o_vb, nh, "sb_fwd", comm=_gather_plan(shards[1:]))
    wba, wbb, wout, wup, wdown, wgate, wproj = [as_weight(g, rs) for g, rs in zip(others, row_sharded[1:])]

    tn_d = tn_of(wba.shape[2])
    za = _mm("nn", ya, wba, tm=tm, tn=tn_d, tk=wd_, out_dtypes=[BF16], name="branch_a")[0]

    def merge(acc, zav, gav, gbv):
        return _sigmoid(gav.astype(F32)) * zav.astype(F32) + _sigmoid(gbv.astype(F32)) * acc, acc

    merged, zb = _mm("nn", yb, wbb, tm=tm, tn=tn_d, tk=wd_, out_dtypes=[BF16, BF16], name="branch_b_merge",
                     epilogue=merge, extras=[(za, 0), (rest, o_ga * HEAD // tn_d), (rest, o_gb * HEAD // tn_d)])
    x1 = _mm("nn", merged, wout, tm=tm, tn=512, tk=d, out_dtypes=[F32], name="out_proj",
             epilogue=lambda acc, xv: (acc + xv,), extras=[(x2, 0)])[0]

    hm = _rms_fwd(x1, g_mlp, "norm_mlp")
    tn_u = min(wup.shape[2], 1024)
    u, act = _mm("nn", hm, wup, tm=tm, tn=tn_u, tk=d, out_dtypes=[BF16, BF16], name="mlp_up",
                 epilogue=lambda acc: (acc, jnp.square(jnp.maximum(acc, 0.0))))
    x3 = _mm("nn", act, wdown, tm=tm, tn=1024, tk=min(dff, 2048), out_dtypes=[F32], name="mlp_down",
             epilogue=lambda acc, xv: (acc + xv,), extras=[(x1, 0)])[0]

    hp = _rms_fwd(x3, g_ple, "norm_ple")
    p_b = _cast_bf16(p[0, 0], "cast_p")
    pp = _mm("nn", p_b, wproj, tm=tm, tn=tn_of(wproj.shape[2]), tk=p_b.shape[1], out_dtypes=[BF16],
             name="ple_proj")[0]

    def head(acc, ppv, xv, tv):
        sg = _sigmoid(acc)
        ppf = ppv.astype(F32)
        err = xv + ppf * sg - tv
        dy = err / d
        sq = jnp.square(err)
        return dy, dy * sg, dy * ppf * sg * (1.0 - sg), sq.reshape(-1, SUBLANES, sq.shape[-1]).sum(axis=0)

    n_i = s // tm
    dy, d_pp, d_gt, sq_parts = _mm(
        "nn", hp, wgate, tm=tm, tn=512, tk=d, out_dtypes=[F32, BF16, BF16], name="ple_gate_loss", epilogue=head,
        extras=[(pp, 0), (x3, 0), (tgt, 0)],
        extra_outs=[((n_i * SUBLANES, d), F32, (SUBLANES, 512), lambda i, j: (i, j))])
    loss_vec = _fold_loss(sq_parts, "loss_fold") * 0.5 / d

    both = [F32, BF16]
    g_wproj = _mm("tn", p_b, d_pp, tm=p_b.shape[1], tn=tn_of(wproj.shape[2]), tk=tk_s, out_dtypes=both,
                  name="grad_w_ple_proj", out_nb=N_DEV)
    g_wgate = _mm("tn", hp, d_gt, tm=1024, tn=1024, tk=tk_s, out_dtypes=both, name="grad_w_ple_gate")
    d_hp = _mm("nt", d_gt, wgate, tm=tm, tn=512, tk=d, out_dtypes=[F32], name="d_hp")[0]
    dx3, dx3_b, g_gple = _rms_bwd(d_hp, x3, g_ple, dy, "norm_ple_bwd")

    d_u = _mm("nt", dx3_b, wdown, tm=tm, tn=1024, tk=d, out_dtypes=[BF16], name="d_u",
              epilogue=lambda acc, uv: (acc * (2.0 * jnp.maximum(uv.astype(F32), 0.0)),), extras=[(u, 0)])[0]
    g_wdown = _mm("tn", act, dx3_b, tm=1024, tn=1024, tk=tk_s, out_dtypes=both, name="grad_w_down")
    g_wup = _mm("tn", hm, d_u, tm=1024, tn=wup.shape[2], tk=tk_s, out_dtypes=both, name="grad_w_up", out_nb=N_DEV)
    d_hm = _mm("nt", d_u, wup, tm=tm, tn=min(d, 2048), tk=wup.shape[2], out_dtypes=[F32], name="d_hm")[0]
    dx1, dx1_b, g_gmlp = _rms_bwd(d_hm, x1, g_mlp, dx3, "norm_mlp_bwd")

    def unmerge(acc, gav, gbv, zav, zbv):
        sa, sb = _sigmoid(gav.astype(F32)), _sigmoid(gbv.astype(F32))
        return acc * sa, acc * sb, acc * zav.astype(F32) * sa * (1.0 - sa), acc * zbv.astype(F32) * sb * (1.0 - sb)

    d_za, d_zb, d_ga, d_gb = _mm(
        "nt", dx1_b, wout, tm=tm, tn=512, tk=d, out_dtypes=[BF16] * 4, name="d_merged", epilogue=unmerge,
        extras=[(rest, o_ga * HEAD // 512), (rest, o_gb * HEAD // 512), (za, 0), (zb, 0)])
    g_wout = _mm("tn", merged, dx1_b, tm=1024, tn=1024, tk=tk_s, out_dtypes=both, name="grad_w_out")
    g_wba = _mm("tn", ya, d_za, tm=wd_, tn=wba.shape[2], tk=tk_s, out_dtypes=both, name="grad_w_branch_a",
                out_nb=N_DEV)
    g_wbb = _mm("tn", yb, d_zb, tm=wd_, tn=wbb.shape[2], tk=tk_s, out_dtypes=both, name="grad_w_branch_b",
                out_nb=N_DEV)
    d_ya = _mm("nt", d_za, wba, tm=tm, tn=wd_, tk=wba.shape[2], out_dtypes=[F32], name="d_ya")[0]
    d_yb = _mm("nt", d_zb, wbb, tm=tm, tn=wd_, tk=wbb.shape[2], out_dtypes=[F32], name="d_yb")[0]

    grads = [None, g_wba, g_wbb, g_wout, g_wup, g_wdown, g_wgate, g_wproj]
    early = _scatter_plan([as_blocks(g[1], rs) for g, rs in zip(grads[1:], row_sharded[1:])])
    d_qb, d_kb, d_vb, *got = _sb_bwd(rest, o_qb, o_kb, o_vb, d_yb, sb_tot, nh, "sb_bwd", comm=early)
    d_ya_b, stats = _mix_stats(d_ya, ya, lse_all, nh, "mix_stats")
    dqs, dks, dvs = [], [], []
    for dil in DILATIONS:
        dq_g, dk_g, dv_g = _dilated_bwd(
            _phase_major(qa, dil), _phase_major(ka, dil), _phase_major(va, dil), 0, _phase_major(d_ya_b, dil),
            _phase_major(stats, dil), nh, f"dilated_bwd_{dil}")
        dqs.append(_token_major(dq_g))
        dks.append(_token_major(dk_g))
        dvs.append(_token_major(dv_g))
    d_qa, g_qn = _headnorm_rope_bwd(dqs, qk_raw, 0, qn_gain, tabs, nh, "rope_q_bwd")
    d_ka, g_kn = _headnorm_rope_bwd(dks, qk_raw, 1, kn_gain, tabs, nh, "rope_k_bwd")
    tmr = 256
    d_va = _ew(lambda a, b, c: a + b + c, [_row_spec(a, tmr) for a in dvs],
               [((s, wd_), BF16, (tmr, wd_), lambda i: (i, 0))], (s // tmr,), "sum_dv")[0]
    d_proj = jnp.concatenate([d_qa, d_ka, d_va, d_qb, d_kb.astype(BF16), d_vb.astype(BF16), d_ga, d_gb], axis=1)

    half = N_DEV // 2
    gw = lambda nm, off, comm: _mm("tn", h, d_proj, tm=1024, tn=win.shape[2], tk=tk_s, out_dtypes=both, name=nm,
                                   out_nb=half, n_off=off, n_step=2, n_cnt=half, comm=comm)
    g_lo = gw("grad_w_in_c0", 0, None)
    g_hi_f, g_hi_b, got_lo = gw("grad_w_in_c1", 1, _scatter_plan([g_lo[1]], only_c=0))
    d_h, got_hi = _mm("nt", d_proj, win, tm=tm, tn=min(d, 2048), tk=win.shape[2], out_dtypes=[F32], name="d_h",
                      comm=_scatter_plan([g_hi_b], only_c=1))
    dx, _, g_gmix = _rms_bwd(d_h, x2, g_mix, dx1, "norm_mix_bwd")

    cx, cy, cc = _coords()
    me = 4 * cx + 2 * cy + cc
    moms = [m_w_in, m_w_branch_a, m_w_branch_b, m_w_out, m_w_up, m_w_down, m_w_ple_gate, m_w_ple_proj]
    vels = [v_w_in, v_w_branch_a, v_w_branch_b, v_w_out, v_w_up, v_w_down, v_w_ple_gate, v_w_ple_proj]
    mine = lambda g: lax.dynamic_index_in_dim(g, 2 * cx + cy, axis=0, keepdims=False)
    owns = [jnp.where(cc == 0, mine(g_lo[0]), mine(g_hi_f))]
    owns += [lax.dynamic_index_in_dim(as_blocks(g[0], rs), me, axis=0, keepdims=False)
             for g, rs in zip(grads[1:], row_sharded[1:])]
    got = [jnp.where(cc == 0, got_lo, got_hi)] + got
    big_out = {}
    for i, nm in enumerate(names):
        big_out[nm] = [a[None] for a in _reduce_adam(big[i], moms[i][0], vels[i][0], owns[i], got[i], "adam_" + nm)]

    small_names = ["g_mix", "qn_gain", "kn_gain", "g_mlp", "g_ple"]
    small_p = [g_mix, qn_gain, kn_gain, g_mlp, g_ple]
    small_m = [m_g_mix, m_qn_gain, m_kn_gain, m_g_mlp, m_g_ple]
    small_v = [v_g_mix, v_qn_gain, v_kn_gain, v_g_mlp, v_g_ple]
    small_g = [g_gmix, g_qn, g_kn, g_gmlp, g_gple]
    widths = [a.shape[1] for a in small_p]
    vec = jnp.concatenate(small_g + [loss_vec], axis=1)
    vec = jnp.pad(vec, ((0, SUBLANES - 1), (0, 0)))
    res = _small_adam(_gather_rows(vec), small_p, small_m, small_v, widths)
    summed = res[0]
    small_out, off = {}, 0
    for i, nm in enumerate(small_names):
        small_out[nm] = [summed[:, off:off + widths[i]]] + list(res[1 + 3 * i:4 + 3 * i])
        off += widths[i]
    loss = summed[0, off]

    order = ["g_mix", "w_in", "qn_gain", "kn_gain", "w_branch_a", "w_branch_b", "w_out", "g_mlp", "w_up", "w_down",
             "g_ple", "w_ple_gate", "w_ple_proj"]
    table = {**big_out, **small_out}
    result = [loss, dx[None]]
    for kind in range(4):
        result += [table[nm][kind] for nm in order]
    return tuple(result)
```

```python
import functools
import math

import jax
import jax.numpy as jnp
from jax import lax
from jax.experimental import pallas as pl
from jax.experimental.pallas import tpu as pltpu

F32 = jnp.float32
BF16 = jnp.bfloat16
MESH = pl.DeviceIdType.MESH

HEAD = 128
WINDOW = 128
DILATIONS = (1, 4, 16)
ROT = HEAD // 4
ROPE_THETA = 500000.0
EPS = 1e-6
NEG = -1e30
N_DEV = 8

ADAM_LR = 0.001
ADAM_B1 = 0.9
ADAM_B2 = 0.999
ADAM_EPS = 1e-08
ADAM_WD = 0.01
ADAM_STEP = 10

SUBLANES = 8
VMEM_CAP_MB = 56

NT_DIMS = (((1,), (1,)), ((), ()))
TN_DIMS = (((0,), (0,)), ((), ()))


def _params(semantics, vmem_mb):
    return pltpu.CompilerParams(dimension_semantics=semantics, vmem_limit_bytes=min(vmem_mb, VMEM_CAP_MB) << 20)


def _sigmoid(x):
    return 0.5 + 0.5 * jnp.tanh(0.5 * x)


def _split_bf16(x):
    hi = x.astype(BF16)
    lo = (x - hi.astype(F32)).astype(BF16)
    return hi, lo


class _Host:
    def __init__(self, comm, grid):
        self.comm, self.grid = comm, grid
        any_spec = pl.BlockSpec(memory_space=pl.ANY)
        self.ins = list(comm["ins"]) if comm else []
        self.out_shape = list(comm["out_shape"]) if comm else []
        self.scratch = list(comm["scratch"]) if comm else []
        self.in_specs = [any_spec] * len(self.ins)
        self.out_specs = [any_spec] * len(self.out_shape)

    def split(self, refs, n_in, n_out):
        pos = n_in
        c_in = refs[pos:pos + len(self.ins)]
        pos += len(self.ins)
        outs = refs[pos:pos + n_out]
        pos += n_out
        c_out = refs[pos:pos + len(self.out_shape)]
        pos += len(self.out_shape)
        own = len(refs) - pos - len(self.scratch)
        return (c_in, c_out, refs[pos + own:]), (outs, refs[pos:pos + own])

    def stage(self, which, comm_refs):
        if self.comm is None or which not in self.comm["stages"]:
            return
        grid = self.grid
        at = {"first": [0] * len(grid), "mid": [grid[0] // 2] + [0] * (len(grid) - 1),
              "last": [g - 1 for g in grid]}[which]
        cond = functools.reduce(jnp.logical_and, [pl.program_id(ax) == v for ax, v in enumerate(at)])

        @pl.when(cond)
        def _():
            self.comm["run"](which, *comm_refs)


def _ew(fn, ins, outs, grid, name, colsums=(), vmem_mb=40):
    n_in, n_out, n_cs = len(ins), len(outs), len(colsums)
    steps = math.prod(grid)

    def body(*refs):
        in_refs = refs[:n_in]
        out_refs = refs[n_in:n_in + n_out]
        cs_refs = refs[n_in + n_out:n_in + n_out + n_cs]
        acc_refs = refs[n_in + n_out + n_cs:]
        vals = fn(*[r[...] for r in in_refs])
        if not isinstance(vals, tuple):
            vals = (vals,)
        for r, v in zip(out_refs, vals[:n_out]):
            r[...] = v.astype(r.dtype)
        if n_cs:
            step = pl.program_id(0)
            for ax in range(1, len(grid)):
                step = step * grid[ax] + pl.program_id(ax)
            for acc, cs, v in zip(acc_refs, cs_refs, vals[n_out:]):
                part = v.reshape(-1, SUBLANES, v.shape[-1]).sum(axis=0)

                @pl.when(step == 0)
                def _(acc=acc, part=part):
                    acc[...] = part

                @pl.when(step > 0)
                def _(acc=acc, part=part):
                    acc[...] += part

                @pl.when(step == steps - 1)
                def _(acc=acc, cs=cs):
                    cs[...] = acc[...].sum(axis=0, keepdims=True)

    out_shape = [jax.ShapeDtypeStruct(s, d) for s, d, _, _ in outs]
    out_specs = [pl.BlockSpec(b, m) for _, _, b, m in outs]
    for w in colsums:
        out_shape.append(jax.ShapeDtypeStruct((1, w), F32))
        out_specs.append(pl.BlockSpec((1, w), lambda *_: (0, 0)))
    sem = ("arbitrary",) * len(grid) if n_cs else ("parallel",) * len(grid)
    res = pl.pallas_call(
        body, name=name, grid=grid,
        in_specs=[pl.BlockSpec(b, m) for _, b, m in ins],
        out_specs=out_specs, out_shape=out_shape,
        scratch_shapes=[pltpu.VMEM((SUBLANES, w), F32) for w in colsums],
        compiler_params=_params(sem, vmem_mb),
    )(*[a for a, _, _ in ins])
    return res


def _row_spec(a, tm):
    return (a, (tm, a.shape[1]), lambda i: (i, 0))


def _const_spec(a):
    return (a, a.shape, lambda *_: (0,) * a.ndim)


def _mm(mode, a, b, *, tm, tn, tk, out_dtypes, name, epilogue=None, extras=(), n_off=0, n_cnt=None,
        out_nb=1, extra_outs=(), vmem_mb=52, comm=None):
    if mode == "nn":
        m, kdim = a.shape
        nb, _, n = b.shape
        npb = n // tn
        ncols = nb * n
        n_tiles = (ncols // tn) if n_cnt is None else n_cnt
        a_spec = pl.BlockSpec((tm, tk), lambda i, j, k: (i, k))
        b_spec = pl.BlockSpec((None, tk, tn), lambda i, j, k: ((j + n_off) // npb, k, (j + n_off) % npb))
        dims = (((1,), (0,)), ((), ()))
    elif mode == "nt":
        m, kdim = a.shape
        nb, nout, n = b.shape
        kpb = n // tk
        n_tiles = nout // tn
        a_spec = pl.BlockSpec((tm, tk), lambda i, j, k: (i, k))
        b_spec = pl.BlockSpec((None, tn, tk), lambda i, j, k: (k // kpb, j, k % kpb))
        dims = NT_DIMS
    else:
        kdim, m = a.shape
        ncols = b.shape[1]
        n_tiles = ncols // tn
        a_spec = pl.BlockSpec((tk, tm), lambda i, j, k: (k, i))
        b_spec = pl.BlockSpec((tk, tn), lambda i, j, k: (k, j))
        dims = TN_DIMS
    nk = kdim // tk
    assert kdim % tk == 0 and m % tm == 0
    grid = (m // tm, n_tiles, nk)
    n_ex, n_out = len(extras), len(out_dtypes) + len(extra_outs)
    host = _Host(comm, grid)

    def body(*refs):
        a_ref, b_ref = refs[0], refs[1]
        ex_refs = refs[2:2 + n_ex]
        comm_refs, (out_refs, scratch) = host.split(refs, 2 + n_ex, n_out)
        host.stage("first", comm_refs)

        def finish(acc):
            vals = (acc,) * n_out if epilogue is None else epilogue(acc, *[r[...] for r in ex_refs])
            for r, v in zip(out_refs, vals):
                r[...] = v.astype(r.dtype)

        def part():
            return lax.dot_general(a_ref[...], b_ref[...], dims, preferred_element_type=F32)

        if nk == 1:
            finish(part())
        else:
            acc_ref = scratch[0]
            k = pl.program_id(2)

            @pl.when(k == 0)
            def _():
                acc_ref[...] = part()

            @pl.when((k > 0) & (k < nk - 1))
            def _():
                acc_ref[...] += part()

            @pl.when(k == nk - 1)
            def _():
                finish(acc_ref[...] + part())

        host.stage("mid", comm_refs)
        host.stage("last", comm_refs)

    if mode == "tn":
        npo = (ncols // out_nb) // tn
        out_shape = [jax.ShapeDtypeStruct((out_nb, m, ncols // out_nb), d) for d in out_dtypes]
        out_specs = [pl.BlockSpec((None, tm, tn), lambda i, j, k: (j // npo, i, j % npo)) for _ in out_dtypes]
    else:
        out_shape = [jax.ShapeDtypeStruct((m, n_tiles * tn), d) for d in out_dtypes]
        out_specs = [pl.BlockSpec((tm, tn), lambda i, j, k: (i, j)) for _ in out_dtypes]
    for s, d, blk, imap in extra_outs:
        out_shape.append(jax.ShapeDtypeStruct(s, d))
        out_specs.append(pl.BlockSpec(blk, lambda i, j, k, imap=imap: imap(i, j)))
    ex_specs = [pl.BlockSpec((tm, tn), lambda i, j, k, off=off: (i, j + off)) for _, off in extras]
    sem = ("parallel", "parallel", "arbitrary") if comm is None else ("arbitrary",) * 3
    return pl.pallas_call(
        body, name=name, grid=grid,
        in_specs=[a_spec, b_spec] + ex_specs + host.in_specs,
        out_specs=out_specs + host.out_specs, out_shape=out_shape + host.out_shape,
        scratch_shapes=([pltpu.VMEM((tm, tn), F32)] if nk > 1 else []) + host.scratch,
        compiler_params=_params(sem, vmem_mb),
    )(a, b, *[e for e, _ in extras], *host.ins)


def _rms_fwd(x, g, name):
    tm = 256

    def fn(xv, gv):
        r = lax.rsqrt(jnp.mean(xv * xv, axis=-1, keepdims=True) + EPS)
        return xv * r * gv

    return _ew(fn, [_row_spec(x, tm), _const_spec(g)], [(x.shape, BF16, (tm, x.shape[1]), lambda i: (i, 0))],
               (x.shape[0] // tm,), name)[0]


def _rms_bwd(dh, x, g, res, name):
    tm = 256
    d = x.shape[1]

    def fn(dhv, xv, gv, rv):
        r = lax.rsqrt(jnp.mean(xv * xv, axis=-1, keepdims=True) + EPS)
        xh = xv * r
        dyg = dhv * gv
        dx = rv + r * (dyg - xh * jnp.mean(dyg * xh, axis=-1, keepdims=True))
        return dx, dx, dhv * xh

    spec = lambda dt: (x.shape, dt, (tm, d), lambda i: (i, 0))
    return _ew(fn, [_row_spec(dh, tm), _row_spec(x, tm), _const_spec(g), _row_spec(res, tm)],
               [spec(F32), spec(BF16)], (x.shape[0] // tm,), name, colsums=(d,))


def _rope_tables(s):
    half = ROT // 2
    pos = jnp.arange(s, dtype=F32)
    inv = ROPE_THETA ** (-jnp.arange(0, ROT, 2, dtype=F32) / ROT)
    ang = pos[:, None] * inv[None, :]
    cos, sin = jnp.cos(ang), jnp.sin(ang)
    pad = jnp.zeros((s, HEAD - ROT), F32)
    c = jnp.concatenate([cos, cos, pad + 1.0], axis=1)
    a = jnp.concatenate([-sin, jnp.zeros_like(sin), pad], axis=1)
    b = jnp.concatenate([jnp.zeros_like(sin), sin, pad], axis=1)
    return c, a, b


def _heads(x, n_heads):
    return [x[:, h * HEAD:(h + 1) * HEAD] for h in range(n_heads)]


def _headnorm_rope(proj, part, gain, tabs, n_heads, name):
    tm = 512
    s = proj.shape[0]
    w = n_heads * HEAD

    def fn(xs, gv, c, a, b):
        outs = []
        for xv in _heads(xs, n_heads):
            r = lax.rsqrt(jnp.mean(xv * xv, axis=-1, keepdims=True) + EPS)
            y = xv * r * gv
            outs.append(c * y + a * pltpu.roll(y, HEAD - ROT // 2, 1) + b * pltpu.roll(y, ROT // 2, 1))
        return jnp.concatenate(outs, axis=1)

    tab = lambda t: (t, (tm, HEAD), lambda i: (i, 0))
    return _ew(fn, [(proj, (tm, w), lambda i: (i, part)), (gain, (1, HEAD), lambda i: (0, 0))] + [tab(t) for t in tabs],
               [((s, w), BF16, (tm, w), lambda i: (i, 0))], (s // tm,), name)[0]


def _headnorm_rope_bwd(dys, proj, part, gain, tabs, n_heads, name):
    tm = 256
    s = proj.shape[0]
    w = n_heads * HEAD
    n_dy = len(dys)

    def fn(*vals):
        dy_all = vals[0]
        for v in vals[1:n_dy]:
            dy_all = dy_all + v
        xs, gv, c, a, b = vals[n_dy:]
        dxs, dgain = [], None
        for dy, xv in zip(_heads(dy_all, n_heads), _heads(xs, n_heads)):
            dn = c * dy + pltpu.roll(a * dy, ROT // 2, 1) + pltpu.roll(b * dy, HEAD - ROT // 2, 1)
            r = lax.rsqrt(jnp.mean(xv * xv, axis=-1, keepdims=True) + EPS)
            xh = xv * r
            dyg = dn * gv
            dxs.append(r * (dyg - xh * jnp.mean(dyg * xh, axis=-1, keepdims=True)))
            dgain = dn * xh if dgain is None else dgain + dn * xh
        return jnp.concatenate(dxs, axis=1), dgain

    tab = lambda t: (t, (tm, HEAD), lambda i: (i, 0))
    return _ew(fn, [(d, (tm, w), lambda i: (i, 0)) for d in dys]
               + [(proj, (tm, w), lambda i: (i, part)), (gain, (1, HEAD), lambda i: (0, 0))] + [tab(t) for t in tabs],
               [((s, w), BF16, (tm, w), lambda i: (i, 0))], (s // tm,), name, colsums=(HEAD,))


def _phase_major(a, d):
    s, w = a.shape
    if d == 1:
        return a.reshape(1, s, w)
    return a.reshape(s // d, d, w).transpose(1, 0, 2)


def _token_major(a):
    d, m, w = a.shape
    if d == 1:
        return a.reshape(m, w)
    return a.transpose(1, 0, 2).reshape(m * d, w)


def _dil_tq(m):
    return min(1024, m)


def _dil_heads(tq, n_heads):
    return max(1, min(n_heads, 1024 // tq))


def _tile_up(parts, hb, per):
    cols = [jnp.concatenate(parts[j * per:(j + 1) * per], axis=0) for j in range(hb)]
    return cols[0] if hb == 1 else jnp.concatenate(cols, axis=1)


def _dilated_fwd(q, k, v, voff, n_heads, name):
    d, m, _ = q.shape
    tq = _dil_tq(m)
    nq = m // tq
    per = tq // WINDOW
    hb = _dil_heads(tq, n_heads)
    scale = HEAD ** -0.5
    blocks = [(j, b) for j in range(hb) for b in range(per)]
    lanes = [slice(j * HEAD, (j + 1) * HEAD) for j in range(hb)]

    def body(q_ref, kc_ref, kp_ref, vc_ref, vp_ref, o_ref, l_ref):
        n = pl.program_id(2)
        kk = jnp.concatenate([kp_ref[...], kc_ref[...]], axis=0)
        vv = jnp.concatenate([vp_ref[...], vc_ref[...]], axis=0)
        row = lax.broadcasted_iota(jnp.int32, (WINDOW, 2 * WINDOW), 0)
        col = lax.broadcasted_iota(jnp.int32, (WINDOW, 2 * WINDOW), 1)
        band = (col >= row) & (col <= row + WINDOW)
        q = q_ref[...]
        rows = [slice(b * WINDOW, (b + 1) * WINDOW) for b in range(per)]
        keys = [slice(b * WINDOW, (b + 2) * WINDOW) for b in range(per)]
        s = [lax.dot_general(q[rows[b], lanes[j]], kk[keys[b], lanes[j]], NT_DIMS,
                             preferred_element_type=F32) * scale for j, b in blocks]
        es, outs, lses = [], [], []
        for (j, b), sb in zip(blocks, s):
            valid = band if b else band & ((n > 0) | (col >= WINDOW))
            sb = jnp.where(valid, sb, NEG)
            mx = jnp.max(sb, axis=-1, keepdims=True)
            e = jnp.exp(sb - mx)
            den = jnp.sum(e, axis=-1, keepdims=True)
            es.append((e.astype(BF16), den))
            lses.append(jnp.broadcast_to(mx + jnp.log(den), (WINDOW, HEAD)))
        for (j, b), (e, den) in zip(blocks, es):
            outs.append(jnp.dot(e, vv[keys[b], lanes[j]], preferred_element_type=F32) / den)
        o_ref[...] = _tile_up(outs, hb, per)
        l_ref[...] = _tile_up(lses, hb, per)

    assert voff % hb == 0
    cur = lambda off: pl.BlockSpec((None, tq, hb * HEAD), lambda r, h, n: (r, n, off // hb + h))
    prev = lambda off: pl.BlockSpec((None, WINDOW, hb * HEAD),
                                    lambda r, h, n: (r, jnp.maximum(n * per - 1, 0), off // hb + h))
    out = jax.ShapeDtypeStruct((d, m, n_heads * HEAD), F32)
    return pl.pallas_call(
        body, name=name, grid=(d, n_heads // hb, nq),
        in_specs=[cur(0), cur(0), prev(0), cur(voff), prev(voff)],
        out_specs=[cur(0), cur(0)], out_shape=[out, out],
        compiler_params=_params(("parallel",) * 3, 32),
    )(q, k, k, v, v)


def _dilated_bwd(q, k, v, voff, dy, stats, n_heads, name):
    d, m, _ = q.shape
    tq = _dil_tq(m)
    nq = m // tq
    per = tq // WINDOW
    last_blk = m // WINDOW - 1
    hb = _dil_heads(tq, n_heads)
    scale = HEAD ** -0.5
    blocks = [(j, b) for j in range(hb) for b in range(per)]
    lanes = [slice(j * HEAD, (j + 1) * HEAD) for j in range(hb)]

    def body(qc_ref, qn_ref, kc_ref, kp_ref, vc_ref, vp_ref, dyc_ref, dyn_ref, sc_ref, sn_ref,
             dq_ref, dk_ref, dv_ref):
        n = pl.program_id(2)
        kk = jnp.concatenate([kp_ref[...], kc_ref[...]], axis=0)
        vv = jnp.concatenate([vp_ref[...], vc_ref[...]], axis=0)
        qq = jnp.concatenate([qc_ref[...], qn_ref[...]], axis=0)
        dyy = jnp.concatenate([dyc_ref[...], dyn_ref[...]], axis=0)
        st = jnp.concatenate([sc_ref[...], sn_ref[...]], axis=0)
        half = HEAD // 2
        row = lax.broadcasted_iota(jnp.int32, (WINDOW, 2 * WINDOW), 0)
        col = lax.broadcasted_iota(jnp.int32, (WINDOW, 2 * WINDOW), 1)
        band = (col >= row) & (col <= row + WINDOW)
        rows = [slice(b * WINDOW, (b + 1) * WINDOW) for b in range(per)]
        wide = [slice(b * WINDOW, (b + 2) * WINDOW) for b in range(per)]
        nt = lambda a, b: lax.dot_general(a, b, NT_DIMS, preferred_element_type=F32)
        s = [nt(qq[rows[b], lanes[j]], kk[wide[b], lanes[j]]) * scale for j, b in blocks]
        dp = [nt(dyy[rows[b], lanes[j]], vv[wide[b], lanes[j]]) for j, b in blocks]
        ds = []
        for (j, b), sb, dpb in zip(blocks, s, dp):
            valid = band if b else band & ((n > 0) | (col >= WINDOW))
            stb = st[rows[b], lanes[j]]
            p = jnp.where(valid, jnp.exp(jnp.minimum(sb - stb[:, :1], 0.0)), 0.0)
            ds.append((p * (dpb - stb[:, half:half + 1]) * scale).astype(BF16))
        dq = [jnp.dot(dsb, kk[wide[b], lanes[j]], preferred_element_type=F32) for (j, b), dsb in zip(blocks, ds)]
        kc, vc = kc_ref[...], vc_ref[...]
        s2 = [nt(kc[rows[b], lanes[j]], qq[wide[b], lanes[j]]) * scale for j, b in blocks]
        dp2 = [nt(vc[rows[b], lanes[j]], dyy[wide[b], lanes[j]]) for j, b in blocks]
        ds2, p2 = [], []
        for (j, b), sb, dpb in zip(blocks, s2, dp2):
            valid = band if b < per - 1 else band & ((n < nq - 1) | (col < WINDOW))
            st_t = st[wide[b], lanes[j]].T
            pb = jnp.where(valid, jnp.exp(jnp.minimum(sb - st_t[:1], 0.0)), 0.0)
            ds2.append((pb * (dpb - st_t[half:half + 1]) * scale).astype(BF16))
            p2.append(pb.astype(BF16))
        dk = [jnp.dot(x, qq[wide[b], lanes[j]], preferred_element_type=F32) for (j, b), x in zip(blocks, ds2)]
        dv = [jnp.dot(x, dyy[wide[b], lanes[j]], preferred_element_type=F32) for (j, b), x in zip(blocks, p2)]
        dq_ref[...] = _tile_up(dq, hb, per)
        dk_ref[...] = _tile_up(dk, hb, per)
        dv_ref[...] = _tile_up(dv, hb, per)

    assert voff % hb == 0
    cur = lambda off: pl.BlockSpec((None, tq, hb * HEAD), lambda r, h, n: (r, n, off // hb + h))
    prev = lambda off: pl.BlockSpec((None, WINDOW, hb * HEAD),
                                    lambda r, h, n: (r, jnp.maximum(n * per - 1, 0), off // hb + h))
    nxt = lambda off: pl.BlockSpec((None, WINDOW, hb * HEAD),
                                   lambda r, h, n: (r, jnp.minimum((n + 1) * per, last_blk), off // hb + h))
    out = jax.ShapeDtypeStruct((d, m, n_heads * HEAD), F32)
    return pl.pallas_call(
        body, name=name, grid=(d, n_heads // hb, nq),
        in_specs=[cur(0), nxt(0), cur(0), prev(0), cur(voff), prev(voff), cur(0), nxt(0), cur(0), nxt(0)],
        out_specs=[cur(0)] * 3, out_shape=[out] * 3,
        compiler_params=_params(("parallel",) * 3, 40),
    )(q, q, k, k, v, v, dy, dy, stats, stats)


def _mix_fwd(os_, ls_, name):
    tm = 256
    s, w = os_[0].shape
    n = len(os_)

    def fn(*vals):
        o, l = vals[:n], vals[n:]
        mx = functools.reduce(jnp.maximum, l)
        e = [jnp.exp(x - mx) for x in l]
        den = functools.reduce(jnp.add, e)
        y = functools.reduce(jnp.add, [ei * oi for ei, oi in zip(e, o)]) / den
        return y, mx + jnp.log(den)

    return _ew(fn, [_row_spec(a, tm) for a in list(os_) + list(ls_)],
               [((s, w), BF16, (tm, w), lambda i: (i, 0)), ((s, w), F32, (tm, w), lambda i: (i, 0))],
               (s // tm,), name)


def _mix_stats(dy, y, lse, n_heads, name):
    tm = 256
    s = dy.shape[0]
    w = n_heads * HEAD

    def fn(dys, ys, ls):
        lane = lax.broadcasted_iota(jnp.int32, (tm, HEAD), 1)
        packed = []
        for a, b, l in zip(_heads(dys, n_heads), _heads(ys, n_heads), _heads(ls, n_heads)):
            delta = jnp.sum(a * b.astype(F32), axis=-1, keepdims=True)
            packed.append(jnp.where(lane < HEAD // 2, l, delta))
        return dys, jnp.concatenate(packed, axis=1)

    blk = lambda a: (a, (tm, w), lambda i: (i, 0))
    out = lambda dt: ((s, w), dt, (tm, w), lambda i: (i, 0))
    return _ew(fn, [blk(dy), blk(y), blk(lse)], [out(BF16), out(F32)], (s // tm,), name)


SB_TQ = 1024
SB_TB = 512
SB_TB_FWD = 1024
SB_TK = 256


def _softplus(z):
    return jnp.where(z > 20.0, z, jnp.log(1.0 + jnp.exp(z)))


def _tri(t, cmp):
    rows = lax.broadcasted_iota(jnp.int32, (2 * t, t), 0)
    cols = lax.broadcasted_iota(jnp.int32, (2 * t, t), 1)
    return jnp.where(cmp(jnp.where(rows >= t, rows - t, rows), cols), 1.0, 0.0).astype(BF16)


def _tri_sum(x, tri):
    return jnp.dot(jnp.concatenate(_split_bf16(x), axis=1), tri, preferred_element_type=F32)


def _causal(rows, cols):
    return lax.broadcasted_iota(jnp.int32, (rows, cols), 1) < lax.broadcasted_iota(jnp.int32, (rows, cols), 0)


def _rowsum(x):
    return jnp.broadcast_to(jnp.sum(x, axis=-1, keepdims=True), (x.shape[0], HEAD))


def _over_keys(c, width):
    return jnp.concatenate([c] * (width // HEAD), axis=1)


def _from(x, r0):
    return x if r0 == 0 else x[r0:]


def _add_from(x, r0, upd):
    return x + upd if r0 == 0 else jnp.concatenate([x[:r0], x[r0:] + upd], axis=0)


def _sb_fwd(qkv, qoff, koff, voff, n_heads, name, comm=None):
    s = qkv.shape[0]
    tq, tb, tk = min(SB_TQ, s), min(SB_TB_FWD, s), SB_TK
    scale = HEAD ** -0.5
    host = _Host(comm, (n_heads, s // tq))

    def body(*refs):
        q_ref, k_ref, v_ref = refs[:3]
        comm_refs, ((o_ref, tot_ref), _) = host.split(refs, 3, 2)
        host.stage("first", comm_refs)
        i = pl.program_id(1)
        q = q_ref[...]
        after = _tri(tk, lambda a, b: a > b)

        def block(base, carry, o, diag_off):
            halves = list(reversed(range(tb // tk)))
            starts = [pl.multiple_of(base + h * tk, tk) for h in halves]
            r0s = [0 if diag_off is None else diag_off + h * tk for h in halves]
            masks = [None if diag_off is None else _causal(tq - r0, tk) for r0 in r0s]
            z = [lax.dot_general(_from(q, r0),k_ref[pl.ds(st, tk), :], NT_DIMS, preferred_element_type=F32) * scale
                 for st, r0 in zip(starts, r0s)]
            sp = [_softplus(zz) for zz in z]
            logsig = [zz - ss for zz, ss in zip(z, sp)]
            sp = [ss if m is None else jnp.where(m, ss, 0.0) for ss, m in zip(sp, masks)]
            sfx = [_tri_sum(ss, after) for ss in sp]
            probs = []
            for ls, sx, ss, m, r0 in zip(logsig, sfx, sp, masks, r0s):
                a = jnp.exp(ls - _over_keys(_from(carry, r0), tk) - sx)
                probs.append((a if m is None else jnp.where(m, a, 0.0)).astype(BF16))
                carry = _add_from(carry, r0, _rowsum(ss))
            for a, st, r0 in zip(probs, starts, r0s):
                o = _add_from(o, r0, jnp.dot(a, v_ref[pl.ds(st, tk), :], preferred_element_type=F32))
            return carry, o

        carry, o = jnp.zeros((tq, HEAD), F32), jnp.zeros((tq, HEAD), F32)
        for b in reversed(range(tq // tb)):
            carry, o = block(i * tq + b * tb, carry, o, b * tb)
        below = i * (tq // tb)
        carry, o = lax.fori_loop(0, below, lambda jj, co: block((below - 1 - jj) * tb, co[0], co[1], None),
                                 (carry, o))
        o_ref[...] = o.astype(o_ref.dtype)
        tot_ref[...] = carry
        host.stage("mid", comm_refs)
        host.stage("last", comm_refs)

    t = tq
    full = lambda off: pl.BlockSpec((s, HEAD), lambda h, i: (0, off + h))
    tile_spec = lambda off: pl.BlockSpec((t, HEAD), lambda h, i: (i, off + h))
    out = lambda dt: jax.ShapeDtypeStruct((s, n_heads * HEAD), dt)
    return pl.pallas_call(
        body, name=name, grid=(n_heads, s // t),
        in_specs=[tile_spec(qoff), full(koff), full(voff)] + host.in_specs,
        out_specs=[tile_spec(0), tile_spec(0)] + host.out_specs,
        out_shape=[out(BF16), out(F32)] + host.out_shape,
        scratch_shapes=host.scratch,
        compiler_params=_params(("parallel" if comm is None else "arbitrary", "arbitrary"), 40),
    )(qkv, qkv, qkv, *host.ins)


def _sb_bwd(qkv, qoff, koff, voff, do, tot, n_heads, name, comm=None):
    s = qkv.shape[0]
    tq, tb, tk = min(SB_TQ, s), SB_TB, SB_TK
    scale = HEAD ** -0.5
    host = _Host(comm, (n_heads, s // tq))

    def body(*refs):
        q_ref, k_ref, v_ref, do_ref, tot_ref = refs[:5]
        comm_refs, ((dq_ref, dk_ref, dv_ref), _) = host.split(refs, 5, 3)
        host.stage("first", comm_refs)
        i = pl.program_id(1)

        @pl.when(i == 0)
        def _():
            dk_ref[...] = jnp.zeros_like(dk_ref)
            dv_ref[...] = jnp.zeros_like(dv_ref)

        q = q_ref[...]
        do_b = do_ref[...].astype(BF16)
        total = tot_ref[...]
        upto = _tri(tk, lambda a, b: a <= b)
        before = _tri(tk, lambda a, b: a < b)[:tk]
        def block(base, lsum, psum, dq, diag_off):
            halves = list(range(tb // tk))
            starts = [pl.multiple_of(base + h * tk, tk) for h in halves]
            r0s = [0 if diag_off is None else diag_off + h * tk for h in halves]
            masks = [None if diag_off is None else _causal(tq - r0, tk) for r0 in r0s]
            keep = lambda x, m: x if m is None else jnp.where(m, x, 0.0)
            ks = [k_ref[pl.ds(st, tk), :] for st in starts]
            z = [lax.dot_general(_from(q, r0),kj, NT_DIMS, preferred_element_type=F32) * scale
                 for kj, r0 in zip(ks, r0s)]
            da = [lax.dot_general(_from(do_b, r0), v_ref[pl.ds(st, tk), :], NT_DIMS, preferred_element_type=F32)
                  for st, r0 in zip(starts, r0s)]
            sp = [_softplus(zz) for zz in z]
            logsig = [zz - ss for zz, ss in zip(z, sp)]
            sp = [keep(ss, m) for ss, m in zip(sp, masks)]
            pre = [_tri_sum(ss, upto) for ss in sp]
            probs, p = [], []
            for ls, px, ss, m, dd, r0 in zip(logsig, pre, sp, masks, da, r0s):
                a = keep(jnp.exp(ls - (_over_keys(_from(total, r0) - _from(lsum, r0), tk) - px)), m)
                probs.append(a.astype(BF16))
                p.append(a * dd)
                lsum = _add_from(lsum, r0, _rowsum(ss))
            cs = [jnp.dot(pp.astype(BF16), before, preferred_element_type=F32) for pp in p]
            dzs = []
            for ls, pp, cc, m, r0 in zip(logsig, p, cs, masks, r0s):
                c_all = _over_keys(_from(psum, r0), tk) + cc
                dzs.append(keep((pp - (pp + c_all) * jnp.exp(ls)) * scale, m).astype(BF16))
                psum = _add_from(psum, r0, _rowsum(pp))
            for dz_b, kj, r0 in zip(dzs, ks, r0s):
                dq = _add_from(dq, r0, jnp.dot(dz_b, kj, preferred_element_type=F32))
            for dz_b, a, st, r0 in zip(dzs, probs, starts, r0s):
                dk_ref[pl.ds(st, tk), :] += lax.dot_general(dz_b, _from(q, r0), TN_DIMS, preferred_element_type=F32)
                dv_ref[pl.ds(st, tk), :] += lax.dot_general(a, _from(do_b, r0), TN_DIMS,
                                                            preferred_element_type=F32)
            return lsum, psum, dq

        zero = jnp.zeros((tq, HEAD), F32)
        state = lax.fori_loop(0, i * (tq // tb), lambda j, c: block(j * tb, c[0], c[1], c[2], None),
                              (zero, zero, jnp.zeros((tq, HEAD), F32)))
        for b in range(tq // tb):
            state = block(i * tq + b * tb, *state, b * tb)
        dq_ref[...] = state[2].astype(dq_ref.dtype)
        host.stage("mid", comm_refs)
        host.stage("last", comm_refs)

    t = tq
    full = lambda off: pl.BlockSpec((s, HEAD), lambda h, i: (0, off + h))
    tile_spec = lambda off: pl.BlockSpec((t, HEAD), lambda h, i: (i, off + h))
    w = n_heads * HEAD
    return pl.pallas_call(
        body, name=name, grid=(n_heads, s // t),
        in_specs=[tile_spec(qoff), full(koff), full(voff), tile_spec(0), tile_spec(0)] + host.in_specs,
        out_specs=[tile_spec(0), full(0), full(0)] + host.out_specs,
        out_shape=[jax.ShapeDtypeStruct((s, w), BF16), jax.ShapeDtypeStruct((s, w), F32),
                   jax.ShapeDtypeStruct((s, w), F32)] + host.out_shape,
        scratch_shapes=host.scratch,
        compiler_params=_params(("parallel" if comm is None else "arbitrary", "arbitrary"), 48),
    )(qkv, qkv, qkv, do, tot, *host.ins)


def _coords():
    return lax.axis_index("x"), lax.axis_index("y"), lax.axis_index("c")


def _gather_plan(shards):
    n = len(shards)

    def run(stage, ins, outs, sems):
        send_sems, recv_sems, local_sems = sems
        x, y, c = _coords()
        me, sibling = (x, y, c), (x, y, 1 - c)
        chips = [(1 - x, y), (x, 1 - y), (1 - x, 1 - y)]

        def copy(w, k, block, to, src=None):
            dst = outs[w].at[4 * block[0] + 2 * block[1] + block[2]]
            return pltpu.make_async_remote_copy(
                src_ref=dst if src is None else src, dst_ref=dst,
                send_sem=send_sems.at[7 * w + k], recv_sem=recv_sems.at[7 * w + k],
                device_id=to, device_id_type=MESH)

        def mine():
            return [pltpu.make_async_copy(ins[w], outs[w].at[4 * x + 2 * y + c], local_sems.at[w]) for w in range(n)]

        def first():
            cps = []
            for w in range(n):
                cps.append(copy(w, 0, me, sibling, src=ins[w]))
                cps += [copy(w, 1 + j, me, (*chip, c), src=ins[w]) for j, chip in enumerate(chips)]
            return cps

        def passed():
            return [copy(w, 4 + j, (*chip, c), sibling) for w in range(n) for j, chip in enumerate(chips)]

        if stage == "first":
            for cp in mine() + first():
                cp.start()
        elif stage == "mid":
            onward = passed()
            for w in range(n):
                for j, chip in enumerate(chips):
                    copy(w, 1 + j, (*chip, c), me).wait_recv()
                    onward[3 * w + j].start()
        else:
            for w in range(n):
                copy(w, 0, sibling, me).wait_recv()
                for j, chip in enumerate(chips):
                    copy(w, 4 + j, (*chip, 1 - c), me).wait_recv()
            for cp in first() + passed():
                cp.wait_send()
            for cp in mine():
                cp.wait()

    return dict(
        ins=list(shards), run=run, stages=("first", "mid", "last"),
        out_shape=[jax.ShapeDtypeStruct((N_DEV,) + a.shape, a.dtype) for a in shards],
        scratch=[pltpu.SemaphoreType.DMA((7 * n,)), pltpu.SemaphoreType.DMA((7 * n,)), pltpu.SemaphoreType.DMA((n,))])


def _scatter_plan(grads):
    n = len(grads)

    def run(stage, ins, outs, sems):
        send_sems, recv_sems = sems
        x, y, c = _coords()
        cps = []
        for w in range(n):
            for mask in range(1, N_DEV):
                px, py, pc = x ^ (mask >> 2), y ^ ((mask >> 1) & 1), c ^ (mask & 1)
                cps.append(pltpu.make_async_remote_copy(
                    src_ref=ins[w].at[4 * px + 2 * py + pc], dst_ref=outs[w].at[mask - 1],
                    send_sem=send_sems.at[7 * w + mask - 1], recv_sem=recv_sems.at[7 * w + mask - 1],
                    device_id=(px, py, pc), device_id_type=MESH))
        for cp in cps:
            if stage == "first":
                cp.start()
            else:
                cp.wait()

    return dict(
        ins=list(grads), run=run, stages=("first", "last"),
        out_shape=[jax.ShapeDtypeStruct((N_DEV - 1,) + a.shape[1:], a.dtype) for a in grads],
        scratch=[pltpu.SemaphoreType.DMA((7 * n,)), pltpu.SemaphoreType.DMA((7 * n,))])


def _run_plan(plan, name):
    n_in, n_out = len(plan["ins"]), len(plan["out_shape"])

    def body(*refs):
        for stage in plan["stages"]:
            plan["run"](stage, refs[:n_in], refs[n_in:n_in + n_out], refs[n_in + n_out:])

    any_spec = pl.BlockSpec(memory_space=pl.ANY)
    return pl.pallas_call(
        body, name=name, in_specs=[any_spec] * n_in, out_specs=[any_spec] * n_out,
        out_shape=plan["out_shape"], scratch_shapes=plan["scratch"],
    )(*plan["ins"])


def _gather_rows(v):
    rows, width = v.shape

    def body(v_ref, out_ref, send_sems, recv_sems):
        x, y, c = _coords()
        out_ref[pl.ds(pl.multiple_of((4 * x + 2 * y + c) * rows, rows), rows), :] = v_ref[...]
        cps = []
        for mask in range(1, N_DEV):
            peer = (x ^ (mask >> 2), y ^ ((mask >> 1) & 1), c ^ (mask & 1))
            dst = out_ref.at[pl.ds(pl.multiple_of((4 * x + 2 * y + c) * rows, rows), rows), :]
            cps.append(pltpu.make_async_remote_copy(
                src_ref=v_ref, dst_ref=dst, send_sem=send_sems.at[mask - 1], recv_sem=recv_sems.at[mask - 1],
                device_id=peer, device_id_type=MESH))
        for cp in cps:
            cp.start()
        for cp in cps:
            cp.wait()

    vmem = pl.BlockSpec(memory_space=pltpu.VMEM)
    return pl.pallas_call(
        body, name="gather_small",
        in_specs=[vmem], out_specs=vmem,
        out_shape=jax.ShapeDtypeStruct((N_DEV * rows, width), F32),
        scratch_shapes=[pltpu.SemaphoreType.DMA((N_DEV - 1,)), pltpu.SemaphoreType.DMA((N_DEV - 1,))],
    )(v)


def _adamw(w, g, m, v):
    m = ADAM_B1 * m + (1.0 - ADAM_B1) * g
    v = ADAM_B2 * v + (1.0 - ADAM_B2) * jnp.square(g)
    m_hat = m / (1.0 - ADAM_B1 ** ADAM_STEP)
    v_hat = v / (1.0 - ADAM_B2 ** ADAM_STEP)
    delta = -ADAM_LR * (m_hat / (jnp.sqrt(v_hat) + ADAM_EPS) + ADAM_WD * w)
    return delta, m, v


def _tile_rows(rows, cols):
    tm = 1 << int(math.log2(max(2 * SUBLANES, (1 << 18) // cols)))
    while rows % tm:
        tm //= 2
    assert tm >= 2 * SUBLANES, (rows, cols)
    return tm


def _reduce_adam(w, m, v, own, got, name):
    r, c = w.shape
    tm = max(2 * SUBLANES, _tile_rows(r, c) // 2)

    def fn(wv, mv, vv, a, *peers):
        g = a
        for pv in peers:
            g = g + pv.astype(F32)
        return (g,) + _adamw(wv, g, mv, vv)

    blk = lambda a: (a, (tm, c), lambda i: (i, 0))
    got_blk = lambda j: (got, (None, tm, c), lambda i, j=j: (j, i, 0))
    return _ew(fn, [blk(w), blk(m), blk(v), blk(own)] + [got_blk(j) for j in range(N_DEV - 1)],
               [((r, c), F32, (tm, c), lambda i: (i, 0))] * 4, (r // tm,), name)


def _small_adam(gathered, params, moms, vels, widths):
    n = len(params)
    total = gathered.shape[1]

    def body(*refs):
        g_ref = refs[0]
        p_refs, m_refs, v_refs = refs[1:1 + n], refs[1 + n:1 + 2 * n], refs[1 + 2 * n:1 + 3 * n]
        sum_ref = refs[1 + 3 * n]
        outs = refs[2 + 3 * n:]
        g = g_ref[0:1, :]
        for p in range(1, N_DEV):
            g = g + g_ref[p * SUBLANES:p * SUBLANES + 1, :]
        sum_ref[...] = g
        off = 0
        for i, wd in enumerate(widths):
            d, m2, v2 = _adamw(p_refs[i][...], g[:, off:off + wd], m_refs[i][...], v_refs[i][...])
            outs[3 * i][...] = d
            outs[3 * i + 1][...] = m2
            outs[3 * i + 2][...] = v2
            off += wd

    vmem = pl.BlockSpec(memory_space=pltpu.VMEM)
    out_shape = [jax.ShapeDtypeStruct((1, total), F32)]
    for wd in widths:
        out_shape += [jax.ShapeDtypeStruct((1, wd), F32)] * 3
    return pl.pallas_call(
        body, name="small_adam",
        in_specs=[vmem] * (1 + 3 * n), out_specs=[vmem] * len(out_shape), out_shape=out_shape,
    )(gathered, *params, *moms, *vels)


def _cast_bf16(a, name):
    r, c = a.shape
    tm = _tile_rows(r, c)
    return _ew(lambda v: v, [(a, (tm, c), lambda i: (i, 0))], [((r, c), BF16, (tm, c), lambda i: (i, 0))],
               (r // tm,), name)[0]


def _fold_loss(parts, name):
    r, c = parts.shape

    def fn(v):
        return jnp.broadcast_to(jnp.sum(jnp.sum(v, axis=0, keepdims=True), axis=1, keepdims=True), (SUBLANES, HEAD))

    return _ew(fn, [_const_spec(parts)], [((SUBLANES, HEAD), F32, (SUBLANES, HEAD), lambda i: (0, 0))], (1,),
               name)[0][0:1]


def kernel(x, p, g_mix, w_in, qn_gain, kn_gain, w_branch_a, w_branch_b, w_out, g_mlp, w_up, w_down, g_ple, w_ple_gate, w_ple_proj, loss_target, m_g_mix, m_w_in, m_qn_gain, m_kn_gain, m_w_branch_a, m_w_branch_b, m_w_out, m_g_mlp, m_w_up, m_w_down, m_g_ple, m_w_ple_gate, m_w_ple_proj, v_g_mix, v_w_in, v_qn_gain, v_kn_gain, v_w_branch_a, v_w_branch_b, v_w_out, v_g_mlp, v_w_up, v_w_down, v_g_ple, v_w_ple_gate, v_w_ple_proj):
    x2 = x[0]
    tgt = loss_target[0]
    s, d = x2.shape
    wd_ = w_branch_a.shape[1]
    nh = wd_ // HEAD
    dff = w_up.shape[1]
    qkv_w = 6 * wd_
    tiles = lambda cols: cols // HEAD

    big = [w_in[0], w_branch_a[0], w_branch_b[0], w_out[0], w_up[0], w_down[0], w_ple_gate[0], w_ple_proj[0]]
    names = ["w_in", "w_branch_a", "w_branch_b", "w_out", "w_up", "w_down", "w_ple_gate", "w_ple_proj"]
    row_sharded = [False, False, False, True, False, True, True, False]
    shards = [_cast_bf16(a, "cast_" + nm) for a, nm in zip(big, names)]
    as_weight = lambda g, rs: g.reshape((1, N_DEV * g.shape[1], g.shape[2])) if rs else g
    as_blocks = lambda g, rs: g.reshape((N_DEV, g.shape[1] // N_DEV, g.shape[2])) if rs else g
    win = _run_plan(_gather_plan(shards[:1]), "all_gather_w_in")[0]

    tm = 1024 if s % 1024 == 0 else s
    tm_in = tk_s = 2048 if s % 2048 == 0 else tm
    tn_of = lambda n: 512 if n % 512 == 0 else (256 if n % 256 == 0 else n)
    tn_in = 256 if win.shape[2] % 256 == 0 else HEAD

    h = _rms_fwd(x2, g_mix, "norm_mix")
    qk_raw = _mm("nn", h, win, tm=tm_in, tn=tn_in, tk=d, out_dtypes=[F32], name="proj_qk",
                 n_off=0, n_cnt=2 * wd_ // tn_in)[0]
    rest = _mm("nn", h, win, tm=4096 if s % 4096 == 0 else tm_in, tn=tn_in, tk=d, out_dtypes=[BF16], name="proj_rest",
               n_off=2 * wd_ // tn_in, n_cnt=(win.shape[0] * win.shape[2] - 2 * wd_) // tn_in)[0]
    o_va, o_qb, o_kb, o_vb, o_ga, o_gb = 0, tiles(wd_), tiles(2 * wd_), tiles(3 * wd_), tiles(4 * wd_), tiles(4 * wd_ + d)
    tabs = _rope_tables(s)
    qa = _headnorm_rope(qk_raw, 0, qn_gain, tabs, nh, "rope_q")
    ka = _headnorm_rope(qk_raw, 1, kn_gain, tabs, nh, "rope_k")
    va = rest[:, :wd_]

    outs, lses = [], []
    for dil in DILATIONS:
        o_g, l_g = _dilated_fwd(_phase_major(qa, dil), _phase_major(ka, dil), _phase_major(va, dil), 0, nh,
                                f"dilated_fwd_{dil}")
        outs.append(_token_major(o_g))
        lses.append(_token_major(l_g))
    ya, lse_all = _mix_fwd(outs, lses, "mix_fwd")
    yb, sb_tot, *others = _sb_fwd(rest, o_qb, o_kb, o_vb, nh, "sb_fwd", comm=_gather_plan(shards[1:]))
    wba, wbb, wout, wup, wdown, wgate, wproj = [as_weight(g, rs) for g, rs in zip(others, row_sharded[1:])]

    tn_d = tn_of(wba.shape[2])
    za = _mm("nn", ya, wba, tm=tm, tn=tn_d, tk=wd_, out_dtypes=[BF16], name="branch_a")[0]

    def merge(acc, zav, gav, gbv):
        return _sigmoid(gav.astype(F32)) * zav.astype(F32) + _sigmoid(gbv.astype(F32)) * acc, acc

    merged, zb = _mm("nn", yb, wbb, tm=tm, tn=tn_d, tk=wd_, out_dtypes=[BF16, BF16], name="branch_b_merge",
                     epilogue=merge, extras=[(za, 0), (rest, o_ga * HEAD // tn_d), (rest, o_gb * HEAD // tn_d)])
    x1 = _mm("nn", merged, wout, tm=tm, tn=512, tk=d, out_dtypes=[F32], name="out_proj",
             epilogue=lambda acc, xv: (acc + xv,), extras=[(x2, 0)])[0]

    hm = _rms_fwd(x1, g_mlp, "norm_mlp")
    tn_u = min(wup.shape[2], 1024)
    u, act = _mm("nn", hm, wup, tm=tm, tn=tn_u, tk=d, out_dtypes=[BF16, BF16], name="mlp_up",
                 epilogue=lambda acc: (acc, jnp.square(jnp.maximum(acc, 0.0))))
    x3 = _mm("nn", act, wdown, tm=tm, tn=1024, tk=min(dff, 2048), out_dtypes=[F32], name="mlp_down",
             epilogue=lambda acc, xv: (acc + xv,), extras=[(x1, 0)])[0]

    hp = _rms_fwd(x3, g_ple, "norm_ple")
    p_b = _cast_bf16(p[0, 0], "cast_p")
    pp = _mm("nn", p_b, wproj, tm=tm, tn=tn_of(wproj.shape[2]), tk=p_b.shape[1], out_dtypes=[BF16],
             name="ple_proj")[0]

    def head(acc, ppv, xv, tv):
        sg = _sigmoid(acc)
        ppf = ppv.astype(F32)
        err = xv + ppf * sg - tv
        dy = err / d
        sq = jnp.square(err)
        return dy, dy * sg, dy * ppf * sg * (1.0 - sg), sq.reshape(-1, SUBLANES, sq.shape[-1]).sum(axis=0)

    n_i = s // tm
    dy, d_pp, d_gt, sq_parts = _mm(
        "nn", hp, wgate, tm=tm, tn=512, tk=d, out_dtypes=[F32, BF16, BF16], name="ple_gate_loss", epilogue=head,
        extras=[(pp, 0), (x3, 0), (tgt, 0)],
        extra_outs=[((n_i * SUBLANES, d), F32, (SUBLANES, 512), lambda i, j: (i, j))])
    loss_vec = _fold_loss(sq_parts, "loss_fold") * 0.5 / d

    both = [F32, BF16]
    g_wproj = _mm("tn", p_b, d_pp, tm=p_b.shape[1], tn=tn_of(wproj.shape[2]), tk=tk_s, out_dtypes=both,
                  name="grad_w_ple_proj", out_nb=N_DEV)
    g_wgate = _mm("tn", hp, d_gt, tm=1024, tn=1024, tk=tk_s, out_dtypes=both, name="grad_w_ple_gate")
    d_hp = _mm("nt", d_gt, wgate, tm=tm, tn=512, tk=d, out_dtypes=[F32], name="d_hp")[0]
    dx3, dx3_b, g_gple = _rms_bwd(d_hp, x3, g_ple, dy, "norm_ple_bwd")

    d_u = _mm("nt", dx3_b, wdown, tm=tm, tn=1024, tk=d, out_dtypes=[BF16], name="d_u",
              epilogue=lambda acc, uv: (acc * (2.0 * jnp.maximum(uv.astype(F32), 0.0)),), extras=[(u, 0)])[0]
    g_wdown = _mm("tn", act, dx3_b, tm=1024, tn=1024, tk=tk_s, out_dtypes=both, name="grad_w_down")
    g_wup = _mm("tn", hm, d_u, tm=1024, tn=wup.shape[2], tk=tk_s, out_dtypes=both, name="grad_w_up", out_nb=N_DEV)
    d_hm = _mm("nt", d_u, wup, tm=tm, tn=min(d, 2048), tk=wup.shape[2], out_dtypes=[F32], name="d_hm")[0]
    dx1, dx1_b, g_gmlp = _rms_bwd(d_hm, x1, g_mlp, dx3, "norm_mlp_bwd")

    def unmerge(acc, gav, gbv, zav, zbv):
        sa, sb = _sigmoid(gav.astype(F32)), _sigmoid(gbv.astype(F32))
        return acc * sa, acc * sb, acc * zav.astype(F32) * sa * (1.0 - sa), acc * zbv.astype(F32) * sb * (1.0 - sb)

    d_za, d_zb, d_ga, d_gb = _mm(
        "nt", dx1_b, wout, tm=tm, tn=512, tk=d, out_dtypes=[BF16] * 4, name="d_merged", epilogue=unmerge,
        extras=[(rest, o_ga * HEAD // 512), (rest, o_gb * HEAD // 512), (za, 0), (zb, 0)])
    g_wout = _mm("tn", merged, dx1_b, tm=1024, tn=1024, tk=tk_s, out_dtypes=both, name="grad_w_out")
    g_wba = _mm("tn", ya, d_za, tm=wd_, tn=wba.shape[2], tk=tk_s, out_dtypes=both, name="grad_w_branch_a",
                out_nb=N_DEV)
    g_wbb = _mm("tn", yb, d_zb, tm=wd_, tn=wbb.shape[2], tk=tk_s, out_dtypes=both, name="grad_w_branch_b",
                out_nb=N_DEV)
    d_ya = _mm("nt", d_za, wba, tm=tm, tn=wd_, tk=wba.shape[2], out_dtypes=[F32], name="d_ya")[0]
    d_yb = _mm("nt", d_zb, wbb, tm=tm, tn=wd_, tk=wbb.shape[2], out_dtypes=[F32], name="d_yb")[0]

    grads = [None, g_wba, g_wbb, g_wout, g_wup, g_wdown, g_wgate, g_wproj]
    early = _scatter_plan([as_blocks(g[1], rs) for g, rs in zip(grads[1:], row_sharded[1:])])
    d_qb, d_kb, d_vb, *got = _sb_bwd(rest, o_qb, o_kb, o_vb, d_yb, sb_tot, nh, "sb_bwd", comm=early)
    d_ya_b, stats = _mix_stats(d_ya, ya, lse_all, nh, "mix_stats")
    dqs, dks, dvs = [], [], []
    for dil in DILATIONS:
        dq_g, dk_g, dv_g = _dilated_bwd(
            _phase_major(qa, dil), _phase_major(ka, dil), _phase_major(va, dil), 0, _phase_major(d_ya_b, dil),
            _phase_major(stats, dil), nh, f"dilated_bwd_{dil}")
        dqs.append(_token_major(dq_g))
        dks.append(_token_major(dk_g))
        dvs.append(_token_major(dv_g))
    d_qa, g_qn = _headnorm_rope_bwd(dqs, qk_raw, 0, qn_gain, tabs, nh, "rope_q_bwd")
    d_ka, g_kn = _headnorm_rope_bwd(dks, qk_raw, 1, kn_gain, tabs, nh, "rope_k_bwd")
    tmr = 256
    d_va = _ew(lambda a, b, c: a + b + c, [_row_spec(a, tmr) for a in dvs],
               [((s, wd_), BF16, (tmr, wd_), lambda i: (i, 0))], (s // tmr,), "sum_dv")[0]
    d_proj = jnp.concatenate([d_qa, d_ka, d_va, d_qb, d_kb.astype(BF16), d_vb.astype(BF16), d_ga, d_gb], axis=1)

    grads[0] = _mm("tn", h, d_proj, tm=1024, tn=win.shape[2], tk=tk_s, out_dtypes=both, name="grad_w_in",
                   out_nb=N_DEV)
    d_h, got_in = _mm("nt", d_proj, win, tm=tm, tn=min(d, 2048), tk=win.shape[2], out_dtypes=[F32], name="d_h",
                      comm=_scatter_plan([grads[0][1]]))
    got = [got_in] + got
    dx, _, g_gmix = _rms_bwd(d_h, x2, g_mix, dx1, "norm_mix_bwd")

    cx, cy, cc = _coords()
    me = 4 * cx + 2 * cy + cc
    moms = [m_w_in, m_w_branch_a, m_w_branch_b, m_w_out, m_w_up, m_w_down, m_w_ple_gate, m_w_ple_proj]
    vels = [v_w_in, v_w_branch_a, v_w_branch_b, v_w_out, v_w_up, v_w_down, v_w_ple_gate, v_w_ple_proj]
    big_out = {}
    for i, nm in enumerate(names):
        own = lax.dynamic_index_in_dim(as_blocks(grads[i][0], row_sharded[i]), me, axis=0, keepdims=False)
        big_out[nm] = [a[None] for a in _reduce_adam(big[i], moms[i][0], vels[i][0], own, got[i], "adam_" + nm)]

    small_names = ["g_mix", "qn_gain", "kn_gain", "g_mlp", "g_ple"]
    small_p = [g_mix, qn_gain, kn_gain, g_mlp, g_ple]
    small_m = [m_g_mix, m_qn_gain, m_kn_gain, m_g_mlp, m_g_ple]
    small_v = [v_g_mix, v_qn_gain, v_kn_gain, v_g_mlp, v_g_ple]
    small_g = [g_gmix, g_qn, g_kn, g_gmlp, g_gple]
    widths = [a.shape[1] for a in small_p]
    vec = jnp.concatenate(small_g + [loss_vec], axis=1)
    vec = jnp.pad(vec, ((0, SUBLANES - 1), (0, 0)))
    res = _small_adam(_gather_rows(vec), small_p, small_m, small_v, widths)
    summed = res[0]
    small_out, off = {}, 0
    for i, nm in enumerate(small_names):
        small_out[nm] = [summed[:, off:off + widths[i]]] + list(res[1 + 3 * i:4 + 3 * i])
        off += widths[i]
    loss = summed[0, off]

    order = ["g_mix", "w_in", "qn_gain", "kn_gain", "w_branch_a", "w_branch_b", "w_out", "g_mlp", "w_up", "w_down",
             "g_ple", "w_ple_gate", "w_ple_proj"]
    table = {**big_out, **small_out}
    result = [loss, dx[None]]
    for kind in range(4):
        result += [table[nm][kind] for nm in order]
    return tuple(result)
```

```python
import functools
import math

import jax
import jax.numpy as jnp
from jax import lax
from jax.experimental import pallas as pl
from jax.experimental.pallas import tpu as pltpu

F32 = jnp.float32
BF16 = jnp.bfloat16
MESH = pl.DeviceIdType.MESH

HEAD = 128
WINDOW = 128
DILATIONS = (1, 4, 16)
ROT = HEAD // 4
ROPE_THETA = 500000.0
EPS = 1e-6
NEG = -1e30
N_DEV = 8

ADAM_LR = 0.001
ADAM_B1 = 0.9
ADAM_B2 = 0.999
ADAM_EPS = 1e-08
ADAM_WD = 0.01
ADAM_STEP = 10

SUBLANES = 8
VMEM_CAP_MB = 56

NT_DIMS = (((1,), (1,)), ((), ()))
TN_DIMS = (((0,), (0,)), ((), ()))


def _params(semantics, vmem_mb):
    return pltpu.CompilerParams(dimension_semantics=semantics, vmem_limit_bytes=min(vmem_mb, VMEM_CAP_MB) << 20)


def _sigmoid(x):
    return 0.5 + 0.5 * jnp.tanh(0.5 * x)


def _split_bf16(x):
    hi = x.astype(BF16)
    lo = (x - hi.astype(F32)).astype(BF16)
    return hi, lo


class _Host:
    def __init__(self, comm, grid):
        self.comm, self.grid = comm, grid
        any_spec = pl.BlockSpec(memory_space=pl.ANY)
        self.ins = list(comm["ins"]) if comm else []
        self.out_shape = list(comm["out_shape"]) if comm else []
        self.scratch = list(comm["scratch"]) if comm else []
        self.in_specs = [any_spec] * len(self.ins)
        self.out_specs = [any_spec] * len(self.out_shape)

    def split(self, refs, n_in, n_out):
        pos = n_in
        c_in = refs[pos:pos + len(self.ins)]
        pos += len(self.ins)
        outs = refs[pos:pos + n_out]
        pos += n_out
        c_out = refs[pos:pos + len(self.out_shape)]
        pos += len(self.out_shape)
        own = len(refs) - pos - len(self.scratch)
        return (c_in, c_out, refs[pos + own:]), (outs, refs[pos:pos + own])

    def stage(self, which, comm_refs):
        if self.comm is None or which not in self.comm["stages"]:
            return
        grid = self.grid
        at = {"first": [0] * len(grid), "mid": [grid[0] // 2] + [0] * (len(grid) - 1),
              "last": [g - 1 for g in grid]}[which]
        cond = functools.reduce(jnp.logical_and, [pl.program_id(ax) == v for ax, v in enumerate(at)])

        @pl.when(cond)
        def _():
            self.comm["run"](which, *comm_refs)


def _ew(fn, ins, outs, grid, name, colsums=(), vmem_mb=40):
    n_in, n_out, n_cs = len(ins), len(outs), len(colsums)
    steps = math.prod(grid)

    def body(*refs):
        in_refs = refs[:n_in]
        out_refs = refs[n_in:n_in + n_out]
        cs_refs = refs[n_in + n_out:n_in + n_out + n_cs]
        acc_refs = refs[n_in + n_out + n_cs:]
        vals = fn(*[r[...] for r in in_refs])
        if not isinstance(vals, tuple):
            vals = (vals,)
        for r, v in zip(out_refs, vals[:n_out]):
            r[...] = v.astype(r.dtype)
        if n_cs:
            step = pl.program_id(0)
            for ax in range(1, len(grid)):
                step = step * grid[ax] + pl.program_id(ax)
            for acc, cs, v in zip(acc_refs, cs_refs, vals[n_out:]):
                part = v.reshape(-1, SUBLANES, v.shape[-1]).sum(axis=0)

                @pl.when(step == 0)
                def _(acc=acc, part=part):
                    acc[...] = part

                @pl.when(step > 0)
                def _(acc=acc, part=part):
                    acc[...] += part

                @pl.when(step == steps - 1)
                def _(acc=acc, cs=cs):
                    cs[...] = acc[...].sum(axis=0, keepdims=True)

    out_shape = [jax.ShapeDtypeStruct(s, d) for s, d, _, _ in outs]
    out_specs = [pl.BlockSpec(b, m) for _, _, b, m in outs]
    for w in colsums:
        out_shape.append(jax.ShapeDtypeStruct((1, w), F32))
        out_specs.append(pl.BlockSpec((1, w), lambda *_: (0, 0)))
    sem = ("arbitrary",) * len(grid) if n_cs else ("parallel",) * len(grid)
    res = pl.pallas_call(
        body, name=name, grid=grid,
        in_specs=[pl.BlockSpec(b, m) for _, b, m in ins],
        out_specs=out_specs, out_shape=out_shape,
        scratch_shapes=[pltpu.VMEM((SUBLANES, w), F32) for w in colsums],
        compiler_params=_params(sem, vmem_mb),
    )(*[a for a, _, _ in ins])
    return res


def _row_spec(a, tm):
    return (a, (tm, a.shape[1]), lambda i: (i, 0))


def _const_spec(a):
    return (a, a.shape, lambda *_: (0,) * a.ndim)


def _mm(mode, a, b, *, tm, tn, tk, out_dtypes, name, epilogue=None, extras=(), n_off=0, n_cnt=None,
        out_nb=1, extra_outs=(), vmem_mb=52, comm=None):
    if mode == "nn":
        m, kdim = a.shape
        nb, _, n = b.shape
        npb = n // tn
        ncols = nb * n
        n_tiles = (ncols // tn) if n_cnt is None else n_cnt
        a_spec = pl.BlockSpec((tm, tk), lambda i, j, k: (i, k))
        b_spec = pl.BlockSpec((None, tk, tn), lambda i, j, k: ((j + n_off) // npb, k, (j + n_off) % npb))
        dims = (((1,), (0,)), ((), ()))
    elif mode == "nt":
        m, kdim = a.shape
        nb, nout, n = b.shape
        kpb = n // tk
        n_tiles = nout // tn
        a_spec = pl.BlockSpec((tm, tk), lambda i, j, k: (i, k))
        b_spec = pl.BlockSpec((None, tn, tk), lambda i, j, k: (k // kpb, j, k % kpb))
        dims = NT_DIMS
    else:
        kdim, m = a.shape
        ncols = b.shape[1]
        n_tiles = ncols // tn
        a_spec = pl.BlockSpec((tk, tm), lambda i, j, k: (k, i))
        b_spec = pl.BlockSpec((tk, tn), lambda i, j, k: (k, j))
        dims = TN_DIMS
    nk = kdim // tk
    assert kdim % tk == 0 and m % tm == 0
    grid = (m // tm, n_tiles, nk)
    n_ex, n_out = len(extras), len(out_dtypes) + len(extra_outs)
    host = _Host(comm, grid)

    def body(*refs):
        a_ref, b_ref = refs[0], refs[1]
        ex_refs = refs[2:2 + n_ex]
        comm_refs, (out_refs, scratch) = host.split(refs, 2 + n_ex, n_out)
        host.stage("first", comm_refs)

        def finish(acc):
            vals = (acc,) * n_out if epilogue is None else epilogue(acc, *[r[...] for r in ex_refs])
            for r, v in zip(out_refs, vals):
                r[...] = v.astype(r.dtype)

        def part():
            return lax.dot_general(a_ref[...], b_ref[...], dims, preferred_element_type=F32)

        if nk == 1:
            finish(part())
        else:
            acc_ref = scratch[0]
            k = pl.program_id(2)

            @pl.when(k == 0)
            def _():
                acc_ref[...] = part()

            @pl.when((k > 0) & (k < nk - 1))
            def _():
                acc_ref[...] += part()

            @pl.when(k == nk - 1)
            def _():
                finish(acc_ref[...] + part())

        host.stage("mid", comm_refs)
        host.stage("last", comm_refs)

    if mode == "tn":
        npo = (ncols // out_nb) // tn
        out_shape = [jax.ShapeDtypeStruct((out_nb, m, ncols // out_nb), d) for d in out_dtypes]
        out_specs = [pl.BlockSpec((None, tm, tn), lambda i, j, k: (j // npo, i, j % npo)) for _ in out_dtypes]
    else:
        out_shape = [jax.ShapeDtypeStruct((m, n_tiles * tn), d) for d in out_dtypes]
        out_specs = [pl.BlockSpec((tm, tn), lambda i, j, k: (i, j)) for _ in out_dtypes]
    for s, d, blk, imap in extra_outs:
        out_shape.append(jax.ShapeDtypeStruct(s, d))
        out_specs.append(pl.BlockSpec(blk, lambda i, j, k, imap=imap: imap(i, j)))
    ex_specs = [pl.BlockSpec((tm, tn), lambda i, j, k, off=off: (i, j + off)) for _, off in extras]
    sem = ("parallel", "parallel", "arbitrary") if comm is None else ("arbitrary",) * 3
    return pl.pallas_call(
        body, name=name, grid=grid,
        in_specs=[a_spec, b_spec] + ex_specs + host.in_specs,
        out_specs=out_specs + host.out_specs, out_shape=out_shape + host.out_shape,
        scratch_shapes=([pltpu.VMEM((tm, tn), F32)] if nk > 1 else []) + host.scratch,
        compiler_params=_params(sem, vmem_mb),
    )(a, b, *[e for e, _ in extras], *host.ins)


def _rms_fwd(x, g, name):
    tm = 256

    def fn(xv, gv):
        r = lax.rsqrt(jnp.mean(xv * xv, axis=-1, keepdims=True) + EPS)
        return xv * r * gv

    return _ew(fn, [_row_spec(x, tm), _const_spec(g)], [(x.shape, BF16, (tm, x.shape[1]), lambda i: (i, 0))],
               (x.shape[0] // tm,), name)[0]


def _rms_bwd(dh, x, g, res, name):
    tm = 256
    d = x.shape[1]

    def fn(dhv, xv, gv, rv):
        r = lax.rsqrt(jnp.mean(xv * xv, axis=-1, keepdims=True) + EPS)
        xh = xv * r
        dyg = dhv * gv
        dx = rv + r * (dyg - xh * jnp.mean(dyg * xh, axis=-1, keepdims=True))
        return dx, dx, dhv * xh

    spec = lambda dt: (x.shape, dt, (tm, d), lambda i: (i, 0))
    return _ew(fn, [_row_spec(dh, tm), _row_spec(x, tm), _const_spec(g), _row_spec(res, tm)],
               [spec(F32), spec(BF16)], (x.shape[0] // tm,), name, colsums=(d,))


def _rope_tables(s):
    half = ROT // 2
    pos = jnp.arange(s, dtype=F32)
    inv = ROPE_THETA ** (-jnp.arange(0, ROT, 2, dtype=F32) / ROT)
    ang = pos[:, None] * inv[None, :]
    cos, sin = jnp.cos(ang), jnp.sin(ang)
    pad = jnp.zeros((s, HEAD - ROT), F32)
    c = jnp.concatenate([cos, cos, pad + 1.0], axis=1)
    a = jnp.concatenate([-sin, jnp.zeros_like(sin), pad], axis=1)
    b = jnp.concatenate([jnp.zeros_like(sin), sin, pad], axis=1)
    return c, a, b


def _heads(x, n_heads):
    return [x[:, h * HEAD:(h + 1) * HEAD] for h in range(n_heads)]


def _headnorm_rope(proj, part, gain, tabs, n_heads, name):
    tm = 512
    s = proj.shape[0]
    w = n_heads * HEAD

    def fn(xs, gv, c, a, b):
        outs = []
        for xv in _heads(xs, n_heads):
            r = lax.rsqrt(jnp.mean(xv * xv, axis=-1, keepdims=True) + EPS)
            y = xv * r * gv
            outs.append(c * y + a * pltpu.roll(y, HEAD - ROT // 2, 1) + b * pltpu.roll(y, ROT // 2, 1))
        return jnp.concatenate(outs, axis=1)

    tab = lambda t: (t, (tm, HEAD), lambda i: (i, 0))
    return _ew(fn, [(proj, (tm, w), lambda i: (i, part)), (gain, (1, HEAD), lambda i: (0, 0))] + [tab(t) for t in tabs],
               [((s, w), BF16, (tm, w), lambda i: (i, 0))], (s // tm,), name)[0]


def _headnorm_rope_bwd(dys, proj, part, gain, tabs, n_heads, name):
    tm = 256
    s = proj.shape[0]
    w = n_heads * HEAD
    n_dy = len(dys)

    def fn(*vals):
        dy_all = vals[0]
        for v in vals[1:n_dy]:
            dy_all = dy_all + v
        xs, gv, c, a, b = vals[n_dy:]
        dxs, dgain = [], None
        for dy, xv in zip(_heads(dy_all, n_heads), _heads(xs, n_heads)):
            dn = c * dy + pltpu.roll(a * dy, ROT // 2, 1) + pltpu.roll(b * dy, HEAD - ROT // 2, 1)
            r = lax.rsqrt(jnp.mean(xv * xv, axis=-1, keepdims=True) + EPS)
            xh = xv * r
            dyg = dn * gv
            dxs.append(r * (dyg - xh * jnp.mean(dyg * xh, axis=-1, keepdims=True)))
            dgain = dn * xh if dgain is None else dgain + dn * xh
        return jnp.concatenate(dxs, axis=1), dgain

    tab = lambda t: (t, (tm, HEAD), lambda i: (i, 0))
    return _ew(fn, [(d, (tm, w), lambda i: (i, 0)) for d in dys]
               + [(proj, (tm, w), lambda i: (i, part)), (gain, (1, HEAD), lambda i: (0, 0))] + [tab(t) for t in tabs],
               [((s, w), BF16, (tm, w), lambda i: (i, 0))], (s // tm,), name, colsums=(HEAD,))


def _phase_major(a, d):
    s, w = a.shape
    if d == 1:
        return a.reshape(1, s, w)
    return a.reshape(s // d, d, w).transpose(1, 0, 2)


def _token_major(a):
    d, m, w = a.shape
    if d == 1:
        return a.reshape(m, w)
    return a.transpose(1, 0, 2).reshape(m * d, w)


DIL_ROWS = 2048


def _dil_tq(m):
    return min(DIL_ROWS, m)


def _dil_heads(tq, n_heads):
    return max(1, min(n_heads, DIL_ROWS // tq))


def _tile_up(parts, hb, per):
    cols = [jnp.concatenate(parts[j * per:(j + 1) * per], axis=0) for j in range(hb)]
    return cols[0] if hb == 1 else jnp.concatenate(cols, axis=1)


def _dilated_fwd(q, k, v, voff, n_heads, name):
    d, m, _ = q.shape
    tq = _dil_tq(m)
    nq = m // tq
    per = tq // WINDOW
    hb = _dil_heads(tq, n_heads)
    scale = HEAD ** -0.5
    blocks = [(j, b) for j in range(hb) for b in range(per)]
    lanes = [slice(j * HEAD, (j + 1) * HEAD) for j in range(hb)]

    def body(q_ref, kc_ref, kp_ref, vc_ref, vp_ref, o_ref, l_ref):
        n = pl.program_id(2)
        kk = jnp.concatenate([kp_ref[...], kc_ref[...]], axis=0)
        vv = jnp.concatenate([vp_ref[...], vc_ref[...]], axis=0)
        row = lax.broadcasted_iota(jnp.int32, (WINDOW, 2 * WINDOW), 0)
        col = lax.broadcasted_iota(jnp.int32, (WINDOW, 2 * WINDOW), 1)
        band = (col >= row) & (col <= row + WINDOW)
        q = q_ref[...]
        rows = [slice(b * WINDOW, (b + 1) * WINDOW) for b in range(per)]
        keys = [slice(b * WINDOW, (b + 2) * WINDOW) for b in range(per)]
        s = [lax.dot_general(q[rows[b], lanes[j]], kk[keys[b], lanes[j]], NT_DIMS,
                             preferred_element_type=F32) * scale for j, b in blocks]
        es, outs, lses = [], [], []
        for (j, b), sb in zip(blocks, s):
            valid = band if b else band & ((n > 0) | (col >= WINDOW))
            sb = jnp.where(valid, sb, NEG)
            mx = jnp.max(sb, axis=-1, keepdims=True)
            e = jnp.exp(sb - mx)
            den = jnp.sum(e, axis=-1, keepdims=True)
            es.append((e.astype(BF16), den))
            lses.append(jnp.broadcast_to(mx + jnp.log(den), (WINDOW, HEAD)))
        for (j, b), (e, den) in zip(blocks, es):
            outs.append(jnp.dot(e, vv[keys[b], lanes[j]], preferred_element_type=F32) / den)
        o_ref[...] = _tile_up(outs, hb, per)
        l_ref[...] = _tile_up(lses, hb, per)

    assert voff % hb == 0
    cur = lambda off: pl.BlockSpec((None, tq, hb * HEAD), lambda r, h, n: (r, n, off // hb + h))
    prev = lambda off: pl.BlockSpec((None, WINDOW, hb * HEAD),
                                    lambda r, h, n: (r, jnp.maximum(n * per - 1, 0), off // hb + h))
    out = jax.ShapeDtypeStruct((d, m, n_heads * HEAD), F32)
    return pl.pallas_call(
        body, name=name, grid=(d, n_heads // hb, nq),
        in_specs=[cur(0), cur(0), prev(0), cur(voff), prev(voff)],
        out_specs=[cur(0), cur(0)], out_shape=[out, out],
        compiler_params=_params(("parallel",) * 3, 32),
    )(q, k, k, v, v)


def _dilated_bwd(q, k, v, voff, dy, stats, n_heads, name):
    d, m, _ = q.shape
    tq = _dil_tq(m)
    nq = m // tq
    per = tq // WINDOW
    last_blk = m // WINDOW - 1
    hb = _dil_heads(tq, n_heads)
    scale = HEAD ** -0.5
    blocks = [(j, b) for j in range(hb) for b in range(per)]
    lanes = [slice(j * HEAD, (j + 1) * HEAD) for j in range(hb)]

    def body(qc_ref, qn_ref, kc_ref, kp_ref, vc_ref, vp_ref, dyc_ref, dyn_ref, sc_ref, sn_ref,
             dq_ref, dk_ref, dv_ref):
        n = pl.program_id(2)
        kk = jnp.concatenate([kp_ref[...], kc_ref[...]], axis=0)
        vv = jnp.concatenate([vp_ref[...], vc_ref[...]], axis=0)
        qq = jnp.concatenate([qc_ref[...], qn_ref[...]], axis=0)
        dyy = jnp.concatenate([dyc_ref[...], dyn_ref[...]], axis=0)
        st = jnp.concatenate([sc_ref[...], sn_ref[...]], axis=0)
        half = HEAD // 2
        row = lax.broadcasted_iota(jnp.int32, (WINDOW, 2 * WINDOW), 0)
        col = lax.broadcasted_iota(jnp.int32, (WINDOW, 2 * WINDOW), 1)
        band = (col >= row) & (col <= row + WINDOW)
        rows = [slice(b * WINDOW, (b + 1) * WINDOW) for b in range(per)]
        wide = [slice(b * WINDOW, (b + 2) * WINDOW) for b in range(per)]
        nt = lambda a, b: lax.dot_general(a, b, NT_DIMS, preferred_element_type=F32)
        s = [nt(qq[rows[b], lanes[j]], kk[wide[b], lanes[j]]) * scale for j, b in blocks]
        dp = [nt(dyy[rows[b], lanes[j]], vv[wide[b], lanes[j]]) for j, b in blocks]
        ds = []
        for (j, b), sb, dpb in zip(blocks, s, dp):
            valid = band if b else band & ((n > 0) | (col >= WINDOW))
            stb = st[rows[b], lanes[j]]
            p = jnp.where(valid, jnp.exp(jnp.minimum(sb - stb[:, :1], 0.0)), 0.0)
            ds.append((p * (dpb - stb[:, half:half + 1]) * scale).astype(BF16))
        dq = [jnp.dot(dsb, kk[wide[b], lanes[j]], preferred_element_type=F32) for (j, b), dsb in zip(blocks, ds)]
        kc, vc = kc_ref[...], vc_ref[...]
        s2 = [nt(kc[rows[b], lanes[j]], qq[wide[b], lanes[j]]) * scale for j, b in blocks]
        dp2 = [nt(vc[rows[b], lanes[j]], dyy[wide[b], lanes[j]]) for j, b in blocks]
        ds2, p2 = [], []
        for (j, b), sb, dpb in zip(blocks, s2, dp2):
            valid = band if b < per - 1 else band & ((n < nq - 1) | (col < WINDOW))
            st_t = st[wide[b], lanes[j]].T
            pb = jnp.where(valid, jnp.exp(jnp.minimum(sb - st_t[:1], 0.0)), 0.0)
            ds2.append((pb * (dpb - st_t[half:half + 1]) * scale).astype(BF16))
            p2.append(pb.astype(BF16))
        dk = [jnp.dot(x, qq[wide[b], lanes[j]], preferred_element_type=F32) for (j, b), x in zip(blocks, ds2)]
        dv = [jnp.dot(x, dyy[wide[b], lanes[j]], preferred_element_type=F32) for (j, b), x in zip(blocks, p2)]
        dq_ref[...] = _tile_up(dq, hb, per)
        dk_ref[...] = _tile_up(dk, hb, per)
        dv_ref[...] = _tile_up(dv, hb, per)

    assert voff % hb == 0
    cur = lambda off: pl.BlockSpec((None, tq, hb * HEAD), lambda r, h, n: (r, n, off // hb + h))
    prev = lambda off: pl.BlockSpec((None, WINDOW, hb * HEAD),
                                    lambda r, h, n: (r, jnp.maximum(n * per - 1, 0), off // hb + h))
    nxt = lambda off: pl.BlockSpec((None, WINDOW, hb * HEAD),
                                   lambda r, h, n: (r, jnp.minimum((n + 1) * per, last_blk), off // hb + h))
    out = jax.ShapeDtypeStruct((d, m, n_heads * HEAD), F32)
    return pl.pallas_call(
        body, name=name, grid=(d, n_heads // hb, nq),
        in_specs=[cur(0), nxt(0), cur(0), prev(0), cur(voff), prev(voff), cur(0), nxt(0), cur(0), nxt(0)],
        out_specs=[cur(0)] * 3, out_shape=[out] * 3,
        compiler_params=_params(("parallel",) * 3, 40),
    )(q, q, k, k, v, v, dy, dy, stats, stats)


def _mix_fwd(os_, ls_, name):
    tm = 256
    s, w = os_[0].shape
    n = len(os_)

    def fn(*vals):
        o, l = vals[:n], vals[n:]
        mx = functools.reduce(jnp.maximum, l)
        e = [jnp.exp(x - mx) for x in l]
        den = functools.reduce(jnp.add, e)
        y = functools.reduce(jnp.add, [ei * oi for ei, oi in zip(e, o)]) / den
        return y, mx + jnp.log(den)

    return _ew(fn, [_row_spec(a, tm) for a in list(os_) + list(ls_)],
               [((s, w), BF16, (tm, w), lambda i: (i, 0)), ((s, w), F32, (tm, w), lambda i: (i, 0))],
               (s // tm,), name)


def _mix_stats(dy, y, lse, n_heads, name):
    tm = 256
    s = dy.shape[0]
    w = n_heads * HEAD

    def fn(dys, ys, ls):
        lane = lax.broadcasted_iota(jnp.int32, (tm, HEAD), 1)
        packed = []
        for a, b, l in zip(_heads(dys, n_heads), _heads(ys, n_heads), _heads(ls, n_heads)):
            delta = jnp.sum(a * b.astype(F32), axis=-1, keepdims=True)
            packed.append(jnp.where(lane < HEAD // 2, l, delta))
        return dys, jnp.concatenate(packed, axis=1)

    blk = lambda a: (a, (tm, w), lambda i: (i, 0))
    out = lambda dt: ((s, w), dt, (tm, w), lambda i: (i, 0))
    return _ew(fn, [blk(dy), blk(y), blk(lse)], [out(BF16), out(F32)], (s // tm,), name)


SB_TQ = 1024
SB_TB = 512
SB_TB_FWD = 1024
SB_TK = 256


def _softplus(z):
    return jnp.where(z > 20.0, z, jnp.log(1.0 + jnp.exp(z)))


def _tri(t, cmp):
    rows = lax.broadcasted_iota(jnp.int32, (2 * t, t), 0)
    cols = lax.broadcasted_iota(jnp.int32, (2 * t, t), 1)
    return jnp.where(cmp(jnp.where(rows >= t, rows - t, rows), cols), 1.0, 0.0).astype(BF16)


def _tri_sum(x, tri):
    return jnp.dot(jnp.concatenate(_split_bf16(x), axis=1), tri, preferred_element_type=F32)


def _causal(rows, cols):
    return lax.broadcasted_iota(jnp.int32, (rows, cols), 1) < lax.broadcasted_iota(jnp.int32, (rows, cols), 0)


def _rowsum(x):
    return jnp.broadcast_to(jnp.sum(x, axis=-1, keepdims=True), (x.shape[0], HEAD))


def _over_keys(c, width):
    return jnp.concatenate([c] * (width // HEAD), axis=1)


def _from(x, r0):
    return x if r0 == 0 else x[r0:]


def _add_from(x, r0, upd):
    return x + upd if r0 == 0 else jnp.concatenate([x[:r0], x[r0:] + upd], axis=0)


def _sb_fwd(qkv, qoff, koff, voff, n_heads, name, comm=None):
    s = qkv.shape[0]
    tq, tb, tk = min(SB_TQ, s), min(SB_TB_FWD, s), SB_TK
    scale = HEAD ** -0.5
    host = _Host(comm, (n_heads, s // tq))

    def body(*refs):
        q_ref, k_ref, v_ref = refs[:3]
        comm_refs, ((o_ref, tot_ref), _) = host.split(refs, 3, 2)
        host.stage("first", comm_refs)
        i = pl.program_id(1)
        q = q_ref[...]
        after = _tri(tk, lambda a, b: a > b)

        def block(base, carry, o, diag_off):
            halves = list(reversed(range(tb // tk)))
            starts = [pl.multiple_of(base + h * tk, tk) for h in halves]
            r0s = [0 if diag_off is None else diag_off + h * tk for h in halves]
            masks = [None if diag_off is None else _causal(tq - r0, tk) for r0 in r0s]
            z = [lax.dot_general(_from(q, r0),k_ref[pl.ds(st, tk), :], NT_DIMS, preferred_element_type=F32) * scale
                 for st, r0 in zip(starts, r0s)]
            sp = [_softplus(zz) for zz in z]
            logsig = [zz - ss for zz, ss in zip(z, sp)]
            sp = [ss if m is None else jnp.where(m, ss, 0.0) for ss, m in zip(sp, masks)]
            sfx = [_tri_sum(ss, after) for ss in sp]
            probs = []
            for ls, sx, ss, m, r0 in zip(logsig, sfx, sp, masks, r0s):
                a = jnp.exp(ls - _over_keys(_from(carry, r0), tk) - sx)
                probs.append((a if m is None else jnp.where(m, a, 0.0)).astype(BF16))
                carry = _add_from(carry, r0, _rowsum(ss))
            for a, st, r0 in zip(probs, starts, r0s):
                o = _add_from(o, r0, jnp.dot(a, v_ref[pl.ds(st, tk), :], preferred_element_type=F32))
            return carry, o

        carry, o = jnp.zeros((tq, HEAD), F32), jnp.zeros((tq, HEAD), F32)
        for b in reversed(range(tq // tb)):
            carry, o = block(i * tq + b * tb, carry, o, b * tb)
        below = i * (tq // tb)
        carry, o = lax.fori_loop(0, below, lambda jj, co: block((below - 1 - jj) * tb, co[0], co[1], None),
                                 (carry, o))
        o_ref[...] = o.astype(o_ref.dtype)
        tot_ref[...] = carry
        host.stage("mid", comm_refs)
        host.stage("last", comm_refs)

    t = tq
    full = lambda off: pl.BlockSpec((s, HEAD), lambda h, i: (0, off + h))
    tile_spec = lambda off: pl.BlockSpec((t, HEAD), lambda h, i: (i, off + h))
    out = lambda dt: jax.ShapeDtypeStruct((s, n_heads * HEAD), dt)
    return pl.pallas_call(
        body, name=name, grid=(n_heads, s // t),
        in_specs=[tile_spec(qoff), full(koff), full(voff)] + host.in_specs,
        out_specs=[tile_spec(0), tile_spec(0)] + host.out_specs,
        out_shape=[out(BF16), out(F32)] + host.out_shape,
        scratch_shapes=host.scratch,
        compiler_params=_params(("parallel" if comm is None else "arbitrary", "arbitrary"), 40),
    )(qkv, qkv, qkv, *host.ins)


def _sb_bwd(qkv, qoff, koff, voff, do, tot, n_heads, name, comm=None):
    s = qkv.shape[0]
    tq, tb, tk = min(SB_TQ, s), SB_TB, SB_TK
    scale = HEAD ** -0.5
    host = _Host(comm, (n_heads, s // tq))

    def body(*refs):
        q_ref, k_ref, v_ref, do_ref, tot_ref = refs[:5]
        comm_refs, ((dq_ref, dk_ref, dv_ref), _) = host.split(refs, 5, 3)
        host.stage("first", comm_refs)
        i = pl.program_id(1)

        @pl.when(i == 0)
        def _():
            dk_ref[...] = jnp.zeros_like(dk_ref)
            dv_ref[...] = jnp.zeros_like(dv_ref)

        q = q_ref[...]
        do_b = do_ref[...].astype(BF16)
        total = tot_ref[...]
        upto = _tri(tk, lambda a, b: a <= b)
        before = _tri(tk, lambda a, b: a < b)[:tk]
        def block(base, lsum, psum, dq, diag_off):
            halves = list(range(tb // tk))
            starts = [pl.multiple_of(base + h * tk, tk) for h in halves]
            r0s = [0 if diag_off is None else diag_off + h * tk for h in halves]
            masks = [None if diag_off is None else _causal(tq - r0, tk) for r0 in r0s]
            keep = lambda x, m: x if m is None else jnp.where(m, x, 0.0)
            ks = [k_ref[pl.ds(st, tk), :] for st in starts]
            z = [lax.dot_general(_from(q, r0),kj, NT_DIMS, preferred_element_type=F32) * scale
                 for kj, r0 in zip(ks, r0s)]
            da = [lax.dot_general(_from(do_b, r0), v_ref[pl.ds(st, tk), :], NT_DIMS, preferred_element_type=F32)
                  for st, r0 in zip(starts, r0s)]
            sp = [_softplus(zz) for zz in z]
            logsig = [zz - ss for zz, ss in zip(z, sp)]
            sp = [keep(ss, m) for ss, m in zip(sp, masks)]
            pre = [_tri_sum(ss, upto) for ss in sp]
            probs, p = [], []
            for ls, px, ss, m, dd, r0 in zip(logsig, pre, sp, masks, da, r0s):
                a = keep(jnp.exp(ls - (_over_keys(_from(total, r0) - _from(lsum, r0), tk) - px)), m)
                probs.append(a.astype(BF16))
                p.append(a * dd)
                lsum = _add_from(lsum, r0, _rowsum(ss))
            cs = [jnp.dot(pp.astype(BF16), before, preferred_element_type=F32) for pp in p]
            dzs = []
            for ls, pp, cc, m, r0 in zip(logsig, p, cs, masks, r0s):
                c_all = _over_keys(_from(psum, r0), tk) + cc
                dzs.append(keep((pp - (pp + c_all) * jnp.exp(ls)) * scale, m).astype(BF16))
                psum = _add_from(psum, r0, _rowsum(pp))
            for dz_b, kj, r0 in zip(dzs, ks, r0s):
                dq = _add_from(dq, r0, jnp.dot(dz_b, kj, preferred_element_type=F32))
            for dz_b, a, st, r0 in zip(dzs, probs, starts, r0s):
                dk_ref[pl.ds(st, tk), :] += lax.dot_general(dz_b, _from(q, r0), TN_DIMS, preferred_element_type=F32)
                dv_ref[pl.ds(st, tk), :] += lax.dot_general(a, _from(do_b, r0), TN_DIMS,
                                                            preferred_element_type=F32)
            return lsum, psum, dq

        zero = jnp.zeros((tq, HEAD), F32)
        state = lax.fori_loop(0, i * (tq // tb), lambda j, c: block(j * tb, c[0], c[1], c[2], None),
                              (zero, zero, jnp.zeros((tq, HEAD), F32)))
        for b in range(tq // tb):
            state = block(i * tq + b * tb, *state, b * tb)
        dq_ref[...] = state[2].astype(dq_ref.dtype)
        host.stage("mid", comm_refs)
        host.stage("last", comm_refs)

    t = tq
    full = lambda off: pl.BlockSpec((s, HEAD), lambda h, i: (0, off + h))
    tile_spec = lambda off: pl.BlockSpec((t, HEAD), lambda h, i: (i, off + h))
    w = n_heads * HEAD
    return pl.pallas_call(
        body, name=name, grid=(n_heads, s // t),
        in_specs=[tile_spec(qoff), full(koff), full(voff), tile_spec(0), tile_spec(0)] + host.in_specs,
        out_specs=[tile_spec(0), full(0), full(0)] + host.out_specs,
        out_shape=[jax.ShapeDtypeStruct((s, w), BF16), jax.ShapeDtypeStruct((s, w), F32),
                   jax.ShapeDtypeStruct((s, w), F32)] + host.out_shape,
        scratch_shapes=host.scratch,
        compiler_params=_params(("parallel" if comm is None else "arbitrary", "arbitrary"), 48),
    )(qkv, qkv, qkv, do, tot, *host.ins)


def _coords():
    return lax.axis_index("x"), lax.axis_index("y"), lax.axis_index("c")


def _gather_plan(shards):
    n = len(shards)

    def run(stage, ins, outs, sems):
        send_sems, recv_sems, local_sems = sems
        x, y, c = _coords()
        me, sibling = (x, y, c), (x, y, 1 - c)
        chips = [(1 - x, y), (x, 1 - y), (1 - x, 1 - y)]

        def copy(w, k, block, to, src=None):
            dst = outs[w].at[4 * block[0] + 2 * block[1] + block[2]]
            return pltpu.make_async_remote_copy(
                src_ref=dst if src is None else src, dst_ref=dst,
                send_sem=send_sems.at[7 * w + k], recv_sem=recv_sems.at[7 * w + k],
                device_id=to, device_id_type=MESH)

        def mine():
            return [pltpu.make_async_copy(ins[w], outs[w].at[4 * x + 2 * y + c], local_sems.at[w]) for w in range(n)]

        def first():
            cps = []
            for w in range(n):
                cps.append(copy(w, 0, me, sibling, src=ins[w]))
                cps += [copy(w, 1 + j, me, (*chip, c), src=ins[w]) for j, chip in enumerate(chips)]
            return cps

        def passed():
            return [copy(w, 4 + j, (*chip, c), sibling) for w in range(n) for j, chip in enumerate(chips)]

        if stage == "first":
            for cp in mine() + first():
                cp.start()
        elif stage == "mid":
            onward = passed()
            for w in range(n):
                for j, chip in enumerate(chips):
                    copy(w, 1 + j, (*chip, c), me).wait_recv()
                    onward[3 * w + j].start()
        else:
            for w in range(n):
                copy(w, 0, sibling, me).wait_recv()
                for j, chip in enumerate(chips):
                    copy(w, 4 + j, (*chip, 1 - c), me).wait_recv()
            for cp in first() + passed():
                cp.wait_send()
            for cp in mine():
                cp.wait()

    return dict(
        ins=list(shards), run=run, stages=("first", "mid", "last"),
        out_shape=[jax.ShapeDtypeStruct((N_DEV,) + a.shape, a.dtype) for a in shards],
        scratch=[pltpu.SemaphoreType.DMA((7 * n,)), pltpu.SemaphoreType.DMA((7 * n,)), pltpu.SemaphoreType.DMA((n,))])


def _scatter_plan(grads):
    n = len(grads)

    def run(stage, ins, outs, sems):
        send_sems, recv_sems = sems
        x, y, c = _coords()
        cps = []
        for w in range(n):
            for mask in range(1, N_DEV):
                px, py, pc = x ^ (mask >> 2), y ^ ((mask >> 1) & 1), c ^ (mask & 1)
                cps.append(pltpu.make_async_remote_copy(
                    src_ref=ins[w].at[4 * px + 2 * py + pc], dst_ref=outs[w].at[mask - 1],
                    send_sem=send_sems.at[7 * w + mask - 1], recv_sem=recv_sems.at[7 * w + mask - 1],
                    device_id=(px, py, pc), device_id_type=MESH))
        for cp in cps:
            if stage == "first":
                cp.start()
            else:
                cp.wait()

    return dict(
        ins=list(grads), run=run, stages=("first", "last"),
        out_shape=[jax.ShapeDtypeStruct((N_DEV - 1,) + a.shape[1:], a.dtype) for a in grads],
        scratch=[pltpu.SemaphoreType.DMA((7 * n,)), pltpu.SemaphoreType.DMA((7 * n,))])


def _run_plan(plan, name):
    n_in, n_out = len(plan["ins"]), len(plan["out_shape"])

    def body(*refs):
        for stage in plan["stages"]:
            plan["run"](stage, refs[:n_in], refs[n_in:n_in + n_out], refs[n_in + n_out:])

    any_spec = pl.BlockSpec(memory_space=pl.ANY)
    return pl.pallas_call(
        body, name=name, in_specs=[any_spec] * n_in, out_specs=[any_spec] * n_out,
        out_shape=plan["out_shape"], scratch_shapes=plan["scratch"],
    )(*plan["ins"])


def _gather_rows(v):
    rows, width = v.shape

    def body(v_ref, out_ref, send_sems, recv_sems):
        x, y, c = _coords()
        out_ref[pl.ds(pl.multiple_of((4 * x + 2 * y + c) * rows, rows), rows), :] = v_ref[...]
        cps = []
        for mask in range(1, N_DEV):
            peer = (x ^ (mask >> 2), y ^ ((mask >> 1) & 1), c ^ (mask & 1))
            dst = out_ref.at[pl.ds(pl.multiple_of((4 * x + 2 * y + c) * rows, rows), rows), :]
            cps.append(pltpu.make_async_remote_copy(
                src_ref=v_ref, dst_ref=dst, send_sem=send_sems.at[mask - 1], recv_sem=recv_sems.at[mask - 1],
                device_id=peer, device_id_type=MESH))
        for cp in cps:
            cp.start()
        for cp in cps:
            cp.wait()

    vmem = pl.BlockSpec(memory_space=pltpu.VMEM)
    return pl.pallas_call(
        body, name="gather_small",
        in_specs=[vmem], out_specs=vmem,
        out_shape=jax.ShapeDtypeStruct((N_DEV * rows, width), F32),
        scratch_shapes=[pltpu.SemaphoreType.DMA((N_DEV - 1,)), pltpu.SemaphoreType.DMA((N_DEV - 1,))],
    )(v)


def _adamw(w, g, m, v):
    m = ADAM_B1 * m + (1.0 - ADAM_B1) * g
    v = ADAM_B2 * v + (1.0 - ADAM_B2) * jnp.square(g)
    m_hat = m / (1.0 - ADAM_B1 ** ADAM_STEP)
    v_hat = v / (1.0 - ADAM_B2 ** ADAM_STEP)
    delta = -ADAM_LR * (m_hat / (jnp.sqrt(v_hat) + ADAM_EPS) + ADAM_WD * w)
    return delta, m, v


def _tile_rows(rows, cols):
    tm = 1 << int(math.log2(max(2 * SUBLANES, (1 << 18) // cols)))
    while rows % tm:
        tm //= 2
    assert tm >= 2 * SUBLANES, (rows, cols)
    return tm


def _reduce_adam(w, m, v, own, got, name):
    r, c = w.shape
    tm = max(2 * SUBLANES, _tile_rows(r, c) // 2)

    def fn(wv, mv, vv, a, *peers):
        g = a
        for pv in peers:
            g = g + pv.astype(F32)
        return (g,) + _adamw(wv, g, mv, vv)

    blk = lambda a: (a, (tm, c), lambda i: (i, 0))
    got_blk = lambda j: (got, (None, tm, c), lambda i, j=j: (j, i, 0))
    return _ew(fn, [blk(w), blk(m), blk(v), blk(own)] + [got_blk(j) for j in range(N_DEV - 1)],
               [((r, c), F32, (tm, c), lambda i: (i, 0))] * 4, (r // tm,), name)


def _small_adam(gathered, params, moms, vels, widths):
    n = len(params)
    total = gathered.shape[1]

    def body(*refs):
        g_ref = refs[0]
        p_refs, m_refs, v_refs = refs[1:1 + n], refs[1 + n:1 + 2 * n], refs[1 + 2 * n:1 + 3 * n]
        sum_ref = refs[1 + 3 * n]
        outs = refs[2 + 3 * n:]
        g = g_ref[0:1, :]
        for p in range(1, N_DEV):
            g = g + g_ref[p * SUBLANES:p * SUBLANES + 1, :]
        sum_ref[...] = g
        off = 0
        for i, wd in enumerate(widths):
            d, m2, v2 = _adamw(p_refs[i][...], g[:, off:off + wd], m_refs[i][...], v_refs[i][...])
            outs[3 * i][...] = d
            outs[3 * i + 1][...] = m2
            outs[3 * i + 2][...] = v2
            off += wd

    vmem = pl.BlockSpec(memory_space=pltpu.VMEM)
    out_shape = [jax.ShapeDtypeStruct((1, total), F32)]
    for wd in widths:
        out_shape += [jax.ShapeDtypeStruct((1, wd), F32)] * 3
    return pl.pallas_call(
        body, name="small_adam",
        in_specs=[vmem] * (1 + 3 * n), out_specs=[vmem] * len(out_shape), out_shape=out_shape,
    )(gathered, *params, *moms, *vels)


def _cast_bf16(a, name):
    r, c = a.shape
    tm = _tile_rows(r, c)
    return _ew(lambda v: v, [(a, (tm, c), lambda i: (i, 0))], [((r, c), BF16, (tm, c), lambda i: (i, 0))],
               (r // tm,), name)[0]


def _fold_loss(parts, name):
    r, c = parts.shape

    def fn(v):
        return jnp.broadcast_to(jnp.sum(jnp.sum(v, axis=0, keepdims=True), axis=1, keepdims=True), (SUBLANES, HEAD))

    return _ew(fn, [_const_spec(parts)], [((SUBLANES, HEAD), F32, (SUBLANES, HEAD), lambda i: (0, 0))], (1,),
               name)[0][0:1]


def kernel(x, p, g_mix, w_in, qn_gain, kn_gain, w_branch_a, w_branch_b, w_out, g_mlp, w_up, w_down, g_ple, w_ple_gate, w_ple_proj, loss_target, m_g_mix, m_w_in, m_qn_gain, m_kn_gain, m_w_branch_a, m_w_branch_b, m_w_out, m_g_mlp, m_w_up, m_w_down, m_g_ple, m_w_ple_gate, m_w_ple_proj, v_g_mix, v_w_in, v_qn_gain, v_kn_gain, v_w_branch_a, v_w_branch_b, v_w_out, v_g_mlp, v_w_up, v_w_down, v_g_ple, v_w_ple_gate, v_w_ple_proj):
    x2 = x[0]
    tgt = loss_target[0]
    s, d = x2.shape
    wd_ = w_branch_a.shape[1]
    nh = wd_ // HEAD
    dff = w_up.shape[1]
    qkv_w = 6 * wd_
    tiles = lambda cols: cols // HEAD

    big = [w_in[0], w_branch_a[0], w_branch_b[0], w_out[0], w_up[0], w_down[0], w_ple_gate[0], w_ple_proj[0]]
    names = ["w_in", "w_branch_a", "w_branch_b", "w_out", "w_up", "w_down", "w_ple_gate", "w_ple_proj"]
    row_sharded = [False, False, False, True, False, True, True, False]
    shards = [_cast_bf16(a, "cast_" + nm) for a, nm in zip(big, names)]
    as_weight = lambda g, rs: g.reshape((1, N_DEV * g.shape[1], g.shape[2])) if rs else g
    as_blocks = lambda g, rs: g.reshape((N_DEV, g.shape[1] // N_DEV, g.shape[2])) if rs else g
    win = _run_plan(_gather_plan(shards[:1]), "all_gather_w_in")[0]

    tm = 1024 if s % 1024 == 0 else s
    tm_in = tk_s = 2048 if s % 2048 == 0 else tm
    tn_of = lambda n: 512 if n % 512 == 0 else (256 if n % 256 == 0 else n)
    tn_in = 256 if win.shape[2] % 256 == 0 else HEAD

    h = _rms_fwd(x2, g_mix, "norm_mix")
    qk_raw = _mm("nn", h, win, tm=tm_in, tn=tn_in, tk=d, out_dtypes=[F32], name="proj_qk",
                 n_off=0, n_cnt=2 * wd_ // tn_in)[0]
    rest = _mm("nn", h, win, tm=4096 if s % 4096 == 0 else tm_in, tn=tn_in, tk=d, out_dtypes=[BF16], name="proj_rest",
               n_off=2 * wd_ // tn_in, n_cnt=(win.shape[0] * win.shape[2] - 2 * wd_) // tn_in)[0]
    o_va, o_qb, o_kb, o_vb, o_ga, o_gb = 0, tiles(wd_), tiles(2 * wd_), tiles(3 * wd_), tiles(4 * wd_), tiles(4 * wd_ + d)
    tabs = _rope_tables(s)
    qa = _headnorm_rope(qk_raw, 0, qn_gain, tabs, nh, "rope_q")
    ka = _headnorm_rope(qk_raw, 1, kn_gain, tabs, nh, "rope_k")
    va = rest[:, :wd_]

    outs, lses = [], []
    for dil in DILATIONS:
        o_g, l_g = _dilated_fwd(_phase_major(qa, dil), _phase_major(ka, dil), _phase_major(va, dil), 0, nh,
                                f"dilated_fwd_{dil}")
        outs.append(_token_major(o_g))
        lses.append(_token_major(l_g))
    ya, lse_all = _mix_fwd(outs, lses, "mix_fwd")
    yb, sb_tot, *others = _sb_fwd(rest, o_qb, o_kb, o_vb, nh, "sb_fwd", comm=_gather_plan(shards[1:]))
    wba, wbb, wout, wup, wdown, wgate, wproj = [as_weight(g, rs) for g, rs in zip(others, row_sharded[1:])]

    tn_d = tn_of(wba.shape[2])
    za = _mm("nn", ya, wba, tm=tm, tn=tn_d, tk=wd_, out_dtypes=[BF16], name="branch_a")[0]

    def merge(acc, zav, gav, gbv):
        return _sigmoid(gav.astype(F32)) * zav.astype(F32) + _sigmoid(gbv.astype(F32)) * acc, acc

    merged, zb = _mm("nn", yb, wbb, tm=tm, tn=tn_d, tk=wd_, out_dtypes=[BF16, BF16], name="branch_b_merge",
                     epilogue=merge, extras=[(za, 0), (rest, o_ga * HEAD // tn_d), (rest, o_gb * HEAD // tn_d)])
    x1 = _mm("nn", merged, wout, tm=tm, tn=512, tk=d, out_dtypes=[F32], name="out_proj",
             epilogue=lambda acc, xv: (acc + xv,), extras=[(x2, 0)])[0]

    hm = _rms_fwd(x1, g_mlp, "norm_mlp")
    tn_u = min(wup.shape[2], 1024)
    u, act = _mm("nn", hm, wup, tm=tm, tn=tn_u, tk=d, out_dtypes=[BF16, BF16], name="mlp_up",
                 epilogue=lambda acc: (acc, jnp.square(jnp.maximum(acc, 0.0))))
    x3 = _mm("nn", act, wdown, tm=tm, tn=1024, tk=min(dff, 2048), out_dtypes=[F32], name="mlp_down",
             epilogue=lambda acc, xv: (acc + xv,), extras=[(x1, 0)])[0]

    hp = _rms_fwd(x3, g_ple, "norm_ple")
    p_b = _cast_bf16(p[0, 0], "cast_p")
    pp = _mm("nn", p_b, wproj, tm=tm, tn=tn_of(wproj.shape[2]), tk=p_b.shape[1], out_dtypes=[BF16],
             name="ple_proj")[0]

    def head(acc, ppv, xv, tv):
        sg = _sigmoid(acc)
        ppf = ppv.astype(F32)
        err = xv + ppf * sg - tv
        dy = err / d
        sq = jnp.square(err)
        return dy, dy * sg, dy * ppf * sg * (1.0 - sg), sq.reshape(-1, SUBLANES, sq.shape[-1]).sum(axis=0)

    n_i = s // tm
    dy, d_pp, d_gt, sq_parts = _mm(
        "nn", hp, wgate, tm=tm, tn=512, tk=d, out_dtypes=[F32, BF16, BF16], name="ple_gate_loss", epilogue=head,
        extras=[(pp, 0), (x3, 0), (tgt, 0)],
        extra_outs=[((n_i * SUBLANES, d), F32, (SUBLANES, 512), lambda i, j: (i, j))])
    loss_vec = _fold_loss(sq_parts, "loss_fold") * 0.5 / d

    both = [F32, BF16]
    g_wproj = _mm("tn", p_b, d_pp, tm=p_b.shape[1], tn=tn_of(wproj.shape[2]), tk=tk_s, out_dtypes=both,
                  name="grad_w_ple_proj", out_nb=N_DEV)
    g_wgate = _mm("tn", hp, d_gt, tm=1024, tn=1024, tk=tk_s, out_dtypes=both, name="grad_w_ple_gate")
    d_hp = _mm("nt", d_gt, wgate, tm=tm, tn=512, tk=d, out_dtypes=[F32], name="d_hp")[0]
    dx3, dx3_b, g_gple = _rms_bwd(d_hp, x3, g_ple, dy, "norm_ple_bwd")

    d_u = _mm("nt", dx3_b, wdown, tm=tm, tn=1024, tk=d, out_dtypes=[BF16], name="d_u",
              epilogue=lambda acc, uv: (acc * (2.0 * jnp.maximum(uv.astype(F32), 0.0)),), extras=[(u, 0)])[0]
    g_wdown = _mm("tn", act, dx3_b, tm=1024, tn=1024, tk=tk_s, out_dtypes=both, name="grad_w_down")
    g_wup = _mm("tn", hm, d_u, tm=1024, tn=wup.shape[2], tk=tk_s, out_dtypes=both, name="grad_w_up", out_nb=N_DEV)
    d_hm = _mm("nt", d_u, wup, tm=tm, tn=min(d, 2048), tk=wup.shape[2], out_dtypes=[F32], name="d_hm")[0]
    dx1, dx1_b, g_gmlp = _rms_bwd(d_hm, x1, g_mlp, dx3, "norm_mlp_bwd")

    def unmerge(acc, gav, gbv, zav, zbv):
        sa, sb = _sigmoid(gav.astype(F32)), _sigmoid(gbv.astype(F32))
        return acc * sa, acc * sb, acc * zav.astype(F32) * sa * (1.0 - sa), acc * zbv.astype(F32) * sb * (1.0 - sb)

    d_za, d_zb, d_ga, d_gb = _mm(
        "nt", dx1_b, wout, tm=tm, tn=512, tk=d, out_dtypes=[BF16] * 4, name="d_merged", epilogue=unmerge,
        extras=[(rest, o_ga * HEAD // 512), (rest, o_gb * HEAD // 512), (za, 0), (zb, 0)])
    g_wout = _mm("tn", merged, dx1_b, tm=1024, tn=1024, tk=tk_s, out_dtypes=both, name="grad_w_out")
    g_wba = _mm("tn", ya, d_za, tm=wd_, tn=wba.shape[2], tk=tk_s, out_dtypes=both, name="grad_w_branch_a",
                out_nb=N_DEV)
    g_wbb = _mm("tn", yb, d_zb, tm=wd_, tn=wbb.shape[2], tk=tk_s, out_dtypes=both, name="grad_w_branch_b",
                out_nb=N_DEV)
    d_ya = _mm("nt", d_za, wba, tm=tm, tn=wd_, tk=wba.shape[2], out_dtypes=[F32], name="d_ya")[0]
    d_yb = _mm("nt", d_zb, wbb, tm=tm, tn=wd_, tk=wbb.shape[2], out_dtypes=[F32], name="d_yb")[0]

    grads = [None, g_wba, g_wbb, g_wout, g_wup, g_wdown, g_wgate, g_wproj]
    early = _scatter_plan([as_blocks(g[1], rs) for g, rs in zip(grads[1:], row_sharded[1:])])
    d_qb, d_kb, d_vb, *got = _sb_bwd(rest, o_qb, o_kb, o_vb, d_yb, sb_tot, nh, "sb_bwd", comm=early)
    d_ya_b, stats = _mix_stats(d_ya, ya, lse_all, nh, "mix_stats")
    dqs, dks, dvs = [], [], []
    for dil in DILATIONS:
        dq_g, dk_g, dv_g = _dilated_bwd(
            _phase_major(qa, dil), _phase_major(ka, dil), _phase_major(va, dil), 0, _phase_major(d_ya_b, dil),
            _phase_major(stats, dil), nh, f"dilated_bwd_{dil}")
        dqs.append(_token_major(dq_g))
        dks.append(_token_major(dk_g))
        dvs.append(_token_major(dv_g))
    d_qa, g_qn = _headnorm_rope_bwd(dqs, qk_raw, 0, qn_gain, tabs, nh, "rope_q_bwd")
    d_ka, g_kn = _headnorm_rope_bwd(dks, qk_raw, 1, kn_gain, tabs, nh, "rope_k_bwd")
    tmr = 256
    d_va = _ew(lambda a, b, c: a + b + c, [_row_spec(a, tmr) for a in dvs],
               [((s, wd_), BF16, (tmr, wd_), lambda i: (i, 0))], (s // tmr,), "sum_dv")[0]
    d_proj = jnp.concatenate([d_qa, d_ka, d_va, d_qb, d_kb.astype(BF16), d_vb.astype(BF16), d_ga, d_gb], axis=1)

    grads[0] = _mm("tn", h, d_proj, tm=1024, tn=win.shape[2], tk=tk_s, out_dtypes=both, name="grad_w_in",
                   out_nb=N_DEV)
    d_h, got_in = _mm("nt", d_proj, win, tm=tm, tn=min(d, 2048), tk=win.shape[2], out_dtypes=[F32], name="d_h",
                      comm=_scatter_plan([grads[0][1]]))
    got = [got_in] + got
    dx, _, g_gmix = _rms_bwd(d_h, x2, g_mix, dx1, "norm_mix_bwd")

    cx, cy, cc = _coords()
    me = 4 * cx + 2 * cy + cc
    moms = [m_w_in, m_w_branch_a, m_w_branch_b, m_w_out, m_w_up, m_w_down, m_w_ple_gate, m_w_ple_proj]
    vels = [v_w_in, v_w_branch_a, v_w_branch_b, v_w_out, v_w_up, v_w_down, v_w_ple_gate, v_w_ple_proj]
    big_out = {}
    for i, nm in enumerate(names):
        own = lax.dynamic_index_in_dim(as_blocks(grads[i][0], row_sharded[i]), me, axis=0, keepdims=False)
        big_out[nm] = [a[None] for a in _reduce_adam(big[i], moms[i][0], vels[i][0], own, got[i], "adam_" + nm)]

    small_names = ["g_mix", "qn_gain", "kn_gain", "g_mlp", "g_ple"]
    small_p = [g_mix, qn_gain, kn_gain, g_mlp, g_ple]
    small_m = [m_g_mix, m_qn_gain, m_kn_gain, m_g_mlp, m_g_ple]
    small_v = [v_g_mix, v_qn_gain, v_kn_gain, v_g_mlp, v_g_ple]
    small_g = [g_gmix, g_qn, g_kn, g_gmlp, g_gple]
    widths = [a.shape[1] for a in small_p]
    vec = jnp.concatenate(small_g + [loss_vec], axis=1)
    vec = jnp.pad(vec, ((0, SUBLANES - 1), (0, 0)))
    res = _small_adam(_gather_rows(vec), small_p, small_m, small_v, widths)
    summed = res[0]
    small_out, off = {}, 0
    for i, nm in enumerate(small_names):
        small_out[nm] = [summed[:, off:off + widths[i]]] + list(res[1 + 3 * i:4 + 3 * i])
        off += widths[i]
    loss = summed[0, off]

    order = ["g_mix", "w_in", "qn_gain", "kn_gain", "w_branch_a", "w_branch_b", "w_out", "g_mlp", "w_up", "w_down",
             "g_ple", "w_ple_gate", "w_ple_proj"]
    table = {**big_out, **small_out}
    result = [loss, dx[None]]
    for kind in range(4):
        result += [table[nm][kind] for nm in order]
    return tuple(result)
```

```python
import functools
import math

import jax
import jax.numpy as jnp
from jax import lax
from jax.experimental import pallas as pl
from jax.experimental.pallas import tpu as pltpu

F32 = jnp.float32
BF16 = jnp.bfloat16
MESH = pl.DeviceIdType.MESH

HEAD = 128
WINDOW = 128
DILATIONS = (1, 4, 16)
ROT = HEAD // 4
ROPE_THETA = 500000.0
EPS = 1e-6
NEG = -1e30
N_DEV = 8

ADAM_LR = 0.001
ADAM_B1 = 0.9
ADAM_B2 = 0.999
ADAM_EPS = 1e-08
ADAM_WD = 0.01
ADAM_STEP = 10

SUBLANES = 8
VMEM_CAP_MB = 56

NT_DIMS = (((1,), (1,)), ((), ()))
TN_DIMS = (((0,), (0,)), ((), ()))


def _params(semantics, vmem_mb):
    return pltpu.CompilerParams(dimension_semantics=semantics, vmem_limit_bytes=min(vmem_mb, VMEM_CAP_MB) << 20)


def _sigmoid(x):
    return 0.5 + 0.5 * jnp.tanh(0.5 * x)


def _split_bf16(x):
    hi = x.astype(BF16)
    lo = (x - hi.astype(F32)).astype(BF16)
    return hi, lo


class _Host:
    def __init__(self, comm, grid):
        self.comm, self.grid = comm, grid
        any_spec = pl.BlockSpec(memory_space=pl.ANY)
        self.ins = list(comm["ins"]) if comm else []
        self.out_shape = list(comm["out_shape"]) if comm else []
        self.scratch = list(comm["scratch"]) if comm else []
        self.in_specs = [any_spec] * len(self.ins)
        self.out_specs = [any_spec] * len(self.out_shape)

    def split(self, refs, n_in, n_out):
        pos = n_in
        c_in = refs[pos:pos + len(self.ins)]
        pos += len(self.ins)
        outs = refs[pos:pos + n_out]
        pos += n_out
        c_out = refs[pos:pos + len(self.out_shape)]
        pos += len(self.out_shape)
        own = len(refs) - pos - len(self.scratch)
        return (c_in, c_out, refs[pos + own:]), (outs, refs[pos:pos + own])

    def stage(self, which, comm_refs):
        if self.comm is None or which not in self.comm["stages"]:
            return
        grid = self.grid
        at = {"first": [0] * len(grid), "mid": [grid[0] // 2] + [0] * (len(grid) - 1),
              "last": [g - 1 for g in grid]}[which]
        cond = functools.reduce(jnp.logical_and, [pl.program_id(ax) == v for ax, v in enumerate(at)])

        @pl.when(cond)
        def _():
            self.comm["run"](which, *comm_refs)


def _ew(fn, ins, outs, grid, name, colsums=(), vmem_mb=40):
    n_in, n_out, n_cs = len(ins), len(outs), len(colsums)
    steps = math.prod(grid)

    def body(*refs):
        in_refs = refs[:n_in]
        out_refs = refs[n_in:n_in + n_out]
        cs_refs = refs[n_in + n_out:n_in + n_out + n_cs]
        acc_refs = refs[n_in + n_out + n_cs:]
        vals = fn(*[r[...] for r in in_refs])
        if not isinstance(vals, tuple):
            vals = (vals,)
        for r, v in zip(out_refs, vals[:n_out]):
            r[...] = v.astype(r.dtype)
        if n_cs:
            step = pl.program_id(0)
            for ax in range(1, len(grid)):
                step = step * grid[ax] + pl.program_id(ax)
            for acc, cs, v in zip(acc_refs, cs_refs, vals[n_out:]):
                part = v.reshape(-1, SUBLANES, v.shape[-1]).sum(axis=0)

                @pl.when(step == 0)
                def _(acc=acc, part=part):
                    acc[...] = part

                @pl.when(step > 0)
                def _(acc=acc, part=part):
                    acc[...] += part

                @pl.when(step == steps - 1)
                def _(acc=acc, cs=cs):
                    cs[...] = acc[...].sum(axis=0, keepdims=True)

    out_shape = [jax.ShapeDtypeStruct(s, d) for s, d, _, _ in outs]
    out_specs = [pl.BlockSpec(b, m) for _, _, b, m in outs]
    for w in colsums:
        out_shape.append(jax.ShapeDtypeStruct((1, w), F32))
        out_specs.append(pl.BlockSpec((1, w), lambda *_: (0, 0)))
    sem = ("arbitrary",) * len(grid) if n_cs else ("parallel",) * len(grid)
    res = pl.pallas_call(
        body, name=name, grid=grid,
        in_specs=[pl.BlockSpec(b, m) for _, b, m in ins],
        out_specs=out_specs, out_shape=out_shape,
        scratch_shapes=[pltpu.VMEM((SUBLANES, w), F32) for w in colsums],
        compiler_params=_params(sem, vmem_mb),
    )(*[a for a, _, _ in ins])
    return res


def _row_spec(a, tm):
    return (a, (tm, a.shape[1]), lambda i: (i, 0))


def _const_spec(a):
    return (a, a.shape, lambda *_: (0,) * a.ndim)


def _mm(mode, a, b, *, tm, tn, tk, out_dtypes, name, epilogue=None, extras=(), n_off=0, n_cnt=None,
        out_nb=1, extra_outs=(), vmem_mb=52, comm=None):
    if mode == "nn":
        m, kdim = a.shape
        nb, _, n = b.shape
        npb = n // tn
        ncols = nb * n
        n_tiles = (ncols // tn) if n_cnt is None else n_cnt
        a_spec = pl.BlockSpec((tm, tk), lambda i, j, k: (i, k))
        b_spec = pl.BlockSpec((None, tk, tn), lambda i, j, k: ((j + n_off) // npb, k, (j + n_off) % npb))
        dims = (((1,), (0,)), ((), ()))
    elif mode == "nt":
        m, kdim = a.shape
        nb, nout, n = b.shape
        kpb = n // tk
        n_tiles = nout // tn
        a_spec = pl.BlockSpec((tm, tk), lambda i, j, k: (i, k))
        b_spec = pl.BlockSpec((None, tn, tk), lambda i, j, k: (k // kpb, j, k % kpb))
        dims = NT_DIMS
    else:
        kdim, m = a.shape
        ncols = b.shape[1]
        n_tiles = ncols // tn
        a_spec = pl.BlockSpec((tk, tm), lambda i, j, k: (k, i))
        b_spec = pl.BlockSpec((tk, tn), lambda i, j, k: (k, j))
        dims = TN_DIMS
    nk = kdim // tk
    assert kdim % tk == 0 and m % tm == 0
    grid = (m // tm, n_tiles, nk)
    n_ex, n_out = len(extras), len(out_dtypes) + len(extra_outs)
    host = _Host(comm, grid)

    def body(*refs):
        a_ref, b_ref = refs[0], refs[1]
        ex_refs = refs[2:2 + n_ex]
        comm_refs, (out_refs, scratch) = host.split(refs, 2 + n_ex, n_out)
        host.stage("first", comm_refs)

        def finish(acc):
            vals = (acc,) * n_out if epilogue is None else epilogue(acc, *[r[...] for r in ex_refs])
            for r, v in zip(out_refs, vals):
                r[...] = v.astype(r.dtype)

        def part():
            return lax.dot_general(a_ref[...], b_ref[...], dims, preferred_element_type=F32)

        if nk == 1:
            finish(part())
        else:
            acc_ref = scratch[0]
            k = pl.program_id(2)

            @pl.when(k == 0)
            def _():
                acc_ref[...] = part()

            @pl.when((k > 0) & (k < nk - 1))
            def _():
                acc_ref[...] += part()

            @pl.when(k == nk - 1)
            def _():
                finish(acc_ref[...] + part())

        host.stage("mid", comm_refs)
        host.stage("last", comm_refs)

    if mode == "tn":
        npo = (ncols // out_nb) // tn
        out_shape = [jax.ShapeDtypeStruct((out_nb, m, ncols // out_nb), d) for d in out_dtypes]
        out_specs = [pl.BlockSpec((None, tm, tn), lambda i, j, k: (j // npo, i, j % npo)) for _ in out_dtypes]
    else:
        out_shape = [jax.ShapeDtypeStruct((m, n_tiles * tn), d) for d in out_dtypes]
        out_specs = [pl.BlockSpec((tm, tn), lambda i, j, k: (i, j)) for _ in out_dtypes]
    for s, d, blk, imap in extra_outs:
        out_shape.append(jax.ShapeDtypeStruct(s, d))
        out_specs.append(pl.BlockSpec(blk, lambda i, j, k, imap=imap: imap(i, j)))
    ex_specs = [pl.BlockSpec((tm, tn), lambda i, j, k, off=off: (i, j + off)) for _, off in extras]
    sem = ("parallel", "parallel", "arbitrary") if comm is None else ("arbitrary",) * 3
    return pl.pallas_call(
        body, name=name, grid=grid,
        in_specs=[a_spec, b_spec] + ex_specs + host.in_specs,
        out_specs=out_specs + host.out_specs, out_shape=out_shape + host.out_shape,
        scratch_shapes=([pltpu.VMEM((tm, tn), F32)] if nk > 1 else []) + host.scratch,
        compiler_params=_params(sem, vmem_mb),
    )(a, b, *[e for e, _ in extras], *host.ins)


def _rms_fwd(x, g, name):
    tm = 256

    def fn(xv, gv):
        r = lax.rsqrt(jnp.mean(xv * xv, axis=-1, keepdims=True) + EPS)
        return xv * r * gv

    return _ew(fn, [_row_spec(x, tm), _const_spec(g)], [(x.shape, BF16, (tm, x.shape[1]), lambda i: (i, 0))],
               (x.shape[0] // tm,), name)[0]


def _rms_bwd(dh, x, g, res, name):
    tm = 256
    d = x.shape[1]

    def fn(dhv, xv, gv, rv):
        r = lax.rsqrt(jnp.mean(xv * xv, axis=-1, keepdims=True) + EPS)
        xh = xv * r
        dyg = dhv * gv
        dx = rv + r * (dyg - xh * jnp.mean(dyg * xh, axis=-1, keepdims=True))
        return dx, dx, dhv * xh

    spec = lambda dt: (x.shape, dt, (tm, d), lambda i: (i, 0))
    return _ew(fn, [_row_spec(dh, tm), _row_spec(x, tm), _const_spec(g), _row_spec(res, tm)],
               [spec(F32), spec(BF16)], (x.shape[0] // tm,), name, colsums=(d,))


def _rope_tables(s):
    half = ROT // 2
    pos = jnp.arange(s, dtype=F32)
    inv = ROPE_THETA ** (-jnp.arange(0, ROT, 2, dtype=F32) / ROT)
    ang = pos[:, None] * inv[None, :]
    cos, sin = jnp.cos(ang), jnp.sin(ang)
    pad = jnp.zeros((s, HEAD - ROT), F32)
    c = jnp.concatenate([cos, cos, pad + 1.0], axis=1)
    a = jnp.concatenate([-sin, jnp.zeros_like(sin), pad], axis=1)
    b = jnp.concatenate([jnp.zeros_like(sin), sin, pad], axis=1)
    return c, a, b


def _heads(x, n_heads):
    return [x[:, h * HEAD:(h + 1) * HEAD] for h in range(n_heads)]


def _headnorm_rope(proj, part, gain, tabs, n_heads, name):
    tm = 512
    s = proj.shape[0]
    w = n_heads * HEAD

    def fn(xs, gv, c, a, b):
        outs = []
        for xv in _heads(xs, n_heads):
            r = lax.rsqrt(jnp.mean(xv * xv, axis=-1, keepdims=True) + EPS)
            y = xv * r * gv
            outs.append(c * y + a * pltpu.roll(y, HEAD - ROT // 2, 1) + b * pltpu.roll(y, ROT // 2, 1))
        return jnp.concatenate(outs, axis=1)

    tab = lambda t: (t, (tm, HEAD), lambda i: (i, 0))
    return _ew(fn, [(proj, (tm, w), lambda i: (i, part)), (gain, (1, HEAD), lambda i: (0, 0))] + [tab(t) for t in tabs],
               [((s, w), BF16, (tm, w), lambda i: (i, 0))], (s // tm,), name)[0]


def _headnorm_rope_bwd(dys, proj, part, gain, tabs, n_heads, name):
    tm = 256
    s = proj.shape[0]
    w = n_heads * HEAD
    n_dy = len(dys)

    def fn(*vals):
        dy_all = vals[0]
        for v in vals[1:n_dy]:
            dy_all = dy_all + v
        xs, gv, c, a, b = vals[n_dy:]
        dxs, dgain = [], None
        for dy, xv in zip(_heads(dy_all, n_heads), _heads(xs, n_heads)):
            dn = c * dy + pltpu.roll(a * dy, ROT // 2, 1) + pltpu.roll(b * dy, HEAD - ROT // 2, 1)
            r = lax.rsqrt(jnp.mean(xv * xv, axis=-1, keepdims=True) + EPS)
            xh = xv * r
            dyg = dn * gv
            dxs.append(r * (dyg - xh * jnp.mean(dyg * xh, axis=-1, keepdims=True)))
            dgain = dn * xh if dgain is None else dgain + dn * xh
        return jnp.concatenate(dxs, axis=1), dgain

    tab = lambda t: (t, (tm, HEAD), lambda i: (i, 0))
    return _ew(fn, [(d, (tm, w), lambda i: (i, 0)) for d in dys]
               + [(proj, (tm, w), lambda i: (i, part)), (gain, (1, HEAD), lambda i: (0, 0))] + [tab(t) for t in tabs],
               [((s, w), BF16, (tm, w), lambda i: (i, 0))], (s // tm,), name, colsums=(HEAD,))


def _phase_major(a, d):
    s, w = a.shape
    if d == 1:
        return a.reshape(1, s, w)
    return a.reshape(s // d, d, w).transpose(1, 0, 2)


def _token_major(a):
    d, m, w = a.shape
    if d == 1:
        return a.reshape(m, w)
    return a.transpose(1, 0, 2).reshape(m * d, w)


DIL_ROWS = 2048


def _dil_tq(m):
    return min(DIL_ROWS, m)


def _dil_heads(tq, n_heads):
    return max(1, min(n_heads, DIL_ROWS // tq))


def _tile_up(parts, hb, per):
    cols = [jnp.concatenate(parts[j * per:(j + 1) * per], axis=0) for j in range(hb)]
    return cols[0] if hb == 1 else jnp.concatenate(cols, axis=1)


def _dilated_fwd(q, k, v, voff, n_heads, name):
    d, m, _ = q.shape
    tq = _dil_tq(m)
    nq = m // tq
    per = tq // WINDOW
    hb = _dil_heads(tq, n_heads)
    scale = HEAD ** -0.5
    blocks = [(j, b) for j in range(hb) for b in range(per)]
    lanes = [slice(j * HEAD, (j + 1) * HEAD) for j in range(hb)]

    def body(q_ref, kc_ref, kp_ref, vc_ref, vp_ref, o_ref, l_ref):
        n = pl.program_id(2)
        kk = jnp.concatenate([kp_ref[...], kc_ref[...]], axis=0)
        vv = jnp.concatenate([vp_ref[...], vc_ref[...]], axis=0)
        row = lax.broadcasted_iota(jnp.int32, (WINDOW, 2 * WINDOW), 0)
        col = lax.broadcasted_iota(jnp.int32, (WINDOW, 2 * WINDOW), 1)
        band = (col >= row) & (col <= row + WINDOW)
        q = q_ref[...]
        rows = [slice(b * WINDOW, (b + 1) * WINDOW) for b in range(per)]
        keys = [slice(b * WINDOW, (b + 2) * WINDOW) for b in range(per)]
        s = [lax.dot_general(q[rows[b], lanes[j]], kk[keys[b], lanes[j]], NT_DIMS,
                             preferred_element_type=F32) * scale for j, b in blocks]
        es, outs, lses = [], [], []
        for (j, b), sb in zip(blocks, s):
            valid = band if b else band & ((n > 0) | (col >= WINDOW))
            sb = jnp.where(valid, sb, NEG)
            mx = jnp.max(sb, axis=-1, keepdims=True)
            e = jnp.exp(sb - mx)
            den = jnp.sum(e, axis=-1, keepdims=True)
            es.append((e.astype(BF16), den))
            lses.append(jnp.broadcast_to(mx + jnp.log(den), (WINDOW, HEAD)))
        for (j, b), (e, den) in zip(blocks, es):
            outs.append(jnp.dot(e, vv[keys[b], lanes[j]], preferred_element_type=F32) / den)
        o_ref[...] = _tile_up(outs, hb, per)
        l_ref[...] = _tile_up(lses, hb, per)

    assert voff % hb == 0
    cur = lambda off: pl.BlockSpec((None, tq, hb * HEAD), lambda r, h, n: (r, n, off // hb + h))
    prev = lambda off: pl.BlockSpec((None, WINDOW, hb * HEAD),
                                    lambda r, h, n: (r, jnp.maximum(n * per - 1, 0), off // hb + h))
    out = jax.ShapeDtypeStruct((d, m, n_heads * HEAD), F32)
    return pl.pallas_call(
        body, name=name, grid=(d, n_heads // hb, nq),
        in_specs=[cur(0), cur(0), prev(0), cur(voff), prev(voff)],
        out_specs=[cur(0), cur(0)], out_shape=[out, out],
        compiler_params=_params(("parallel",) * 3, 32),
    )(q, k, k, v, v)


def _dilated_bwd(q, k, v, voff, dy, stats, n_heads, name):
    d, m, _ = q.shape
    tq = _dil_tq(m)
    nq = m // tq
    per = tq // WINDOW
    last_blk = m // WINDOW - 1
    hb = _dil_heads(tq, n_heads)
    scale = HEAD ** -0.5
    blocks = [(j, b) for j in range(hb) for b in range(per)]
    lanes = [slice(j * HEAD, (j + 1) * HEAD) for j in range(hb)]

    def body(qc_ref, qn_ref, kc_ref, kp_ref, vc_ref, vp_ref, dyc_ref, dyn_ref, sc_ref, sn_ref,
             dq_ref, dk_ref, dv_ref):
        n = pl.program_id(2)
        kk = jnp.concatenate([kp_ref[...], kc_ref[...]], axis=0)
        vv = jnp.concatenate([vp_ref[...], vc_ref[...]], axis=0)
        qq = jnp.concatenate([qc_ref[...], qn_ref[...]], axis=0)
        dyy = jnp.concatenate([dyc_ref[...], dyn_ref[...]], axis=0)
        st = jnp.concatenate([sc_ref[...], sn_ref[...]], axis=0)
        half = HEAD // 2
        row = lax.broadcasted_iota(jnp.int32, (WINDOW, 2 * WINDOW), 0)
        col = lax.broadcasted_iota(jnp.int32, (WINDOW, 2 * WINDOW), 1)
        band = (col >= row) & (col <= row + WINDOW)
        rows = [slice(b * WINDOW, (b + 1) * WINDOW) for b in range(per)]
        wide = [slice(b * WINDOW, (b + 2) * WINDOW) for b in range(per)]
        nt = lambda a, b: lax.dot_general(a, b, NT_DIMS, preferred_element_type=F32)
        s = [nt(qq[rows[b], lanes[j]], kk[wide[b], lanes[j]]) * scale for j, b in blocks]
        dp = [nt(dyy[rows[b], lanes[j]], vv[wide[b], lanes[j]]) for j, b in blocks]
        ds = []
        for (j, b), sb, dpb in zip(blocks, s, dp):
            valid = band if b else band & ((n > 0) | (col >= WINDOW))
            stb = st[rows[b], lanes[j]]
            p = jnp.where(valid, jnp.exp(jnp.minimum(sb - stb[:, :1], 0.0)), 0.0)
            ds.append((p * (dpb - stb[:, half:half + 1]) * scale).astype(BF16))
        dq = [jnp.dot(dsb, kk[wide[b], lanes[j]], preferred_element_type=F32) for (j, b), dsb in zip(blocks, ds)]
        kc, vc = kc_ref[...], vc_ref[...]
        s2 = [nt(kc[rows[b], lanes[j]], qq[wide[b], lanes[j]]) * scale for j, b in blocks]
        dp2 = [nt(vc[rows[b], lanes[j]], dyy[wide[b], lanes[j]]) for j, b in blocks]
        ds2, p2 = [], []
        for (j, b), sb, dpb in zip(blocks, s2, dp2):
            valid = band if b < per - 1 else band & ((n < nq - 1) | (col < WINDOW))
            st_t = st[wide[b], lanes[j]].T
            pb = jnp.where(valid, jnp.exp(jnp.minimum(sb - st_t[:1], 0.0)), 0.0)
            ds2.append((pb * (dpb - st_t[half:half + 1]) * scale).astype(BF16))
            p2.append(pb.astype(BF16))
        dk = [jnp.dot(x, qq[wide[b], lanes[j]], preferred_element_type=F32) for (j, b), x in zip(blocks, ds2)]
        dv = [jnp.dot(x, dyy[wide[b], lanes[j]], preferred_element_type=F32) for (j, b), x in zip(blocks, p2)]
        dq_ref[...] = _tile_up(dq, hb, per)
        dk_ref[...] = _tile_up(dk, hb, per)
        dv_ref[...] = _tile_up(dv, hb, per)

    assert voff % hb == 0
    cur = lambda off: pl.BlockSpec((None, tq, hb * HEAD), lambda r, h, n: (r, n, off // hb + h))
    prev = lambda off: pl.BlockSpec((None, WINDOW, hb * HEAD),
                                    lambda r, h, n: (r, jnp.maximum(n * per - 1, 0), off // hb + h))
    nxt = lambda off: pl.BlockSpec((None, WINDOW, hb * HEAD),
                                   lambda r, h, n: (r, jnp.minimum((n + 1) * per, last_blk), off // hb + h))
    out = jax.ShapeDtypeStruct((d, m, n_heads * HEAD), F32)
    return pl.pallas_call(
        body, name=name, grid=(d, n_heads // hb, nq),
        in_specs=[cur(0), nxt(0), cur(0), prev(0), cur(voff), prev(voff), cur(0), nxt(0), cur(0), nxt(0)],
        out_specs=[cur(0)] * 3, out_shape=[out] * 3,
        compiler_params=_params(("parallel",) * 3, 40),
    )(q, q, k, k, v, v, dy, dy, stats, stats)


def _mix_fwd(os_, ls_, name):
    tm = 256
    s, w = os_[0].shape
    n = len(os_)

    def fn(*vals):
        o, l = vals[:n], vals[n:]
        mx = functools.reduce(jnp.maximum, l)
        e = [jnp.exp(x - mx) for x in l]
        den = functools.reduce(jnp.add, e)
        y = functools.reduce(jnp.add, [ei * oi for ei, oi in zip(e, o)]) / den
        return y, mx + jnp.log(den)

    return _ew(fn, [_row_spec(a, tm) for a in list(os_) + list(ls_)],
               [((s, w), BF16, (tm, w), lambda i: (i, 0)), ((s, w), F32, (tm, w), lambda i: (i, 0))],
               (s // tm,), name)


def _mix_stats(dy, y, lse, n_heads, name):
    tm = 256
    s = dy.shape[0]
    w = n_heads * HEAD

    def fn(dys, ys, ls):
        lane = lax.broadcasted_iota(jnp.int32, (tm, HEAD), 1)
        packed = []
        for a, b, l in zip(_heads(dys, n_heads), _heads(ys, n_heads), _heads(ls, n_heads)):
            delta = jnp.sum(a * b.astype(F32), axis=-1, keepdims=True)
            packed.append(jnp.where(lane < HEAD // 2, l, delta))
        return dys, jnp.concatenate(packed, axis=1)

    blk = lambda a: (a, (tm, w), lambda i: (i, 0))
    out = lambda dt: ((s, w), dt, (tm, w), lambda i: (i, 0))
    return _ew(fn, [blk(dy), blk(y), blk(lse)], [out(BF16), out(F32)], (s // tm,), name)


SB_TQ = 1024
SB_TB = 512
SB_TB_FWD = 1024
SB_TK = 256


def _softplus(z):
    return jnp.where(z > 20.0, z, jnp.log(1.0 + jnp.exp(z)))


def _tri(t, cmp):
    rows = lax.broadcasted_iota(jnp.int32, (2 * t, t), 0)
    cols = lax.broadcasted_iota(jnp.int32, (2 * t, t), 1)
    return jnp.where(cmp(jnp.where(rows >= t, rows - t, rows), cols), 1.0, 0.0).astype(BF16)


def _tri_sum(x, tri):
    return jnp.dot(jnp.concatenate(_split_bf16(x), axis=1), tri, preferred_element_type=F32)


def _causal(rows, cols):
    return lax.broadcasted_iota(jnp.int32, (rows, cols), 1) < lax.broadcasted_iota(jnp.int32, (rows, cols), 0)


def _rowsum(x):
    return jnp.broadcast_to(jnp.sum(x, axis=-1, keepdims=True), (x.shape[0], HEAD))


def _over_keys(c, width):
    return jnp.concatenate([c] * (width // HEAD), axis=1)


def _from(x, r0):
    return x if r0 == 0 else x[r0:]


def _add_from(x, r0, upd):
    return x + upd if r0 == 0 else jnp.concatenate([x[:r0], x[r0:] + upd], axis=0)


def _sb_fwd(qkv, qoff, koff, voff, n_heads, name, comm=None):
    s = qkv.shape[0]
    tq, tb, tk = min(SB_TQ, s), min(SB_TB_FWD, s), SB_TK
    scale = HEAD ** -0.5
    host = _Host(comm, (n_heads, s // tq))

    def body(*refs):
        q_ref, k_ref, v_ref = refs[:3]
        comm_refs, ((o_ref, tot_ref), _) = host.split(refs, 3, 2)
        host.stage("first", comm_refs)
        i = pl.program_id(1)
        q = q_ref[...]
        after = _tri(tk, lambda a, b: a > b)

        def block(base, carry, o, diag_off):
            halves = list(reversed(range(tb // tk)))
            starts = [pl.multiple_of(base + h * tk, tk) for h in halves]
            r0s = [0 if diag_off is None else diag_off + h * tk for h in halves]
            masks = [None if diag_off is None else _causal(tq - r0, tk) for r0 in r0s]
            z = [lax.dot_general(_from(q, r0),k_ref[pl.ds(st, tk), :], NT_DIMS, preferred_element_type=F32) * scale
                 for st, r0 in zip(starts, r0s)]
            sp = [_softplus(zz) for zz in z]
            logsig = [zz - ss for zz, ss in zip(z, sp)]
            sp = [ss if m is None else jnp.where(m, ss, 0.0) for ss, m in zip(sp, masks)]
            sfx = [_tri_sum(ss, after) for ss in sp]
            probs = []
            for ls, sx, ss, m, r0 in zip(logsig, sfx, sp, masks, r0s):
                a = jnp.exp(ls - _over_keys(_from(carry, r0), tk) - sx)
                probs.append((a if m is None else jnp.where(m, a, 0.0)).astype(BF16))
                carry = _add_from(carry, r0, _rowsum(ss))
            for a, st, r0 in zip(probs, starts, r0s):
                o = _add_from(o, r0, jnp.dot(a, v_ref[pl.ds(st, tk), :], preferred_element_type=F32))
            return carry, o

        carry, o = jnp.zeros((tq, HEAD), F32), jnp.zeros((tq, HEAD), F32)
        for b in reversed(range(tq // tb)):
            carry, o = block(i * tq + b * tb, carry, o, b * tb)
        below = i * (tq // tb)
        carry, o = lax.fori_loop(0, below, lambda jj, co: block((below - 1 - jj) * tb, co[0], co[1], None),
                                 (carry, o))
        o_ref[...] = o.astype(o_ref.dtype)
        tot_ref[...] = carry
        host.stage("mid", comm_refs)
        host.stage("last", comm_refs)

    t = tq
    full = lambda off: pl.BlockSpec((s, HEAD), lambda h, i: (0, off + h))
    tile_spec = lambda off: pl.BlockSpec((t, HEAD), lambda h, i: (i, off + h))
    out = lambda dt: jax.ShapeDtypeStruct((s, n_heads * HEAD), dt)
    return pl.pallas_call(
        body, name=name, grid=(n_heads, s // t),
        in_specs=[tile_spec(qoff), full(koff), full(voff)] + host.in_specs,
        out_specs=[tile_spec(0), tile_spec(0)] + host.out_specs,
        out_shape=[out(BF16), out(F32)] + host.out_shape,
        scratch_shapes=host.scratch,
        compiler_params=_params(("parallel" if comm is None else "arbitrary", "arbitrary"), 40),
    )(qkv, qkv, qkv, *host.ins)


def _sb_bwd(qkv, qoff, koff, voff, do, tot, n_heads, name, comm=None):
    s = qkv.shape[0]
    tq, tb, tk = min(SB_TQ, s), SB_TB, SB_TK
    scale = HEAD ** -0.5
    host = _Host(comm, (n_heads, s // tq))

    def body(*refs):
        q_ref, k_ref, v_ref, do_ref, tot_ref = refs[:5]
        comm_refs, ((dq_ref, dk_ref, dv_ref), _) = host.split(refs, 5, 3)
        host.stage("first", comm_refs)
        i = pl.program_id(1)

        @pl.when(i == 0)
        def _():
            dk_ref[...] = jnp.zeros_like(dk_ref)
            dv_ref[...] = jnp.zeros_like(dv_ref)

        q = q_ref[...]
        do_b = do_ref[...].astype(BF16)
        total = tot_ref[...]
        upto = _tri(tk, lambda a, b: a <= b)
        before = _tri(tk, lambda a, b: a < b)[:tk]
        def block(base, lsum, psum, dq, diag_off):
            halves = list(range(tb // tk))
            starts = [pl.multiple_of(base + h * tk, tk) for h in halves]
            r0s = [0 if diag_off is None else diag_off + h * tk for h in halves]
            masks = [None if diag_off is None else _causal(tq - r0, tk) for r0 in r0s]
            keep = lambda x, m: x if m is None else jnp.where(m, x, 0.0)
            ks = [k_ref[pl.ds(st, tk), :] for st in starts]
            z = [lax.dot_general(_from(q, r0),kj, NT_DIMS, preferred_element_type=F32) * scale
                 for kj, r0 in zip(ks, r0s)]
            da = [lax.dot_general(_from(do_b, r0), v_ref[pl.ds(st, tk), :], NT_DIMS, preferred_element_type=F32)
                  for st, r0 in zip(starts, r0s)]
            sp = [_softplus(zz) for zz in z]
            logsig = [zz - ss for zz, ss in zip(z, sp)]
            sp = [keep(ss, m) for ss, m in zip(sp, masks)]
            pre = [_tri_sum(ss, upto) for ss in sp]
            probs, p = [], []
            for ls, px, ss, m, dd, r0 in zip(logsig, pre, sp, masks, da, r0s):
                a = keep(jnp.exp(ls - (_over_keys(_from(total, r0) - _from(lsum, r0), tk) - px)), m)
                probs.append(a.astype(BF16))
                p.append(a * dd)
                lsum = _add_from(lsum, r0, _rowsum(ss))
            cs = [jnp.dot(pp.astype(BF16), before, preferred_element_type=F32) for pp in p]
            dzs = []
            for ls, pp, cc, m, r0 in zip(logsig, p, cs, masks, r0s):
                c_all = _over_keys(_from(psum, r0), tk) + cc
                dzs.append(keep((pp - (pp + c_all) * jnp.exp(ls)) * scale, m).astype(BF16))
                psum = _add_from(psum, r0, _rowsum(pp))
            for dz_b, kj, r0 in zip(dzs, ks, r0s):
                dq = _add_from(dq, r0, jnp.dot(dz_b, kj, preferred_element_type=F32))
            for dz_b, a, st, r0 in zip(dzs, probs, starts, r0s):
                dk_ref[pl.ds(st, tk), :] += lax.dot_general(dz_b, _from(q, r0), TN_DIMS, preferred_element_type=F32)
                dv_ref[pl.ds(st, tk), :] += lax.dot_general(a, _from(do_b, r0), TN_DIMS,
                                                            preferred_element_type=F32)
            return lsum, psum, dq

        zero = jnp.zeros((tq, HEAD), F32)
        state = lax.fori_loop(0, i * (tq // tb), lambda j, c: block(j * tb, c[0], c[1], c[2], None),
                              (zero, zero, jnp.zeros((tq, HEAD), F32)))
        for b in range(tq // tb):
            state = block(i * tq + b * tb, *state, b * tb)
        dq_ref[...] = state[2].astype(dq_ref.dtype)
        host.stage("mid", comm_refs)
        host.stage("last", comm_refs)

    t = tq
    full = lambda off: pl.BlockSpec((s, HEAD), lambda h, i: (0, off + h))
    tile_spec = lambda off: pl.BlockSpec((t, HEAD), lambda h, i: (i, off + h))
    w = n_heads * HEAD
    return pl.pallas_call(
        body, name=name, grid=(n_heads, s // t),
        in_specs=[tile_spec(qoff), full(koff), full(voff), tile_spec(0), tile_spec(0)] + host.in_specs,
        out_specs=[tile_spec(0), full(0), full(0)] + host.out_specs,
        out_shape=[jax.ShapeDtypeStruct((s, w), BF16), jax.ShapeDtypeStruct((s, w), F32),
                   jax.ShapeDtypeStruct((s, w), F32)] + host.out_shape,
        scratch_shapes=host.scratch,
        compiler_params=_params(("parallel" if comm is None else "arbitrary", "arbitrary"), 48),
    )(qkv, qkv, qkv, do, tot, *host.ins)


def _coords():
    return lax.axis_index("x"), lax.axis_index("y"), lax.axis_index("c")


def _gather_plan(shards):
    n = len(shards)

    def run(stage, ins, outs, sems):
        send_sems, recv_sems, local_sems = sems
        x, y, c = _coords()
        me, sibling = (x, y, c), (x, y, 1 - c)
        chips = [(1 - x, y), (x, 1 - y), (1 - x, 1 - y)]

        def copy(w, k, block, to, src=None):
            dst = outs[w].at[4 * block[0] + 2 * block[1] + block[2]]
            return pltpu.make_async_remote_copy(
                src_ref=dst if src is None else src, dst_ref=dst,
                send_sem=send_sems.at[7 * w + k], recv_sem=recv_sems.at[7 * w + k],
                device_id=to, device_id_type=MESH)

        def mine():
            return [pltpu.make_async_copy(ins[w], outs[w].at[4 * x + 2 * y + c], local_sems.at[w]) for w in range(n)]

        def first():
            cps = []
            for w in range(n):
                cps.append(copy(w, 0, me, sibling, src=ins[w]))
                cps += [copy(w, 1 + j, me, (*chip, c), src=ins[w]) for j, chip in enumerate(chips)]
            return cps

        def passed():
            return [copy(w, 4 + j, (*chip, c), sibling) for w in range(n) for j, chip in enumerate(chips)]

        if stage == "first":
            for cp in mine() + first():
                cp.start()
        elif stage == "mid":
            onward = passed()
            for w in range(n):
                for j, chip in enumerate(chips):
                    copy(w, 1 + j, (*chip, c), me).wait_recv()
                    onward[3 * w + j].start()
        else:
            for w in range(n):
                copy(w, 0, sibling, me).wait_recv()
                for j, chip in enumerate(chips):
                    copy(w, 4 + j, (*chip, 1 - c), me).wait_recv()
            for cp in first() + passed():
                cp.wait_send()
            for cp in mine():
                cp.wait()

    return dict(
        ins=list(shards), run=run, stages=("first", "mid", "last"),
        out_shape=[jax.ShapeDtypeStruct((N_DEV,) + a.shape, a.dtype) for a in shards],
        scratch=[pltpu.SemaphoreType.DMA((7 * n,)), pltpu.SemaphoreType.DMA((7 * n,)), pltpu.SemaphoreType.DMA((n,))])


def _scatter_plan(grads):
    n = len(grads)

    def run(stage, ins, outs, sems):
        send_sems, recv_sems = sems
        x, y, c = _coords()
        cps = []
        for w in range(n):
            for mask in range(1, N_DEV):
                px, py, pc = x ^ (mask >> 2), y ^ ((mask >> 1) & 1), c ^ (mask & 1)
                cps.append(pltpu.make_async_remote_copy(
                    src_ref=ins[w].at[4 * px + 2 * py + pc], dst_ref=outs[w].at[mask - 1],
                    send_sem=send_sems.at[7 * w + mask - 1], recv_sem=recv_sems.at[7 * w + mask - 1],
                    device_id=(px, py, pc), device_id_type=MESH))
        for cp in cps:
            if stage == "first":
                cp.start()
            else:
                cp.wait()

    return dict(
        ins=list(grads), run=run, stages=("first", "last"),
        out_shape=[jax.ShapeDtypeStruct((N_DEV - 1,) + a.shape[1:], a.dtype) for a in grads],
        scratch=[pltpu.SemaphoreType.DMA((7 * n,)), pltpu.SemaphoreType.DMA((7 * n,))])


def _run_plan(plan, name):
    n_in, n_out = len(plan["ins"]), len(plan["out_shape"])

    def body(*refs):
        for stage in plan["stages"]:
            plan["run"](stage, refs[:n_in], refs[n_in:n_in + n_out], refs[n_in + n_out:])

    any_spec = pl.BlockSpec(memory_space=pl.ANY)
    return pl.pallas_call(
        body, name=name, in_specs=[any_spec] * n_in, out_specs=[any_spec] * n_out,
        out_shape=plan["out_shape"], scratch_shapes=plan["scratch"],
    )(*plan["ins"])


def _gather_rows(v):
    rows, width = v.shape

    def body(v_ref, out_ref, send_sems, recv_sems):
        x, y, c = _coords()
        out_ref[pl.ds(pl.multiple_of((4 * x + 2 * y + c) * rows, rows), rows), :] = v_ref[...]
        cps = []
        for mask in range(1, N_DEV):
            peer = (x ^ (mask >> 2), y ^ ((mask >> 1) & 1), c ^ (mask & 1))
            dst = out_ref.at[pl.ds(pl.multiple_of((4 * x + 2 * y + c) * rows, rows), rows), :]
            cps.append(pltpu.make_async_remote_copy(
                src_ref=v_ref, dst_ref=dst, send_sem=send_sems.at[mask - 1], recv_sem=recv_sems.at[mask - 1],
                device_id=peer, device_id_type=MESH))
        for cp in cps:
            cp.start()
        for cp in cps:
            cp.wait()

    vmem = pl.BlockSpec(memory_space=pltpu.VMEM)
    return pl.pallas_call(
        body, name="gather_small",
        in_specs=[vmem], out_specs=vmem,
        out_shape=jax.ShapeDtypeStruct((N_DEV * rows, width), F32),
        scratch_shapes=[pltpu.SemaphoreType.DMA((N_DEV - 1,)), pltpu.SemaphoreType.DMA((N_DEV - 1,))],
    )(v)


def _adamw(w, g, m, v):
    m = ADAM_B1 * m + (1.0 - ADAM_B1) * g
    v = ADAM_B2 * v + (1.0 - ADAM_B2) * jnp.square(g)
    m_hat = m / (1.0 - ADAM_B1 ** ADAM_STEP)
    v_hat = v / (1.0 - ADAM_B2 ** ADAM_STEP)
    delta = -ADAM_LR * (m_hat / (jnp.sqrt(v_hat) + ADAM_EPS) + ADAM_WD * w)
    return delta, m, v


def _tile_rows(rows, cols):
    tm = 1 << int(math.log2(max(2 * SUBLANES, (1 << 18) // cols)))
    while rows % tm:
        tm //= 2
    assert tm >= 2 * SUBLANES, (rows, cols)
    return tm


def _reduce_adam(w, m, v, own, got, name):
    r, c = w.shape
    tm = max(2 * SUBLANES, _tile_rows(r, c) // 2)

    def fn(wv, mv, vv, a, *peers):
        g = a
        for pv in peers:
            g = g + pv.astype(F32)
        return (g,) + _adamw(wv, g, mv, vv)

    blk = lambda a: (a, (tm, c), lambda i: (i, 0))
    got_blk = lambda j: (got, (None, tm, c), lambda i, j=j: (j, i, 0))
    return _ew(fn, [blk(w), blk(m), blk(v), blk(own)] + [got_blk(j) for j in range(N_DEV - 1)],
               [((r, c), F32, (tm, c), lambda i: (i, 0))] * 4, (r // tm,), name)


def _small_adam(gathered, params, moms, vels, widths):
    n = len(params)
    total = gathered.shape[1]

    def body(*refs):
        g_ref = refs[0]
        p_refs, m_refs, v_refs = refs[1:1 + n], refs[1 + n:1 + 2 * n], refs[1 + 2 * n:1 + 3 * n]
        sum_ref = refs[1 + 3 * n]
        outs = refs[2 + 3 * n:]
        g = g_ref[0:1, :]
        for p in range(1, N_DEV):
            g = g + g_ref[p * SUBLANES:p * SUBLANES + 1, :]
        sum_ref[...] = g
        off = 0
        for i, wd in enumerate(widths):
            d, m2, v2 = _adamw(p_refs[i][...], g[:, off:off + wd], m_refs[i][...], v_refs[i][...])
            outs[3 * i][...] = d
            outs[3 * i + 1][...] = m2
            outs[3 * i + 2][...] = v2
            off += wd

    vmem = pl.BlockSpec(memory_space=pltpu.VMEM)
    out_shape = [jax.ShapeDtypeStruct((1, total), F32)]
    for wd in widths:
        out_shape += [jax.ShapeDtypeStruct((1, wd), F32)] * 3
    return pl.pallas_call(
        body, name="small_adam",
        in_specs=[vmem] * (1 + 3 * n), out_specs=[vmem] * len(out_shape), out_shape=out_shape,
    )(gathered, *params, *moms, *vels)


def _cast_bf16(a, name):
    r, c = a.shape
    tm = _tile_rows(r, c)
    return _ew(lambda v: v, [(a, (tm, c), lambda i: (i, 0))], [((r, c), BF16, (tm, c), lambda i: (i, 0))],
               (r // tm,), name)[0]


def _fold_loss(parts, name):
    r, c = parts.shape

    def fn(v):
        return jnp.broadcast_to(jnp.sum(jnp.sum(v, axis=0, keepdims=True), axis=1, keepdims=True), (SUBLANES, HEAD))

    return _ew(fn, [_const_spec(parts)], [((SUBLANES, HEAD), F32, (SUBLANES, HEAD), lambda i: (0, 0))], (1,),
               name)[0][0:1]


def kernel(x, p, g_mix, w_in, qn_gain, kn_gain, w_branch_a, w_branch_b, w_out, g_mlp, w_up, w_down, g_ple, w_ple_gate, w_ple_proj, loss_target, m_g_mix, m_w_in, m_qn_gain, m_kn_gain, m_w_branch_a, m_w_branch_b, m_w_out, m_g_mlp, m_w_up, m_w_down, m_g_ple, m_w_ple_gate, m_w_ple_proj, v_g_mix, v_w_in, v_qn_gain, v_kn_gain, v_w_branch_a, v_w_branch_b, v_w_out, v_g_mlp, v_w_up, v_w_down, v_g_ple, v_w_ple_gate, v_w_ple_proj):
    x2 = x[0]
    tgt = loss_target[0]
    s, d = x2.shape
    wd_ = w_branch_a.shape[1]
    nh = wd_ // HEAD
    dff = w_up.shape[1]
    qkv_w = 6 * wd_
    tiles = lambda cols: cols // HEAD

    big = [w_in[0], w_branch_a[0], w_branch_b[0], w_out[0], w_up[0], w_down[0], w_ple_gate[0], w_ple_proj[0]]
    names = ["w_in", "w_branch_a", "w_branch_b", "w_out", "w_up", "w_down", "w_ple_gate", "w_ple_proj"]
    row_sharded = [False, False, False, True, False, True, True, False]
    shards = [_cast_bf16(a, "cast_" + nm) for a, nm in zip(big, names)]
    as_weight = lambda g, rs: g.reshape((1, N_DEV * g.shape[1], g.shape[2])) if rs else g
    as_blocks = lambda g, rs: g.reshape((N_DEV, g.shape[1] // N_DEV, g.shape[2])) if rs else g
    win = _run_plan(_gather_plan(shards[:1]), "all_gather_w_in")[0]

    tm = 1024 if s % 1024 == 0 else s
    tm_in = tk_s = 2048 if s % 2048 == 0 else tm
    tn_of = lambda n: 512 if n % 512 == 0 else (256 if n % 256 == 0 else n)
    tn_in = 256 if win.shape[2] % 256 == 0 else HEAD

    h = _rms_fwd(x2, g_mix, "norm_mix")
    qk_raw = _mm("nn", h, win, tm=tm_in, tn=tn_in, tk=d, out_dtypes=[F32], name="proj_qk",
                 n_off=0, n_cnt=2 * wd_ // tn_in)[0]
    rest = _mm("nn", h, win, tm=4096 if s % 4096 == 0 else tm_in, tn=tn_in, tk=d, out_dtypes=[BF16], name="proj_rest",
               n_off=2 * wd_ // tn_in, n_cnt=(win.shape[0] * win.shape[2] - 2 * wd_) // tn_in)[0]
    o_va, o_qb, o_kb, o_vb, o_ga, o_gb = 0, tiles(wd_), tiles(2 * wd_), tiles(3 * wd_), tiles(4 * wd_), tiles(4 * wd_ + d)
    tabs = _rope_tables(s)
    qa = _headnorm_rope(qk_raw, 0, qn_gain, tabs, nh, "rope_q")
    ka = _headnorm_rope(qk_raw, 1, kn_gain, tabs, nh, "rope_k")
    va = rest[:, :wd_]

    outs, lses = [], []
    for dil in DILATIONS:
        o_g, l_g = _dilated_fwd(_phase_major(qa, dil), _phase_major(ka, dil), _phase_major(va, dil), 0, nh,
                                f"dilated_fwd_{dil}")
        outs.append(_token_major(o_g))
        lses.append(_token_major(l_g))
    ya, lse_all = _mix_fwd(outs, lses, "mix_fwd")
    yb, sb_tot, *others = _sb_fwd(rest, o_qb, o_kb, o_vb, nh, "sb_fwd", comm=_gather_plan(shards[1:]))
    wba, wbb, wout, wup, wdown, wgate, wproj = [as_weight(g, rs) for g, rs in zip(others, row_sharded[1:])]

    plain = lambda g: g.transpose(1, 0, 2).reshape(1, g.shape[1], N_DEV * g.shape[2])
    wba_p, wbb_p = plain(wba), plain(wbb)
    tn_d = tn_of(d)
    za = _mm("nn", ya, wba_p, tm=tm, tn=tn_d, tk=wd_, out_dtypes=[BF16], name="branch_a")[0]

    def merge(acc, zav, gav, gbv):
        return _sigmoid(gav.astype(F32)) * zav.astype(F32) + _sigmoid(gbv.astype(F32)) * acc, acc

    merged, zb = _mm("nn", yb, wbb_p, tm=tm, tn=tn_d, tk=wd_, out_dtypes=[BF16, BF16], name="branch_b_merge",
                     epilogue=merge, extras=[(za, 0), (rest, o_ga * HEAD // tn_d), (rest, o_gb * HEAD // tn_d)])
    x1 = _mm("nn", merged, wout, tm=tm, tn=512, tk=d, out_dtypes=[F32], name="out_proj",
             epilogue=lambda acc, xv: (acc + xv,), extras=[(x2, 0)])[0]

    hm = _rms_fwd(x1, g_mlp, "norm_mlp")
    tn_u = min(wup.shape[2], 1024)
    u, act = _mm("nn", hm, wup, tm=tm, tn=tn_u, tk=d, out_dtypes=[BF16, BF16], name="mlp_up",
                 epilogue=lambda acc: (acc, jnp.square(jnp.maximum(acc, 0.0))))
    x3 = _mm("nn", act, wdown, tm=tm, tn=1024, tk=min(dff, 2048), out_dtypes=[F32], name="mlp_down",
             epilogue=lambda acc, xv: (acc + xv,), extras=[(x1, 0)])[0]

    hp = _rms_fwd(x3, g_ple, "norm_ple")
    p_b = _cast_bf16(p[0, 0], "cast_p")
    pp = _mm("nn", p_b, wproj, tm=tm, tn=tn_of(wproj.shape[2]), tk=p_b.shape[1], out_dtypes=[BF16],
             name="ple_proj")[0]

    def head(acc, ppv, xv, tv):
        sg = _sigmoid(acc)
        ppf = ppv.astype(F32)
        err = xv + ppf * sg - tv
        dy = err / d
        sq = jnp.square(err)
        return dy, dy * sg, dy * ppf * sg * (1.0 - sg), sq.reshape(-1, SUBLANES, sq.shape[-1]).sum(axis=0)

    n_i = s // tm
    dy, d_pp, d_gt, sq_parts = _mm(
        "nn", hp, wgate, tm=tm, tn=512, tk=d, out_dtypes=[F32, BF16, BF16], name="ple_gate_loss", epilogue=head,
        extras=[(pp, 0), (x3, 0), (tgt, 0)],
        extra_outs=[((n_i * SUBLANES, d), F32, (SUBLANES, 512), lambda i, j: (i, j))])
    loss_vec = _fold_loss(sq_parts, "loss_fold") * 0.5 / d

    both = [F32, BF16]
    g_wproj = _mm("tn", p_b, d_pp, tm=p_b.shape[1], tn=tn_of(wproj.shape[2]), tk=tk_s, out_dtypes=both,
                  name="grad_w_ple_proj", out_nb=N_DEV)
    g_wgate = _mm("tn", hp, d_gt, tm=1024, tn=1024, tk=tk_s, out_dtypes=both, name="grad_w_ple_gate")
    d_hp = _mm("nt", d_gt, wgate, tm=tm, tn=512, tk=d, out_dtypes=[F32], name="d_hp")[0]
    dx3, dx3_b, g_gple = _rms_bwd(d_hp, x3, g_ple, dy, "norm_ple_bwd")

    d_u = _mm("nt", dx3_b, wdown, tm=tm, tn=1024, tk=d, out_dtypes=[BF16], name="d_u",
              epilogue=lambda acc, uv: (acc * (2.0 * jnp.maximum(uv.astype(F32), 0.0)),), extras=[(u, 0)])[0]
    g_wdown = _mm("tn", act, dx3_b, tm=1024, tn=1024, tk=tk_s, out_dtypes=both, name="grad_w_down")
    g_wup = _mm("tn", hm, d_u, tm=1024, tn=wup.shape[2], tk=tk_s, out_dtypes=both, name="grad_w_up", out_nb=N_DEV)
    d_hm = _mm("nt", d_u, wup, tm=tm, tn=min(d, 2048), tk=wup.shape[2], out_dtypes=[F32], name="d_hm")[0]
    dx1, dx1_b, g_gmlp = _rms_bwd(d_hm, x1, g_mlp, dx3, "norm_mlp_bwd")

    def unmerge(acc, gav, gbv, zav, zbv):
        sa, sb = _sigmoid(gav.astype(F32)), _sigmoid(gbv.astype(F32))
        return acc * sa, acc * sb, acc * zav.astype(F32) * sa * (1.0 - sa), acc * zbv.astype(F32) * sb * (1.0 - sb)

    d_za, d_zb, d_ga, d_gb = _mm(
        "nt", dx1_b, wout, tm=tm, tn=512, tk=d, out_dtypes=[BF16] * 4, name="d_merged", epilogue=unmerge,
        extras=[(rest, o_ga * HEAD // 512), (rest, o_gb * HEAD // 512), (za, 0), (zb, 0)])
    g_wout = _mm("tn", merged, dx1_b, tm=1024, tn=1024, tk=tk_s, out_dtypes=both, name="grad_w_out")
    g_wba = _mm("tn", ya, d_za, tm=wd_, tn=wba.shape[2], tk=tk_s, out_dtypes=both, name="grad_w_branch_a",
                out_nb=N_DEV)
    g_wbb = _mm("tn", yb, d_zb, tm=wd_, tn=wbb.shape[2], tk=tk_s, out_dtypes=both, name="grad_w_branch_b",
                out_nb=N_DEV)
    d_ya = _mm("nt", d_za, wba_p, tm=tm, tn=wd_, tk=d, out_dtypes=[F32], name="d_ya")[0]
    d_yb = _mm("nt", d_zb, wbb_p, tm=tm, tn=wd_, tk=d, out_dtypes=[F32], name="d_yb")[0]

    grads = [None, g_wba, g_wbb, g_wout, g_wup, g_wdown, g_wgate, g_wproj]
    early = _scatter_plan([as_blocks(g[1], rs) for g, rs in zip(grads[1:], row_sharded[1:])])
    d_qb, d_kb, d_vb, *got = _sb_bwd(rest, o_qb, o_kb, o_vb, d_yb, sb_tot, nh, "sb_bwd", comm=early)
    d_ya_b, stats = _mix_stats(d_ya, ya, lse_all, nh, "mix_stats")
    dqs, dks, dvs = [], [], []
    for dil in DILATIONS:
        dq_g, dk_g, dv_g = _dilated_bwd(
            _phase_major(qa, dil), _phase_major(ka, dil), _phase_major(va, dil), 0, _phase_major(d_ya_b, dil),
            _phase_major(stats, dil), nh, f"dilated_bwd_{dil}")
        dqs.append(_token_major(dq_g))
        dks.append(_token_major(dk_g))
        dvs.append(_token_major(dv_g))
    d_qa, g_qn = _headnorm_rope_bwd(dqs, qk_raw, 0, qn_gain, tabs, nh, "rope_q_bwd")
    d_ka, g_kn = _headnorm_rope_bwd(dks, qk_raw, 1, kn_gain, tabs, nh, "rope_k_bwd")
    tmr = 256
    d_va = _ew(lambda a, b, c: a + b + c, [_row_spec(a, tmr) for a in dvs],
               [((s, wd_), BF16, (tmr, wd_), lambda i: (i, 0))], (s // tmr,), "sum_dv")[0]
    d_proj = jnp.concatenate([d_qa, d_ka, d_va, d_qb, d_kb.astype(BF16), d_vb.astype(BF16), d_ga, d_gb], axis=1)

    grads[0] = _mm("tn", h, d_proj, tm=1024, tn=win.shape[2], tk=tk_s, out_dtypes=both, name="grad_w_in",
                   out_nb=N_DEV)
    d_h, got_in = _mm("nt", d_proj, win, tm=tm, tn=min(d, 2048), tk=win.shape[2], out_dtypes=[F32], name="d_h",
                      comm=_scatter_plan([grads[0][1]]))
    got = [got_in] + got
    dx, _, g_gmix = _rms_bwd(d_h, x2, g_mix, dx1, "norm_mix_bwd")

    cx, cy, cc = _coords()
    me = 4 * cx + 2 * cy + cc
    moms = [m_w_in, m_w_branch_a, m_w_branch_b, m_w_out, m_w_up, m_w_down, m_w_ple_gate, m_w_ple_proj]
    vels = [v_w_in, v_w_branch_a, v_w_branch_b, v_w_out, v_w_up, v_w_down, v_w_ple_gate, v_w_ple_proj]
    big_out = {}
    for i, nm in enumerate(names):
        own = lax.dynamic_index_in_dim(as_blocks(grads[i][0], row_sharded[i]), me, axis=0, keepdims=False)
        big_out[nm] = [a[None] for a in _reduce_adam(big[i], moms[i][0], vels[i][0], own, got[i], "adam_" + nm)]

    small_names = ["g_mix", "qn_gain", "kn_gain", "g_mlp", "g_ple"]
    small_p = [g_mix, qn_gain, kn_gain, g_mlp, g_ple]
    small_m = [m_g_mix, m_qn_gain, m_kn_gain, m_g_mlp, m_g_ple]
    small_v = [v_g_mix, v_qn_gain, v_kn_gain, v_g_mlp, v_g_ple]
    small_g = [g_gmix, g_qn, g_kn, g_gmlp, g_gple]
    widths = [a.shape[1] for a in small_p]
    vec = jnp.concatenate(small_g + [loss_vec], axis=1)
    vec = jnp.pad(vec, ((0, SUBLANES - 1), (0, 0)))
    res = _small_adam(_gather_rows(vec), small_p, small_m, small_v, widths)
    summed = res[0]
    small_out, off = {}, 0
    for i, nm in enumerate(small_names):
        small_out[nm] = [summed[:, off:off + widths[i]]] + list(res[1 + 3 * i:4 + 3 * i])
        off += widths[i]
    loss = summed[0, off]

    order = ["g_mix", "w_in", "qn_gain", "kn_gain", "w_branch_a", "w_branch_b", "w_out", "g_mlp", "w_up", "w_down",
             "g_ple", "w_ple_gate", "w_ple_proj"]
    table = {**big_out, **small_out}
    result = [loss, dx[None]]
    for kind in range(4):
        result += [table[nm][kind] for nm in order]
    return tuple(result)
```

```python
import functools
import math

import jax
import jax.numpy as jnp
from jax import lax
from jax.experimental import pallas as pl
from jax.experimental.pallas import tpu as pltpu

F32 = jnp.float32
BF16 = jnp.bfloat16
MESH = pl.DeviceIdType.MESH

HEAD = 128
WINDOW = 128
DILATIONS = (1, 4, 16)
ROT = HEAD // 4
ROPE_THETA = 500000.0
EPS = 1e-6
NEG = -1e30
N_DEV = 8

ADAM_LR = 0.001
ADAM_B1 = 0.9
ADAM_B2 = 0.999
ADAM_EPS = 1e-08
ADAM_WD = 0.01
ADAM_STEP = 10

SUBLANES = 8
VMEM_CAP_MB = 56

NT_DIMS = (((1,), (1,)), ((), ()))
TN_DIMS = (((0,), (0,)), ((), ()))


def _params(semantics, vmem_mb):
    return pltpu.CompilerParams(dimension_semantics=semantics, vmem_limit_bytes=min(vmem_mb, VMEM_CAP_MB) << 20)


def _sigmoid(x):
    return 0.5 + 0.5 * jnp.tanh(0.5 * x)


def _split_bf16(x):
    hi = x.astype(BF16)
    lo = (x - hi.astype(F32)).astype(BF16)
    return hi, lo


class _Host:
    def __init__(self, comm, grid):
        self.comm, self.grid = comm, grid
        any_spec = pl.BlockSpec(memory_space=pl.ANY)
        self.ins = list(comm["ins"]) if comm else []
        self.out_shape = list(comm["out_shape"]) if comm else []
        self.scratch = list(comm["scratch"]) if comm else []
        self.in_specs = [any_spec] * len(self.ins)
        self.out_specs = [any_spec] * len(self.out_shape)

    def split(self, refs, n_in, n_out):
        pos = n_in
        c_in = refs[pos:pos + len(self.ins)]
        pos += len(self.ins)
        outs = refs[pos:pos + n_out]
        pos += n_out
        c_out = refs[pos:pos + len(self.out_shape)]
        pos += len(self.out_shape)
        own = len(refs) - pos - len(self.scratch)
        return (c_in, c_out, refs[pos + own:]), (outs, refs[pos:pos + own])

    def stage(self, which, comm_refs):
        if self.comm is None or which not in self.comm["stages"]:
            return
        grid = self.grid
        at = {"first": [0] * len(grid), "mid": [grid[0] // 2] + [0] * (len(grid) - 1),
              "last": [g - 1 for g in grid]}[which]
        cond = functools.reduce(jnp.logical_and, [pl.program_id(ax) == v for ax, v in enumerate(at)])

        @pl.when(cond)
        def _():
            self.comm["run"](which, *comm_refs)


def _ew(fn, ins, outs, grid, name, colsums=(), vmem_mb=40, comm=None):
    n_in, n_out, n_cs = len(ins), len(outs), len(colsums)
    steps = math.prod(grid)
    host = _Host(comm, grid)

    def body(*refs):
        in_refs = refs[:n_in]
        comm_refs, (own_outs, acc_refs) = host.split(refs, n_in, n_out + n_cs)
        out_refs, cs_refs = own_outs[:n_out], own_outs[n_out:]
        host.stage("first", comm_refs)
        vals = fn(*[r[...] for r in in_refs])
        if not isinstance(vals, tuple):
            vals = (vals,)
        for r, v in zip(out_refs, vals[:n_out]):
            r[...] = v.astype(r.dtype)
        if n_cs:
            step = pl.program_id(0)
            for ax in range(1, len(grid)):
                step = step * grid[ax] + pl.program_id(ax)
            for acc, cs, v in zip(acc_refs, cs_refs, vals[n_out:]):
                part = v.reshape(-1, SUBLANES, v.shape[-1]).sum(axis=0)

                @pl.when(step == 0)
                def _(acc=acc, part=part):
                    acc[...] = part

                @pl.when(step > 0)
                def _(acc=acc, part=part):
                    acc[...] += part

                @pl.when(step == steps - 1)
                def _(acc=acc, cs=cs):
                    cs[...] = acc[...].sum(axis=0, keepdims=True)

        host.stage("mid", comm_refs)
        host.stage("last", comm_refs)

    out_shape = [jax.ShapeDtypeStruct(s, d) for s, d, _, _ in outs]
    out_specs = [pl.BlockSpec(b, m) for _, _, b, m in outs]
    for w in colsums:
        out_shape.append(jax.ShapeDtypeStruct((1, w), F32))
        out_specs.append(pl.BlockSpec((1, w), lambda *_: (0, 0)))
    sem = ("arbitrary",) * len(grid) if n_cs or comm else ("parallel",) * len(grid)
    res = pl.pallas_call(
        body, name=name, grid=grid,
        in_specs=[pl.BlockSpec(b, m) for _, b, m in ins] + host.in_specs,
        out_specs=out_specs + host.out_specs, out_shape=out_shape + host.out_shape,
        scratch_shapes=[pltpu.VMEM((SUBLANES, w), F32) for w in colsums] + host.scratch,
        compiler_params=_params(sem, vmem_mb),
    )(*[a for a, _, _ in ins], *host.ins)
    return res


def _row_spec(a, tm):
    return (a, (tm, a.shape[1]), lambda i: (i, 0))


def _const_spec(a):
    return (a, a.shape, lambda *_: (0,) * a.ndim)


def _mm(mode, a, b, *, tm, tn, tk, out_dtypes, name, epilogue=None, extras=(), n_off=0, n_cnt=None,
        out_nb=1, extra_outs=(), vmem_mb=52, comm=None):
    if mode == "nn":
        m, kdim = a.shape
        nb, _, n = b.shape
        npb = n // tn
        ncols = nb * n
        n_tiles = (ncols // tn) if n_cnt is None else n_cnt
        a_spec = pl.BlockSpec((tm, tk), lambda i, j, k: (i, k))
        b_spec = pl.BlockSpec((None, tk, tn), lambda i, j, k: ((j + n_off) // npb, k, (j + n_off) % npb))
        dims = (((1,), (0,)), ((), ()))
    elif mode == "nt":
        m, kdim = a.shape
        nb, nout, n = b.shape
        kpb = n // tk
        n_tiles = nout // tn
        a_spec = pl.BlockSpec((tm, tk), lambda i, j, k: (i, k))
        b_spec = pl.BlockSpec((None, tn, tk), lambda i, j, k: (k // kpb, j, k % kpb))
        dims = NT_DIMS
    else:
        kdim, m = a.shape
        ncols = b.shape[1]
        n_tiles = ncols // tn
        a_spec = pl.BlockSpec((tk, tm), lambda i, j, k: (k, i))
        b_spec = pl.BlockSpec((tk, tn), lambda i, j, k: (k, j))
        dims = TN_DIMS
    nk = kdim // tk
    assert kdim % tk == 0 and m % tm == 0
    grid = (m // tm, n_tiles, nk)
    n_ex, n_out = len(extras), len(out_dtypes) + len(extra_outs)
    host = _Host(comm, grid)

    def body(*refs):
        a_ref, b_ref = refs[0], refs[1]
        ex_refs = refs[2:2 + n_ex]
        comm_refs, (out_refs, scratch) = host.split(refs, 2 + n_ex, n_out)
        host.stage("first", comm_refs)

        def finish(acc):
            vals = (acc,) * n_out if epilogue is None else epilogue(acc, *[r[...] for r in ex_refs])
            for r, v in zip(out_refs, vals):
                r[...] = v.astype(r.dtype)

        def part():
            return lax.dot_general(a_ref[...], b_ref[...], dims, preferred_element_type=F32)

        if nk == 1:
            finish(part())
        else:
            acc_ref = scratch[0]
            k = pl.program_id(2)

            @pl.when(k == 0)
            def _():
                acc_ref[...] = part()

            @pl.when((k > 0) & (k < nk - 1))
            def _():
                acc_ref[...] += part()

            @pl.when(k == nk - 1)
            def _():
                finish(acc_ref[...] + part())

        host.stage("mid", comm_refs)
        host.stage("last", comm_refs)

    if mode == "tn":
        npo = (ncols // out_nb) // tn
        out_shape = [jax.ShapeDtypeStruct((out_nb, m, ncols // out_nb), d) for d in out_dtypes]
        out_specs = [pl.BlockSpec((None, tm, tn), lambda i, j, k: (j // npo, i, j % npo)) for _ in out_dtypes]
    else:
        out_shape = [jax.ShapeDtypeStruct((m, n_tiles * tn), d) for d in out_dtypes]
        out_specs = [pl.BlockSpec((tm, tn), lambda i, j, k: (i, j)) for _ in out_dtypes]
    for s, d, blk, imap in extra_outs:
        out_shape.append(jax.ShapeDtypeStruct(s, d))
        out_specs.append(pl.BlockSpec(blk, lambda i, j, k, imap=imap: imap(i, j)))
    ex_specs = [pl.BlockSpec((tm, tn), lambda i, j, k, off=off: (i, j + off)) for _, off in extras]
    sem = ("parallel", "parallel", "arbitrary") if comm is None else ("arbitrary",) * 3
    return pl.pallas_call(
        body, name=name, grid=grid,
        in_specs=[a_spec, b_spec] + ex_specs + host.in_specs,
        out_specs=out_specs + host.out_specs, out_shape=out_shape + host.out_shape,
        scratch_shapes=([pltpu.VMEM((tm, tn), F32)] if nk > 1 else []) + host.scratch,
        compiler_params=_params(sem, vmem_mb),
    )(a, b, *[e for e, _ in extras], *host.ins)


def _rms_fwd(x, g, name, comm=None):
    tm = 256

    def fn(xv, gv):
        r = lax.rsqrt(jnp.mean(xv * xv, axis=-1, keepdims=True) + EPS)
        return xv * r * gv

    res = _ew(fn, [_row_spec(x, tm), _const_spec(g)], [(x.shape, BF16, (tm, x.shape[1]), lambda i: (i, 0))],
              (x.shape[0] // tm,), name, comm=comm)
    return res[0] if comm is None else res


def _rms_bwd(dh, x, g, res, name):
    tm = 256
    d = x.shape[1]

    def fn(dhv, xv, gv, rv):
        r = lax.rsqrt(jnp.mean(xv * xv, axis=-1, keepdims=True) + EPS)
        xh = xv * r
        dyg = dhv * gv
        dx = rv + r * (dyg - xh * jnp.mean(dyg * xh, axis=-1, keepdims=True))
        return dx, dx, dhv * xh

    spec = lambda dt: (x.shape, dt, (tm, d), lambda i: (i, 0))
    return _ew(fn, [_row_spec(dh, tm), _row_spec(x, tm), _const_spec(g), _row_spec(res, tm)],
               [spec(F32), spec(BF16)], (x.shape[0] // tm,), name, colsums=(d,))


def _rope_tables(s):
    half = ROT // 2
    pos = jnp.arange(s, dtype=F32)
    inv = ROPE_THETA ** (-jnp.arange(0, ROT, 2, dtype=F32) / ROT)
    ang = pos[:, None] * inv[None, :]
    cos, sin = jnp.cos(ang), jnp.sin(ang)
    pad = jnp.zeros((s, HEAD - ROT), F32)
    c = jnp.concatenate([cos, cos, pad + 1.0], axis=1)
    a = jnp.concatenate([-sin, jnp.zeros_like(sin), pad], axis=1)
    b = jnp.concatenate([jnp.zeros_like(sin), sin, pad], axis=1)
    return c, a, b


def _heads(x, n_heads):
    return [x[:, h * HEAD:(h + 1) * HEAD] for h in range(n_heads)]


def _headnorm_rope(proj, part, gain, tabs, n_heads, name):
    tm = 512
    s = proj.shape[0]
    w = n_heads * HEAD

    def fn(xs, gv, c, a, b):
        outs = []
        for xv in _heads(xs, n_heads):
            r = lax.rsqrt(jnp.mean(xv * xv, axis=-1, keepdims=True) + EPS)
            y = xv * r * gv
            outs.append(c * y + a * pltpu.roll(y, HEAD - ROT // 2, 1) + b * pltpu.roll(y, ROT // 2, 1))
        return jnp.concatenate(outs, axis=1)

    tab = lambda t: (t, (tm, HEAD), lambda i: (i, 0))
    return _ew(fn, [(proj, (tm, w), lambda i: (i, part)), (gain, (1, HEAD), lambda i: (0, 0))] + [tab(t) for t in tabs],
               [((s, w), BF16, (tm, w), lambda i: (i, 0))], (s // tm,), name)[0]


def _headnorm_rope_bwd(dys, proj, part, gain, tabs, n_heads, name):
    tm = 256
    s = proj.shape[0]
    w = n_heads * HEAD
    n_dy = len(dys)

    def fn(*vals):
        dy_all = vals[0]
        for v in vals[1:n_dy]:
            dy_all = dy_all + v
        xs, gv, c, a, b = vals[n_dy:]
        dxs, dgain = [], None
        for dy, xv in zip(_heads(dy_all, n_heads), _heads(xs, n_heads)):
            dn = c * dy + pltpu.roll(a * dy, ROT // 2, 1) + pltpu.roll(b * dy, HEAD - ROT // 2, 1)
            r = lax.rsqrt(jnp.mean(xv * xv, axis=-1, keepdims=True) + EPS)
            xh = xv * r
            dyg = dn * gv
            dxs.append(r * (dyg - xh * jnp.mean(dyg * xh, axis=-1, keepdims=True)))
            dgain = dn * xh if dgain is None else dgain + dn * xh
        return jnp.concatenate(dxs, axis=1), dgain

    tab = lambda t: (t, (tm, HEAD), lambda i: (i, 0))
    return _ew(fn, [(d, (tm, w), lambda i: (i, 0)) for d in dys]
               + [(proj, (tm, w), lambda i: (i, part)), (gain, (1, HEAD), lambda i: (0, 0))] + [tab(t) for t in tabs],
               [((s, w), BF16, (tm, w), lambda i: (i, 0))], (s // tm,), name, colsums=(HEAD,))


def _phase_major(a, d):
    s, w = a.shape
    if d == 1:
        return a.reshape(1, s, w)
    return a.reshape(s // d, d, w).transpose(1, 0, 2)


def _token_major(a):
    d, m, w = a.shape
    if d == 1:
        return a.reshape(m, w)
    return a.transpose(1, 0, 2).reshape(m * d, w)


DIL_ROWS = 2048


def _dil_tq(m):
    return min(DIL_ROWS, m)


def _dil_heads(tq, n_heads):
    return max(1, min(n_heads, DIL_ROWS // tq))


def _tile_up(parts, hb, per):
    cols = [jnp.concatenate(parts[j * per:(j + 1) * per], axis=0) for j in range(hb)]
    return cols[0] if hb == 1 else jnp.concatenate(cols, axis=1)


def _dilated_fwd(q, k, v, voff, n_heads, name):
    d, m, _ = q.shape
    tq = _dil_tq(m)
    nq = m // tq
    per = tq // WINDOW
    hb = _dil_heads(tq, n_heads)
    scale = HEAD ** -0.5
    blocks = [(j, b) for j in range(hb) for b in range(per)]
    lanes = [slice(j * HEAD, (j + 1) * HEAD) for j in range(hb)]

    def body(q_ref, kc_ref, kp_ref, vc_ref, vp_ref, o_ref, l_ref):
        n = pl.program_id(2)
        kk = jnp.concatenate([kp_ref[...], kc_ref[...]], axis=0)
        vv = jnp.concatenate([vp_ref[...], vc_ref[...]], axis=0)
        row = lax.broadcasted_iota(jnp.int32, (WINDOW, 2 * WINDOW), 0)
        col = lax.broadcasted_iota(jnp.int32, (WINDOW, 2 * WINDOW), 1)
        band = (col >= row) & (col <= row + WINDOW)
        q = q_ref[...]
        rows = [slice(b * WINDOW, (b + 1) * WINDOW) for b in range(per)]
        keys = [slice(b * WINDOW, (b + 2) * WINDOW) for b in range(per)]
        s = [lax.dot_general(q[rows[b], lanes[j]], kk[keys[b], lanes[j]], NT_DIMS,
                             preferred_element_type=F32) * scale for j, b in blocks]
        es, outs, lses = [], [], []
        for (j, b), sb in zip(blocks, s):
            valid = band if b else band & ((n > 0) | (col >= WINDOW))
            sb = jnp.where(valid, sb, NEG)
            mx = jnp.max(sb, axis=-1, keepdims=True)
            e = jnp.exp(sb - mx)
            den = jnp.sum(e, axis=-1, keepdims=True)
            es.append((e.astype(BF16), den))
            lses.append(jnp.broadcast_to(mx + jnp.log(den), (WINDOW, HEAD)))
        for (j, b), (e, den) in zip(blocks, es):
            outs.append(jnp.dot(e, vv[keys[b], lanes[j]], preferred_element_type=F32) / den)
        o_ref[...] = _tile_up(outs, hb, per)
        l_ref[...] = _tile_up(lses, hb, per)

    assert voff % hb == 0
    cur = lambda off: pl.BlockSpec((None, tq, hb * HEAD), lambda r, h, n: (r, n, off // hb + h))
    prev = lambda off: pl.BlockSpec((None, WINDOW, hb * HEAD),
                                    lambda r, h, n: (r, jnp.maximum(n * per - 1, 0), off // hb + h))
    out = jax.ShapeDtypeStruct((d, m, n_heads * HEAD), F32)
    return pl.pallas_call(
        body, name=name, grid=(d, n_heads // hb, nq),
        in_specs=[cur(0), cur(0), prev(0), cur(voff), prev(voff)],
        out_specs=[cur(0), cur(0)], out_shape=[out, out],
        compiler_params=_params(("parallel",) * 3, 32),
    )(q, k, k, v, v)


def _dilated_bwd(q, k, v, voff, dy, stats, n_heads, name):
    d, m, _ = q.shape
    tq = _dil_tq(m)
    nq = m // tq
    per = tq // WINDOW
    last_blk = m // WINDOW - 1
    hb = _dil_heads(tq, n_heads)
    scale = HEAD ** -0.5
    blocks = [(j, b) for j in range(hb) for b in range(per)]
    lanes = [slice(j * HEAD, (j + 1) * HEAD) for j in range(hb)]

    def body(qc_ref, qn_ref, kc_ref, kp_ref, vc_ref, vp_ref, dyc_ref, dyn_ref, sc_ref, sn_ref,
             dq_ref, dk_ref, dv_ref):
        n = pl.program_id(2)
        kk = jnp.concatenate([kp_ref[...], kc_ref[...]], axis=0)
        vv = jnp.concatenate([vp_ref[...], vc_ref[...]], axis=0)
        qq = jnp.concatenate([qc_ref[...], qn_ref[...]], axis=0)
        dyy = jnp.concatenate([dyc_ref[...], dyn_ref[...]], axis=0)
        st = jnp.concatenate([sc_ref[...], sn_ref[...]], axis=0)
        half = HEAD // 2
        row = lax.broadcasted_iota(jnp.int32, (WINDOW, 2 * WINDOW), 0)
        col = lax.broadcasted_iota(jnp.int32, (WINDOW, 2 * WINDOW), 1)
        band = (col >= row) & (col <= row + WINDOW)
        rows = [slice(b * WINDOW, (b + 1) * WINDOW) for b in range(per)]
        wide = [slice(b * WINDOW, (b + 2) * WINDOW) for b in range(per)]
        nt = lambda a, b: lax.dot_general(a, b, NT_DIMS, preferred_element_type=F32)
        s = [nt(qq[rows[b], lanes[j]], kk[wide[b], lanes[j]]) * scale for j, b in blocks]
        dp = [nt(dyy[rows[b], lanes[j]], vv[wide[b], lanes[j]]) for j, b in blocks]
        ds = []
        for (j, b), sb, dpb in zip(blocks, s, dp):
            valid = band if b else band & ((n > 0) | (col >= WINDOW))
            stb = st[rows[b], lanes[j]]
            p = jnp.where(valid, jnp.exp(jnp.minimum(sb - stb[:, :1], 0.0)), 0.0)
            ds.append((p * (dpb - stb[:, half:half + 1]) * scale).astype(BF16))
        dq = [jnp.dot(dsb, kk[wide[b], lanes[j]], preferred_element_type=F32) for (j, b), dsb in zip(blocks, ds)]
        kc, vc = kc_ref[...], vc_ref[...]
        s2 = [nt(kc[rows[b], lanes[j]], qq[wide[b], lanes[j]]) * scale for j, b in blocks]
        dp2 = [nt(vc[rows[b], lanes[j]], dyy[wide[b], lanes[j]]) for j, b in blocks]
        ds2, p2 = [], []
        for (j, b), sb, dpb in zip(blocks, s2, dp2):
            valid = band if b < per - 1 else band & ((n < nq - 1) | (col < WINDOW))
            st_t = st[wide[b], lanes[j]].T
            pb = jnp.where(valid, jnp.exp(jnp.minimum(sb - st_t[:1], 0.0)), 0.0)
            ds2.append((pb * (dpb - st_t[half:half + 1]) * scale).astype(BF16))
            p2.append(pb.astype(BF16))
        dk = [jnp.dot(x, qq[wide[b], lanes[j]], preferred_element_type=F32) for (j, b), x in zip(blocks, ds2)]
        dv = [jnp.dot(x, dyy[wide[b], lanes[j]], preferred_element_type=F32) for (j, b), x in zip(blocks, p2)]
        dq_ref[...] = _tile_up(dq, hb, per)
        dk_ref[...] = _tile_up(dk, hb, per)
        dv_ref[...] = _tile_up(dv, hb, per)

    assert voff % hb == 0
    cur = lambda off: pl.BlockSpec((None, tq, hb * HEAD), lambda r, h, n: (r, n, off // hb + h))
    prev = lambda off: pl.BlockSpec((None, WINDOW, hb * HEAD),
                                    lambda r, h, n: (r, jnp.maximum(n * per - 1, 0), off // hb + h))
    nxt = lambda off: pl.BlockSpec((None, WINDOW, hb * HEAD),
                                   lambda r, h, n: (r, jnp.minimum((n + 1) * per, last_blk), off // hb + h))
    out = jax.ShapeDtypeStruct((d, m, n_heads * HEAD), F32)
    return pl.pallas_call(
        body, name=name, grid=(d, n_heads // hb, nq),
        in_specs=[cur(0), nxt(0), cur(0), prev(0), cur(voff), prev(voff), cur(0), nxt(0), cur(0), nxt(0)],
        out_specs=[cur(0)] * 3, out_shape=[out] * 3,
        compiler_params=_params(("parallel",) * 3, 40),
    )(q, q, k, k, v, v, dy, dy, stats, stats)


def _mix_fwd(os_, ls_, name):
    tm = 256
    s, w = os_[0].shape
    n = len(os_)

    def fn(*vals):
        o, l = vals[:n], vals[n:]
        mx = functools.reduce(jnp.maximum, l)
        e = [jnp.exp(x - mx) for x in l]
        den = functools.reduce(jnp.add, e)
        y = functools.reduce(jnp.add, [ei * oi for ei, oi in zip(e, o)]) / den
        return y, mx + jnp.log(den)

    return _ew(fn, [_row_spec(a, tm) for a in list(os_) + list(ls_)],
               [((s, w), BF16, (tm, w), lambda i: (i, 0)), ((s, w), F32, (tm, w), lambda i: (i, 0))],
               (s // tm,), name)


def _mix_stats(dy, y, lse, n_heads, name):
    tm = 256
    s = dy.shape[0]
    w = n_heads * HEAD

    def fn(dys, ys, ls):
        lane = lax.broadcasted_iota(jnp.int32, (tm, HEAD), 1)
        packed = []
        for a, b, l in zip(_heads(dys, n_heads), _heads(ys, n_heads), _heads(ls, n_heads)):
            delta = jnp.sum(a * b.astype(F32), axis=-1, keepdims=True)
            packed.append(jnp.where(lane < HEAD // 2, l, delta))
        return dys, jnp.concatenate(packed, axis=1)

    blk = lambda a: (a, (tm, w), lambda i: (i, 0))
    out = lambda dt: ((s, w), dt, (tm, w), lambda i: (i, 0))
    return _ew(fn, [blk(dy), blk(y), blk(lse)], [out(BF16), out(F32)], (s // tm,), name)


SB_TQ = 1024
SB_TB = 512
SB_TB_FWD = 1024
SB_TK = 256


def _softplus(z):
    return jnp.where(z > 20.0, z, jnp.log(1.0 + jnp.exp(z)))


def _tri(t, cmp):
    rows = lax.broadcasted_iota(jnp.int32, (2 * t, t), 0)
    cols = lax.broadcasted_iota(jnp.int32, (2 * t, t), 1)
    return jnp.where(cmp(jnp.where(rows >= t, rows - t, rows), cols), 1.0, 0.0).astype(BF16)


def _tri_sum(x, tri):
    return jnp.dot(jnp.concatenate(_split_bf16(x), axis=1), tri, preferred_element_type=F32)


def _causal(rows, cols):
    return lax.broadcasted_iota(jnp.int32, (rows, cols), 1) < lax.broadcasted_iota(jnp.int32, (rows, cols), 0)


def _rowsum(x):
    return jnp.broadcast_to(jnp.sum(x, axis=-1, keepdims=True), (x.shape[0], HEAD))


def _over_keys(c, width):
    return jnp.concatenate([c] * (width // HEAD), axis=1)


def _from(x, r0):
    return x if r0 == 0 else x[r0:]


def _add_from(x, r0, upd):
    return x + upd if r0 == 0 else jnp.concatenate([x[:r0], x[r0:] + upd], axis=0)


def _sb_fwd(qkv, qoff, koff, voff, n_heads, name, comm=None):
    s = qkv.shape[0]
    tq, tb, tk = min(SB_TQ, s), min(SB_TB_FWD, s), SB_TK
    scale = HEAD ** -0.5
    host = _Host(comm, (n_heads, s // tq))

    def body(*refs):
        q_ref, k_ref, v_ref = refs[:3]
        comm_refs, ((o_ref, tot_ref), _) = host.split(refs, 3, 2)
        host.stage("first", comm_refs)
        i = pl.program_id(1)
        q = q_ref[...]
        after = _tri(tk, lambda a, b: a > b)

        def block(base, carry, o, diag_off):
            halves = list(reversed(range(tb // tk)))
            starts = [pl.multiple_of(base + h * tk, tk) for h in halves]
            r0s = [0 if diag_off is None else diag_off + h * tk for h in halves]
            masks = [None if diag_off is None else _causal(tq - r0, tk) for r0 in r0s]
            z = [lax.dot_general(_from(q, r0),k_ref[pl.ds(st, tk), :], NT_DIMS, preferred_element_type=F32) * scale
                 for st, r0 in zip(starts, r0s)]
            sp = [_softplus(zz) for zz in z]
            logsig = [zz - ss for zz, ss in zip(z, sp)]
            sp = [ss if m is None else jnp.where(m, ss, 0.0) for ss, m in zip(sp, masks)]
            sfx = [_tri_sum(ss, after) for ss in sp]
            probs = []
            for ls, sx, ss, m, r0 in zip(logsig, sfx, sp, masks, r0s):
                a = jnp.exp(ls - _over_keys(_from(carry, r0), tk) - sx)
                probs.append((a if m is None else jnp.where(m, a, 0.0)).astype(BF16))
                carry = _add_from(carry, r0, _rowsum(ss))
            for a, st, r0 in zip(probs, starts, r0s):
                o = _add_from(o, r0, jnp.dot(a, v_ref[pl.ds(st, tk), :], preferred_element_type=F32))
            return carry, o

        carry, o = jnp.zeros((tq, HEAD), F32), jnp.zeros((tq, HEAD), F32)
        for b in reversed(range(tq // tb)):
            carry, o = block(i * tq + b * tb, carry, o, b * tb)
        below = i * (tq // tb)
        carry, o = lax.fori_loop(0, below, lambda jj, co: block((below - 1 - jj) * tb, co[0], co[1], None),
                                 (carry, o))
        o_ref[...] = o.astype(o_ref.dtype)
        tot_ref[...] = carry
        host.stage("mid", comm_refs)
        host.stage("last", comm_refs)

    t = tq
    full = lambda off: pl.BlockSpec((s, HEAD), lambda h, i: (0, off + h))
    tile_spec = lambda off: pl.BlockSpec((t, HEAD), lambda h, i: (i, off + h))
    out = lambda dt: jax.ShapeDtypeStruct((s, n_heads * HEAD), dt)
    return pl.pallas_call(
        body, name=name, grid=(n_heads, s // t),
        in_specs=[tile_spec(qoff), full(koff), full(voff)] + host.in_specs,
        out_specs=[tile_spec(0), tile_spec(0)] + host.out_specs,
        out_shape=[out(BF16), out(F32)] + host.out_shape,
        scratch_shapes=host.scratch,
        compiler_params=_params(("parallel" if comm is None else "arbitrary", "arbitrary"), 40),
    )(qkv, qkv, qkv, *host.ins)


def _sb_bwd(qkv, qoff, koff, voff, do, tot, n_heads, name, comm=None):
    s = qkv.shape[0]
    tq, tb, tk = min(SB_TQ, s), SB_TB, SB_TK
    scale = HEAD ** -0.5
    host = _Host(comm, (n_heads, s // tq))

    def body(*refs):
        q_ref, k_ref, v_ref, do_ref, tot_ref = refs[:5]
        comm_refs, ((dq_ref, dk_ref, dv_ref), _) = host.split(refs, 5, 3)
        host.stage("first", comm_refs)
        i = pl.program_id(1)

        @pl.when(i == 0)
        def _():
            dk_ref[...] = jnp.zeros_like(dk_ref)
            dv_ref[...] = jnp.zeros_like(dv_ref)

        q = q_ref[...]
        do_b = do_ref[...].astype(BF16)
        total = tot_ref[...]
        upto = _tri(tk, lambda a, b: a <= b)
        before = _tri(tk, lambda a, b: a < b)[:tk]
        def block(base, lsum, psum, dq, diag_off):
            halves = list(range(tb // tk))
            starts = [pl.multiple_of(base + h * tk, tk) for h in halves]
            r0s = [0 if diag_off is None else diag_off + h * tk for h in halves]
            masks = [None if diag_off is None else _causal(tq - r0, tk) for r0 in r0s]
            keep = lambda x, m: x if m is None else jnp.where(m, x, 0.0)
            ks = [k_ref[pl.ds(st, tk), :] for st in starts]
            z = [lax.dot_general(_from(q, r0),kj, NT_DIMS, preferred_element_type=F32) * scale
                 for kj, r0 in zip(ks, r0s)]
            da = [lax.dot_general(_from(do_b, r0), v_ref[pl.ds(st, tk), :], NT_DIMS, preferred_element_type=F32)
                  for st, r0 in zip(starts, r0s)]
            sp = [_softplus(zz) for zz in z]
            logsig = [zz - ss for zz, ss in zip(z, sp)]
            sp = [keep(ss, m) for ss, m in zip(sp, masks)]
            pre = [_tri_sum(ss, upto) for ss in sp]
            probs, p = [], []
            for ls, px, ss, m, dd, r0 in zip(logsig, pre, sp, masks, da, r0s):
                a = keep(jnp.exp(ls - (_over_keys(_from(total, r0) - _from(lsum, r0), tk) - px)), m)
                probs.append(a.astype(BF16))
                p.append(a * dd)
                lsum = _add_from(lsum, r0, _rowsum(ss))
            cs = [jnp.dot(pp.astype(BF16), before, preferred_element_type=F32) for pp in p]
            dzs = []
            for ls, pp, cc, m, r0 in zip(logsig, p, cs, masks, r0s):
                c_all = _over_keys(_from(psum, r0), tk) + cc
                dzs.append(keep((pp - (pp + c_all) * jnp.exp(ls)) * scale, m).astype(BF16))
                psum = _add_from(psum, r0, _rowsum(pp))
            for dz_b, kj, r0 in zip(dzs, ks, r0s):
                dq = _add_from(dq, r0, jnp.dot(dz_b, kj, preferred_element_type=F32))
            for dz_b, a, st, r0 in zip(dzs, probs, starts, r0s):
                dk_ref[pl.ds(st, tk), :] += lax.dot_general(dz_b, _from(q, r0), TN_DIMS, preferred_element_type=F32)
                dv_ref[pl.ds(st, tk), :] += lax.dot_general(a, _from(do_b, r0), TN_DIMS,
                                                            preferred_element_type=F32)
            return lsum, psum, dq

        zero = jnp.zeros((tq, HEAD), F32)
        state = lax.fori_loop(0, i * (tq // tb), lambda j, c: block(j * tb, c[0], c[1], c[2], None),
                              (zero, zero, jnp.zeros((tq, HEAD), F32)))
        for b in range(tq // tb):
            state = block(i * tq + b * tb, *state, b * tb)
        dq_ref[...] = state[2].astype(dq_ref.dtype)
        host.stage("mid", comm_refs)
        host.stage("last", comm_refs)

    t = tq
    full = lambda off: pl.BlockSpec((s, HEAD), lambda h, i: (0, off + h))
    tile_spec = lambda off: pl.BlockSpec((t, HEAD), lambda h, i: (i, off + h))
    w = n_heads * HEAD
    return pl.pallas_call(
        body, name=name, grid=(n_heads, s // t),
        in_specs=[tile_spec(qoff), full(koff), full(voff), tile_spec(0), tile_spec(0)] + host.in_specs,
        out_specs=[tile_spec(0), full(0), full(0)] + host.out_specs,
        out_shape=[jax.ShapeDtypeStruct((s, w), BF16), jax.ShapeDtypeStruct((s, w), F32),
                   jax.ShapeDtypeStruct((s, w), F32)] + host.out_shape,
        scratch_shapes=host.scratch,
        compiler_params=_params(("parallel" if comm is None else "arbitrary", "arbitrary"), 48),
    )(qkv, qkv, qkv, do, tot, *host.ins)


def _coords():
    return lax.axis_index("x"), lax.axis_index("y"), lax.axis_index("c")


def _gather_plan(shards):
    n = len(shards)

    def run(stage, ins, outs, sems):
        send_sems, recv_sems, local_sems = sems
        x, y, c = _coords()
        me, sibling = (x, y, c), (x, y, 1 - c)
        chips = [(1 - x, y), (x, 1 - y), (1 - x, 1 - y)]

        def copy(w, k, block, to, src=None):
            dst = outs[w].at[4 * block[0] + 2 * block[1] + block[2]]
            return pltpu.make_async_remote_copy(
                src_ref=dst if src is None else src, dst_ref=dst,
                send_sem=send_sems.at[7 * w + k], recv_sem=recv_sems.at[7 * w + k],
                device_id=to, device_id_type=MESH)

        def mine():
            return [pltpu.make_async_copy(ins[w], outs[w].at[4 * x + 2 * y + c], local_sems.at[w]) for w in range(n)]

        def first():
            cps = []
            for w in range(n):
                cps.append(copy(w, 0, me, sibling, src=ins[w]))
                cps += [copy(w, 1 + j, me, (*chip, c), src=ins[w]) for j, chip in enumerate(chips)]
            return cps

        def passed():
            return [copy(w, 4 + j, (*chip, c), sibling) for w in range(n) for j, chip in enumerate(chips)]

        if stage == "first":
            for cp in mine() + first():
                cp.start()
        elif stage == "mid":
            onward = passed()
            for w in range(n):
                for j, chip in enumerate(chips):
                    copy(w, 1 + j, (*chip, c), me).wait_recv()
                    onward[3 * w + j].start()
        else:
            for w in range(n):
                copy(w, 0, sibling, me).wait_recv()
                for j, chip in enumerate(chips):
                    copy(w, 4 + j, (*chip, 1 - c), me).wait_recv()
            for cp in first() + passed():
                cp.wait_send()
            for cp in mine():
                cp.wait()

    return dict(
        ins=list(shards), run=run, stages=("first", "mid", "last"),
        out_shape=[jax.ShapeDtypeStruct((N_DEV,) + a.shape, a.dtype) for a in shards],
        scratch=[pltpu.SemaphoreType.DMA((7 * n,)), pltpu.SemaphoreType.DMA((7 * n,)), pltpu.SemaphoreType.DMA((n,))])


def _scatter_plan(grads):
    n = len(grads)

    def run(stage, ins, outs, sems):
        send_sems, recv_sems = sems
        x, y, c = _coords()
        cps = []
        for w in range(n):
            for mask in range(1, N_DEV):
                px, py, pc = x ^ (mask >> 2), y ^ ((mask >> 1) & 1), c ^ (mask & 1)
                cps.append(pltpu.make_async_remote_copy(
                    src_ref=ins[w].at[4 * px + 2 * py + pc], dst_ref=outs[w].at[mask - 1],
                    send_sem=send_sems.at[7 * w + mask - 1], recv_sem=recv_sems.at[7 * w + mask - 1],
                    device_id=(px, py, pc), device_id_type=MESH))
        for cp in cps:
            if stage == "first":
                cp.start()
            else:
                cp.wait()

    return dict(
        ins=list(grads), run=run, stages=("first", "last"),
        out_shape=[jax.ShapeDtypeStruct((N_DEV - 1,) + a.shape[1:], a.dtype) for a in grads],
        scratch=[pltpu.SemaphoreType.DMA((7 * n,)), pltpu.SemaphoreType.DMA((7 * n,))])


def _run_plan(plan, name):
    n_in, n_out = len(plan["ins"]), len(plan["out_shape"])

    def body(*refs):
        for stage in plan["stages"]:
            plan["run"](stage, refs[:n_in], refs[n_in:n_in + n_out], refs[n_in + n_out:])

    any_spec = pl.BlockSpec(memory_space=pl.ANY)
    return pl.pallas_call(
        body, name=name, in_specs=[any_spec] * n_in, out_specs=[any_spec] * n_out,
        out_shape=plan["out_shape"], scratch_shapes=plan["scratch"],
    )(*plan["ins"])


def _gather_rows(v):
    rows, width = v.shape

    def body(v_ref, out_ref, send_sems, recv_sems):
        x, y, c = _coords()
        out_ref[pl.ds(pl.multiple_of((4 * x + 2 * y + c) * rows, rows), rows), :] = v_ref[...]
        cps = []
        for mask in range(1, N_DEV):
            peer = (x ^ (mask >> 2), y ^ ((mask >> 1) & 1), c ^ (mask & 1))
            dst = out_ref.at[pl.ds(pl.multiple_of((4 * x + 2 * y + c) * rows, rows), rows), :]
            cps.append(pltpu.make_async_remote_copy(
                src_ref=v_ref, dst_ref=dst, send_sem=send_sems.at[mask - 1], recv_sem=recv_sems.at[mask - 1],
                device_id=peer, device_id_type=MESH))
        for cp in cps:
            cp.start()
        for cp in cps:
            cp.wait()

    vmem = pl.BlockSpec(memory_space=pltpu.VMEM)
    return pl.pallas_call(
        body, name="gather_small",
        in_specs=[vmem], out_specs=vmem,
        out_shape=jax.ShapeDtypeStruct((N_DEV * rows, width), F32),
        scratch_shapes=[pltpu.SemaphoreType.DMA((N_DEV - 1,)), pltpu.SemaphoreType.DMA((N_DEV - 1,))],
    )(v)


def _adamw(w, g, m, v):
    m = ADAM_B1 * m + (1.0 - ADAM_B1) * g
    v = ADAM_B2 * v + (1.0 - ADAM_B2) * jnp.square(g)
    m_hat = m / (1.0 - ADAM_B1 ** ADAM_STEP)
    v_hat = v / (1.0 - ADAM_B2 ** ADAM_STEP)
    delta = -ADAM_LR * (m_hat / (jnp.sqrt(v_hat) + ADAM_EPS) + ADAM_WD * w)
    return delta, m, v


def _tile_rows(rows, cols):
    tm = 1 << int(math.log2(max(2 * SUBLANES, (1 << 18) // cols)))
    while rows % tm:
        tm //= 2
    assert tm >= 2 * SUBLANES, (rows, cols)
    return tm


def _reduce_adam(w, m, v, own, got, name):
    r, c = w.shape
    tm = max(2 * SUBLANES, _tile_rows(r, c) // 2)

    def fn(wv, mv, vv, a, *peers):
        g = a
        for pv in peers:
            g = g + pv.astype(F32)
        return (g,) + _adamw(wv, g, mv, vv)

    blk = lambda a: (a, (tm, c), lambda i: (i, 0))
    got_blk = lambda j: (got, (None, tm, c), lambda i, j=j: (j, i, 0))
    return _ew(fn, [blk(w), blk(m), blk(v), blk(own)] + [got_blk(j) for j in range(N_DEV - 1)],
               [((r, c), F32, (tm, c), lambda i: (i, 0))] * 4, (r // tm,), name)


def _small_adam(gathered, params, moms, vels, widths):
    n = len(params)
    total = gathered.shape[1]

    def body(*refs):
        g_ref = refs[0]
        p_refs, m_refs, v_refs = refs[1:1 + n], refs[1 + n:1 + 2 * n], refs[1 + 2 * n:1 + 3 * n]
        sum_ref = refs[1 + 3 * n]
        outs = refs[2 + 3 * n:]
        g = g_ref[0:1, :]
        for p in range(1, N_DEV):
            g = g + g_ref[p * SUBLANES:p * SUBLANES + 1, :]
        sum_ref[...] = g
        off = 0
        for i, wd in enumerate(widths):
            d, m2, v2 = _adamw(p_refs[i][...], g[:, off:off + wd], m_refs[i][...], v_refs[i][...])
            outs[3 * i][...] = d
            outs[3 * i + 1][...] = m2
            outs[3 * i + 2][...] = v2
            off += wd

    vmem = pl.BlockSpec(memory_space=pltpu.VMEM)
    out_shape = [jax.ShapeDtypeStruct((1, total), F32)]
    for wd in widths:
        out_shape += [jax.ShapeDtypeStruct((1, wd), F32)] * 3
    return pl.pallas_call(
        body, name="small_adam",
        in_specs=[vmem] * (1 + 3 * n), out_specs=[vmem] * len(out_shape), out_shape=out_shape,
    )(gathered, *params, *moms, *vels)


def _cast_bf16(a, name):
    r, c = a.shape
    tm = _tile_rows(r, c)
    return _ew(lambda v: v, [(a, (tm, c), lambda i: (i, 0))], [((r, c), BF16, (tm, c), lambda i: (i, 0))],
               (r // tm,), name)[0]


def _fold_loss(parts, name):
    r, c = parts.shape

    def fn(v):
        return jnp.broadcast_to(jnp.sum(jnp.sum(v, axis=0, keepdims=True), axis=1, keepdims=True), (SUBLANES, HEAD))

    return _ew(fn, [_const_spec(parts)], [((SUBLANES, HEAD), F32, (SUBLANES, HEAD), lambda i: (0, 0))], (1,),
               name)[0][0:1]


def kernel(x, p, g_mix, w_in, qn_gain, kn_gain, w_branch_a, w_branch_b, w_out, g_mlp, w_up, w_down, g_ple, w_ple_gate, w_ple_proj, loss_target, m_g_mix, m_w_in, m_qn_gain, m_kn_gain, m_w_branch_a, m_w_branch_b, m_w_out, m_g_mlp, m_w_up, m_w_down, m_g_ple, m_w_ple_gate, m_w_ple_proj, v_g_mix, v_w_in, v_qn_gain, v_kn_gain, v_w_branch_a, v_w_branch_b, v_w_out, v_g_mlp, v_w_up, v_w_down, v_g_ple, v_w_ple_gate, v_w_ple_proj):
    x2 = x[0]
    tgt = loss_target[0]
    s, d = x2.shape
    wd_ = w_branch_a.shape[1]
    nh = wd_ // HEAD
    dff = w_up.shape[1]
    qkv_w = 6 * wd_
    tiles = lambda cols: cols // HEAD

    big = [w_in[0], w_branch_a[0], w_branch_b[0], w_out[0], w_up[0], w_down[0], w_ple_gate[0], w_ple_proj[0]]
    names = ["w_in", "w_branch_a", "w_branch_b", "w_out", "w_up", "w_down", "w_ple_gate", "w_ple_proj"]
    row_sharded = [False, False, False, True, False, True, True, False]
    shards = [_cast_bf16(a, "cast_" + nm) for a, nm in zip(big, names)]
    as_weight = lambda g, rs: g.reshape((1, N_DEV * g.shape[1], g.shape[2])) if rs else g
    as_blocks = lambda g, rs: g.reshape((N_DEV, g.shape[1] // N_DEV, g.shape[2])) if rs else g
    h, win = _rms_fwd(x2, g_mix, "norm_mix", comm=_gather_plan(shards[:1]))

    tm = 1024 if s % 1024 == 0 else s
    tm_in = tk_s = 2048 if s % 2048 == 0 else tm
    tn_of = lambda n: 512 if n % 512 == 0 else (256 if n % 256 == 0 else n)
    tn_in = 256 if win.shape[2] % 256 == 0 else HEAD

    qk_raw =_mm("nn", h, win, tm=tm_in, tn=tn_in, tk=d, out_dtypes=[F32], name="proj_qk",
                 n_off=0, n_cnt=2 * wd_ // tn_in)[0]
    rest = _mm("nn", h, win, tm=4096 if s % 4096 == 0 else tm_in, tn=tn_in, tk=d, out_dtypes=[BF16], name="proj_rest",
               n_off=2 * wd_ // tn_in, n_cnt=(win.shape[0] * win.shape[2] - 2 * wd_) // tn_in)[0]
    o_va, o_qb, o_kb, o_vb, o_ga, o_gb = 0, tiles(wd_), tiles(2 * wd_), tiles(3 * wd_), tiles(4 * wd_), tiles(4 * wd_ + d)
    tabs = _rope_tables(s)
    qa = _headnorm_rope(qk_raw, 0, qn_gain, tabs, nh, "rope_q")
    ka = _headnorm_rope(qk_raw, 1, kn_gain, tabs, nh, "rope_k")
    va = rest[:, :wd_]

    outs, lses = [], []
    for dil in DILATIONS:
        o_g, l_g = _dilated_fwd(_phase_major(qa, dil), _phase_major(ka, dil), _phase_major(va, dil), 0, nh,
                                f"dilated_fwd_{dil}")
        outs.append(_token_major(o_g))
        lses.append(_token_major(l_g))
    ya, lse_all = _mix_fwd(outs, lses, "mix_fwd")
    yb, sb_tot, *others = _sb_fwd(rest, o_qb, o_kb, o_vb, nh, "sb_fwd", comm=_gather_plan(shards[1:]))
    wba, wbb, wout, wup, wdown, wgate, wproj = [as_weight(g, rs) for g, rs in zip(others, row_sharded[1:])]

    plain = lambda g: g.transpose(1, 0, 2).reshape(1, g.shape[1], N_DEV * g.shape[2])
    wba_p, wbb_p = plain(wba), plain(wbb)
    tn_d = tn_of(d)
    za = _mm("nn", ya, wba_p, tm=tm, tn=tn_d, tk=wd_, out_dtypes=[BF16], name="branch_a")[0]

    def merge(acc, zav, gav, gbv):
        return _sigmoid(gav.astype(F32)) * zav.astype(F32) + _sigmoid(gbv.astype(F32)) * acc, acc

    merged, zb = _mm("nn", yb, wbb_p, tm=tm, tn=tn_d, tk=wd_, out_dtypes=[BF16, BF16], name="branch_b_merge",
                     epilogue=merge, extras=[(za, 0), (rest, o_ga * HEAD // tn_d), (rest, o_gb * HEAD // tn_d)])
    x1 = _mm("nn", merged, wout, tm=tm, tn=512, tk=d, out_dtypes=[F32], name="out_proj",
             epilogue=lambda acc, xv: (acc + xv,), extras=[(x2, 0)])[0]

    hm = _rms_fwd(x1, g_mlp, "norm_mlp")
    tn_u = min(wup.shape[2], 1024)
    u, act = _mm("nn", hm, wup, tm=tm, tn=tn_u, tk=d, out_dtypes=[BF16, BF16], name="mlp_up",
                 epilogue=lambda acc: (acc, jnp.square(jnp.maximum(acc, 0.0))))
    x3 = _mm("nn", act, wdown, tm=tm, tn=1024, tk=min(dff, 2048), out_dtypes=[F32], name="mlp_down",
             epilogue=lambda acc, xv: (acc + xv,), extras=[(x1, 0)])[0]

    hp = _rms_fwd(x3, g_ple, "norm_ple")
    p_b = _cast_bf16(p[0, 0], "cast_p")
    pp = _mm("nn", p_b, wproj, tm=tm, tn=tn_of(wproj.shape[2]), tk=p_b.shape[1], out_dtypes=[BF16],
             name="ple_proj")[0]

    def head(acc, ppv, xv, tv):
        sg = _sigmoid(acc)
        ppf = ppv.astype(F32)
        err = xv + ppf * sg - tv
        dy = err / d
        sq = jnp.square(err)
        return dy, dy * sg, dy * ppf * sg * (1.0 - sg), sq.reshape(-1, SUBLANES, sq.shape[-1]).sum(axis=0)

    n_i = s // tm
    dy, d_pp, d_gt, sq_parts = _mm(
        "nn", hp, wgate, tm=tm, tn=512, tk=d, out_dtypes=[F32, BF16, BF16], name="ple_gate_loss", epilogue=head,
        extras=[(pp, 0), (x3, 0), (tgt, 0)],
        extra_outs=[((n_i * SUBLANES, d), F32, (SUBLANES, 512), lambda i, j: (i, j))])
    loss_vec = _fold_loss(sq_parts, "loss_fold") * 0.5 / d

    both = [F32, BF16]
    g_wproj = _mm("tn", p_b, d_pp, tm=p_b.shape[1], tn=tn_of(wproj.shape[2]), tk=tk_s, out_dtypes=both,
                  name="grad_w_ple_proj", out_nb=N_DEV)
    g_wgate = _mm("tn", hp, d_gt, tm=1024, tn=1024, tk=tk_s, out_dtypes=both, name="grad_w_ple_gate")
    d_hp = _mm("nt", d_gt, wgate, tm=tm, tn=512, tk=d, out_dtypes=[F32], name="d_hp")[0]
    dx3, dx3_b, g_gple = _rms_bwd(d_hp, x3, g_ple, dy, "norm_ple_bwd")

    d_u = _mm("nt", dx3_b, wdown, tm=tm, tn=1024, tk=d, out_dtypes=[BF16], name="d_u",
              epilogue=lambda acc, uv: (acc * (2.0 * jnp.maximum(uv.astype(F32), 0.0)),), extras=[(u, 0)])[0]
    g_wdown = _mm("tn", act, dx3_b, tm=1024, tn=1024, tk=tk_s, out_dtypes=both, name="grad_w_down")
    g_wup = _mm("tn", hm, d_u, tm=1024, tn=wup.shape[2], tk=tk_s, out_dtypes=both, name="grad_w_up", out_nb=N_DEV)
    d_hm = _mm("nt", d_u, wup, tm=tm, tn=min(d, 2048), tk=wup.shape[2], out_dtypes=[F32], name="d_hm")[0]
    dx1, dx1_b, g_gmlp = _rms_bwd(d_hm, x1, g_mlp, dx3, "norm_mlp_bwd")

    def unmerge(acc, gav, gbv, zav, zbv):
        sa, sb = _sigmoid(gav.astype(F32)), _sigmoid(gbv.astype(F32))
        return acc * sa, acc * sb, acc * zav.astype(F32) * sa * (1.0 - sa), acc * zbv.astype(F32) * sb * (1.0 - sb)

    d_za, d_zb, d_ga, d_gb = _mm(
        "nt", dx1_b, wout, tm=tm, tn=512, tk=d, out_dtypes=[BF16] * 4, name="d_merged", epilogue=unmerge,
        extras=[(rest, o_ga * HEAD // 512), (rest, o_gb * HEAD // 512), (za, 0), (zb, 0)])
    g_wout = _mm("tn", merged, dx1_b, tm=1024, tn=1024, tk=tk_s, out_dtypes=both, name="grad_w_out")
    g_wba = _mm("tn", ya, d_za, tm=wd_, tn=wba.shape[2], tk=tk_s, out_dtypes=both, name="grad_w_branch_a",
                out_nb=N_DEV)
    g_wbb = _mm("tn", yb, d_zb, tm=wd_, tn=wbb.shape[2], tk=tk_s, out_dtypes=both, name="grad_w_branch_b",
                out_nb=N_DEV)
    d_ya = _mm("nt", d_za, wba_p, tm=tm, tn=wd_, tk=d, out_dtypes=[F32], name="d_ya")[0]
    d_yb = _mm("nt", d_zb, wbb_p, tm=tm, tn=wd_, tk=d, out_dtypes=[F32], name="d_yb")[0]

    grads = [None, g_wba, g_wbb, g_wout, g_wup, g_wdown, g_wgate, g_wproj]
    early = _scatter_plan([as_blocks(g[1], rs) for g, rs in zip(grads[1:], row_sharded[1:])])
    d_qb, d_kb, d_vb, *got = _sb_bwd(rest, o_qb, o_kb, o_vb, d_yb, sb_tot, nh, "sb_bwd", comm=early)
    d_ya_b, stats = _mix_stats(d_ya, ya, lse_all, nh, "mix_stats")
    dqs, dks, dvs = [], [], []
    for dil in DILATIONS:
        dq_g, dk_g, dv_g = _dilated_bwd(
            _phase_major(qa, dil), _phase_major(ka, dil), _phase_major(va, dil), 0, _phase_major(d_ya_b, dil),
            _phase_major(stats, dil), nh, f"dilated_bwd_{dil}")
        dqs.append(_token_major(dq_g))
        dks.append(_token_major(dk_g))
        dvs.append(_token_major(dv_g))
    d_qa, g_qn = _headnorm_rope_bwd(dqs, qk_raw, 0, qn_gain, tabs, nh, "rope_q_bwd")
    d_ka, g_kn = _headnorm_rope_bwd(dks, qk_raw, 1, kn_gain, tabs, nh, "rope_k_bwd")
    tmr = 256
    d_va = _ew(lambda a, b, c: a + b + c, [_row_spec(a, tmr) for a in dvs],
               [((s, wd_), BF16, (tmr, wd_), lambda i: (i, 0))], (s // tmr,), "sum_dv")[0]
    d_proj = jnp.concatenate([d_qa, d_ka, d_va, d_qb, d_kb.astype(BF16), d_vb.astype(BF16), d_ga, d_gb], axis=1)

    grads[0] = _mm("tn", h, d_proj, tm=1024, tn=win.shape[2], tk=tk_s, out_dtypes=both, name="grad_w_in",
                   out_nb=N_DEV)
    d_h, got_in = _mm("nt", d_proj, win, tm=tm, tn=min(d, 2048), tk=win.shape[2], out_dtypes=[F32], name="d_h",
                      comm=_scatter_plan([grads[0][1]]))
    got = [got_in] + got
    dx, _, g_gmix = _rms_bwd(d_h, x2, g_mix, dx1, "norm_mix_bwd")

    cx, cy, cc = _coords()
    me = 4 * cx + 2 * cy + cc
    moms = [m_w_in, m_w_branch_a, m_w_branch_b, m_w_out, m_w_up, m_w_down, m_w_ple_gate, m_w_ple_proj]
    vels = [v_w_in, v_w_branch_a, v_w_branch_b, v_w_out, v_w_up, v_w_down, v_w_ple_gate, v_w_ple_proj]
    big_out = {}
    for i, nm in enumerate(names):
        own = lax.dynamic_index_in_dim(as_blocks(grads[i][0], row_sharded[i]), me, axis=0, keepdims=False)
        big_out[nm] = [a[None] for a in _reduce_adam(big[i], moms[i][0], vels[i][0], own, got[i], "adam_" + nm)]

    small_names = ["g_mix", "qn_gain", "kn_gain", "g_mlp", "g_ple"]
    small_p = [g_mix, qn_gain, kn_gain, g_mlp, g_ple]
    small_m = [m_g_mix, m_qn_gain, m_kn_gain, m_g_mlp, m_g_ple]
    small_v = [v_g_mix, v_qn_gain, v_kn_gain, v_g_mlp, v_g_ple]
    small_g = [g_gmix, g_qn, g_kn, g_gmlp, g_gple]
    widths = [a.shape[1] for a in small_p]
    vec = jnp.concatenate(small_g + [loss_vec], axis=1)
    vec = jnp.pad(vec, ((0, SUBLANES - 1), (0, 0)))
    res = _small_adam(_gather_rows(vec), small_p, small_m, small_v, widths)
    summed = res[0]
    small_out, off = {}, 0
    for i, nm in enumerate(small_names):
        small_out[nm] = [summed[:, off:off + widths[i]]] + list(res[1 + 3 * i:4 + 3 * i])
        off += widths[i]
    loss = summed[0, off]

    order = ["g_mix", "w_in", "qn_gain", "kn_gain", "w_branch_a", "w_branch_b", "w_out", "g_mlp", "w_up", "w_down",
             "g_ple", "w_ple_gate", "w_ple_proj"]
    table = {**big_out, **small_out}
    result = [loss, dx[None]]
    for kind in range(4):
        result += [table[nm][kind] for nm in order]
    return tuple(result)
```
